```python
import jax, jax.numpy as jnp
from jax import lax
import numpy as np

D_MODEL = 2048
BATCH = 2
SEQ = 8192
DEPTH = 4

GRID_W = 64
CTX_LEN = 256
N_MIXERS = 3
EXPAND = 2
D_BRANCH = EXPAND * D_MODEL
FNET_GROUPS = 16
FNET_GROUP_DIM = D_BRANCH // FNET_GROUPS
ATTN_HEAD_DIM = 64
ATTN_Q_HEADS = D_BRANCH // ATTN_HEAD_DIM
ATTN_KV_HEADS = 8
ATTN_GROUP = ATTN_Q_HEADS // ATTN_KV_HEADS
WINDOW = 128
ATTN_BLOCK = 128
ROPE_BASE = 10000.0
GMLP_CHUNK = 128
GMLP_GROUPS = 16
GMLP_GROUP_DIM = D_BRANCH // GMLP_GROUPS
EPS = 1e-6
NEG_INF = -1e30

kernel_name = 'hybrid_fnet_swa_gmlp_prefix_dit'


def _rmsnorm(x, g):
    x32 = x.astype(jnp.float32)
    y = x32 * lax.rsqrt(jnp.mean(x32 * x32, axis=-1, keepdims=True) + EPS)
    return (y * g.astype(jnp.float32)).astype(x.dtype)


def _layernorm(x, g, b):
    x32 = x.astype(jnp.float32)
    mu = jnp.mean(x32, axis=-1, keepdims=True)
    var = jnp.mean(jnp.square(x32 - mu), axis=-1, keepdims=True)
    y = (x32 - mu) * lax.rsqrt(var + EPS)
    return (y * g.astype(jnp.float32) + b.astype(jnp.float32)).astype(x.dtype)


def _rope_half(xp, pos):
    nf = xp.shape[-1] // 2
    inv = ROPE_BASE ** (-jnp.arange(nf, dtype=jnp.float32) / nf)
    ang = pos[:, None] * inv[None, :]
    cos = jnp.cos(ang)[None, :, None, :]
    sin = jnp.sin(ang)[None, :, None, :]
    x1, x2 = xp[..., :nf], xp[..., nf:]
    return jnp.concatenate([x1 * cos - x2 * sin, x1 * sin + x2 * cos], axis=-1)


def _axial_rope(x, rows, cols):
    half = x.shape[-1] // 2
    x32 = x.astype(jnp.float32)
    out = jnp.concatenate([_rope_half(x32[..., :half], rows), _rope_half(x32[..., half:], cols)], axis=-1)
    return out.astype(x.dtype)


def _fourier_branch(h, w_in, w_mix):
    b, n, _ = h.shape
    u, z = jnp.split(h @ w_in, 2, axis=-1)
    ug = u.reshape(b, n, FNET_GROUPS, FNET_GROUP_DIM).astype(jnp.float32)
    f = jnp.fft.fft2(ug, axes=(1, 3), norm='ortho').real.astype(h.dtype)
    f = jnp.einsum('bngc,gcd->bngd', f, w_mix).reshape(b, n, D_BRANCH)
    return f * jax.nn.silu(z)


def _gmlp_branch(h, w_in, w_s, b_s, ln_g, ln_b):
    b, n, _ = h.shape
    uv, z = jnp.split(h @ w_in, [2 * D_BRANCH], axis=-1)
    u, v = jnp.split(jax.nn.gelu(uv), 2, axis=-1)
    v = _layernorm(v, ln_g, ln_b)
    vc = v.reshape(b, n // GMLP_CHUNK, GMLP_CHUNK, GMLP_GROUPS, GMLP_GROUP_DIM)
    s = jnp.einsum('gst,bktgc->bksgc', w_s, vc) + b_s.T[:, :, None]
    return u * s.reshape(b, n, D_BRANCH) * jax.nn.silu(z)


def _attend_with_sink(q, k, v, sink, mask=None):
    s = jnp.einsum('bqhgd,bkhd->bhgqk', q, k).astype(jnp.float32) * (ATTN_HEAD_DIM ** -0.5)
    if mask is not None:
        s = jnp.where(mask, s, NEG_INF)
    sk = jnp.broadcast_to(sink.astype(jnp.float32)[None, :, :, None, None], s.shape[:-1] + (1,))
    p = jax.nn.softmax(jnp.concatenate([s, sk], axis=-1), axis=-1)[..., :-1]
    return jnp.einsum('bhgqk,bkhd->bqhgd', p.astype(v.dtype), v)


def _attention_branch(h, hc, w_in, sink, rows, cols, need_ctx):
    b, n, _ = h.shape
    lc = hc.shape[1]
    kvw = ATTN_KV_HEADS * ATTN_HEAD_DIM
    q, k, v, z = jnp.split(h @ w_in, [D_BRANCH, D_BRANCH + kvw, D_BRANCH + 2 * kvw], axis=-1)
    kc, vc = jnp.split(hc @ w_in[:, D_BRANCH:D_BRANCH + 2 * kvw], 2, axis=-1)
    kc = kc.reshape(b, lc, ATTN_KV_HEADS, ATTN_HEAD_DIM)
    vc = vc.reshape(b, lc, ATTN_KV_HEADS, ATTN_HEAD_DIM)
    sink = sink.reshape(ATTN_KV_HEADS, ATTN_GROUP)
    q = _axial_rope(q.reshape(b, n, ATTN_Q_HEADS, ATTN_HEAD_DIM), rows, cols)
    k = _axial_rope(k.reshape(b, n, ATTN_KV_HEADS, ATTN_HEAD_DIM), rows, cols)
    v = v.reshape(b, n, ATTN_KV_HEADS, ATTN_HEAD_DIM)
    nb = n // ATTN_BLOCK
    qb = q.reshape(b, nb, ATTN_BLOCK, ATTN_KV_HEADS, ATTN_GROUP, ATTN_HEAD_DIM)

    def band(t):
        tp = jnp.pad(t, ((0, 0), (ATTN_BLOCK, ATTN_BLOCK), (0, 0), (0, 0)))
        tp = tp.reshape(b, nb + 2, ATTN_BLOCK, ATTN_KV_HEADS, ATTN_HEAD_DIM)
        return jnp.concatenate([tp[:, :-2], tp[:, 1:-1], tp[:, 2:]], axis=2)

    kw, vw = band(k), band(v)
    q_off = jnp.arange(ATTN_BLOCK)
    k_off = jnp.arange(3 * ATTN_BLOCK) - ATTN_BLOCK
    ctx_mask = jnp.ones((ATTN_BLOCK, lc), dtype=bool)

    def block(args):
        qi, ki, vi, blk = args
        qpos = blk * ATTN_BLOCK + q_off
        kpos = blk * ATTN_BLOCK + k_off
        valid = (jnp.abs(qpos[:, None] - kpos[None, :]) <= WINDOW) & ((kpos >= 0) & (kpos < n))[None, :]
        mask = jnp.concatenate([valid, ctx_mask], axis=1)
        keys = jnp.concatenate([ki, kc], axis=1)
        vals = jnp.concatenate([vi, vc], axis=1)
        return _attend_with_sink(qi, keys, vals, sink, mask)

    o = lax.map(block, (jnp.moveaxis(qb, 1, 0), jnp.moveaxis(kw, 1, 0), jnp.moveaxis(vw, 1, 0), jnp.arange(nb)))
    y = jnp.moveaxis(o, 0, 1).reshape(b, n, D_BRANCH) * jax.nn.silu(z)
    if not need_ctx:
        return y, None
    qc = (hc @ w_in[:, :D_BRANCH]).reshape(b, lc, ATTN_KV_HEADS, ATTN_GROUP, ATTN_HEAD_DIM)
    zc = hc @ w_in[:, D_BRANCH + 2 * kvw:]
    oc = _attend_with_sink(qc, kc, vc, sink).reshape(b, lc, D_BRANCH)
    return y, oc * jax.nn.silu(zc)


def setup_inputs(seed: int = 0) -> dict:
    key = jax.random.key(seed)
    ks = jax.random.split(key, 18)
    n_of = [len(range(m, DEPTH, N_MIXERS)) for m in range(N_MIXERS)]
    kvw2 = 2 * ATTN_KV_HEADS * ATTN_HEAD_DIM

    def nrm(k, shape, s):
        return jax.random.normal(k, shape, jnp.float32) * s

    return {
        'x': nrm(ks[0], (BATCH, SEQ, D_MODEL), 1.0),
        'c': nrm(ks[1], (BATCH, D_MODEL), 1.0),
        'ctx': nrm(ks[2], (BATCH, CTX_LEN, D_MODEL), 1.0),
        'c_ctx': nrm(ks[3], (D_MODEL,), 1.0),
        'norm_g': 1.0 + nrm(ks[4], (DEPTH, D_MODEL), 0.02),
        'ada_w': nrm(ks[5], (DEPTH, D_MODEL, 3 * D_MODEL), 0.5 * D_MODEL ** -0.5),
        'ada_b': nrm(ks[6], (DEPTH, 3 * D_MODEL), 0.02),
        'w_out': nrm(ks[7], (DEPTH, D_BRANCH, D_MODEL), D_BRANCH ** -0.5),
        'fnet_w_in': nrm(ks[8], (n_of[0], D_MODEL, 2 * D_BRANCH), D_MODEL ** -0.5),
        'fnet_w_mix': nrm(ks[9], (n_of[0], FNET_GROUPS, FNET_GROUP_DIM, FNET_GROUP_DIM), FNET_GROUP_DIM ** -0.5),
        'attn_w_in': nrm(ks[10], (n_of[1], D_MODEL, 2 * D_BRANCH + kvw2), D_MODEL ** -0.5),
        'attn_sink': nrm(ks[11], (n_of[1], ATTN_Q_HEADS), 0.5),
        'gmlp_w_in': nrm(ks[12], (n_of[2], D_MODEL, 3 * D_BRANCH), D_MODEL ** -0.5),
        'gmlp_w_s': nrm(ks[13], (n_of[2], GMLP_GROUPS, GMLP_CHUNK, GMLP_CHUNK), GMLP_CHUNK ** -0.5),
        'gmlp_b_s': 1.0 + nrm(ks[14], (n_of[2], GMLP_GROUPS, GMLP_CHUNK), 0.02),
        'gmlp_ln_g': 1.0 + nrm(ks[15], (n_of[2], D_BRANCH), 0.02),
        'gmlp_ln_b': nrm(ks[16], (n_of[2], D_BRANCH), 0.02),
        'final_g': 1.0 + nrm(ks[17], (D_MODEL,), 0.02),
    }


def reference(x, c, ctx, c_ctx, norm_g, ada_w, ada_b, w_out, fnet_w_in, fnet_w_mix, attn_w_in, attn_sink,
              gmlp_w_in, gmlp_w_s, gmlp_b_s, gmlp_ln_g, gmlp_ln_b, final_g):
    n = x.shape[1]
    ROWS = n // GRID_W
    rows = jnp.repeat(jnp.arange(ROWS, dtype=jnp.float32), GRID_W)
    cols = jnp.tile(jnp.arange(GRID_W, dtype=jnp.float32), ROWS)
    xc = ctx
    for i in range(DEPTH):
        kind, j = i % N_MIXERS, i // N_MIXERS
        need_ctx = i < DEPTH - 1
        shift, scale, gate = jnp.split((jax.nn.silu(c) @ ada_w[i] + ada_b[i])[:, None, :], 3, axis=-1)
        h = _rmsnorm(x, norm_g[i]) * (1 + scale) + shift
        hc = None
        if need_ctx or kind == 1:
            shift_c, scale_c, gate_c = jnp.split(jax.nn.silu(c_ctx) @ ada_w[i] + ada_b[i], 3)
            hc = _rmsnorm(xc, norm_g[i]) * (1 + scale_c) + shift_c
        if kind == 0:
            y = _fourier_branch(h, fnet_w_in[j], fnet_w_mix[j])
            yc = _fourier_branch(hc, fnet_w_in[j], fnet_w_mix[j]) if need_ctx else None
        elif kind == 1:
            y, yc = _attention_branch(h, hc, attn_w_in[j], attn_sink[j], rows, cols, need_ctx)
        else:
            y = _gmlp_branch(h, gmlp_w_in[j], gmlp_w_s[j], gmlp_b_s[j], gmlp_ln_g[j], gmlp_ln_b[j])
            yc = (_gmlp_branch(hc, gmlp_w_in[j], gmlp_w_s[j], gmlp_b_s[j], gmlp_ln_g[j], gmlp_ln_b[j])
                  if need_ctx else None)
        x = x + gate * (y @ w_out[i])
        if need_ctx:
            xc = xc + gate_c * (yc @ w_out[i])
    return _rmsnorm(x, final_g)
```

```python
import functools
import math

import numpy as np
import jax
import jax.numpy as jnp
from jax import lax
from jax.experimental import pallas as pl
from jax.experimental.pallas import tpu as pltpu

F32 = jnp.float32
BF16 = jnp.bfloat16

D_MODEL = 2048
D_BRANCH = 4096
DEPTH = 4
GRID_W = 64
FNET_GROUPS = 16
FNET_GROUP_DIM = 256
HEAD_DIM = 64
KV_HEADS = 8
Q_GROUP = 8
ATTN_BLOCK = 128
ROPE_BASE = 10000.0
GMLP_CHUNK = 128
GMLP_GROUPS = 16
EPS = 1e-6
NEG_INF = -1e30

LANES = 128
DFT_NB = 128
SLABS = D_BRANCH // LANES
VMEM_LIMIT = 56 * 1024 * 1024


def _cparams(n_axes, vmem=VMEM_LIMIT):
    return pltpu.CompilerParams(dimension_semantics=("arbitrary",) * n_axes,
                                vmem_limit_bytes=vmem)


def _silu(z):
    return z / (1.0 + jnp.exp(-z))


def _gelu_tanh(x):
    c = math.sqrt(2.0 / math.pi)
    return 0.5 * x * (1.0 + jnp.tanh(c * (x + 0.044715 * (x * x * x))))


def _mod_rmsnorm(x, g, scale, shift):
    y = x * lax.rsqrt(jnp.mean(x * x, axis=-1, keepdims=True) + EPS) * g
    return y * (1.0 + scale) + shift


def _channel_dft_matrix():
    n = FNET_GROUP_DIM
    k = np.arange(n, dtype=np.float64)
    ang = 2.0 * np.pi * np.outer(k, k) / n
    s = 1.0 / math.sqrt(n)
    return np.concatenate([np.cos(ang) * s, -np.sin(ang) * s], axis=1).astype(np.float32)


def _position_dft_matrices(seq):
    na, nb = seq // DFT_NB, DFT_NB
    a = np.arange(na, dtype=np.float64)
    b = np.arange(nb, dtype=np.float64)
    ang = 2.0 * np.pi * (a[None, None, :] * a[None, :, None] / na + b[:, None, None] * a[None, :, None] / seq)
    mr = np.cos(ang) / math.sqrt(na)
    mi = -np.sin(ang) / math.sqrt(na)
    fa = np.concatenate([np.concatenate([mr, -mi], axis=2), np.concatenate([mi, mr], axis=2)], axis=1)
    angb = 2.0 * np.pi * np.outer(b, b) / nb
    fb = np.concatenate([np.cos(angb), np.sin(angb)], axis=1) / math.sqrt(nb)
    return fa.astype(np.float32), fb.astype(np.float32)


def _dense_dft_matrix(n):
    k = np.arange(n, dtype=np.float64)
    ang = 2.0 * np.pi * np.outer(k, k) / n
    return (np.concatenate([np.cos(ang), np.sin(ang)], axis=1) / math.sqrt(n)).astype(np.float32)


def _rope_tables(seq):
    nf = HEAD_DIM // 4
    inv = ROPE_BASE ** (-np.arange(nf, dtype=np.float64) / nf)
    t = np.arange(seq)
    rows = (t // GRID_W).astype(np.float64)
    cols = (t % GRID_W).astype(np.float64)
    parts_c, parts_s = [], []
    for pos in (rows, cols):
        ang = pos[:, None] * inv[None, :]
        parts_c += [np.cos(ang), np.cos(ang)]
        parts_s += [-np.sin(ang), np.sin(ang)]
    cos = np.concatenate(parts_c, axis=1)
    sin = np.concatenate(parts_s, axis=1)
    reps = LANES // HEAD_DIM
    return (np.tile(cos, (1, reps)).astype(np.float32), np.tile(sin, (1, reps)).astype(np.float32))


def _mods_kernel(cv_ref, w_ref, b_ref, o_ref):
    a = _silu(cv_ref[...])
    o_ref[...] = jnp.dot(a, w_ref[...], preferred_element_type=F32,
                         precision=lax.Precision.HIGHEST) + b_ref[...]


def _mods(cvec, ada_w, ada_b):
    depth, d, n3 = ada_w.shape
    tn = 1024
    return pl.pallas_call(
        _mods_kernel,
        grid=(depth, n3 // tn),
        in_specs=[pl.BlockSpec((8, d), lambda i, j: (0, 0)),
                  pl.BlockSpec((None, d, tn), lambda i, j: (i, 0, j)),
                  pl.BlockSpec((None, 1, tn), lambda i, j: (i, 0, j))],
        out_specs=pl.BlockSpec((None, 8, tn), lambda i, j: (i, 0, j)),
        out_shape=jax.ShapeDtypeStruct((depth, 8, n3), F32),
        compiler_params=_cparams(2),
        name="ada_mods",
    )(cvec, ada_w, ada_b.reshape(depth, 1, n3))


def _mod_spec(layer, kind, who_of_row):
    return pl.BlockSpec((None, None, None, 1, D_MODEL),
                        lambda r, *_: (layer, who_of_row(r), kind, 0, 0))


def _row_spec(vec_layer):
    return pl.BlockSpec((None, 1, D_MODEL), lambda r, *_: (vec_layer, 0, 0))


def _prenorm_kernel(x_ref, g_ref, sc_ref, sh_ref, h_ref):
    h_ref[...] = _mod_rmsnorm(x_ref[...], g_ref[...], sc_ref[...], sh_ref[...]).astype(BF16)


def _prenorm(x2, norm_g3, mods, layer, who, tm):
    m = x2.shape[0]
    return pl.pallas_call(
        _prenorm_kernel,
        grid=(m // tm,),
        in_specs=[pl.BlockSpec((tm, D_MODEL), lambda r: (r, 0)),
                  _row_spec(layer), _mod_spec(layer, 1, who), _mod_spec(layer, 0, who)],
        out_specs=pl.BlockSpec((tm, D_MODEL), lambda r: (r, 0)),
        out_shape=jax.ShapeDtypeStruct((m, D_MODEL), BF16),
        compiler_params=_cparams(1),
        name="prenorm",
    )(x2, norm_g3, mods, mods)


def _proj_kernel(epilogue, n_extra, a_ref, w_ref, *rest):
    acc = jnp.dot(a_ref[...], w_ref[...], preferred_element_type=F32)
    epilogue(acc, rest[:n_extra], rest[n_extra:])


def _proj(h, w, col0, ncols, tm, tn, epilogue, extras, extra_specs, out_shapes, out_specs, name,
          w_single_buffer=False):
    m, k = h.shape
    off = col0 // tn
    w_kwargs = dict(pipeline_mode=pl.Buffered(1)) if w_single_buffer else {}
    return pl.pallas_call(
        functools.partial(_proj_kernel, epilogue, len(extras)),
        grid=(m // tm, ncols // tn),
        in_specs=[pl.BlockSpec((tm, k), lambda i, j: (i, 0)),
                  pl.BlockSpec((k, tn), lambda i, j: (0, j + off), **w_kwargs)] + list(extra_specs),
        out_specs=out_specs,
        out_shape=out_shapes,
        compiler_params=_cparams(2),
        name=name,
    )(h, w, *extras)


def _epi_silu(acc, extras, outs):
    outs[0][...] = _silu(acc).astype(BF16)


def _epi_gelu(acc, extras, outs):
    outs[0][...] = _gelu_tanh(acc).astype(BF16)


def _epi_cast(acc, extras, outs):
    outs[0][...] = acc.astype(BF16)


def _epi_gelu_layernorm(acc, extras, outs):
    g_ref, b_ref = extras
    ge = _gelu_tanh(acc)
    mu = jnp.mean(ge, axis=-1, keepdims=True)
    d = ge - mu
    var = jnp.mean(d * d, axis=-1, keepdims=True)
    outs[0][...] = (d * lax.rsqrt(var + EPS) * g_ref[...] + b_ref[...]).astype(BF16)


def _epi_channel_dft(acc, extras, outs):
    cs = extras[0][...]
    ar_ref, ai_ref = outs
    for g in range(acc.shape[1] // FNET_GROUP_DIM):
        ub = acc[:, g * FNET_GROUP_DIM:(g + 1) * FNET_GROUP_DIM].astype(BF16)
        ab = jnp.dot(ub, cs, preferred_element_type=F32)
        ar_ref[2 * g] = ab[:, 0:128]
        ar_ref[2 * g + 1] = ab[:, 128:256]
        ai_ref[2 * g] = ab[:, 256:384]
        ai_ref[2 * g + 1] = ab[:, 384:512]


def _epi_rope(scale, acc, extras, outs):
    cos = extras[0][...]
    sin = extras[1][...]
    lane = lax.broadcasted_iota(jnp.int32, (1, LANES), 1)
    first = (lane % 32) < 16
    for t in range(acc.shape[1] // LANES):
        x = acc[:, t * LANES:(t + 1) * LANES]
        partner = jnp.where(first, pltpu.roll(x, LANES - 16, 1), pltpu.roll(x, 16, 1))
        y = x * cos + partner * sin
        if scale != 1.0:
            y = y * scale
        outs[0][:, t * LANES:(t + 1) * LANES] = y.astype(BF16)


def _proj_simple(h, w, col0, ncols, epilogue, name, tm, tn=512):
    m = h.shape[0]
    return _proj(h, w, col0, ncols, tm, tn, epilogue, (), (),
                 jax.ShapeDtypeStruct((m, ncols), BF16),
                 pl.BlockSpec((tm, tn), lambda i, j: (i, j)), name)


def _proj_rope(h, w, col0, ncols, scale, cos_t, sin_t, seq, name, tm, tn=512):
    m = h.shape[0]
    per_batch = seq // tm
    tab_spec = pl.BlockSpec((tm, LANES), lambda i, j: (i % per_batch, 0))
    return _proj(h, w, col0, ncols, tm, tn, functools.partial(_epi_rope, scale),
                 (cos_t, sin_t), (tab_spec, tab_spec),
                 jax.ShapeDtypeStruct((m, ncols), BF16),
                 pl.BlockSpec((tm, tn), lambda i, j: (i, j)), name)


def _proj_channel_dft(h, w, cs, name, tm, tn=512):
    m = h.shape[0]
    slab_shape = jax.ShapeDtypeStruct((SLABS, m, LANES), F32)
    slab_spec = pl.BlockSpec((tn // LANES, tm, LANES), lambda i, j: (j, i, 0))
    return _proj(h, w, 0, D_BRANCH, tm, tn, _epi_channel_dft,
                 (cs,), (pl.BlockSpec(cs.shape, lambda i, j: (0, 0)),),
                 (slab_shape, slab_shape), (slab_spec, slab_spec), name)


def _proj_gelu_layernorm(h, w, col0, ln_g, ln_b, name, tm):
    m = h.shape[0]
    vec_spec = pl.BlockSpec((1, D_BRANCH), lambda i, j: (0, 0))
    return _proj(h, w, col0, D_BRANCH, tm, D_BRANCH, _epi_gelu_layernorm,
                 (ln_g.reshape(1, D_BRANCH), ln_b.reshape(1, D_BRANCH)), (vec_spec, vec_spec),
                 jax.ShapeDtypeStruct((m, D_BRANCH), BF16),
                 pl.BlockSpec((tm, D_BRANCH), lambda i, j: (i, j)), name, w_single_buffer=True)


def _dft_kernel(na, pitch, ar_ref, ai_ref, fa_ref, fb_ref, o_ref, er_ref, ei_ref):
    def stage_a(b, carry):
        zr = ar_ref[pl.ds(b, na, stride=DFT_NB), :]
        zi = ai_ref[pl.ds(b, na, stride=DFT_NB), :]
        d = jnp.concatenate([zr, zi], axis=0).astype(BF16)
        e = jnp.dot(fa_ref[b], d, preferred_element_type=F32)
        off = pl.multiple_of(b * pitch, 8)
        er_ref[pl.ds(off, na), :] = e[:na]
        ei_ref[pl.ds(off, na), :] = e[na:]
        return carry

    lax.fori_loop(0, DFT_NB, stage_a, 0)

    def stage_b(ka, carry):
        er = er_ref[pl.ds(ka, DFT_NB, stride=pitch), :]
        ei = ei_ref[pl.ds(ka, DFT_NB, stride=pitch), :]
        d = jnp.concatenate([er, ei], axis=0).astype(BF16)
        o_ref[pl.ds(ka, DFT_NB, stride=na), :] = jnp.dot(fb_ref[...], d, preferred_element_type=F32)
        return carry

    lax.fori_loop(0, na, stage_b, 0)


def _position_dft(ar, ai, batch, seq):
    na = seq // DFT_NB
    pitch = na + 8
    fa_np, fb_np = _position_dft_matrices(seq)
    fa = jnp.asarray(fa_np).astype(BF16)
    fb = jnp.asarray(fb_np).astype(BF16)
    slab = pl.BlockSpec((None, seq, LANES), lambda s, b: (s, b, 0))
    return pl.pallas_call(
        functools.partial(_dft_kernel, na, pitch),
        grid=(SLABS, batch),
        in_specs=[slab, slab,
                  pl.BlockSpec(fa.shape, lambda s, b: (0, 0, 0)),
                  pl.BlockSpec(fb.shape, lambda s, b: (0, 0))],
        out_specs=slab,
        out_shape=jax.ShapeDtypeStruct(ar.shape, F32),
        scratch_shapes=[pltpu.VMEM((DFT_NB * pitch, LANES), F32),
                        pltpu.VMEM((DFT_NB * pitch, LANES), F32)],
        compiler_params=_cparams(2),
        name="position_dft",
    )(ar, ai, fa, fb)


def _dft_dense_kernel(ar_ref, ai_ref, fd_ref, o_ref):
    d = jnp.concatenate([ar_ref[...], ai_ref[...]], axis=0).astype(BF16)
    o_ref[...] = jnp.dot(fd_ref[...], d, preferred_element_type=F32)


def _position_dft_dense(ar, ai, batch, seq):
    fd = jnp.asarray(_dense_dft_matrix(seq)).astype(BF16)
    slab = pl.BlockSpec((None, seq, LANES), lambda s, b: (s, b, 0))
    return pl.pallas_call(
        _dft_dense_kernel,
        grid=(SLABS, batch),
        in_specs=[slab, slab, pl.BlockSpec(fd.shape, lambda s, b: (0, 0))],
        out_specs=slab,
        out_shape=jax.ShapeDtypeStruct(ar.shape, F32),
        compiler_params=_cparams(2),
        name="position_dft_dense",
    )(ar, ai, fd)


def _mix_gate_kernel(f_ref, wm_ref, zg_ref, y_ref):
    f = jnp.concatenate([f_ref[0], f_ref[1]], axis=1).astype(BF16)
    mixed = jnp.dot(f, wm_ref[...], preferred_element_type=F32)
    y_ref[...] = (mixed * zg_ref[...].astype(F32)).astype(BF16)


def _mix_gate(f, w_mix, zg, tm):
    m = zg.shape[0]
    gd = FNET_GROUP_DIM
    return pl.pallas_call(
        _mix_gate_kernel,
        grid=(m // tm, FNET_GROUPS),
        in_specs=[pl.BlockSpec((gd // LANES, tm, LANES), lambda r, g: (g, r, 0)),
                  pl.BlockSpec((None, gd, gd), lambda r, g: (g, 0, 0)),
                  pl.BlockSpec((tm, gd), lambda r, g: (r, g))],
        out_specs=pl.BlockSpec((tm, gd), lambda r, g: (r, g)),
        out_shape=jax.ShapeDtypeStruct((m, D_BRANCH), BF16),
        compiler_params=_cparams(2),
        name="fnet_mix_gate",
    )(f, w_mix, zg)


def _attn_kernel(n_band, nblk, sink_ref, q_ref, *refs):
    k_refs = refs[:n_band + 1]
    v_refs = refs[n_band + 1:2 * n_band + 2]
    zg_ref, y_ref = refs[2 * n_band + 2:]
    h = pl.program_id(1)
    i = pl.program_id(2)
    blk = ATTN_BLOCK
    n_pair = Q_GROUP // 2

    q = q_ref[...]
    q4 = jnp.concatenate([q[:, t * LANES:(t + 1) * LANES] for t in range(n_pair)], axis=0)
    lane = lax.broadcasted_iota(jnp.int32, (1, LANES), 1)
    lo = lane < HEAD_DIM
    zero = jnp.zeros_like(q4)
    keys = jnp.concatenate([r[...] for r in k_refs], axis=0)
    vals = jnp.concatenate([r[...] for r in v_refs], axis=0)
    nk = keys.shape[0]
    dn = (((1,), (1,)), ((), ()))
    s_lo = lax.dot_general(jnp.where(lo, q4, zero), keys, dn, preferred_element_type=F32)
    s_hi = lax.dot_general(jnp.where(lo, zero, q4), keys, dn, preferred_element_type=F32)

    if n_band:
        r = lax.broadcasted_iota(jnp.int32, (n_pair * blk, nk), 0) % blk
        c = lax.broadcasted_iota(jnp.int32, (n_pair * blk, nk), 1)
        prev_ok = c >= r + jnp.where(i > 0, 0, blk)
        next_ok = (c - 2 * blk) <= r - jnp.where(i < nblk - 1, 0, blk)
        ok = ((c >= blk) | prev_ok) & ((c < 2 * blk) | (c >= 3 * blk) | next_ok)
        s_lo = jnp.where(ok, s_lo, NEG_INF)
        s_hi = jnp.where(ok, s_hi, NEG_INF)

    def softmax_parts(s, hi):
        sk = jnp.concatenate([jnp.full((blk, 1), sink_ref[h, 2 * t + hi], F32) for t in range(n_pair)], axis=0)
        mx = jnp.maximum(jnp.max(s, axis=-1, keepdims=True), sk)
        p = jnp.exp(s - mx)
        denom = jnp.sum(p, axis=-1, keepdims=True) + jnp.exp(sk - mx)
        return p.astype(BF16), 1.0 / denom

    p_lo, r_lo = softmax_parts(s_lo, 0)
    p_hi, r_hi = softmax_parts(s_hi, 1)
    p = jnp.concatenate([p_lo, p_hi], axis=1)
    vzero = jnp.zeros_like(vals)
    v2 = jnp.concatenate([jnp.where(lo, vals, vzero), jnp.where(lo, vzero, vals)], axis=0)
    o = jnp.dot(p, v2, preferred_element_type=F32)
    o = o * jnp.where(lo, r_lo, r_hi)
    for t in range(n_pair):
        sl = slice(t * LANES, (t + 1) * LANES)
        y_ref[:, sl] = (o[t * blk:(t + 1) * blk] * zg_ref[:, sl].astype(F32)).astype(BF16)


def _attention(q, kd, vd, kcd, vcd, zg, sink, batch, seq, ctx_len, use_band):
    blk = ATTN_BLOCK
    nblk = seq // blk
    qw = Q_GROUP * HEAD_DIM
    q_spec = pl.BlockSpec((blk, qw), lambda b, h, i: (b * nblk + i, h))
    ctx_spec = pl.BlockSpec((ctx_len, LANES), lambda b, h, i: (b, h))
    if use_band:
        def band(delta):
            return pl.BlockSpec(
                (blk, LANES), lambda b, h, i: (b * nblk + jnp.clip(i + delta, 0, nblk - 1), h))
        k_specs = [band(-1), band(0), band(1), ctx_spec]
        k_args, v_args = [kd, kd, kd, kcd], [vd, vd, vd, vcd]
        n_band = 3
    else:
        k_specs, k_args, v_args, n_band = [ctx_spec], [kcd], [vcd], 0
    return pl.pallas_call(
        functools.partial(_attn_kernel, n_band, nblk),
        grid=(batch, KV_HEADS, nblk),
        in_specs=[pl.BlockSpec(memory_space=pltpu.SMEM), q_spec] + k_specs + k_specs + [q_spec],
        out_specs=q_spec,
        out_shape=jax.ShapeDtypeStruct(q.shape, BF16),
        compiler_params=_cparams(3),
        name="attention_band" if use_band else "attention_ctx",
    )(sink, q, *k_args, *v_args, zg)


def _sgu_kernel(n_chunks, gu_ref, vn_ref, zg_ref, ws_ref, bs_ref, y_ref):
    ws = ws_ref[...]
    bs = bs_ref[...]
    for c in range(n_chunks):
        sl = slice(c * GMLP_CHUNK, (c + 1) * GMLP_CHUNK)
        s = jnp.dot(ws, vn_ref[sl, :], preferred_element_type=F32) + bs
        y_ref[sl, :] = (gu_ref[sl, :].astype(F32) * s * zg_ref[sl, :].astype(F32)).astype(BF16)


def _spatial_gate(gu, vn, zg, w_s, b_s, tm):
    m = gu.shape[0]
    gd = D_BRANCH // GMLP_GROUPS
    tile = pl.BlockSpec((tm, gd), lambda r, g: (r, g))
    return pl.pallas_call(
        functools.partial(_sgu_kernel, tm // GMLP_CHUNK),
        grid=(m // tm, GMLP_GROUPS),
        in_specs=[tile, tile, tile,
                  pl.BlockSpec((None, GMLP_CHUNK, GMLP_CHUNK), lambda r, g: (g, 0, 0)),
                  pl.BlockSpec((None, GMLP_CHUNK, 1), lambda r, g: (g, 0, 0))],
        out_specs=tile,
        out_shape=jax.ShapeDtypeStruct(gu.shape, BF16),
        compiler_params=_cparams(2),
        name="gmlp_spatial_gate",
    )(gu, vn, zg, w_s, b_s.reshape(GMLP_GROUPS, GMLP_CHUNK, 1))


def _wout_kernel(mode, y_ref, w_ref, x_ref, gate_ref, *refs):
    acc = jnp.dot(y_ref[...], w_ref[...], preferred_element_type=F32)
    xn = x_ref[...] + gate_ref[...] * acc
    if mode == "final":
        g_ref, o_ref = refs
        o_ref[...] = xn * lax.rsqrt(jnp.mean(xn * xn, axis=-1, keepdims=True) + EPS) * g_ref[...]
    elif mode == "next":
        g_ref, sc_ref, sh_ref, xo_ref, h_ref = refs
        xo_ref[...] = xn
        h_ref[...] = _mod_rmsnorm(xn, g_ref[...], sc_ref[...], sh_ref[...]).astype(BF16)
    else:
        refs[0][...] = xn


def _out_proj(y, w_out, x2, mods, layer, who, mode, norm_vec, tm):
    m = x2.shape[0]
    row = pl.BlockSpec((tm, D_MODEL), lambda r: (r, 0))
    in_specs = [pl.BlockSpec((tm, D_BRANCH), lambda r: (r, 0)),
                pl.BlockSpec((None, D_BRANCH, D_MODEL), lambda r: (layer, 0, 0),
                             pipeline_mode=pl.Buffered(1)),
                row, _mod_spec(layer, 2, who)]
    args = [y, w_out, x2, mods]
    xs = jax.ShapeDtypeStruct((m, D_MODEL), F32)
    if mode == "final":
        in_specs.append(pl.BlockSpec((1, D_MODEL), lambda r: (0, 0)))
        args.append(norm_vec.reshape(1, D_MODEL))
        out_shape, out_specs = xs, row
    elif mode == "next":
        in_specs += [_row_spec(layer + 1), _mod_spec(layer + 1, 1, who), _mod_spec(layer + 1, 0, who)]
        args += [norm_vec, mods, mods]
        out_shape = (xs, jax.ShapeDtypeStruct((m, D_MODEL), BF16))
        out_specs = (row, row)
    else:
        out_shape, out_specs = xs, row
    return pl.pallas_call(
        functools.partial(_wout_kernel, mode),
        grid=(m // tm,),
        in_specs=in_specs,
        out_specs=out_specs,
        out_shape=out_shape,
        compiler_params=_cparams(1),
        name="out_proj_" + mode,
    )(*args)


def kernel(x, c, ctx, c_ctx, norm_g, ada_w, ada_b, w_out, fnet_w_in, fnet_w_mix, attn_w_in, attn_sink,
           gmlp_w_in, gmlp_w_s, gmlp_b_s, gmlp_ln_g, gmlp_ln_b, final_g):
    batch, seq, d = x.shape
    ctx_len = ctx.shape[1]
    assert d == D_MODEL and seq % (DFT_NB * 8) == 0 and seq % GRID_W == 0 and batch < 8
    m_lat, m_ctx = batch * seq, batch * ctx_len
    tm_lat = min(1024, seq)
    tm_ctx = ctx_len
    tm_out = 256

    def who_lat(tm):
        return lambda r: (r * tm) // seq

    who_ctx = lambda r: batch

    cvec = jnp.zeros((8, d), F32).at[:batch].set(c).at[batch].set(c_ctx)
    mods = _mods(cvec, ada_w, ada_b).reshape(DEPTH, 8, 3, 1, d)
    norm_g3 = norm_g.reshape(DEPTH, 1, d)
    w_out_b = w_out.astype(BF16)

    xl = x.reshape(m_lat, d)
    xc = ctx.reshape(m_ctx, d)
    hl = _prenorm(xl, norm_g3, mods, 0, who_lat(512), 512)
    hc = _prenorm(xc, norm_g3, mods, 0, who_ctx, tm_ctx)

    cs = jnp.asarray(_channel_dft_matrix()).astype(BF16)
    cos_np, sin_np = _rope_tables(seq)
    cos_t, sin_t = jnp.asarray(cos_np), jnp.asarray(sin_np)

    def finish(y, x2, layer, who_fn, need_next, is_final):
        if is_final:
            return _out_proj(y, w_out_b, x2, mods, layer, who_fn, "final", final_g, tm_out), None
        if need_next:
            return _out_proj(y, w_out_b, x2, mods, layer, who_fn, "next", norm_g3, tm_out)
        return _out_proj(y, w_out_b, x2, mods, layer, who_fn, "plain", None, tm_out), None

    out = None
    for i in range(DEPTH):
        kind, j = i % 3, i // 3
        need_ctx = i < DEPTH - 1
        last = i == DEPTH - 1
        if kind == 0:
            w = fnet_w_in[j].astype(BF16)
            wm = fnet_w_mix[j].astype(BF16)

            def fnet_branch(h, tm, dense, nseq):
                ar, ai = _proj_channel_dft(h, w, cs, "fnet_in_u", tm)
                zg = _proj_simple(h, w, D_BRANCH, D_BRANCH, _epi_silu, "fnet_in_z", tm)
                dft = _position_dft_dense if dense else _position_dft
                f = dft(ar, ai, batch, nseq)
                return _mix_gate(f, wm, zg, min(tm, 512))

            y = fnet_branch(hl, tm_lat, False, seq)
            yc = fnet_branch(hc, tm_ctx, True, ctx_len) if need_ctx else None
        elif kind == 1:
            w = attn_w_in[j]
            kvw = KV_HEADS * HEAD_DIM
            wq = w[:, :D_BRANCH].astype(BF16)
            wz = w[:, D_BRANCH + 2 * kvw:].astype(BF16)

            def dup(wpart):
                w3 = wpart.reshape(d, KV_HEADS, 1, HEAD_DIM)
                return jnp.broadcast_to(w3, (d, KV_HEADS, 2, HEAD_DIM)).reshape(d, 2 * kvw).astype(BF16)

            wk = dup(w[:, D_BRANCH:D_BRANCH + kvw])
            wv = dup(w[:, D_BRANCH + kvw:D_BRANCH + 2 * kvw])
            sink = attn_sink[j].reshape(KV_HEADS, Q_GROUP)
            scale = HEAD_DIM ** -0.5
            q = _proj_rope(hl, wq, 0, D_BRANCH, scale, cos_t, sin_t, seq, "attn_in_q", tm_lat)
            kd = _proj_rope(hl, wk, 0, 2 * kvw, 1.0, cos_t, sin_t, seq, "attn_in_k", tm_lat)
            vd = _proj_simple(hl, wv, 0, 2 * kvw, _epi_cast, "attn_in_v", tm_lat)
            zg = _proj_simple(hl, wz, 0, D_BRANCH, _epi_silu, "attn_in_z", tm_lat)
            kcd = _proj_simple(hc, wk, 0, 2 * kvw, _epi_cast, "attn_in_kc", tm_ctx)
            vcd = _proj_simple(hc, wv, 0, 2 * kvw, _epi_cast, "attn_in_vc", tm_ctx)
            y = _attention(q, kd, vd, kcd, vcd, zg, sink, batch, seq, ctx_len, True)
            yc = None
            if need_ctx:
                qc = _proj_simple(hc, wq * jnp.asarray(scale, BF16), 0, D_BRANCH, _epi_cast, "attn_in_qc", tm_ctx)
                zgc = _proj_simple(hc, wz, 0, D_BRANCH, _epi_silu, "attn_in_zc", tm_ctx)
                yc = _attention(qc, None, None, kcd, vcd, zgc, sink, batch, ctx_len, ctx_len, False)
        else:
            w = gmlp_w_in[j].astype(BF16)
            ws = gmlp_w_s[j].astype(BF16)

            def gmlp_branch(h, tm):
                gu = _proj_simple(h, w, 0, D_BRANCH, _epi_gelu, "gmlp_in_u", tm)
                vn = _proj_gelu_layernorm(h, w, D_BRANCH, gmlp_ln_g[j], gmlp_ln_b[j], "gmlp_in_v", 256)
                zg = _proj_simple(h, w, 2 * D_BRANCH, D_BRANCH, _epi_silu, "gmlp_in_z", tm)
                return _spatial_gate(gu, vn, zg, ws, gmlp_b_s[j], min(tm, 512))

            y = gmlp_branch(hl, tm_lat)
            yc = gmlp_branch(hc, tm_ctx) if need_ctx else None

        res, hl = finish(y, xl, i, who_lat(tm_out), not last, last)
        if last:
            out = res
        else:
            xl = res
        if need_ctx:
            xc, hc = finish(yc, xc, i, who_ctx, i + 1 < DEPTH - 1 or (i + 1) % 3 == 1, False)
    return out.reshape(batch, seq, d)
```

```python
import functools
import math

import numpy as np
import jax
import jax.numpy as jnp
from jax import lax
from jax.experimental import pallas as pl
from jax.experimental.pallas import tpu as pltpu

F32 = jnp.float32
BF16 = jnp.bfloat16

D_MODEL = 2048
D_BRANCH = 4096
DEPTH = 4
GRID_W = 64
FNET_GROUPS = 16
FNET_GROUP_DIM = 256
HEAD_DIM = 64
KV_HEADS = 8
Q_GROUP = 8
ATTN_BLOCK = 128
ROPE_BASE = 10000.0
GMLP_CHUNK = 128
GMLP_GROUPS = 16
EPS = 1e-6
NEG_INF = -1e30

LANES = 128
DFT_NB = 128
DFT_ROWS = 16
SLABS = D_BRANCH // LANES
VMEM_LIMIT = 56 * 1024 * 1024


def _cparams(n_axes, vmem=VMEM_LIMIT):
    return pltpu.CompilerParams(dimension_semantics=("arbitrary",) * n_axes,
                                vmem_limit_bytes=vmem)


def _silu(z):
    return z / (1.0 + jnp.exp(-z))


def _gelu_tanh(x):
    c = math.sqrt(2.0 / math.pi)
    return 0.5 * x * (1.0 + jnp.tanh(c * (x + 0.044715 * (x * x * x))))


def _mod_rmsnorm(x, g, scale, shift):
    y = x * lax.rsqrt(jnp.mean(x * x, axis=-1, keepdims=True) + EPS) * g
    return y * (1.0 + scale) + shift


def _channel_dft_matrix():
    n = FNET_GROUP_DIM
    k = np.arange(n, dtype=np.float64)
    ang = 2.0 * np.pi * np.outer(k, k) / n
    s = 1.0 / math.sqrt(n)
    return np.concatenate([np.cos(ang) * s, -np.sin(ang) * s], axis=1).astype(np.float32)


def _position_dft_matrices(seq):
    na, nb = seq // DFT_NB, DFT_NB
    a = np.arange(na, dtype=np.float64)
    b = np.arange(nb, dtype=np.float64)
    ang = 2.0 * np.pi * (a[None, None, :] * a[None, :, None] / na + b[:, None, None] * a[None, :, None] / seq)
    mr = np.cos(ang) / math.sqrt(na)
    mi = -np.sin(ang) / math.sqrt(na)
    fa = np.concatenate([np.concatenate([mr, -mi], axis=2), np.concatenate([mi, mr], axis=2)], axis=1)
    angb = 2.0 * np.pi * np.outer(b, b) / nb
    fb = np.concatenate([np.cos(angb), np.sin(angb)], axis=1) / math.sqrt(nb)
    return fa.astype(np.float32), fb.astype(np.float32)


def _dense_dft_matrix(n):
    k = np.arange(n, dtype=np.float64)
    ang = 2.0 * np.pi * np.outer(k, k) / n
    return (np.concatenate([np.cos(ang), np.sin(ang)], axis=1) / math.sqrt(n)).astype(np.float32)


def _rope_tables(seq):
    nf = HEAD_DIM // 4
    inv = ROPE_BASE ** (-np.arange(nf, dtype=np.float64) / nf)
    t = np.arange(seq)
    rows = (t // GRID_W).astype(np.float64)
    cols = (t % GRID_W).astype(np.float64)
    parts_c, parts_s = [], []
    for pos in (rows, cols):
        ang = pos[:, None] * inv[None, :]
        parts_c += [np.cos(ang), np.cos(ang)]
        parts_s += [-np.sin(ang), np.sin(ang)]
    cos = np.concatenate(parts_c, axis=1)
    sin = np.concatenate(parts_s, axis=1)
    reps = LANES // HEAD_DIM
    return (np.tile(cos, (1, reps)).astype(np.float32), np.tile(sin, (1, reps)).astype(np.float32))


def _mods_kernel(cv_ref, w_ref, b_ref, o_ref):
    a = _silu(cv_ref[...])
    o_ref[...] = jnp.dot(a, w_ref[...], preferred_element_type=F32,
                         precision=lax.Precision.HIGHEST) + b_ref[...]


def _mods(cvec, ada_w, ada_b):
    depth, d, n3 = ada_w.shape
    tn = 1024
    return pl.pallas_call(
        _mods_kernel,
        grid=(depth, n3 // tn),
        in_specs=[pl.BlockSpec((8, d), lambda i, j: (0, 0)),
                  pl.BlockSpec((None, d, tn), lambda i, j: (i, 0, j)),
                  pl.BlockSpec((None, 1, tn), lambda i, j: (i, 0, j))],
        out_specs=pl.BlockSpec((None, 8, tn), lambda i, j: (i, 0, j)),
        out_shape=jax.ShapeDtypeStruct((depth, 8, n3), F32),
        compiler_params=_cparams(2),
        name="ada_mods",
    )(cvec, ada_w, ada_b.reshape(depth, 1, n3))


def _mod_spec(layer, kind, who_of_row):
    return pl.BlockSpec((None, None, None, 1, D_MODEL),
                        lambda r, *_: (layer, who_of_row(r), kind, 0, 0))


def _row_spec(vec_layer):
    return pl.BlockSpec((None, 1, D_MODEL), lambda r, *_: (vec_layer, 0, 0))


def _prenorm_kernel(x_ref, g_ref, sc_ref, sh_ref, h_ref):
    h_ref[...] = _mod_rmsnorm(x_ref[...], g_ref[...], sc_ref[...], sh_ref[...]).astype(BF16)


def _prenorm(x2, norm_g3, mods, layer, who, tm):
    m = x2.shape[0]
    return pl.pallas_call(
        _prenorm_kernel,
        grid=(m // tm,),
        in_specs=[pl.BlockSpec((tm, D_MODEL), lambda r: (r, 0)),
                  _row_spec(layer), _mod_spec(layer, 1, who), _mod_spec(layer, 0, who)],
        out_specs=pl.BlockSpec((tm, D_MODEL), lambda r: (r, 0)),
        out_shape=jax.ShapeDtypeStruct((m, D_MODEL), BF16),
        compiler_params=_cparams(1),
        name="prenorm",
    )(x2, norm_g3, mods, mods)


def _proj_kernel(epilogue, n_extra, a_ref, w_ref, *rest):
    a = a_ref[...]
    if a.ndim == 3:
        a = a.reshape(a.shape[0] * a.shape[1], a.shape[2])
    acc = jnp.dot(a, w_ref[...], preferred_element_type=F32)
    epilogue(acc, rest[:n_extra], rest[n_extra:])


def _proj(h, w, col0, ncols, tm, tn, epilogue, extras, extra_specs, out_shapes, out_specs, name,
          w_single_buffer=False, lhs_spec=None):
    k = h.shape[-1]
    m = h.size // k
    off = col0 // tn
    w_kwargs = dict(pipeline_mode=pl.Buffered(1)) if w_single_buffer else {}
    if lhs_spec is None:
        lhs_spec = pl.BlockSpec((tm, k), lambda i, j: (i, 0))
    return pl.pallas_call(
        functools.partial(_proj_kernel, epilogue, len(extras)),
        grid=(m // tm, ncols // tn),
        in_specs=[lhs_spec,
                  pl.BlockSpec((k, tn), lambda i, j: (0, j + off), **w_kwargs)] + list(extra_specs),
        out_specs=out_specs,
        out_shape=out_shapes,
        compiler_params=_cparams(2),
        name=name,
    )(h, w, *extras)


def _epi_silu(acc, extras, outs):
    outs[0][...] = _silu(acc).astype(BF16)


def _epi_silu_slabs(acc, extras, outs):
    for t in range(acc.shape[1] // LANES):
        outs[0][t] = _silu(acc[:, t * LANES:(t + 1) * LANES]).astype(BF16)


def _epi_gelu(acc, extras, outs):
    outs[0][...] = _gelu_tanh(acc).astype(BF16)


def _epi_cast(acc, extras, outs):
    outs[0][...] = acc.astype(BF16)


def _epi_gelu_layernorm(acc, extras, outs):
    g_ref, b_ref = extras
    ge = _gelu_tanh(acc)
    mu = jnp.mean(ge, axis=-1, keepdims=True)
    d = ge - mu
    var = jnp.mean(d * d, axis=-1, keepdims=True)
    outs[0][...] = (d * lax.rsqrt(var + EPS) * g_ref[...] + b_ref[...]).astype(BF16)


def _epi_channel_dft(acc, extras, outs):
    cs_ref = extras[0]
    ar_ref, ai_ref = outs
    for g in range(acc.shape[1] // FNET_GROUP_DIM):
        ub = acc[:, g * FNET_GROUP_DIM:(g + 1) * FNET_GROUP_DIM].astype(BF16)
        ab = jnp.dot(ub, cs_ref[g], preferred_element_type=F32)
        ar_ref[2 * g] = ab[:, 0:128]
        ar_ref[2 * g + 1] = ab[:, 128:256]
        ai_ref[2 * g] = ab[:, 256:384]
        ai_ref[2 * g + 1] = ab[:, 384:512]


def _epi_rope(scale, acc, extras, outs):
    cos = extras[0][...]
    sin = extras[1][...]
    lane = lax.broadcasted_iota(jnp.int32, (1, LANES), 1)
    first = (lane % 32) < 16
    for t in range(acc.shape[1] // LANES):
        x = acc[:, t * LANES:(t + 1) * LANES]
        partner = jnp.where(first, pltpu.roll(x, LANES - 16, 1), pltpu.roll(x, 16, 1))
        y = x * cos + partner * sin
        if scale != 1.0:
            y = y * scale
        outs[0][:, t * LANES:(t + 1) * LANES] = y.astype(BF16)


def _proj_simple(h, w, col0, ncols, epilogue, name, tm, tn=512):
    m = h.shape[0]
    return _proj(h, w, col0, ncols, tm, tn, epilogue, (), (),
                 jax.ShapeDtypeStruct((m, ncols), BF16),
                 pl.BlockSpec((tm, tn), lambda i, j: (i, j)), name)


def _proj_rope(h, w, col0, ncols, scale, cos_t, sin_t, seq, name, tm, tn=512):
    m = h.shape[0]
    per_batch = seq // tm
    tab_spec = pl.BlockSpec((tm, LANES), lambda i, j: (i % per_batch, 0))
    return _proj(h, w, col0, ncols, tm, tn, functools.partial(_epi_rope, scale),
                 (cos_t, sin_t), (tab_spec, tab_spec),
                 jax.ShapeDtypeStruct((m, ncols), BF16),
                 pl.BlockSpec((tm, tn), lambda i, j: (i, j)), name)


def _proj_silu_slabs(h, w, col0, name, tm, tn=512):
    m = h.shape[0]
    return _proj(h, w, col0, D_BRANCH, tm, tn, _epi_silu_slabs, (), (),
                 jax.ShapeDtypeStruct((SLABS, m, LANES), BF16),
                 pl.BlockSpec((tn // LANES, tm, LANES), lambda i, j: (j, i, 0)), name)


def _proj_channel_dft(h, w, cs, name, tm, seq=None, tn=512):
    m = h.shape[0]
    lhs_spec = None
    if seq is not None:
        na, tiles = seq // DFT_NB, DFT_NB // DFT_ROWS
        assert tm == na * DFT_ROWS
        h = h.reshape(m // seq, na, DFT_NB, D_MODEL)
        lhs_spec = pl.BlockSpec((None, na, DFT_ROWS, D_MODEL), lambda i, j: (i // tiles, 0, i % tiles, 0))
    slab_shape = jax.ShapeDtypeStruct((SLABS, m, LANES), F32)
    slab_spec = pl.BlockSpec((tn // LANES, tm, LANES), lambda i, j: (j, i, 0))
    groups = tn // FNET_GROUP_DIM
    cs_spec = pl.BlockSpec((groups,) + cs.shape[1:], lambda i, j: (j, 0, 0))
    return _proj(h, w, 0, D_BRANCH, tm, tn, _epi_channel_dft, (cs,), (cs_spec,),
                 (slab_shape, slab_shape), (slab_spec, slab_spec), name, lhs_spec=lhs_spec)


def _fold_mix_kernel(c_ref, s_ref, wm_ref, o_ref):
    wm = wm_ref[...]
    gd = FNET_GROUP_DIM
    hp = lax.Precision.HIGHEST
    o_ref[:, 0:gd] = jnp.dot(c_ref[...], wm, preferred_element_type=F32, precision=hp).astype(BF16)
    o_ref[:, gd:2 * gd] = jnp.dot(s_ref[...], wm, preferred_element_type=F32, precision=hp).astype(BF16)


def _fold_mix(w_mix):
    gd = FNET_GROUP_DIM
    cs = _channel_dft_matrix()
    mat = pl.BlockSpec((gd, gd), lambda g: (0, 0))
    return pl.pallas_call(
        _fold_mix_kernel,
        grid=(FNET_GROUPS,),
        in_specs=[mat, mat, pl.BlockSpec((None, gd, gd), lambda g: (g, 0, 0))],
        out_specs=pl.BlockSpec((None, gd, 2 * gd), lambda g: (g, 0, 0)),
        out_shape=jax.ShapeDtypeStruct((FNET_GROUPS, gd, 2 * gd), BF16),
        compiler_params=_cparams(1),
        name="fnet_fold_mix",
    )(jnp.asarray(cs[:, :gd]), jnp.asarray(cs[:, gd:]), w_mix)


def _proj_gelu_layernorm(h, w, col0, ln_g, ln_b, name, tm):
    m = h.shape[0]
    vec_spec = pl.BlockSpec((1, D_BRANCH), lambda i, j: (0, 0))
    return _proj(h, w, col0, D_BRANCH, tm, D_BRANCH, _epi_gelu_layernorm,
                 (ln_g.reshape(1, D_BRANCH), ln_b.reshape(1, D_BRANCH)), (vec_spec, vec_spec),
                 jax.ShapeDtypeStruct((m, D_BRANCH), BF16),
                 pl.BlockSpec((tm, D_BRANCH), lambda i, j: (i, j)), name, w_single_buffer=True)


def _dft_kernel(na, pitch, ar_ref, ai_ref, zg_ref, fa_ref, fb_ref, y_ref, er_ref, ei_ref, g_ref):
    tile_rows = na * DFT_ROWS

    def stage_a(b, carry):
        start = (b // DFT_ROWS) * tile_rows + b % DFT_ROWS
        zr = ar_ref[pl.ds(start, na, stride=DFT_ROWS), :]
        zi = ai_ref[pl.ds(start, na, stride=DFT_ROWS), :]
        d = jnp.concatenate([zr, zi], axis=0).astype(BF16)
        e = jnp.dot(fa_ref[b], d, preferred_element_type=F32)
        off = pl.multiple_of(b * pitch, 8)
        er_ref[pl.ds(off, na), :] = e[:na]
        ei_ref[pl.ds(off, na), :] = e[na:]
        return carry

    lax.fori_loop(0, DFT_NB, stage_a, 0, unroll=8)

    def stage_b(ka, carry):
        er = er_ref[pl.ds(ka, DFT_NB, stride=pitch), :]
        ei = ei_ref[pl.ds(ka, DFT_NB, stride=pitch), :]
        d = jnp.concatenate([er, ei], axis=0).astype(BF16)
        g_ref[pl.ds(ka, DFT_NB, stride=na), :] = jnp.dot(fb_ref[...], d, preferred_element_type=F32)
        return carry

    lax.fori_loop(0, na, stage_b, 0, unroll=4)
    y_ref[...] = (g_ref[...] * zg_ref[...].astype(F32)).astype(BF16)


def _position_dft(ar, ai, zg, batch, seq):
    na = seq // DFT_NB
    pitch = na + 8
    fa_np, fb_np = _position_dft_matrices(seq)
    fa = jnp.asarray(fa_np).astype(BF16)
    fb = jnp.asarray(fb_np).astype(BF16)
    slab = pl.BlockSpec((None, seq, LANES), lambda s, b: (s, b, 0))
    return pl.pallas_call(
        functools.partial(_dft_kernel, na, pitch),
        grid=(SLABS, batch),
        in_specs=[slab, slab, slab,
                  pl.BlockSpec(fa.shape, lambda s, b: (0, 0, 0)),
                  pl.BlockSpec(fb.shape, lambda s, b: (0, 0))],
        out_specs=slab,
        out_shape=jax.ShapeDtypeStruct(ar.shape, BF16),
        scratch_shapes=[pltpu.VMEM((DFT_NB * pitch, LANES), F32),
                        pltpu.VMEM((DFT_NB * pitch, LANES), F32),
                        pltpu.VMEM((seq, LANES), F32)],
        compiler_params=_cparams(2),
        name="position_dft",
    )(ar, ai, zg, fa, fb)


def _dft_dense_kernel(ar_ref, ai_ref, zg_ref, fd_ref, y_ref):
    d = jnp.concatenate([ar_ref[...], ai_ref[...]], axis=0).astype(BF16)
    g = jnp.dot(fd_ref[...], d, preferred_element_type=F32)
    y_ref[...] = (g * zg_ref[...].astype(F32)).astype(BF16)


def _position_dft_dense(ar, ai, zg, batch, seq):
    fd = jnp.asarray(_dense_dft_matrix(seq)).astype(BF16)
    slab = pl.BlockSpec((None, seq, LANES), lambda s, b: (s, b, 0))
    return pl.pallas_call(
        _dft_dense_kernel,
        grid=(SLABS, batch),
        in_specs=[slab, slab, slab, pl.BlockSpec(fd.shape, lambda s, b: (0, 0))],
        out_specs=slab,
        out_shape=jax.ShapeDtypeStruct(ar.shape, BF16),
        compiler_params=_cparams(2),
        name="position_dft_dense",
    )(ar, ai, zg, fd)


def _attn_kernel(n_band, nblk, sink_ref, q_ref, *refs):
    k_refs = refs[:n_band + 1]
    v_refs = refs[n_band + 1:2 * n_band + 2]
    zg_ref, y_ref = refs[2 * n_band + 2:]
    h = pl.program_id(1)
    i = pl.program_id(2)
    blk = ATTN_BLOCK
    n_pair = Q_GROUP // 2

    q = q_ref[...]
    q4 = jnp.concatenate([q[:, t * LANES:(t + 1) * LANES] for t in range(n_pair)], axis=0)
    lane = lax.broadcasted_iota(jnp.int32, (1, LANES), 1)
    lo = lane < HEAD_DIM
    zero = jnp.zeros_like(q4)
    keys = jnp.concatenate([r[...] for r in k_refs], axis=0)
    vals = jnp.concatenate([r[...] for r in v_refs], axis=0)
    nk = keys.shape[0]
    dn = (((1,), (1,)), ((), ()))
    s_lo = lax.dot_general(jnp.where(lo, q4, zero), keys, dn, preferred_element_type=F32)
    s_hi = lax.dot_general(jnp.where(lo, zero, q4), keys, dn, preferred_element_type=F32)

    if n_band:
        r = lax.broadcasted_iota(jnp.int32, (n_pair * blk, nk), 0) % blk
        c = lax.broadcasted_iota(jnp.int32, (n_pair * blk, nk), 1)
        prev_ok = c >= r + jnp.where(i > 0, 0, blk)
        next_ok = (c - 2 * blk) <= r - jnp.where(i < nblk - 1, 0, blk)
        ok = ((c >= blk) | prev_ok) & ((c < 2 * blk) | (c >= 3 * blk) | next_ok)
        s_lo = jnp.where(ok, s_lo, NEG_INF)
        s_hi = jnp.where(ok, s_hi, NEG_INF)

    def softmax_parts(s, hi):
        sk = jnp.concatenate([jnp.full((blk, 1), sink_ref[h, 2 * t + hi], F32) for t in range(n_pair)], axis=0)
        mx = jnp.maximum(jnp.max(s, axis=-1, keepdims=True), sk)
        p = jnp.exp(s - mx)
        denom = jnp.sum(p, axis=-1, keepdims=True) + jnp.exp(sk - mx)
        return p.astype(BF16), 1.0 / denom

    p_lo, r_lo = softmax_parts(s_lo, 0)
    p_hi, r_hi = softmax_parts(s_hi, 1)
    p = jnp.concatenate([p_lo, p_hi], axis=1)
    vzero = jnp.zeros_like(vals)
    v2 = jnp.concatenate([jnp.where(lo, vals, vzero), jnp.where(lo, vzero, vals)], axis=0)
    o = jnp.dot(p, v2, preferred_element_type=F32)
    o = o * jnp.where(lo, r_lo, r_hi)
    for t in range(n_pair):
        sl = slice(t * LANES, (t + 1) * LANES)
        y_ref[:, sl] = (o[t * blk:(t + 1) * blk] * zg_ref[:, sl].astype(F32)).astype(BF16)


def _attention(q, kd, vd, kcd, vcd, zg, sink, batch, seq, ctx_len, use_band):
    blk = ATTN_BLOCK
    nblk = seq // blk
    qw = Q_GROUP * HEAD_DIM
    q_spec = pl.BlockSpec((blk, qw), lambda b, h, i: (b * nblk + i, h))
    ctx_spec = pl.BlockSpec((ctx_len, LANES), lambda b, h, i: (b, h))
    if use_band:
        def band(delta):
            return pl.BlockSpec(
                (blk, LANES), lambda b, h, i: (b * nblk + jnp.clip(i + delta, 0, nblk - 1), h))
        k_specs = [band(-1), band(0), band(1), ctx_spec]
        k_args, v_args = [kd, kd, kd, kcd], [vd, vd, vd, vcd]
        n_band = 3
    else:
        k_specs, k_args, v_args, n_band = [ctx_spec], [kcd], [vcd], 0
    return pl.pallas_call(
        functools.partial(_attn_kernel, n_band, nblk),
        grid=(batch, KV_HEADS, nblk),
        in_specs=[pl.BlockSpec(memory_space=pltpu.SMEM), q_spec] + k_specs + k_specs + [q_spec],
        out_specs=q_spec,
        out_shape=jax.ShapeDtypeStruct(q.shape, BF16),
        compiler_params=_cparams(3),
        name="attention_band" if use_band else "attention_ctx",
    )(sink, q, *k_args, *v_args, zg)


def _sgu_kernel(n_chunks, gu_ref, vn_ref, zg_ref, ws_ref, bs_ref, y_ref):
    ws = ws_ref[...]
    bs = bs_ref[...]
    for c in range(n_chunks):
        sl = slice(c * GMLP_CHUNK, (c + 1) * GMLP_CHUNK)
        s = jnp.dot(ws, vn_ref[sl, :], preferred_element_type=F32) + bs
        y_ref[sl, :] = (gu_ref[sl, :].astype(F32) * s * zg_ref[sl, :].astype(F32)).astype(BF16)


def _spatial_gate(gu, vn, zg, w_s, b_s, tm):
    m = gu.shape[0]
    gd = D_BRANCH // GMLP_GROUPS
    tile = pl.BlockSpec((tm, gd), lambda r, g: (r, g))
    return pl.pallas_call(
        functools.partial(_sgu_kernel, tm // GMLP_CHUNK),
        grid=(m // tm, GMLP_GROUPS),
        in_specs=[tile, tile, tile,
                  pl.BlockSpec((None, GMLP_CHUNK, GMLP_CHUNK), lambda r, g: (g, 0, 0)),
                  pl.BlockSpec((None, GMLP_CHUNK, 1), lambda r, g: (g, 0, 0))],
        out_specs=tile,
        out_shape=jax.ShapeDtypeStruct(gu.shape, BF16),
        compiler_params=_cparams(2),
        name="gmlp_spatial_gate",
    )(gu, vn, zg, w_s, b_s.reshape(GMLP_GROUPS, GMLP_CHUNK, 1))


def _wout_kernel(mode, y_ref, w_ref, x_ref, gate_ref, *refs):
    if len(y_ref.shape) == 3:
        y = jnp.concatenate([y_ref[s] for s in range(y_ref.shape[0])], axis=1)
    else:
        y = y_ref[...]
    acc = jnp.dot(y, w_ref[...], preferred_element_type=F32)
    xn = x_ref[...] + gate_ref[...] * acc
    if mode == "final":
        g_ref, o_ref = refs
        o_ref[...] = xn * lax.rsqrt(jnp.mean(xn * xn, axis=-1, keepdims=True) + EPS) * g_ref[...]
    elif mode == "next":
        g_ref, sc_ref, sh_ref, xo_ref, h_ref = refs
        xo_ref[...] = xn
        h_ref[...] = _mod_rmsnorm(xn, g_ref[...], sc_ref[...], sh_ref[...]).astype(BF16)
    else:
        refs[0][...] = xn


def _out_proj(y, w_out, x2, mods, layer, who, mode, norm_vec, tm):
    m = x2.shape[0]
    row = pl.BlockSpec((tm, D_MODEL), lambda r: (r, 0))
    if y.ndim == 3:
        y_spec = pl.BlockSpec((SLABS, tm, LANES), lambda r: (0, r, 0))
    else:
        y_spec = pl.BlockSpec((tm, D_BRANCH), lambda r: (r, 0))
    in_specs = [y_spec,
                pl.BlockSpec((None, D_BRANCH, D_MODEL), lambda r: (layer, 0, 0),
                             pipeline_mode=pl.Buffered(1)),
                row, _mod_spec(layer, 2, who)]
    args = [y, w_out, x2, mods]
    xs = jax.ShapeDtypeStruct((m, D_MODEL), F32)
    if mode == "final":
        in_specs.append(pl.BlockSpec((1, D_MODEL), lambda r: (0, 0)))
        args.append(norm_vec.reshape(1, D_MODEL))
        out_shape, out_specs = xs, row
    elif mode == "next":
        in_specs += [_row_spec(layer + 1), _mod_spec(layer + 1, 1, who), _mod_spec(layer + 1, 0, who)]
        args += [norm_vec, mods, mods]
        out_shape = (xs, jax.ShapeDtypeStruct((m, D_MODEL), BF16))
        out_specs = (row, row)
    else:
        out_shape, out_specs = xs, row
    return pl.pallas_call(
        functools.partial(_wout_kernel, mode),
        grid=(m // tm,),
        in_specs=in_specs,
        out_specs=out_specs,
        out_shape=out_shape,
        compiler_params=_cparams(1),
        name="out_proj_" + mode,
    )(*args)


def kernel(x, c, ctx, c_ctx, norm_g, ada_w, ada_b, w_out, fnet_w_in, fnet_w_mix, attn_w_in, attn_sink,
           gmlp_w_in, gmlp_w_s, gmlp_b_s, gmlp_ln_g, gmlp_ln_b, final_g):
    batch, seq, d = x.shape
    ctx_len = ctx.shape[1]
    assert d == D_MODEL and seq % (DFT_NB * 8) == 0 and seq % GRID_W == 0 and batch < 8
    m_lat, m_ctx = batch * seq, batch * ctx_len
    tm_lat = min(1024, seq)
    tm_ctx = ctx_len
    tm_out = 256

    def who_lat(tm):
        return lambda r: (r * tm) // seq

    who_ctx = lambda r: batch

    cvec = jnp.zeros((8, d), F32).at[:batch].set(c).at[batch].set(c_ctx)
    mods = _mods(cvec, ada_w, ada_b).reshape(DEPTH, 8, 3, 1, d)
    norm_g3 = norm_g.reshape(DEPTH, 1, d)
    w_out_b = w_out.astype(BF16)

    xl = x.reshape(m_lat, d)
    xc = ctx.reshape(m_ctx, d)
    hl = _prenorm(xl, norm_g3, mods, 0, who_lat(512), 512)
    hc = _prenorm(xc, norm_g3, mods, 0, who_ctx, tm_ctx)

    cos_np, sin_np = _rope_tables(seq)
    cos_t, sin_t = jnp.asarray(cos_np), jnp.asarray(sin_np)

    def finish(y, x2, layer, who_fn, need_next, is_final):
        if is_final:
            return _out_proj(y, w_out_b, x2, mods, layer, who_fn, "final", final_g, tm_out), None
        if need_next:
            return _out_proj(y, w_out_b, x2, mods, layer, who_fn, "next", norm_g3, tm_out)
        return _out_proj(y, w_out_b, x2, mods, layer, who_fn, "plain", None, tm_out), None

    out = None
    for i in range(DEPTH):
        kind, j = i % 3, i // 3
        need_ctx = i < DEPTH - 1
        last = i == DEPTH - 1
        if kind == 0:
            w = fnet_w_in[j].astype(BF16)
            cs = _fold_mix(fnet_w_mix[j])
            tm_dft = (seq // DFT_NB) * DFT_ROWS
            ar, ai = _proj_channel_dft(hl, w, cs, "fnet_in_u", tm_dft, seq=seq)
            zg = _proj_silu_slabs(hl, w, D_BRANCH, "fnet_in_z", tm_lat)
            y = _position_dft(ar, ai, zg, batch, seq)
            yc = None
            if need_ctx:
                ar, ai = _proj_channel_dft(hc, w, cs, "fnet_in_u_ctx", tm_ctx)
                zg = _proj_silu_slabs(hc, w, D_BRANCH, "fnet_in_z_ctx", tm_ctx)
                yc = _position_dft_dense(ar, ai, zg, batch, ctx_len)
        elif kind == 1:
            w = attn_w_in[j]
            kvw = KV_HEADS * HEAD_DIM
            wq = w[:, :D_BRANCH].astype(BF16)
            wz = w[:, D_BRANCH + 2 * kvw:].astype(BF16)

            def dup(wpart):
                w3 = wpart.reshape(d, KV_HEADS, 1, HEAD_DIM)
                return jnp.broadcast_to(w3, (d, KV_HEADS, 2, HEAD_DIM)).reshape(d, 2 * kvw).astype(BF16)

            wk = dup(w[:, D_BRANCH:D_BRANCH + kvw])
            wv = dup(w[:, D_BRANCH + kvw:D_BRANCH + 2 * kvw])
            sink = attn_sink[j].reshape(KV_HEADS, Q_GROUP)
            scale = HEAD_DIM ** -0.5
            q = _proj_rope(hl, wq, 0, D_BRANCH, scale, cos_t, sin_t, seq, "attn_in_q", tm_lat)
            kd = _proj_rope(hl, wk, 0, 2 * kvw, 1.0, cos_t, sin_t, seq, "attn_in_k", tm_lat)
            vd = _proj_simple(hl, wv, 0, 2 * kvw, _epi_cast, "attn_in_v", tm_lat)
            zg = _proj_simple(hl, wz, 0, D_BRANCH, _epi_silu, "attn_in_z", tm_lat)
            kcd = _proj_simple(hc, wk, 0, 2 * kvw, _epi_cast, "attn_in_kc", tm_ctx)
            vcd = _proj_simple(hc, wv, 0, 2 * kvw, _epi_cast, "attn_in_vc", tm_ctx)
            y = _attention(q, kd, vd, kcd, vcd, zg, sink, batch, seq, ctx_len, True)
            yc = None
            if need_ctx:
                qc = _proj_simple(hc, wq * jnp.asarray(scale, BF16), 0, D_BRANCH, _epi_cast, "attn_in_qc", tm_ctx)
                zgc = _proj_simple(hc, wz, 0, D_BRANCH, _epi_silu, "attn_in_zc", tm_ctx)
                yc = _attention(qc, None, None, kcd, vcd, zgc, sink, batch, ctx_len, ctx_len, False)
        else:
            w = gmlp_w_in[j].astype(BF16)
            ws = gmlp_w_s[j].astype(BF16)

            def gmlp_branch(h, tm):
                gu = _proj_simple(h, w, 0, D_BRANCH, _epi_gelu, "gmlp_in_u", tm)
                vn = _proj_gelu_layernorm(h, w, D_BRANCH, gmlp_ln_g[j], gmlp_ln_b[j], "gmlp_in_v", 256)
                zg = _proj_simple(h, w, 2 * D_BRANCH, D_BRANCH, _epi_silu, "gmlp_in_z", tm)
                return _spatial_gate(gu, vn, zg, ws, gmlp_b_s[j], min(tm, 512))

            y = gmlp_branch(hl, tm_lat)
            yc = gmlp_branch(hc, tm_ctx) if need_ctx else None

        res, hl = finish(y, xl, i, who_lat(tm_out), not last, last)
        if last:
            out = res
        else:
            xl = res
        if need_ctx:
            xc, hc = finish(yc, xc, i, who_ctx, i + 1 < DEPTH - 1 or (i + 1) % 3 == 1, False)
    return out.reshape(batch, seq, d)
```

```python
import functools
import math

import numpy as np
import jax
import jax.numpy as jnp
from jax import lax
from jax.experimental import pallas as pl
from jax.experimental.pallas import tpu as pltpu

F32 = jnp.float32
BF16 = jnp.bfloat16

D_MODEL = 2048
D_BRANCH = 4096
DEPTH = 4
GRID_W = 64
FNET_GROUPS = 16
FNET_GROUP_DIM = 256
HEAD_DIM = 64
KV_HEADS = 8
Q_GROUP = 8
ATTN_BLOCK = 128
ROPE_BASE = 10000.0
GMLP_CHUNK = 128
GMLP_GROUPS = 16
EPS = 1e-6
NEG_INF = -1e30

LANES = 128
MXU_COLS = 256
PROJ_TN = 1024
DFT_NB = 128
DFT_ROWS = 16
SLABS = D_BRANCH // LANES
VMEM_LIMIT = 56 * 1024 * 1024


def _cparams(n_axes, vmem=VMEM_LIMIT):
    return pltpu.CompilerParams(dimension_semantics=("arbitrary",) * n_axes,
                                vmem_limit_bytes=vmem)


def _silu(z):
    return z / (1.0 + jnp.exp(-z))


def _gelu_tanh(x):
    c = math.sqrt(2.0 / math.pi)
    return 0.5 * x * (1.0 + jnp.tanh(c * (x + 0.044715 * (x * x * x))))


def _mod_rmsnorm(x, g, scale, shift):
    y = x * lax.rsqrt(jnp.mean(x * x, axis=-1, keepdims=True) + EPS) * g
    return y * (1.0 + scale) + shift


def _channel_dft_matrix():
    n = FNET_GROUP_DIM
    k = np.arange(n, dtype=np.float64)
    ang = 2.0 * np.pi * np.outer(k, k) / n
    s = 1.0 / math.sqrt(n)
    return np.concatenate([np.cos(ang) * s, -np.sin(ang) * s], axis=1).astype(np.float32)


def _position_dft_matrices(seq):
    na, nb = seq // DFT_NB, DFT_NB
    a = np.arange(na, dtype=np.float64)
    b = np.arange(nb, dtype=np.float64)
    ang = 2.0 * np.pi * (a[None, None, :] * a[None, :, None] / na + b[:, None, None] * a[None, :, None] / seq)
    mr = np.cos(ang) / math.sqrt(na)
    mi = -np.sin(ang) / math.sqrt(na)
    fa = np.concatenate([np.concatenate([mr, -mi], axis=2), np.concatenate([mi, mr], axis=2)], axis=1)
    angb = 2.0 * np.pi * np.outer(b, b) / nb
    fb = np.concatenate([np.cos(angb), np.sin(angb)], axis=1) / math.sqrt(nb)
    return fa.astype(np.float32), fb.astype(np.float32)


def _dense_dft_matrix(n):
    k = np.arange(n, dtype=np.float64)
    ang = 2.0 * np.pi * np.outer(k, k) / n
    return (np.concatenate([np.cos(ang), np.sin(ang)], axis=1) / math.sqrt(n)).astype(np.float32)


def _rope_tables(seq):
    nf = HEAD_DIM // 4
    inv = ROPE_BASE ** (-np.arange(nf, dtype=np.float64) / nf)
    t = np.arange(seq)
    rows = (t // GRID_W).astype(np.float64)
    cols = (t % GRID_W).astype(np.float64)
    parts_c, parts_s = [], []
    for pos in (rows, cols):
        ang = pos[:, None] * inv[None, :]
        parts_c += [np.cos(ang), np.cos(ang)]
        parts_s += [-np.sin(ang), np.sin(ang)]
    cos = np.concatenate(parts_c, axis=1)
    sin = np.concatenate(parts_s, axis=1)
    reps = LANES // HEAD_DIM
    return (np.tile(cos, (1, reps)).astype(np.float32), np.tile(sin, (1, reps)).astype(np.float32))


def _mods_kernel(cv_ref, w_ref, b_ref, o_ref):
    a = _silu(cv_ref[...])
    o_ref[...] = jnp.dot(a, w_ref[...], preferred_element_type=F32,
                         precision=lax.Precision.HIGHEST) + b_ref[...]


def _mods(cvec, ada_w, ada_b):
    depth, d, n3 = ada_w.shape
    tn = 1024
    return pl.pallas_call(
        _mods_kernel,
        grid=(depth, n3 // tn),
        in_specs=[pl.BlockSpec((8, d), lambda i, j: (0, 0)),
                  pl.BlockSpec((None, d, tn), lambda i, j: (i, 0, j)),
                  pl.BlockSpec((None, 1, tn), lambda i, j: (i, 0, j))],
        out_specs=pl.BlockSpec((None, 8, tn), lambda i, j: (i, 0, j)),
        out_shape=jax.ShapeDtypeStruct((depth, 8, n3), F32),
        compiler_params=_cparams(2),
        name="ada_mods",
    )(cvec, ada_w, ada_b.reshape(depth, 1, n3))


def _mod_spec(layer, kind, who_of_row):
    return pl.BlockSpec((None, None, None, 1, D_MODEL),
                        lambda r, *_: (layer, who_of_row(r), kind, 0, 0))


def _row_spec(vec_layer):
    return pl.BlockSpec((None, 1, D_MODEL), lambda r, *_: (vec_layer, 0, 0))


def _prenorm_kernel(x_ref, g_ref, sc_ref, sh_ref, h_ref):
    h_ref[...] = _mod_rmsnorm(x_ref[...], g_ref[...], sc_ref[...], sh_ref[...]).astype(BF16)


def _prenorm(x2, norm_g3, mods, layer, who, tm):
    m = x2.shape[0]
    return pl.pallas_call(
        _prenorm_kernel,
        grid=(m // tm,),
        in_specs=[pl.BlockSpec((tm, D_MODEL), lambda r: (r, 0)),
                  _row_spec(layer), _mod_spec(layer, 1, who), _mod_spec(layer, 0, who)],
        out_specs=pl.BlockSpec((tm, D_MODEL), lambda r: (r, 0)),
        out_shape=jax.ShapeDtypeStruct((m, D_MODEL), BF16),
        compiler_params=_cparams(1),
        name="prenorm",
    )(x2, norm_g3, mods, mods)


def _proj_kernel(epilogue, n_extra, chunk, a_ref, w_ref, *rest):
    if len(a_ref.shape) == 3:
        a_flat = rest[-1]
        rest = rest[:-1]

        @pl.when(pl.program_id(1) == 0)
        def _():
            a_flat[...] = a_ref[...].reshape(a_flat.shape)

        a = a_flat[...]
    else:
        a = a_ref[...]
    for c0 in range(0, w_ref.shape[1], chunk):
        acc = jnp.dot(a, w_ref[:, c0:c0 + chunk], preferred_element_type=F32)
        epilogue(acc, c0, rest[:n_extra], rest[n_extra:])


def _proj(h, w, col0, ncols, tm, tn, epilogue, extras, extra_specs, out_shapes, out_specs, name,
          w_single_buffer=False, lhs_spec=None, chunk=MXU_COLS):
    k = h.shape[-1]
    m = h.size // k
    off = col0 // tn
    w_kwargs = dict(pipeline_mode=pl.Buffered(1)) if w_single_buffer else {}
    scratch = [pltpu.VMEM((tm, k), h.dtype)] if lhs_spec is not None else []
    if lhs_spec is None:
        lhs_spec = pl.BlockSpec((tm, k), lambda i, j: (i, 0))
    return pl.pallas_call(
        functools.partial(_proj_kernel, epilogue, len(extras), min(chunk, tn)),
        grid=(m // tm, ncols // tn),
        in_specs=[lhs_spec,
                  pl.BlockSpec((k, tn), lambda i, j: (0, j + off), **w_kwargs)] + list(extra_specs),
        out_specs=out_specs,
        out_shape=out_shapes,
        scratch_shapes=scratch,
        compiler_params=_cparams(2),
        name=name,
    )(h, w, *extras)


def _epi_silu(acc, c0, extras, outs):
    outs[0][:, c0:c0 + acc.shape[1]] = _silu(acc).astype(BF16)


def _epi_silu_slabs(acc, c0, extras, outs):
    for t in range(acc.shape[1] // LANES):
        outs[0][c0 // LANES + t] = _silu(acc[:, t * LANES:(t + 1) * LANES]).astype(BF16)


def _epi_gelu(acc, c0, extras, outs):
    outs[0][:, c0:c0 + acc.shape[1]] = _gelu_tanh(acc).astype(BF16)


def _epi_cast(acc, c0, extras, outs):
    outs[0][:, c0:c0 + acc.shape[1]] = acc.astype(BF16)


def _epi_gelu_layernorm(acc, c0, extras, outs):
    g_ref, b_ref = extras
    ge = _gelu_tanh(acc)
    mu = jnp.mean(ge, axis=-1, keepdims=True)
    d = ge - mu
    var = jnp.mean(d * d, axis=-1, keepdims=True)
    outs[0][...] = (d * lax.rsqrt(var + EPS) * g_ref[...] + b_ref[...]).astype(BF16)


def _epi_channel_dft(split_rows, acc, c0, extras, outs):
    cs_ref = extras[0]
    ar_ref, ai_ref = outs
    tm = acc.shape[0]
    for gl in range(acc.shape[1] // FNET_GROUP_DIM):
        g = c0 // FNET_GROUP_DIM + gl
        ub = acc[:, gl * FNET_GROUP_DIM:(gl + 1) * FNET_GROUP_DIM].astype(BF16)
        ab = jnp.dot(ub, cs_ref[g], preferred_element_type=F32)
        if split_rows:
            ab4 = ab.reshape(tm // DFT_ROWS, 2, DFT_ROWS // 2, ab.shape[1])
            ab = jnp.concatenate([ab4[:, hf].reshape(tm // 2, ab.shape[1]) for hf in range(2)], axis=0)
        ar_ref[2 * g] = ab[:, 0:128]
        ar_ref[2 * g + 1] = ab[:, 128:256]
        ai_ref[2 * g] = ab[:, 256:384]
        ai_ref[2 * g + 1] = ab[:, 384:512]


def _epi_rope(scale, acc, c0, extras, outs):
    cos = extras[0][...]
    sin = extras[1][...]
    lane = lax.broadcasted_iota(jnp.int32, (1, LANES), 1)
    first = (lane % 32) < 16
    for t in range(acc.shape[1] // LANES):
        x = acc[:, t * LANES:(t + 1) * LANES]
        partner = jnp.where(first, pltpu.roll(x, LANES - 16, 1), pltpu.roll(x, 16, 1))
        y = x * cos + partner * sin
        if scale != 1.0:
            y = y * scale
        outs[0][:, c0 + t * LANES:c0 + (t + 1) * LANES] = y.astype(BF16)


def _proj_simple(h, w, col0, ncols, epilogue, name, tm, tn=PROJ_TN):
    m = h.shape[0]
    return _proj(h, w, col0, ncols, tm, tn, epilogue, (), (),
                 jax.ShapeDtypeStruct((m, ncols), BF16),
                 pl.BlockSpec((tm, tn), lambda i, j: (i, j)), name)


def _proj_rope(h, w, col0, ncols, scale, cos_t, sin_t, seq, name, tm, tn=PROJ_TN):
    m = h.shape[0]
    per_batch = seq // tm
    tab_spec = pl.BlockSpec((tm, LANES), lambda i, j: (i % per_batch, 0))
    return _proj(h, w, col0, ncols, tm, tn, functools.partial(_epi_rope, scale),
                 (cos_t, sin_t), (tab_spec, tab_spec),
                 jax.ShapeDtypeStruct((m, ncols), BF16),
                 pl.BlockSpec((tm, tn), lambda i, j: (i, j)), name)


def _proj_silu_slabs(h, w, col0, name, tm, tn=PROJ_TN):
    m = h.shape[0]
    return _proj(h, w, col0, D_BRANCH, tm, tn, _epi_silu_slabs, (), (),
                 jax.ShapeDtypeStruct((SLABS, m, LANES), BF16),
                 pl.BlockSpec((tn // LANES, tm, LANES), lambda i, j: (j, i, 0)), name)


def _proj_channel_dft(h, w, cs, name, tm, seq=None, tn=PROJ_TN):
    m = h.shape[0]
    lhs_spec = None
    if seq is not None:
        na, tiles = seq // DFT_NB, DFT_NB // DFT_ROWS
        assert tm == na * DFT_ROWS
        h = h.reshape(m // seq, na, DFT_NB, D_MODEL)
        lhs_spec = pl.BlockSpec((None, na, DFT_ROWS, D_MODEL), lambda i, j: (i // tiles, 0, i % tiles, 0))
    slab_shape = jax.ShapeDtypeStruct((SLABS, m, LANES), F32)
    slab_spec = pl.BlockSpec((tn // LANES, tm, LANES), lambda i, j: (j, i, 0))
    groups = tn // FNET_GROUP_DIM
    cs_spec = pl.BlockSpec((groups,) + cs.shape[1:], lambda i, j: (j, 0, 0))
    return _proj(h, w, 0, D_BRANCH, tm, tn, functools.partial(_epi_channel_dft, seq is not None), (cs,), (cs_spec,),
                 (slab_shape, slab_shape), (slab_spec, slab_spec), name, lhs_spec=lhs_spec,
                 chunk=2 * MXU_COLS)


def _fold_mix_kernel(c_ref, s_ref, wm_ref, o_ref):
    wm = wm_ref[...]
    gd = FNET_GROUP_DIM
    hp = lax.Precision.HIGHEST
    o_ref[:, 0:gd] = jnp.dot(c_ref[...], wm, preferred_element_type=F32, precision=hp).astype(BF16)
    o_ref[:, gd:2 * gd] = jnp.dot(s_ref[...], wm, preferred_element_type=F32, precision=hp).astype(BF16)


def _fold_mix(w_mix):
    gd = FNET_GROUP_DIM
    cs = _channel_dft_matrix()
    mat = pl.BlockSpec((gd, gd), lambda g: (0, 0))
    return pl.pallas_call(
        _fold_mix_kernel,
        grid=(FNET_GROUPS,),
        in_specs=[mat, mat, pl.BlockSpec((None, gd, gd), lambda g: (g, 0, 0))],
        out_specs=pl.BlockSpec((None, gd, 2 * gd), lambda g: (g, 0, 0)),
        out_shape=jax.ShapeDtypeStruct((FNET_GROUPS, gd, 2 * gd), BF16),
        compiler_params=_cparams(1),
        name="fnet_fold_mix",
    )(jnp.asarray(cs[:, :gd]), jnp.asarray(cs[:, gd:]), w_mix)


def _proj_gelu_layernorm(h, w, col0, ln_g, ln_b, name, tm):
    m = h.shape[0]
    vec_spec = pl.BlockSpec((1, D_BRANCH), lambda i, j: (0, 0))
    return _proj(h, w, col0, D_BRANCH, tm, D_BRANCH, _epi_gelu_layernorm,
                 (ln_g.reshape(1, D_BRANCH), ln_b.reshape(1, D_BRANCH)), (vec_spec, vec_spec),
                 jax.ShapeDtypeStruct((m, D_BRANCH), BF16),
                 pl.BlockSpec((tm, D_BRANCH), lambda i, j: (i, j)), name, w_single_buffer=True,
                 chunk=D_BRANCH)


def _dft_kernel(na, pitch, gpitch, ar_ref, ai_ref, zg_ref, fa_ref, fb_ref, y_ref, er_ref, ei_ref, g_ref):
    half_rows = na * (DFT_ROWS // 2)

    def stage_a(b, carry):
        start = (b // (DFT_ROWS // 2)) * half_rows + b % (DFT_ROWS // 2)
        zr = ar_ref[pl.ds(start, na, stride=DFT_ROWS // 2), :]
        zi = ai_ref[pl.ds(start, na, stride=DFT_ROWS // 2), :]
        d = jnp.concatenate([zr, zi], axis=0).astype(BF16)
        e = jnp.dot(fa_ref[b], d, preferred_element_type=F32)
        off = pl.multiple_of(b * pitch, 8)
        er_ref[pl.ds(off, na), :] = e[:na]
        ei_ref[pl.ds(off, na), :] = e[na:]
        return carry

    lax.fori_loop(0, DFT_NB, stage_a, 0, unroll=8)

    def stage_b(ka, carry):
        er = er_ref[pl.ds(ka, DFT_NB, stride=pitch), :]
        ei = ei_ref[pl.ds(ka, DFT_NB, stride=pitch), :]
        d = jnp.concatenate([er, ei], axis=0).astype(BF16)
        off = pl.multiple_of(ka * gpitch, 8)
        g_ref[pl.ds(off, DFT_NB), :] = jnp.dot(fb_ref[...], d, preferred_element_type=F32)
        return carry

    lax.fori_loop(0, na, stage_b, 0, unroll=4)

    def gate(kb, carry):
        rows = pl.ds(pl.multiple_of(kb * na, na), na)
        g = g_ref[pl.ds(kb, na, stride=gpitch), :]
        y_ref[rows, :] = (g * zg_ref[rows, :].astype(F32)).astype(BF16)
        return carry

    lax.fori_loop(0, DFT_NB, gate, 0, unroll=8)


def _position_dft(ar, ai, zg, batch, seq):
    na = seq // DFT_NB
    pitch = na + 8
    gpitch = DFT_NB + 8
    fa_np, fb_np = _position_dft_matrices(seq)
    fa = jnp.asarray(fa_np).astype(BF16)
    fb = jnp.asarray(fb_np).astype(BF16)
    slab = pl.BlockSpec((None, seq, LANES), lambda s, b: (s, b, 0))
    return pl.pallas_call(
        functools.partial(_dft_kernel, na, pitch, gpitch),
        grid=(SLABS, batch),
        in_specs=[slab, slab, slab,
                  pl.BlockSpec(fa.shape, lambda s, b: (0, 0, 0)),
                  pl.BlockSpec(fb.shape, lambda s, b: (0, 0))],
        out_specs=slab,
        out_shape=jax.ShapeDtypeStruct(ar.shape, BF16),
        scratch_shapes=[pltpu.VMEM((DFT_NB * pitch, LANES), F32),
                        pltpu.VMEM((DFT_NB * pitch, LANES), F32),
                        pltpu.VMEM((na * gpitch, LANES), F32)],
        compiler_params=_cparams(2),
        name="position_dft",
    )(ar, ai, zg, fa, fb)


def _dft_dense_kernel(ar_ref, ai_ref, zg_ref, fd_ref, y_ref):
    d = jnp.concatenate([ar_ref[...], ai_ref[...]], axis=0).astype(BF16)
    g = jnp.dot(fd_ref[...], d, preferred_element_type=F32)
    y_ref[...] = (g * zg_ref[...].astype(F32)).astype(BF16)


def _position_dft_dense(ar, ai, zg, batch, seq):
    fd = jnp.asarray(_dense_dft_matrix(seq)).astype(BF16)
    slab = pl.BlockSpec((None, seq, LANES), lambda s, b: (s, b, 0))
    return pl.pallas_call(
        _dft_dense_kernel,
        grid=(SLABS, batch),
        in_specs=[slab, slab, slab, pl.BlockSpec(fd.shape, lambda s, b: (0, 0))],
        out_specs=slab,
        out_shape=jax.ShapeDtypeStruct(ar.shape, BF16),
        compiler_params=_cparams(2),
        name="position_dft_dense",
    )(ar, ai, zg, fd)


def _attn_kernel(n_band, nblk, sink_ref, q_ref, *refs):
    k_refs = refs[:n_band + 1]
    v_refs = refs[n_band + 1:2 * n_band + 2]
    zg_ref, y_ref = refs[2 * n_band + 2:]
    h = pl.program_id(1)
    i = pl.program_id(2)
    blk = ATTN_BLOCK
    n_pair = Q_GROUP // 2

    q = q_ref[...]
    q4 = jnp.concatenate([q[:, t * LANES:(t + 1) * LANES] for t in range(n_pair)], axis=0)
    lane = lax.broadcasted_iota(jnp.int32, (1, LANES), 1)
    lo = lane < HEAD_DIM
    zero = jnp.zeros_like(q4)
    keys = jnp.concatenate([r[...] for r in k_refs], axis=0)
    vals = jnp.concatenate([r[...] for r in v_refs], axis=0)
    nk = keys.shape[0]
    dn = (((1,), (1,)), ((), ()))
    s_lo = lax.dot_general(jnp.where(lo, q4, zero), keys, dn, preferred_element_type=F32)
    s_hi = lax.dot_general(jnp.where(lo, zero, q4), keys, dn, preferred_element_type=F32)

    if n_band:
        r = lax.broadcasted_iota(jnp.int32, (n_pair * blk, nk), 0) % blk
        c = lax.broadcasted_iota(jnp.int32, (n_pair * blk, nk), 1)
        prev_ok = c >= r + jnp.where(i > 0, 0, blk)
        next_ok = (c - 2 * blk) <= r - jnp.where(i < nblk - 1, 0, blk)
        ok = ((c >= blk) | prev_ok) & ((c < 2 * blk) | (c >= 3 * blk) | next_ok)
        s_lo = jnp.where(ok, s_lo, NEG_INF)
        s_hi = jnp.where(ok, s_hi, NEG_INF)

    def softmax_parts(s, hi):
        sk = jnp.concatenate([jnp.full((blk, 1), sink_ref[h, 2 * t + hi], F32) for t in range(n_pair)], axis=0)
        mx = jnp.maximum(jnp.max(s, axis=-1, keepdims=True), sk)
        p = jnp.exp(s - mx)
        denom = jnp.sum(p, axis=-1, keepdims=True) + jnp.exp(sk - mx)
        return p.astype(BF16), 1.0 / denom

    p_lo, r_lo = softmax_parts(s_lo, 0)
    p_hi, r_hi = softmax_parts(s_hi, 1)
    p = jnp.concatenate([p_lo, p_hi], axis=1)
    vzero = jnp.zeros_like(vals)
    v2 = jnp.concatenate([jnp.where(lo, vals, vzero), jnp.where(lo, vzero, vals)], axis=0)
    o = jnp.dot(p, v2, preferred_element_type=F32)
    o = o * jnp.where(lo, r_lo, r_hi)
    for t in range(n_pair):
        sl = slice(t * LANES, (t + 1) * LANES)
        y_ref[:, sl] = (o[t * blk:(t + 1) * blk] * zg_ref[:, sl].astype(F32)).astype(BF16)


def _attention(q, kd, vd, kcd, vcd, zg, sink, batch, seq, ctx_len, use_band):
    blk = ATTN_BLOCK
    nblk = seq // blk
    qw = Q_GROUP * HEAD_DIM
    q_spec = pl.BlockSpec((blk, qw), lambda b, h, i: (b * nblk + i, h))
    ctx_spec = pl.BlockSpec((ctx_len, LANES), lambda b, h, i: (b, h))
    if use_band:
        def band(delta):
            return pl.BlockSpec(
                (blk, LANES), lambda b, h, i: (b * nblk + jnp.clip(i + delta, 0, nblk - 1), h))
        k_specs = [band(-1), band(0), band(1), ctx_spec]
        k_args, v_args = [kd, kd, kd, kcd], [vd, vd, vd, vcd]
        n_band = 3
    else:
        k_specs, k_args, v_args, n_band = [ctx_spec], [kcd], [vcd], 0
    return pl.pallas_call(
        functools.partial(_attn_kernel, n_band, nblk),
        grid=(batch, KV_HEADS, nblk),
        in_specs=[pl.BlockSpec(memory_space=pltpu.SMEM), q_spec] + k_specs + k_specs + [q_spec],
        out_specs=q_spec,
        out_shape=jax.ShapeDtypeStruct(q.shape, BF16),
        compiler_params=_cparams(3),
        name="attention_band" if use_band else "attention_ctx",
    )(sink, q, *k_args, *v_args, zg)


def _sgu_kernel(n_chunks, gu_ref, vn_ref, zg_ref, ws_ref, bs_ref, y_ref):
    ws = ws_ref[...]
    bs = bs_ref[...]
    for c in range(n_chunks):
        sl = slice(c * GMLP_CHUNK, (c + 1) * GMLP_CHUNK)
        s = jnp.dot(ws, vn_ref[sl, :], preferred_element_type=F32) + bs
        y_ref[sl, :] = (gu_ref[sl, :].astype(F32) * s * zg_ref[sl, :].astype(F32)).astype(BF16)


def _spatial_gate(gu, vn, zg, w_s, b_s, tm):
    m = gu.shape[0]
    gd = D_BRANCH // GMLP_GROUPS
    tile = pl.BlockSpec((tm, gd), lambda r, g: (r, g))
    return pl.pallas_call(
        functools.partial(_sgu_kernel, tm // GMLP_CHUNK),
        grid=(m // tm, GMLP_GROUPS),
        in_specs=[tile, tile, tile,
                  pl.BlockSpec((None, GMLP_CHUNK, GMLP_CHUNK), lambda r, g: (g, 0, 0)),
                  pl.BlockSpec((None, GMLP_CHUNK, 1), lambda r, g: (g, 0, 0))],
        out_specs=tile,
        out_shape=jax.ShapeDtypeStruct(gu.shape, BF16),
        compiler_params=_cparams(2),
        name="gmlp_spatial_gate",
    )(gu, vn, zg, w_s, b_s.reshape(GMLP_GROUPS, GMLP_CHUNK, 1))


def _wout_kernel(mode, y_ref, w_ref, x_ref, gate_ref, *refs):
    if len(y_ref.shape) == 3:
        y = jnp.concatenate([y_ref[s] for s in range(y_ref.shape[0])], axis=1)
    else:
        y = y_ref[...]
    acc = jnp.dot(y, w_ref[...], preferred_element_type=F32)
    xn = x_ref[...] + gate_ref[...] * acc
    if mode == "final":
        g_ref, o_ref = refs
        o_ref[...] = xn * lax.rsqrt(jnp.mean(xn * xn, axis=-1, keepdims=True) + EPS) * g_ref[...]
    elif mode == "next":
        g_ref, sc_ref, sh_ref, xo_ref, h_ref = refs
        xo_ref[...] = xn
        h_ref[...] = _mod_rmsnorm(xn, g_ref[...], sc_ref[...], sh_ref[...]).astype(BF16)
    else:
        refs[0][...] = xn


def _out_proj(y, w_out, x2, mods, layer, who, mode, norm_vec, tm):
    m = x2.shape[0]
    row = pl.BlockSpec((tm, D_MODEL), lambda r: (r, 0))
    if y.ndim == 3:
        y_spec = pl.BlockSpec((SLABS, tm, LANES), lambda r: (0, r, 0))
    else:
        y_spec = pl.BlockSpec((tm, D_BRANCH), lambda r: (r, 0))
    in_specs = [y_spec,
                pl.BlockSpec((None, D_BRANCH, D_MODEL), lambda r: (layer, 0, 0),
                             pipeline_mode=pl.Buffered(1)),
                row, _mod_spec(layer, 2, who)]
    args = [y, w_out, x2, mods]
    xs = jax.ShapeDtypeStruct((m, D_MODEL), F32)
    if mode == "final":
        in_specs.append(pl.BlockSpec((1, D_MODEL), lambda r: (0, 0)))
        args.append(norm_vec.reshape(1, D_MODEL))
        out_shape, out_specs = xs, row
    elif mode == "next":
        in_specs += [_row_spec(layer + 1), _mod_spec(layer + 1, 1, who), _mod_spec(layer + 1, 0, who)]
        args += [norm_vec, mods, mods]
        out_shape = (xs, jax.ShapeDtypeStruct((m, D_MODEL), BF16))
        out_specs = (row, row)
    else:
        out_shape, out_specs = xs, row
    return pl.pallas_call(
        functools.partial(_wout_kernel, mode),
        grid=(m // tm,),
        in_specs=in_specs,
        out_specs=out_specs,
        out_shape=out_shape,
        compiler_params=_cparams(1),
        name="out_proj_" + mode,
    )(*args)


def kernel(x, c, ctx, c_ctx, norm_g, ada_w, ada_b, w_out, fnet_w_in, fnet_w_mix, attn_w_in, attn_sink,
           gmlp_w_in, gmlp_w_s, gmlp_b_s, gmlp_ln_g, gmlp_ln_b, final_g):
    batch, seq, d = x.shape
    ctx_len = ctx.shape[1]
    assert d == D_MODEL and seq % (DFT_NB * 8) == 0 and seq % GRID_W == 0 and batch < 8
    m_lat, m_ctx = batch * seq, batch * ctx_len
    tm_lat = min(1024, seq)
    tm_ctx = ctx_len
    tm_out = 256

    def who_lat(tm):
        return lambda r: (r * tm) // seq

    who_ctx = lambda r: batch

    cvec = jnp.zeros((8, d), F32).at[:batch].set(c).at[batch].set(c_ctx)
    mods = _mods(cvec, ada_w, ada_b).reshape(DEPTH, 8, 3, 1, d)
    norm_g3 = norm_g.reshape(DEPTH, 1, d)
    w_out_b = w_out.astype(BF16)

    xl = x.reshape(m_lat, d)
    xc = ctx.reshape(m_ctx, d)
    hl = _prenorm(xl, norm_g3, mods, 0, who_lat(512), 512)
    hc = _prenorm(xc, norm_g3, mods, 0, who_ctx, tm_ctx)

    cos_np, sin_np = _rope_tables(seq)
    cos_t, sin_t = jnp.asarray(cos_np), jnp.asarray(sin_np)

    def finish(y, x2, layer, who_fn, need_next, is_final):
        if is_final:
            return _out_proj(y, w_out_b, x2, mods, layer, who_fn, "final", final_g, tm_out), None
        if need_next:
            return _out_proj(y, w_out_b, x2, mods, layer, who_fn, "next", norm_g3, tm_out)
        return _out_proj(y, w_out_b, x2, mods, layer, who_fn, "plain", None, tm_out), None

    out = None
    for i in range(DEPTH):
        kind, j = i % 3, i // 3
        need_ctx = i < DEPTH - 1
        last = i == DEPTH - 1
        if kind == 0:
            w = fnet_w_in[j].astype(BF16)
            cs = _fold_mix(fnet_w_mix[j])
            tm_dft = (seq // DFT_NB) * DFT_ROWS
            ar, ai = _proj_channel_dft(hl, w, cs, "fnet_in_u", tm_dft, seq=seq)
            zg = _proj_silu_slabs(hl, w, D_BRANCH, "fnet_in_z", tm_lat)
            y = _position_dft(ar, ai, zg, batch, seq)
            yc = None
            if need_ctx:
                ar, ai = _proj_channel_dft(hc, w, cs, "fnet_in_u_ctx", tm_ctx)
                zg = _proj_silu_slabs(hc, w, D_BRANCH, "fnet_in_z_ctx", tm_ctx)
                yc = _position_dft_dense(ar, ai, zg, batch, ctx_len)
        elif kind == 1:
            w = attn_w_in[j]
            kvw = KV_HEADS * HEAD_DIM
            wq = w[:, :D_BRANCH].astype(BF16)
            wz = w[:, D_BRANCH + 2 * kvw:].astype(BF16)

            def dup(wpart):
                w3 = wpart.reshape(d, KV_HEADS, 1, HEAD_DIM)
                return jnp.broadcast_to(w3, (d, KV_HEADS, 2, HEAD_DIM)).reshape(d, 2 * kvw).astype(BF16)

            wk = dup(w[:, D_BRANCH:D_BRANCH + kvw])
            wv = dup(w[:, D_BRANCH + kvw:D_BRANCH + 2 * kvw])
            sink = attn_sink[j].reshape(KV_HEADS, Q_GROUP)
            scale = HEAD_DIM ** -0.5
            q = _proj_rope(hl, wq, 0, D_BRANCH, scale, cos_t, sin_t, seq, "attn_in_q", tm_lat)
            kd = _proj_rope(hl, wk, 0, 2 * kvw, 1.0, cos_t, sin_t, seq, "attn_in_k", tm_lat)
            vd = _proj_simple(hl, wv, 0, 2 * kvw, _epi_cast, "attn_in_v", tm_lat)
            zg = _proj_simple(hl, wz, 0, D_BRANCH, _epi_silu, "attn_in_z", tm_lat)
            kcd = _proj_simple(hc, wk, 0, 2 * kvw, _epi_cast, "attn_in_kc", tm_ctx)
            vcd = _proj_simple(hc, wv, 0, 2 * kvw, _epi_cast, "attn_in_vc", tm_ctx)
            y = _attention(q, kd, vd, kcd, vcd, zg, sink, batch, seq, ctx_len, True)
            yc = None
            if need_ctx:
                qc = _proj_simple(hc, wq * jnp.asarray(scale, BF16), 0, D_BRANCH, _epi_cast, "attn_in_qc", tm_ctx)
                zgc = _proj_simple(hc, wz, 0, D_BRANCH, _epi_silu, "attn_in_zc", tm_ctx)
                yc = _attention(qc, None, None, kcd, vcd, zgc, sink, batch, ctx_len, ctx_len, False)
        else:
            w = gmlp_w_in[j].astype(BF16)
            ws = gmlp_w_s[j].astype(BF16)

            def gmlp_branch(h, tm):
                gu = _proj_simple(h, w, 0, D_BRANCH, _epi_gelu, "gmlp_in_u", tm)
                vn = _proj_gelu_layernorm(h, w, D_BRANCH, gmlp_ln_g[j], gmlp_ln_b[j], "gmlp_in_v", 256)
                zg = _proj_simple(h, w, 2 * D_BRANCH, D_BRANCH, _epi_silu, "gmlp_in_z", tm)
                return _spatial_gate(gu, vn, zg, ws, gmlp_b_s[j], min(tm, 512))

            y = gmlp_branch(hl, tm_lat)
            yc = gmlp_branch(hc, tm_ctx) if need_ctx else None

        res, hl = finish(y, xl, i, who_lat(tm_out), not last, last)
        if last:
            out = res
        else:
            xl = res
        if need_ctx:
            xc, hc = finish(yc, xc, i, who_ctx, i + 1 < DEPTH - 1 or (i + 1) % 3 == 1, False)
    return out.reshape(batch, seq, d)
```

```python
import functools
import math

import numpy as np
import jax
import jax.numpy as jnp
from jax import lax
from jax.experimental import pallas as pl
from jax.experimental.pallas import tpu as pltpu

F32 = jnp.float32
BF16 = jnp.bfloat16

D_MODEL = 2048
D_BRANCH = 4096
DEPTH = 4
GRID_W = 64
FNET_GROUPS = 16
FNET_GROUP_DIM = 256
HEAD_DIM = 64
KV_HEADS = 8
Q_GROUP = 8
ATTN_BLOCK = 128
ROPE_BASE = 10000.0
GMLP_CHUNK = 128
GMLP_GROUPS = 16
EPS = 1e-6
NEG_INF = -1e30
LOG2_E = math.log2(math.e)

LANES = 128
MXU_COLS = 256
PROJ_TN = 1024
DFT_NB = 128
DFT_ROWS = 16
SLABS = D_BRANCH // LANES
VMEM_LIMIT = 56 * 1024 * 1024


def _cparams(n_axes, vmem=VMEM_LIMIT):
    return pltpu.CompilerParams(dimension_semantics=("arbitrary",) * n_axes,
                                vmem_limit_bytes=vmem)


def _silu(z):
    return z / (1.0 + jnp.exp(-z))


def _gelu_tanh(x):
    c = math.sqrt(2.0 / math.pi)
    return 0.5 * x * (1.0 + jnp.tanh(c * (x + 0.044715 * (x * x * x))))


def _mod_rmsnorm(x, g, scale, shift):
    y = x * lax.rsqrt(jnp.mean(x * x, axis=-1, keepdims=True) + EPS) * g
    return y * (1.0 + scale) + shift


def _channel_dft_matrix():
    n = FNET_GROUP_DIM
    k = np.arange(n, dtype=np.float64)
    ang = 2.0 * np.pi * np.outer(k, k) / n
    s = 1.0 / math.sqrt(n)
    return np.concatenate([np.cos(ang) * s, -np.sin(ang) * s], axis=1).astype(np.float32)


def _position_dft_matrices(seq):
    na, nb = seq // DFT_NB, DFT_NB
    a = np.arange(na, dtype=np.float64)
    b = np.arange(nb, dtype=np.float64)
    ang = 2.0 * np.pi * (a[None, None, :] * a[None, :, None] / na + b[:, None, None] * a[None, :, None] / seq)
    mr = np.cos(ang) / math.sqrt(na)
    mi = -np.sin(ang) / math.sqrt(na)
    fa = np.concatenate([np.concatenate([mr, -mi], axis=2), np.concatenate([mi, mr], axis=2)], axis=1)
    angb = 2.0 * np.pi * np.outer(b, b) / nb
    fb = np.concatenate([np.cos(angb), np.sin(angb)], axis=1) / math.sqrt(nb)
    return fa.astype(np.float32), fb.astype(np.float32)


def _dense_dft_matrix(n):
    k = np.arange(n, dtype=np.float64)
    ang = 2.0 * np.pi * np.outer(k, k) / n
    return (np.concatenate([np.cos(ang), np.sin(ang)], axis=1) / math.sqrt(n)).astype(np.float32)


def _rope_tables(seq):
    nf = HEAD_DIM // 4
    inv = ROPE_BASE ** (-np.arange(nf, dtype=np.float64) / nf)
    t = np.arange(seq)
    rows = (t // GRID_W).astype(np.float64)
    cols = (t % GRID_W).astype(np.float64)
    parts_c, parts_s = [], []
    for pos in (rows, cols):
        ang = pos[:, None] * inv[None, :]
        parts_c += [np.cos(ang), np.cos(ang)]
        parts_s += [-np.sin(ang), np.sin(ang)]
    cos = np.concatenate(parts_c, axis=1)
    sin = np.concatenate(parts_s, axis=1)
    reps = LANES // HEAD_DIM
    return (np.tile(cos, (1, reps)).astype(np.float32), np.tile(sin, (1, reps)).astype(np.float32))


def _mods_kernel(cv_ref, w_ref, b_ref, o_ref):
    a = _silu(cv_ref[...])
    o_ref[...] = jnp.dot(a, w_ref[...], preferred_element_type=F32,
                         precision=lax.Precision.HIGHEST) + b_ref[...]


def _mods(cvec, ada_w, ada_b):
    depth, d, n3 = ada_w.shape
    tn = 1024
    return pl.pallas_call(
        _mods_kernel,
        grid=(depth, n3 // tn),
        in_specs=[pl.BlockSpec((8, d), lambda i, j: (0, 0)),
                  pl.BlockSpec((None, d, tn), lambda i, j: (i, 0, j)),
                  pl.BlockSpec((None, 1, tn), lambda i, j: (i, 0, j))],
        out_specs=pl.BlockSpec((None, 8, tn), lambda i, j: (i, 0, j)),
        out_shape=jax.ShapeDtypeStruct((depth, 8, n3), F32),
        compiler_params=_cparams(2),
        name="ada_mods",
    )(cvec, ada_w, ada_b.reshape(depth, 1, n3))


def _mod_spec(layer, kind, who_of_row):
    return pl.BlockSpec((None, None, None, 1, D_MODEL),
                        lambda r, *_: (layer, who_of_row(r), kind, 0, 0))


def _row_spec(vec_layer):
    return pl.BlockSpec((None, 1, D_MODEL), lambda r, *_: (vec_layer, 0, 0))


def _prenorm_kernel(x_ref, g_ref, sc_ref, sh_ref, h_ref):
    h_ref[...] = _mod_rmsnorm(x_ref[...], g_ref[...], sc_ref[...], sh_ref[...]).astype(BF16)


def _prenorm(x2, norm_g3, mods, layer, who, tm):
    m = x2.shape[0]
    return pl.pallas_call(
        _prenorm_kernel,
        grid=(m // tm,),
        in_specs=[pl.BlockSpec((tm, D_MODEL), lambda r: (r, 0)),
                  _row_spec(layer), _mod_spec(layer, 1, who), _mod_spec(layer, 0, who)],
        out_specs=pl.BlockSpec((tm, D_MODEL), lambda r: (r, 0)),
        out_shape=jax.ShapeDtypeStruct((m, D_MODEL), BF16),
        compiler_params=_cparams(1),
        name="prenorm",
    )(x2, norm_g3, mods, mods)


def _proj_kernel(epilogue, n_extra, chunk, a_ref, w_ref, *rest):
    if len(a_ref.shape) == 3:
        a_flat = rest[-1]
        rest = rest[:-1]

        @pl.when(pl.program_id(1) == 0)
        def _():
            a_flat[...] = a_ref[...].reshape(a_flat.shape)

        a = a_flat[...]
    else:
        a = a_ref[...]
    for c0 in range(0, w_ref.shape[1], chunk):
        acc = jnp.dot(a, w_ref[:, c0:c0 + chunk], preferred_element_type=F32)
        epilogue(acc, c0, rest[:n_extra], rest[n_extra:])


def _proj(h, w, col0, ncols, tm, tn, epilogue, extras, extra_specs, out_shapes, out_specs, name,
          w_single_buffer=False, lhs_spec=None, chunk=MXU_COLS):
    k = h.shape[-1]
    m = h.size // k
    off = col0 // tn
    w_kwargs = dict(pipeline_mode=pl.Buffered(1)) if w_single_buffer else {}
    scratch = [pltpu.VMEM((tm, k), h.dtype)] if lhs_spec is not None else []
    if lhs_spec is None:
        lhs_spec = pl.BlockSpec((tm, k), lambda i, j: (i, 0))
    return pl.pallas_call(
        functools.partial(_proj_kernel, epilogue, len(extras), min(chunk, tn)),
        grid=(m // tm, ncols // tn),
        in_specs=[lhs_spec,
                  pl.BlockSpec((k, tn), lambda i, j: (0, j + off), **w_kwargs)] + list(extra_specs),
        out_specs=out_specs,
        out_shape=out_shapes,
        scratch_shapes=scratch,
        compiler_params=_cparams(2),
        name=name,
    )(h, w, *extras)


def _epi_silu(acc, c0, extras, outs):
    outs[0][:, c0:c0 + acc.shape[1]] = _silu(acc).astype(BF16)


def _epi_silu_slabs(acc, c0, extras, outs):
    for t in range(acc.shape[1] // LANES):
        outs[0][c0 // LANES + t] = _silu(acc[:, t * LANES:(t + 1) * LANES]).astype(BF16)


def _epi_gelu(acc, c0, extras, outs):
    outs[0][:, c0:c0 + acc.shape[1]] = _gelu_tanh(acc).astype(BF16)


def _epi_cast(acc, c0, extras, outs):
    outs[0][:, c0:c0 + acc.shape[1]] = acc.astype(BF16)


def _epi_scale_cast(scale, acc, c0, extras, outs):
    outs[0][:, c0:c0 + acc.shape[1]] = (acc * scale).astype(BF16)


def _epi_gelu_layernorm(acc, c0, extras, outs):
    g_ref, b_ref = extras
    ge = _gelu_tanh(acc)
    mu = jnp.mean(ge, axis=-1, keepdims=True)
    d = ge - mu
    var = jnp.mean(d * d, axis=-1, keepdims=True)
    outs[0][...] = (d * lax.rsqrt(var + EPS) * g_ref[...] + b_ref[...]).astype(BF16)


def _epi_channel_dft(split_rows, acc, c0, extras, outs):
    cs_ref = extras[0]
    ar_ref, ai_ref = outs
    tm = acc.shape[0]
    for gl in range(acc.shape[1] // FNET_GROUP_DIM):
        g = c0 // FNET_GROUP_DIM + gl
        ub = acc[:, gl * FNET_GROUP_DIM:(gl + 1) * FNET_GROUP_DIM].astype(BF16)
        ab = jnp.dot(ub, cs_ref[g], preferred_element_type=F32)
        if split_rows:
            ab4 = ab.reshape(tm // DFT_ROWS, 2, DFT_ROWS // 2, ab.shape[1])
            ab = jnp.concatenate([ab4[:, hf].reshape(tm // 2, ab.shape[1]) for hf in range(2)], axis=0)
        ar_ref[2 * g] = ab[:, 0:128]
        ar_ref[2 * g + 1] = ab[:, 128:256]
        ai_ref[2 * g] = ab[:, 256:384]
        ai_ref[2 * g + 1] = ab[:, 384:512]


def _epi_rope(scale, acc, c0, extras, outs):
    cos = extras[0][...]
    sin = extras[1][...]
    lane = lax.broadcasted_iota(jnp.int32, (1, LANES), 1)
    first = (lane % 32) < 16
    for t in range(acc.shape[1] // LANES):
        x = acc[:, t * LANES:(t + 1) * LANES]
        partner = jnp.where(first, pltpu.roll(x, LANES - 16, 1), pltpu.roll(x, 16, 1))
        y = x * cos + partner * sin
        if scale != 1.0:
            y = y * scale
        outs[0][:, c0 + t * LANES:c0 + (t + 1) * LANES] = y.astype(BF16)


def _proj_simple(h, w, col0, ncols, epilogue, name, tm, tn=PROJ_TN):
    m = h.shape[0]
    return _proj(h, w, col0, ncols, tm, tn, epilogue, (), (),
                 jax.ShapeDtypeStruct((m, ncols), BF16),
                 pl.BlockSpec((tm, tn), lambda i, j: (i, j)), name)


def _proj_rope(h, w, col0, ncols, scale, cos_t, sin_t, seq, name, tm, tn=PROJ_TN):
    m = h.shape[0]
    per_batch = seq // tm
    tab_spec = pl.BlockSpec((tm, LANES), lambda i, j: (i % per_batch, 0))
    return _proj(h, w, col0, ncols, tm, tn, functools.partial(_epi_rope, scale),
                 (cos_t, sin_t), (tab_spec, tab_spec),
                 jax.ShapeDtypeStruct((m, ncols), BF16),
                 pl.BlockSpec((tm, tn), lambda i, j: (i, j)), name)


def _proj_silu_slabs(h, w, col0, name, tm, tn=PROJ_TN):
    m = h.shape[0]
    return _proj(h, w, col0, D_BRANCH, tm, tn, _epi_silu_slabs, (), (),
                 jax.ShapeDtypeStruct((SLABS, m, LANES), BF16),
                 pl.BlockSpec((tn // LANES, tm, LANES), lambda i, j: (j, i, 0)), name)


def _proj_channel_dft(h, w, cs, name, tm, seq=None, tn=PROJ_TN):
    m = h.shape[0]
    lhs_spec = None
    if seq is not None:
        na, tiles = seq // DFT_NB, DFT_NB // DFT_ROWS
        assert tm == na * DFT_ROWS
        h = h.reshape(m // seq, na, DFT_NB, D_MODEL)
        lhs_spec = pl.BlockSpec((None, na, DFT_ROWS, D_MODEL), lambda i, j: (i // tiles, 0, i % tiles, 0))
    slab_shape = jax.ShapeDtypeStruct((SLABS, m, LANES), F32)
    slab_spec = pl.BlockSpec((tn // LANES, tm, LANES), lambda i, j: (j, i, 0))
    groups = tn // FNET_GROUP_DIM
    cs_spec = pl.BlockSpec((groups,) + cs.shape[1:], lambda i, j: (j, 0, 0))
    return _proj(h, w, 0, D_BRANCH, tm, tn, functools.partial(_epi_channel_dft, seq is not None), (cs,), (cs_spec,),
                 (slab_shape, slab_shape), (slab_spec, slab_spec), name, lhs_spec=lhs_spec,
                 chunk=2 * MXU_COLS)


def _fold_mix_kernel(c_ref, s_ref, wm_ref, o_ref):
    wm = wm_ref[...]
    gd = FNET_GROUP_DIM
    hp = lax.Precision.HIGHEST
    o_ref[:, 0:gd] = jnp.dot(c_ref[...], wm, preferred_element_type=F32, precision=hp).astype(BF16)
    o_ref[:, gd:2 * gd] = jnp.dot(s_ref[...], wm, preferred_element_type=F32, precision=hp).astype(BF16)


def _fold_mix(w_mix):
    gd = FNET_GROUP_DIM
    cs = _channel_dft_matrix()
    mat = pl.BlockSpec((gd, gd), lambda g: (0, 0))
    return pl.pallas_call(
        _fold_mix_kernel,
        grid=(FNET_GROUPS,),
        in_specs=[mat, mat, pl.BlockSpec((None, gd, gd), lambda g: (g, 0, 0))],
        out_specs=pl.BlockSpec((None, gd, 2 * gd), lambda g: (g, 0, 0)),
        out_shape=jax.ShapeDtypeStruct((FNET_GROUPS, gd, 2 * gd), BF16),
        compiler_params=_cparams(1),
        name="fnet_fold_mix",
    )(jnp.asarray(cs[:, :gd]), jnp.asarray(cs[:, gd:]), w_mix)


def _proj_gelu_layernorm(h, w, col0, ln_g, ln_b, name, tm):
    m = h.shape[0]
    vec_spec = pl.BlockSpec((1, D_BRANCH), lambda i, j: (0, 0))
    return _proj(h, w, col0, D_BRANCH, tm, D_BRANCH, _epi_gelu_layernorm,
                 (ln_g.reshape(1, D_BRANCH), ln_b.reshape(1, D_BRANCH)), (vec_spec, vec_spec),
                 jax.ShapeDtypeStruct((m, D_BRANCH), BF16),
                 pl.BlockSpec((tm, D_BRANCH), lambda i, j: (i, j)), name, w_single_buffer=True,
                 chunk=D_BRANCH)


def _dft_kernel(na, pitch, gpitch, ar_ref, ai_ref, zg_ref, fa_ref, fb_ref, y_ref, er_ref, ei_ref, g_ref):
    half_rows = na * (DFT_ROWS // 2)

    def stage_a(b, carry):
        start = (b // (DFT_ROWS // 2)) * half_rows + b % (DFT_ROWS // 2)
        zr = ar_ref[pl.ds(start, na, stride=DFT_ROWS // 2), :]
        zi = ai_ref[pl.ds(start, na, stride=DFT_ROWS // 2), :]
        d = jnp.concatenate([zr, zi], axis=0).astype(BF16)
        e = jnp.dot(fa_ref[b], d, preferred_element_type=F32)
        off = pl.multiple_of(b * pitch, 8)
        er_ref[pl.ds(off, na), :] = e[:na]
        ei_ref[pl.ds(off, na), :] = e[na:]
        return carry

    lax.fori_loop(0, DFT_NB, stage_a, 0, unroll=8)

    def stage_b(ka, carry):
        er = er_ref[pl.ds(ka, DFT_NB, stride=pitch), :]
        ei = ei_ref[pl.ds(ka, DFT_NB, stride=pitch), :]
        d = jnp.concatenate([er, ei], axis=0).astype(BF16)
        off = pl.multiple_of(ka * gpitch, 8)
        g_ref[pl.ds(off, DFT_NB), :] = jnp.dot(fb_ref[...], d, preferred_element_type=F32)
        return carry

    lax.fori_loop(0, na, stage_b, 0, unroll=4)

    def gate(kb, carry):
        rows = pl.ds(pl.multiple_of(kb * na, na), na)
        g = g_ref[pl.ds(kb, na, stride=gpitch), :]
        y_ref[rows, :] = (g * zg_ref[rows, :].astype(F32)).astype(BF16)
        return carry

    lax.fori_loop(0, DFT_NB, gate, 0, unroll=8)


def _position_dft(ar, ai, zg, batch, seq):
    na = seq // DFT_NB
    pitch = na + 8
    gpitch = DFT_NB + 8
    fa_np, fb_np = _position_dft_matrices(seq)
    fa = jnp.asarray(fa_np).astype(BF16)
    fb = jnp.asarray(fb_np).astype(BF16)
    slab = pl.BlockSpec((None, seq, LANES), lambda s, b: (s, b, 0))
    return pl.pallas_call(
        functools.partial(_dft_kernel, na, pitch, gpitch),
        grid=(SLABS, batch),
        in_specs=[slab, slab, slab,
                  pl.BlockSpec(fa.shape, lambda s, b: (0, 0, 0)),
                  pl.BlockSpec(fb.shape, lambda s, b: (0, 0))],
        out_specs=slab,
        out_shape=jax.ShapeDtypeStruct(ar.shape, BF16),
        scratch_shapes=[pltpu.VMEM((DFT_NB * pitch, LANES), F32),
                        pltpu.VMEM((DFT_NB * pitch, LANES), F32),
                        pltpu.VMEM((na * gpitch, LANES), F32)],
        compiler_params=_cparams(2),
        name="position_dft",
    )(ar, ai, zg, fa, fb)


def _dft_dense_kernel(ar_ref, ai_ref, zg_ref, fd_ref, y_ref):
    d = jnp.concatenate([ar_ref[...], ai_ref[...]], axis=0).astype(BF16)
    g = jnp.dot(fd_ref[...], d, preferred_element_type=F32)
    y_ref[...] = (g * zg_ref[...].astype(F32)).astype(BF16)


def _position_dft_dense(ar, ai, zg, batch, seq):
    fd = jnp.asarray(_dense_dft_matrix(seq)).astype(BF16)
    slab = pl.BlockSpec((None, seq, LANES), lambda s, b: (s, b, 0))
    return pl.pallas_call(
        _dft_dense_kernel,
        grid=(SLABS, batch),
        in_specs=[slab, slab, slab, pl.BlockSpec(fd.shape, lambda s, b: (0, 0))],
        out_specs=slab,
        out_shape=jax.ShapeDtypeStruct(ar.shape, BF16),
        compiler_params=_cparams(2),
        name="position_dft_dense",
    )(ar, ai, zg, fd)


def _attn_kernel(n_band, nblk, sink_ref, q_ref, *refs):
    k_refs = refs[:n_band + 1]
    v_refs = refs[n_band + 1:2 * n_band + 2]
    zg_ref, y_ref = refs[2 * n_band + 2:]
    h = pl.program_id(1)
    i = pl.program_id(2)
    blk = ATTN_BLOCK
    n_pair = Q_GROUP // 2

    lane = lax.broadcasted_iota(jnp.int32, (1, LANES), 1)
    lo = lane < HEAD_DIM
    keys = jnp.concatenate([r[...] for r in k_refs], axis=0)
    vals = jnp.concatenate([r[...] for r in v_refs], axis=0)
    ones = jnp.ones_like(vals)
    v_ext = (jnp.where(lo, vals, ones), jnp.where(lo, ones, vals))
    dn = (((1,), (1,)), ((), ()))
    if n_band:
        r = lax.broadcasted_iota(jnp.int32, (blk, blk), 0)
        c = lax.broadcasted_iota(jnp.int32, (blk, blk), 1)
        prev_ok = c >= r + jnp.where(i > 0, 0, blk)
        next_ok = c <= r - jnp.where(i < nblk - 1, 0, blk)

    def scores(t):
        q = q_ref[:, t * LANES:(t + 1) * LANES]
        qzero = jnp.zeros_like(q)
        out = []
        for qm in (jnp.where(lo, q, qzero), jnp.where(lo, qzero, q)):
            s = lax.dot_general(qm, keys, dn, preferred_element_type=F32)
            parts = [s[:, j * blk:(j + 1) * blk] for j in range(s.shape[1] // blk)]
            if n_band:
                parts[0] = jnp.where(prev_ok, parts[0], NEG_INF)
                parts[2] = jnp.where(next_ok, parts[2], NEG_INF)
            out.append(parts)
        return out

    def finish(t, both):
        tile = slice(t * LANES, (t + 1) * LANES)
        o_ext, sink_term = [], []
        for hd, parts in enumerate(both):
            sk = sink_ref[h, 2 * t + hd] * LOG2_E
            mx = parts[0]
            for part in parts[1:]:
                mx = jnp.maximum(mx, part)
            mx = jnp.maximum(jnp.max(mx, axis=-1, keepdims=True), sk)
            p = jnp.concatenate([jnp.exp2(part - mx).astype(BF16) for part in parts], axis=1)
            o_ext.append(jnp.dot(p, v_ext[hd], preferred_element_type=F32))
            sink_term.append(jnp.exp2(sk - mx))
        o = jnp.where(lo, o_ext[0], o_ext[1])
        denom = jnp.where(lo, pltpu.roll(o_ext[0], HEAD_DIM, 1) + sink_term[0],
                          pltpu.roll(o_ext[1], HEAD_DIM, 1) + sink_term[1])
        y_ref[:, tile] = (o / denom * zg_ref[:, tile].astype(F32)).astype(BF16)

    pending = scores(0)
    for t in range(n_pair):
        upcoming = scores(t + 1) if t + 1 < n_pair else None
        finish(t, pending)
        pending = upcoming


def _attention(q, kd, vd, kcd, vcd, zg, sink, batch, seq, ctx_len, use_band):
    blk = ATTN_BLOCK
    nblk = seq // blk
    qw = Q_GROUP * HEAD_DIM
    q_spec = pl.BlockSpec((blk, qw), lambda b, h, i: (b * nblk + i, h))
    ctx_spec = pl.BlockSpec((ctx_len, LANES), lambda b, h, i: (b, h))
    if use_band:
        def band(delta):
            return pl.BlockSpec(
                (blk, LANES), lambda b, h, i: (b * nblk + jnp.clip(i + delta, 0, nblk - 1), h))
        k_specs = [band(-1), band(0), band(1), ctx_spec]
        k_args, v_args = [kd, kd, kd, kcd], [vd, vd, vd, vcd]
        n_band = 3
    else:
        k_specs, k_args, v_args, n_band = [ctx_spec], [kcd], [vcd], 0
    return pl.pallas_call(
        functools.partial(_attn_kernel, n_band, nblk),
        grid=(batch, KV_HEADS, nblk),
        in_specs=[pl.BlockSpec(memory_space=pltpu.SMEM), q_spec] + k_specs + k_specs + [q_spec],
        out_specs=q_spec,
        out_shape=jax.ShapeDtypeStruct(q.shape, BF16),
        compiler_params=_cparams(3),
        name="attention_band" if use_band else "attention_ctx",
    )(sink, q, *k_args, *v_args, zg)


def _sgu_kernel(n_chunks, gu_ref, vn_ref, zg_ref, ws_ref, bs_ref, y_ref):
    ws = ws_ref[...]
    bs = bs_ref[...]
    for c in range(n_chunks):
        sl = slice(c * GMLP_CHUNK, (c + 1) * GMLP_CHUNK)
        s = jnp.dot(ws, vn_ref[sl, :], preferred_element_type=F32) + bs
        y_ref[sl, :] = (gu_ref[sl, :].astype(F32) * s * zg_ref[sl, :].astype(F32)).astype(BF16)


def _spatial_gate(gu, vn, zg, w_s, b_s, tm):
    m = gu.shape[0]
    gd = D_BRANCH // GMLP_GROUPS
    tile = pl.BlockSpec((tm, gd), lambda r, g: (r, g))
    return pl.pallas_call(
        functools.partial(_sgu_kernel, tm // GMLP_CHUNK),
        grid=(m // tm, GMLP_GROUPS),
        in_specs=[tile, tile, tile,
                  pl.BlockSpec((None, GMLP_CHUNK, GMLP_CHUNK), lambda r, g: (g, 0, 0)),
                  pl.BlockSpec((None, GMLP_CHUNK, 1), lambda r, g: (g, 0, 0))],
        out_specs=tile,
        out_shape=jax.ShapeDtypeStruct(gu.shape, BF16),
        compiler_params=_cparams(2),
        name="gmlp_spatial_gate",
    )(gu, vn, zg, w_s, b_s.reshape(GMLP_GROUPS, GMLP_CHUNK, 1))


def _wout_kernel(mode, y_ref, w_ref, x_ref, gate_ref, *refs):
    if len(y_ref.shape) == 3:
        y = jnp.concatenate([y_ref[s] for s in range(y_ref.shape[0])], axis=1)
    else:
        y = y_ref[...]
    acc = jnp.dot(y, w_ref[...], preferred_element_type=F32)
    xn = x_ref[...] + gate_ref[...] * acc
    if mode == "final":
        g_ref, o_ref = refs
        o_ref[...] = xn * lax.rsqrt(jnp.mean(xn * xn, axis=-1, keepdims=True) + EPS) * g_ref[...]
    elif mode == "next":
        g_ref, sc_ref, sh_ref, xo_ref, h_ref = refs
        xo_ref[...] = xn
        h_ref[...] = _mod_rmsnorm(xn, g_ref[...], sc_ref[...], sh_ref[...]).astype(BF16)
    else:
        refs[0][...] = xn


def _out_proj(y, w_out, x2, mods, layer, who, mode, norm_vec, tm):
    m = x2.shape[0]
    row = pl.BlockSpec((tm, D_MODEL), lambda r: (r, 0))
    if y.ndim == 3:
        y_spec = pl.BlockSpec((SLABS, tm, LANES), lambda r: (0, r, 0))
    else:
        y_spec = pl.BlockSpec((tm, D_BRANCH), lambda r: (r, 0))
    in_specs = [y_spec,
                pl.BlockSpec((None, D_BRANCH, D_MODEL), lambda r: (layer, 0, 0),
                             pipeline_mode=pl.Buffered(1)),
                row, _mod_spec(layer, 2, who)]
    args = [y, w_out, x2, mods]
    xs = jax.ShapeDtypeStruct((m, D_MODEL), F32)
    if mode == "final":
        in_specs.append(pl.BlockSpec((1, D_MODEL), lambda r: (0, 0)))
        args.append(norm_vec.reshape(1, D_MODEL))
        out_shape, out_specs = xs, row
    elif mode == "next":
        in_specs += [_row_spec(layer + 1), _mod_spec(layer + 1, 1, who), _mod_spec(layer + 1, 0, who)]
        args += [norm_vec, mods, mods]
        out_shape = (xs, jax.ShapeDtypeStruct((m, D_MODEL), BF16))
        out_specs = (row, row)
    else:
        out_shape, out_specs = xs, row
    return pl.pallas_call(
        functools.partial(_wout_kernel, mode),
        grid=(m // tm,),
        in_specs=in_specs,
        out_specs=out_specs,
        out_shape=out_shape,
        compiler_params=_cparams(1),
        name="out_proj_" + mode,
    )(*args)


def kernel(x, c, ctx, c_ctx, norm_g, ada_w, ada_b, w_out, fnet_w_in, fnet_w_mix, attn_w_in, attn_sink,
           gmlp_w_in, gmlp_w_s, gmlp_b_s, gmlp_ln_g, gmlp_ln_b, final_g):
    batch, seq, d = x.shape
    ctx_len = ctx.shape[1]
    assert d == D_MODEL and seq % (DFT_NB * 8) == 0 and seq % GRID_W == 0 and batch < 8
    m_lat, m_ctx = batch * seq, batch * ctx_len
    tm_lat = min(1024, seq)
    tm_ctx = ctx_len
    tm_out = 256

    def who_lat(tm):
        return lambda r: (r * tm) // seq

    who_ctx = lambda r: batch

    cvec = jnp.zeros((8, d), F32).at[:batch].set(c).at[batch].set(c_ctx)
    mods = _mods(cvec, ada_w, ada_b).reshape(DEPTH, 8, 3, 1, d)
    norm_g3 = norm_g.reshape(DEPTH, 1, d)
    w_out_b = w_out.astype(BF16)

    xl = x.reshape(m_lat, d)
    xc = ctx.reshape(m_ctx, d)
    hl = _prenorm(xl, norm_g3, mods, 0, who_lat(512), 512)
    hc = _prenorm(xc, norm_g3, mods, 0, who_ctx, tm_ctx)

    cos_np, sin_np = _rope_tables(seq)
    cos_t, sin_t = jnp.asarray(cos_np), jnp.asarray(sin_np)

    def finish(y, x2, layer, who_fn, need_next, is_final):
        if is_final:
            return _out_proj(y, w_out_b, x2, mods, layer, who_fn, "final", final_g, tm_out), None
        if need_next:
            return _out_proj(y, w_out_b, x2, mods, layer, who_fn, "next", norm_g3, tm_out)
        return _out_proj(y, w_out_b, x2, mods, layer, who_fn, "plain", None, tm_out), None

    out = None
    for i in range(DEPTH):
        kind, j = i % 3, i // 3
        need_ctx = i < DEPTH - 1
        last = i == DEPTH - 1
        if kind == 0:
            w = fnet_w_in[j].astype(BF16)
            cs = _fold_mix(fnet_w_mix[j])
            tm_dft = (seq // DFT_NB) * DFT_ROWS
            ar, ai = _proj_channel_dft(hl, w, cs, "fnet_in_u", tm_dft, seq=seq)
            zg = _proj_silu_slabs(hl, w, D_BRANCH, "fnet_in_z", tm_lat)
            y = _position_dft(ar, ai, zg, batch, seq)
            yc = None
            if need_ctx:
                ar, ai = _proj_channel_dft(hc, w, cs, "fnet_in_u_ctx", tm_ctx)
                zg = _proj_silu_slabs(hc, w, D_BRANCH, "fnet_in_z_ctx", tm_ctx)
                yc = _position_dft_dense(ar, ai, zg, batch, ctx_len)
        elif kind == 1:
            w = attn_w_in[j]
            kvw = KV_HEADS * HEAD_DIM
            wq = w[:, :D_BRANCH].astype(BF16)
            wz = w[:, D_BRANCH + 2 * kvw:].astype(BF16)

            def dup(wpart):
                w3 = wpart.reshape(d, KV_HEADS, 1, HEAD_DIM)
                return jnp.broadcast_to(w3, (d, KV_HEADS, 2, HEAD_DIM)).reshape(d, 2 * kvw).astype(BF16)

            wk = dup(w[:, D_BRANCH:D_BRANCH + kvw])
            wv = dup(w[:, D_BRANCH + kvw:D_BRANCH + 2 * kvw])
            sink = attn_sink[j].reshape(KV_HEADS, Q_GROUP)
            scale = HEAD_DIM ** -0.5 * LOG2_E
            q = _proj_rope(hl, wq, 0, D_BRANCH, scale, cos_t, sin_t, seq, "attn_in_q", tm_lat)
            kd = _proj_rope(hl, wk, 0, 2 * kvw, 1.0, cos_t, sin_t, seq, "attn_in_k", tm_lat)
            vd = _proj_simple(hl, wv, 0, 2 * kvw, _epi_cast, "attn_in_v", tm_lat)
            zg = _proj_simple(hl, wz, 0, D_BRANCH, _epi_silu, "attn_in_z", tm_lat)
            kcd = _proj_simple(hc, wk, 0, 2 * kvw, _epi_cast, "attn_in_kc", tm_ctx)
            vcd = _proj_simple(hc, wv, 0, 2 * kvw, _epi_cast, "attn_in_vc", tm_ctx)
            y = _attention(q, kd, vd, kcd, vcd, zg, sink, batch, seq, ctx_len, True)
            yc = None
            if need_ctx:
                qc = _proj_simple(hc, wq, 0, D_BRANCH, functools.partial(_epi_scale_cast, scale),
                                  "attn_in_qc", tm_ctx)
                zgc = _proj_simple(hc, wz, 0, D_BRANCH, _epi_silu, "attn_in_zc", tm_ctx)
                yc = _attention(qc, None, None, kcd, vcd, zgc, sink, batch, ctx_len, ctx_len, False)
        else:
            w = gmlp_w_in[j].astype(BF16)
            ws = gmlp_w_s[j].astype(BF16)

            def gmlp_branch(h, tm):
                gu = _proj_simple(h, w, 0, D_BRANCH, _epi_gelu, "gmlp_in_u", tm)
                vn = _proj_gelu_layernorm(h, w, D_BRANCH, gmlp_ln_g[j], gmlp_ln_b[j], "gmlp_in_v", 256)
                zg = _proj_simple(h, w, 2 * D_BRANCH, D_BRANCH, _epi_silu, "gmlp_in_z", tm)
                return _spatial_gate(gu, vn, zg, ws, gmlp_b_s[j], min(tm, 512))

            y = gmlp_branch(hl, tm_lat)
            yc = gmlp_branch(hc, tm_ctx) if need_ctx else None

        res, hl = finish(y, xl, i, who_lat(tm_out), not last, last)
        if last:
            out = res
        else:
            xl = res
        if need_ctx:
            xc, hc = finish(yc, xc, i, who_ctx, i + 1 < DEPTH - 1 or (i + 1) % 3 == 1, False)
    return out.reshape(batch, seq, d)
```

```python
import functools
import math

import numpy as np
import jax
import jax.numpy as jnp
from jax import lax
from jax.experimental import pallas as pl
from jax.experimental.pallas import tpu as pltpu

F32 = jnp.float32
BF16 = jnp.bfloat16

D_MODEL = 2048
D_BRANCH = 4096
DEPTH = 4
GRID_W = 64
FNET_GROUPS = 16
FNET_GROUP_DIM = 256
HEAD_DIM = 64
KV_HEADS = 8
Q_GROUP = 8
ATTN_BLOCK = 128
ROPE_BASE = 10000.0
GMLP_CHUNK = 128
GMLP_GROUPS = 16
EPS = 1e-6
NEG_INF = -1e30
LOG2_E = math.log2(math.e)

LANES = 128
MXU_COLS = 256
PROJ_TN = 1024
DFT_NB = 128
DFT_ROWS = 16
SLABS = D_BRANCH // LANES
VMEM_LIMIT = 56 * 1024 * 1024


def _cparams(n_axes, vmem=VMEM_LIMIT):
    return pltpu.CompilerParams(dimension_semantics=("arbitrary",) * n_axes,
                                vmem_limit_bytes=vmem)


def _silu(z):
    return z / (1.0 + jnp.exp(-z))


def _gelu_tanh(x):
    c = math.sqrt(2.0 / math.pi)
    return 0.5 * x * (1.0 + jnp.tanh(c * (x + 0.044715 * (x * x * x))))


def _mod_rmsnorm(x, g, scale, shift):
    y = x * lax.rsqrt(jnp.mean(x * x, axis=-1, keepdims=True) + EPS) * g
    return y * (1.0 + scale) + shift


def _channel_dft_matrix():
    n = FNET_GROUP_DIM
    k = np.arange(n, dtype=np.float64)
    ang = 2.0 * np.pi * np.outer(k, k) / n
    s = 1.0 / math.sqrt(n)
    return np.concatenate([np.cos(ang) * s, -np.sin(ang) * s], axis=1).astype(np.float32)


def _position_dft_matrices(seq):
    na, nb = seq // DFT_NB, DFT_NB
    a = np.arange(na, dtype=np.float64)
    b = np.arange(nb, dtype=np.float64)
    ang = 2.0 * np.pi * (a[None, None, :] * a[None, :, None] / na + b[:, None, None] * a[None, :, None] / seq)
    mr = np.cos(ang) / math.sqrt(na)
    mi = -np.sin(ang) / math.sqrt(na)
    fa = np.concatenate([np.concatenate([mr, -mi], axis=2), np.concatenate([mi, mr], axis=2)], axis=1)
    fa = np.concatenate([fa[0::2], fa[1::2]], axis=2)
    angb = 2.0 * np.pi * np.outer(b, b) / nb
    fb = np.concatenate([np.cos(angb), np.sin(angb)], axis=1) / math.sqrt(nb)
    return fa.astype(np.float32), fb.astype(np.float32)


def _dense_dft_matrix(n):
    k = np.arange(n, dtype=np.float64)
    ang = 2.0 * np.pi * np.outer(k, k) / n
    return (np.concatenate([np.cos(ang), np.sin(ang)], axis=1) / math.sqrt(n)).astype(np.float32)


def _rope_tables(seq):
    nf = HEAD_DIM // 4
    inv = ROPE_BASE ** (-np.arange(nf, dtype=np.float64) / nf)
    t = np.arange(seq)
    rows = (t // GRID_W).astype(np.float64)
    cols = (t % GRID_W).astype(np.float64)
    parts_c, parts_s = [], []
    for pos in (rows, cols):
        ang = pos[:, None] * inv[None, :]
        parts_c += [np.cos(ang), np.cos(ang)]
        parts_s += [-np.sin(ang), np.sin(ang)]
    cos = np.concatenate(parts_c, axis=1)
    sin = np.concatenate(parts_s, axis=1)
    reps = LANES // HEAD_DIM
    return (np.tile(cos, (1, reps)).astype(np.float32), np.tile(sin, (1, reps)).astype(np.float32))


def _mods_kernel(cv_ref, w_ref, b_ref, o_ref):
    a = _silu(cv_ref[...])
    o_ref[...] = jnp.dot(a, w_ref[...], preferred_element_type=F32,
                         precision=lax.Precision.HIGHEST) + b_ref[...]


def _mods(cvec, ada_w, ada_b):
    depth, d, n3 = ada_w.shape
    tn = 1024
    return pl.pallas_call(
        _mods_kernel,
        grid=(depth, n3 // tn),
        in_specs=[pl.BlockSpec((8, d), lambda i, j: (0, 0)),
                  pl.BlockSpec((None, d, tn), lambda i, j: (i, 0, j)),
                  pl.BlockSpec((None, 1, tn), lambda i, j: (i, 0, j))],
        out_specs=pl.BlockSpec((None, 8, tn), lambda i, j: (i, 0, j)),
        out_shape=jax.ShapeDtypeStruct((depth, 8, n3), F32),
        compiler_params=_cparams(2),
        name="ada_mods",
    )(cvec, ada_w, ada_b.reshape(depth, 1, n3))


def _mod_spec(layer, kind, who_of_row):
    return pl.BlockSpec((None, None, None, 1, D_MODEL),
                        lambda r, *_: (layer, who_of_row(r), kind, 0, 0))


def _row_spec(vec_layer):
    return pl.BlockSpec((None, 1, D_MODEL), lambda r, *_: (vec_layer, 0, 0))


def _prenorm_kernel(x_ref, g_ref, sc_ref, sh_ref, h_ref):
    h_ref[...] = _mod_rmsnorm(x_ref[...], g_ref[...], sc_ref[...], sh_ref[...]).astype(BF16)


def _prenorm(x2, norm_g3, mods, layer, who, tm):
    m = x2.shape[0]
    return pl.pallas_call(
        _prenorm_kernel,
        grid=(m // tm,),
        in_specs=[pl.BlockSpec((tm, D_MODEL), lambda r: (r, 0)),
                  _row_spec(layer), _mod_spec(layer, 1, who), _mod_spec(layer, 0, who)],
        out_specs=pl.BlockSpec((tm, D_MODEL), lambda r: (r, 0)),
        out_shape=jax.ShapeDtypeStruct((m, D_MODEL), BF16),
        compiler_params=_cparams(1),
        name="prenorm",
    )(x2, norm_g3, mods, mods)


def _proj_kernel(epilogue, n_extra, chunk, a_ref, w_ref, *rest):
    if len(a_ref.shape) == 3:
        a_flat = rest[-1]
        rest = rest[:-1]

        @pl.when(pl.program_id(1) == 0)
        def _():
            a_flat[...] = a_ref[...].reshape(a_flat.shape)

        a = a_flat[...]
    else:
        a = a_ref[...]
    for c0 in range(0, w_ref.shape[1], chunk):
        acc = jnp.dot(a, w_ref[:, c0:c0 + chunk], preferred_element_type=F32)
        epilogue(acc, c0, rest[:n_extra], rest[n_extra:])


def _proj(h, w, col0, ncols, tm, tn, epilogue, extras, extra_specs, out_shapes, out_specs, name,
          w_single_buffer=False, lhs_spec=None, chunk=MXU_COLS):
    k = h.shape[-1]
    m = h.size // k
    off = col0 // tn
    w_kwargs = dict(pipeline_mode=pl.Buffered(1)) if w_single_buffer else {}
    scratch = [pltpu.VMEM((tm, k), h.dtype)] if lhs_spec is not None else []
    if lhs_spec is None:
        lhs_spec = pl.BlockSpec((tm, k), lambda i, j: (i, 0))
    return pl.pallas_call(
        functools.partial(_proj_kernel, epilogue, len(extras), min(chunk, tn)),
        grid=(m // tm, ncols // tn),
        in_specs=[lhs_spec,
                  pl.BlockSpec((k, tn), lambda i, j: (0, j + off), **w_kwargs)] + list(extra_specs),
        out_specs=out_specs,
        out_shape=out_shapes,
        scratch_shapes=scratch,
        compiler_params=_cparams(2),
        name=name,
    )(h, w, *extras)


def _epi_silu(acc, c0, extras, outs):
    outs[0][:, c0:c0 + acc.shape[1]] = _silu(acc).astype(BF16)


def _epi_silu_slabs(acc, c0, extras, outs):
    for t in range(acc.shape[1] // LANES):
        outs[0][c0 // LANES + t] = _silu(acc[:, t * LANES:(t + 1) * LANES]).astype(BF16)


def _epi_gelu(acc, c0, extras, outs):
    outs[0][:, c0:c0 + acc.shape[1]] = _gelu_tanh(acc).astype(BF16)


def _epi_cast(acc, c0, extras, outs):
    outs[0][:, c0:c0 + acc.shape[1]] = acc.astype(BF16)


def _epi_scale_cast(scale, acc, c0, extras, outs):
    outs[0][:, c0:c0 + acc.shape[1]] = (acc * scale).astype(BF16)


def _epi_channel_dft(split_rows, acc, c0, extras, outs):
    cs_ref = extras[0]
    ar_ref, ai_ref = outs
    tm = acc.shape[0]
    for gl in range(acc.shape[1] // FNET_GROUP_DIM):
        g = c0 // FNET_GROUP_DIM + gl
        ub = acc[:, gl * FNET_GROUP_DIM:(gl + 1) * FNET_GROUP_DIM].astype(BF16)
        ab = jnp.dot(ub, cs_ref[g], preferred_element_type=F32)
        if split_rows:
            ab4 = ab.reshape(tm // DFT_ROWS, 2, DFT_ROWS // 2, ab.shape[1])
            ab = jnp.concatenate([ab4[:, hf].reshape(tm // 2, ab.shape[1]) for hf in range(2)], axis=0)
        ar_ref[2 * g] = ab[:, 0:128]
        ar_ref[2 * g + 1] = ab[:, 128:256]
        ai_ref[2 * g] = ab[:, 256:384]
        ai_ref[2 * g + 1] = ab[:, 384:512]


def _epi_rope(scale, acc, c0, extras, outs):
    cos = extras[0][...]
    sin = extras[1][...]
    lane = lax.broadcasted_iota(jnp.int32, (1, LANES), 1)
    first = (lane % 32) < 16
    for t in range(acc.shape[1] // LANES):
        x = acc[:, t * LANES:(t + 1) * LANES]
        partner = jnp.where(first, pltpu.roll(x, LANES - 16, 1), pltpu.roll(x, 16, 1))
        y = x * cos + partner * sin
        if scale != 1.0:
            y = y * scale
        outs[0][:, c0 + t * LANES:c0 + (t + 1) * LANES] = y.astype(BF16)


def _proj_simple(h, w, col0, ncols, epilogue, name, tm, tn=PROJ_TN):
    m = h.shape[0]
    return _proj(h, w, col0, ncols, tm, tn, epilogue, (), (),
                 jax.ShapeDtypeStruct((m, ncols), BF16),
                 pl.BlockSpec((tm, tn), lambda i, j: (i, j)), name)


def _proj_rope(h, w, col0, ncols, scale, cos_t, sin_t, seq, name, tm, tn=PROJ_TN):
    m = h.shape[0]
    per_batch = seq // tm
    tab_spec = pl.BlockSpec((tm, LANES), lambda i, j: (i % per_batch, 0))
    return _proj(h, w, col0, ncols, tm, tn, functools.partial(_epi_rope, scale),
                 (cos_t, sin_t), (tab_spec, tab_spec),
                 jax.ShapeDtypeStruct((m, ncols), BF16),
                 pl.BlockSpec((tm, tn), lambda i, j: (i, j)), name)


def _proj_silu_slabs(h, w, col0, name, tm, tn=PROJ_TN):
    m = h.shape[0]
    return _proj(h, w, col0, D_BRANCH, tm, tn, _epi_silu_slabs, (), (),
                 jax.ShapeDtypeStruct((SLABS, m, LANES), BF16),
                 pl.BlockSpec((tn // LANES, tm, LANES), lambda i, j: (j, i, 0)), name)


def _proj_channel_dft(h, w, cs, name, tm, seq=None, tn=PROJ_TN):
    m = h.shape[0]
    lhs_spec = None
    if seq is not None:
        na, tiles = seq // DFT_NB, DFT_NB // DFT_ROWS
        assert tm == na * DFT_ROWS
        h = h.reshape(m // seq, na, DFT_NB, D_MODEL)
        lhs_spec = pl.BlockSpec((None, na, DFT_ROWS, D_MODEL), lambda i, j: (i // tiles, 0, i % tiles, 0))
    slab_shape = jax.ShapeDtypeStruct((SLABS, m, LANES), F32)
    slab_spec = pl.BlockSpec((tn // LANES, tm, LANES), lambda i, j: (j, i, 0))
    groups = tn // FNET_GROUP_DIM
    cs_spec = pl.BlockSpec((groups,) + cs.shape[1:], lambda i, j: (j, 0, 0))
    return _proj(h, w, 0, D_BRANCH, tm, tn, functools.partial(_epi_channel_dft, seq is not None), (cs,), (cs_spec,),
                 (slab_shape, slab_shape), (slab_spec, slab_spec), name, lhs_spec=lhs_spec,
                 chunk=2 * MXU_COLS)


def _fold_mix_kernel(c_ref, s_ref, wm_ref, o_ref):
    wm = wm_ref[...]
    gd = FNET_GROUP_DIM
    hp = lax.Precision.HIGHEST
    o_ref[:, 0:gd] = jnp.dot(c_ref[...], wm, preferred_element_type=F32, precision=hp).astype(BF16)
    o_ref[:, gd:2 * gd] = jnp.dot(s_ref[...], wm, preferred_element_type=F32, precision=hp).astype(BF16)


def _fold_mix(w_mix):
    gd = FNET_GROUP_DIM
    cs = _channel_dft_matrix()
    mat = pl.BlockSpec((gd, gd), lambda g: (0, 0))
    return pl.pallas_call(
        _fold_mix_kernel,
        grid=(FNET_GROUPS,),
        in_specs=[mat, mat, pl.BlockSpec((None, gd, gd), lambda g: (g, 0, 0))],
        out_specs=pl.BlockSpec((None, gd, 2 * gd), lambda g: (g, 0, 0)),
        out_shape=jax.ShapeDtypeStruct((FNET_GROUPS, gd, 2 * gd), BF16),
        compiler_params=_cparams(1),
        name="fnet_fold_mix",
    )(jnp.asarray(cs[:, :gd]), jnp.asarray(cs[:, gd:]), w_mix)


def _dft_kernel(na, pitch, gpitch, ar_ref, ai_ref, zg_ref, fa_ref, fb_ref, y_ref, er_ref, ei_ref, g_ref):
    half_rows = na * (DFT_ROWS // 2)

    def gather_a(b):
        start = (b // (DFT_ROWS // 2)) * half_rows + b % (DFT_ROWS // 2)
        zr = ar_ref[pl.ds(start, na, stride=DFT_ROWS // 2), :]
        zi = ai_ref[pl.ds(start, na, stride=DFT_ROWS // 2), :]
        return jnp.concatenate([zr, zi], axis=0).astype(BF16)

    def stage_a(pair, carry):
        b0 = 2 * pair
        d0, d1 = gather_a(b0), gather_a(b0 + 1)
        zero = jnp.zeros_like(d0)
        rhs = jnp.concatenate([jnp.concatenate([d0, zero], axis=1),
                               jnp.concatenate([zero, d1], axis=1)], axis=0)
        e = jnp.dot(fa_ref[pair], rhs, preferred_element_type=F32)
        for j in range(2):
            off = pl.multiple_of((b0 + j) * pitch, 8)
            er_ref[pl.ds(off, na), :] = e[:na, j * LANES:(j + 1) * LANES]
            ei_ref[pl.ds(off, na), :] = e[na:, j * LANES:(j + 1) * LANES]
        return carry

    lax.fori_loop(0, DFT_NB // 2, stage_a, 0, unroll=16)

    def gather_b(ka):
        er = er_ref[pl.ds(ka, DFT_NB, stride=pitch), :]
        ei = ei_ref[pl.ds(ka, DFT_NB, stride=pitch), :]
        return jnp.concatenate([er, ei], axis=0).astype(BF16)

    def stage_b(pair, carry):
        ka0 = 2 * pair
        rhs = jnp.concatenate([gather_b(ka0), gather_b(ka0 + 1)], axis=1)
        g = jnp.dot(fb_ref[...], rhs, preferred_element_type=F32)
        for j in range(2):
            off = pl.multiple_of((ka0 + j) * gpitch, 8)
            g_ref[pl.ds(off, DFT_NB), :] = g[:, j * LANES:(j + 1) * LANES]
        return carry

    lax.fori_loop(0, na // 2, stage_b, 0, unroll=8)

    def gate(kb, carry):
        rows = pl.ds(pl.multiple_of(kb * na, na), na)
        g = g_ref[pl.ds(kb, na, stride=gpitch), :]
        y_ref[rows, :] = (g * zg_ref[rows, :].astype(F32)).astype(BF16)
        return carry

    lax.fori_loop(0, DFT_NB, gate, 0, unroll=16)


def _position_dft(ar, ai, zg, batch, seq):
    na = seq // DFT_NB
    pitch = na + 8
    gpitch = DFT_NB + 8
    fa_np, fb_np = _position_dft_matrices(seq)
    fa = jnp.asarray(fa_np).astype(BF16)
    fb = jnp.asarray(fb_np).astype(BF16)
    slab = pl.BlockSpec((None, seq, LANES), lambda s, b: (s, b, 0))
    return pl.pallas_call(
        functools.partial(_dft_kernel, na, pitch, gpitch),
        grid=(SLABS, batch),
        in_specs=[slab, slab, slab,
                  pl.BlockSpec(fa.shape, lambda s, b: (0, 0, 0)),
                  pl.BlockSpec(fb.shape, lambda s, b: (0, 0))],
        out_specs=slab,
        out_shape=jax.ShapeDtypeStruct(ar.shape, BF16),
        scratch_shapes=[pltpu.VMEM((DFT_NB * pitch, LANES), F32),
                        pltpu.VMEM((DFT_NB * pitch, LANES), F32),
                        pltpu.VMEM((na * gpitch, LANES), F32)],
        compiler_params=_cparams(2),
        name="position_dft",
    )(ar, ai, zg, fa, fb)


def _dft_dense_kernel(ar_ref, ai_ref, zg_ref, fd_ref, y_ref):
    d = jnp.concatenate([ar_ref[...], ai_ref[...]], axis=0).astype(BF16)
    g = jnp.dot(fd_ref[...], d, preferred_element_type=F32)
    y_ref[...] = (g * zg_ref[...].astype(F32)).astype(BF16)


def _position_dft_dense(ar, ai, zg, batch, seq):
    fd = jnp.asarray(_dense_dft_matrix(seq)).astype(BF16)
    slab = pl.BlockSpec((None, seq, LANES), lambda s, b: (s, b, 0))
    return pl.pallas_call(
        _dft_dense_kernel,
        grid=(SLABS, batch),
        in_specs=[slab, slab, slab, pl.BlockSpec(fd.shape, lambda s, b: (0, 0))],
        out_specs=slab,
        out_shape=jax.ShapeDtypeStruct(ar.shape, BF16),
        compiler_params=_cparams(2),
        name="position_dft_dense",
    )(ar, ai, zg, fd)


def _attn_kernel(n_band, nblk, sink_ref, q_ref, *refs):
    k_refs = refs[:n_band + 1]
    v_refs = refs[n_band + 1:2 * n_band + 2]
    zg_ref, y_ref = refs[2 * n_band + 2:]
    h = pl.program_id(1)
    i = pl.program_id(2)
    blk = ATTN_BLOCK
    n_pair = Q_GROUP // 2

    lane = lax.broadcasted_iota(jnp.int32, (1, LANES), 1)
    lo = lane < HEAD_DIM
    keys = jnp.concatenate([r[...] for r in k_refs], axis=0)
    vals = jnp.concatenate([r[...] for r in v_refs], axis=0)
    ones = jnp.ones_like(vals)
    v_ext = (jnp.where(lo, vals, ones), jnp.where(lo, ones, vals))
    dn = (((1,), (1,)), ((), ()))
    if n_band:
        r = lax.broadcasted_iota(jnp.int32, (blk, blk), 0)
        c = lax.broadcasted_iota(jnp.int32, (blk, blk), 1)
        prev_ok = c >= r + jnp.where(i > 0, 0, blk)
        next_ok = c <= r - jnp.where(i < nblk - 1, 0, blk)

    def scores(t):
        q = q_ref[:, t * LANES:(t + 1) * LANES]
        qzero = jnp.zeros_like(q)
        out = []
        for qm in (jnp.where(lo, q, qzero), jnp.where(lo, qzero, q)):
            s = lax.dot_general(qm, keys, dn, preferred_element_type=F32)
            parts = [s[:, j * blk:(j + 1) * blk] for j in range(s.shape[1] // blk)]
            if n_band:
                parts[0] = jnp.where(prev_ok, parts[0], NEG_INF)
                parts[2] = jnp.where(next_ok, parts[2], NEG_INF)
            out.append(parts)
        return out

    def finish(t, both):
        tile = slice(t * LANES, (t + 1) * LANES)
        o_ext, sink_term = [], []
        for hd, parts in enumerate(both):
            sk = sink_ref[h, 2 * t + hd] * LOG2_E
            mx = parts[0]
            for part in parts[1:]:
                mx = jnp.maximum(mx, part)
            mx = jnp.maximum(jnp.max(mx, axis=-1, keepdims=True), sk)
            p = jnp.concatenate([jnp.exp2(part - mx).astype(BF16) for part in parts], axis=1)
            o_ext.append(jnp.dot(p, v_ext[hd], preferred_element_type=F32))
            sink_term.append(jnp.exp2(sk - mx))
        o = jnp.where(lo, o_ext[0], o_ext[1])
        denom = jnp.where(lo, pltpu.roll(o_ext[0], HEAD_DIM, 1) + sink_term[0],
                          pltpu.roll(o_ext[1], HEAD_DIM, 1) + sink_term[1])
        y_ref[:, tile] = (o / denom * zg_ref[:, tile].astype(F32)).astype(BF16)

    pending = scores(0)
    for t in range(n_pair):
        upcoming = scores(t + 1) if t + 1 < n_pair else None
        finish(t, pending)
        pending = upcoming


def _attention(q, kd, vd, kcd, vcd, zg, sink, batch, seq, ctx_len, use_band):
    blk = ATTN_BLOCK
    nblk = seq // blk
    qw = Q_GROUP * HEAD_DIM
    q_spec = pl.BlockSpec((blk, qw), lambda b, h, i: (b * nblk + i, h))
    ctx_spec = pl.BlockSpec((ctx_len, LANES), lambda b, h, i: (b, h))
    if use_band:
        def band(delta):
            return pl.BlockSpec(
                (blk, LANES), lambda b, h, i: (b * nblk + jnp.clip(i + delta, 0, nblk - 1), h))
        k_specs = [band(-1), band(0), band(1), ctx_spec]
        k_args, v_args = [kd, kd, kd, kcd], [vd, vd, vd, vcd]
        n_band = 3
    else:
        k_specs, k_args, v_args, n_band = [ctx_spec], [kcd], [vcd], 0
    return pl.pallas_call(
        functools.partial(_attn_kernel, n_band, nblk),
        grid=(batch, KV_HEADS, nblk),
        in_specs=[pl.BlockSpec(memory_space=pltpu.SMEM), q_spec] + k_specs + k_specs + [q_spec],
        out_specs=q_spec,
        out_shape=jax.ShapeDtypeStruct(q.shape, BF16),
        compiler_params=_cparams(3),
        name="attention_band" if use_band else "attention_ctx",
    )(sink, q, *k_args, *v_args, zg)


def _gelu_stats_kernel(chunk, a_ref, w_ref, gv_ref, mu_ref, rstd_ref):
    a = a_ref[...]
    n = w_ref.shape[1]
    mean = m2 = None
    for idx, c0 in enumerate(range(0, n, chunk)):
        ge = _gelu_tanh(jnp.dot(a, w_ref[:, c0:c0 + chunk], preferred_element_type=F32))
        gv_ref[:, c0:c0 + chunk] = ge.astype(BF16)
        cmean = jnp.mean(ge, axis=-1, keepdims=True)
        d = ge - cmean
        cm2 = jnp.sum(d * d, axis=-1, keepdims=True)
        if idx == 0:
            mean, m2 = cmean, cm2
        else:
            delta = cmean - mean
            mean = mean + delta * (1.0 / (idx + 1))
            m2 = m2 + cm2 + delta * delta * (chunk * idx / (idx + 1))
    mu_ref[...] = mean
    rstd_ref[...] = lax.rsqrt(m2 * (1.0 / n) + EPS)


def _proj_gelu_stats(h, w, col0, tm, chunk=2 * MXU_COLS):
    m, k = h.shape
    n = D_BRANCH
    col = pl.BlockSpec((tm, 1), lambda i: (i, 0))
    stat = jax.ShapeDtypeStruct((m, 1), F32)
    return pl.pallas_call(
        functools.partial(_gelu_stats_kernel, chunk),
        grid=(m // tm,),
        in_specs=[pl.BlockSpec((tm, k), lambda i: (i, 0)),
                  pl.BlockSpec((k, n), lambda i: (0, col0 // n), pipeline_mode=pl.Buffered(1))],
        out_specs=(pl.BlockSpec((tm, n), lambda i: (i, 0)), col, col),
        out_shape=(jax.ShapeDtypeStruct((m, n), BF16), stat, stat),
        compiler_params=_cparams(1),
        name="gmlp_in_v",
    )(h, w)


def _sgu_kernel(n_chunks, n_groups, gu_ref, gv_ref, zg_ref, mu_ref, rstd_ref, lg_ref, lb_ref,
                ws_ref, bs_ref, y_ref):
    gd = D_BRANCH // GMLP_GROUPS
    mu = mu_ref[...]
    rstd = rstd_ref[...]
    for gl in range(n_groups):
        cols = slice(gl * gd, (gl + 1) * gd)
        ws = ws_ref[gl]
        bs = bs_ref[gl]
        lg = lg_ref[:, cols]
        lb = lb_ref[:, cols]
        for c in range(n_chunks):
            rows = slice(c * GMLP_CHUNK, (c + 1) * GMLP_CHUNK)
            vn = ((gv_ref[rows, cols].astype(F32) - mu[rows]) * rstd[rows] * lg + lb).astype(BF16)
            s = jnp.dot(ws, vn, preferred_element_type=F32) + bs
            y_ref[rows, cols] = (gu_ref[rows, cols].astype(F32) * s
                                 * zg_ref[rows, cols].astype(F32)).astype(BF16)


def _spatial_gate(gu, gv, zg, mu, rstd, ln_g, ln_b, w_s, b_s, tm, n_groups=2):
    m = gu.shape[0]
    gd = D_BRANCH // GMLP_GROUPS
    tn = n_groups * gd
    tile = pl.BlockSpec((tm, tn), lambda r, g: (r, g))
    col = pl.BlockSpec((tm, 1), lambda r, g: (r, 0))
    vec = pl.BlockSpec((1, tn), lambda r, g: (0, g))
    return pl.pallas_call(
        functools.partial(_sgu_kernel, tm // GMLP_CHUNK, n_groups),
        grid=(m // tm, GMLP_GROUPS // n_groups),
        in_specs=[tile, tile, tile, col, col, vec, vec,
                  pl.BlockSpec((n_groups, GMLP_CHUNK, GMLP_CHUNK), lambda r, g: (g, 0, 0)),
                  pl.BlockSpec((n_groups, GMLP_CHUNK, 1), lambda r, g: (g, 0, 0))],
        out_specs=tile,
        out_shape=jax.ShapeDtypeStruct(gu.shape, BF16),
        compiler_params=_cparams(2),
        name="gmlp_spatial_gate",
    )(gu, gv, zg, mu, rstd, ln_g.reshape(1, D_BRANCH), ln_b.reshape(1, D_BRANCH),
      w_s, b_s.reshape(GMLP_GROUPS, GMLP_CHUNK, 1))


def _wout_kernel(mode, y_ref, w_ref, x_ref, gate_ref, *refs):
    if len(y_ref.shape) == 3:
        y = jnp.concatenate([y_ref[s] for s in range(y_ref.shape[0])], axis=1)
    else:
        y = y_ref[...]
    acc = jnp.dot(y, w_ref[...], preferred_element_type=F32)
    xn = x_ref[...] + gate_ref[...] * acc
    if mode == "final":
        g_ref, o_ref = refs
        o_ref[...] = xn * lax.rsqrt(jnp.mean(xn * xn, axis=-1, keepdims=True) + EPS) * g_ref[...]
    elif mode == "next":
        g_ref, sc_ref, sh_ref, xo_ref, h_ref = refs
        xo_ref[...] = xn
        h_ref[...] = _mod_rmsnorm(xn, g_ref[...], sc_ref[...], sh_ref[...]).astype(BF16)
    else:
        refs[0][...] = xn


def _out_proj(y, w_out, x2, mods, layer, who, mode, norm_vec, tm):
    m = x2.shape[0]
    row = pl.BlockSpec((tm, D_MODEL), lambda r: (r, 0))
    if y.ndim == 3:
        y_spec = pl.BlockSpec((SLABS, tm, LANES), lambda r: (0, r, 0))
    else:
        y_spec = pl.BlockSpec((tm, D_BRANCH), lambda r: (r, 0))
    in_specs = [y_spec,
                pl.BlockSpec((None, D_BRANCH, D_MODEL), lambda r: (layer, 0, 0),
                             pipeline_mode=pl.Buffered(1)),
                row, _mod_spec(layer, 2, who)]
    args = [y, w_out, x2, mods]
    xs = jax.ShapeDtypeStruct((m, D_MODEL), F32)
    if mode == "final":
        in_specs.append(pl.BlockSpec((1, D_MODEL), lambda r: (0, 0)))
        args.append(norm_vec.reshape(1, D_MODEL))
        out_shape, out_specs = xs, row
    elif mode == "next":
        in_specs += [_row_spec(layer + 1), _mod_spec(layer + 1, 1, who), _mod_spec(layer + 1, 0, who)]
        args += [norm_vec, mods, mods]
        out_shape = (xs, jax.ShapeDtypeStruct((m, D_MODEL), BF16))
        out_specs = (row, row)
    else:
        out_shape, out_specs = xs, row
    return pl.pallas_call(
        functools.partial(_wout_kernel, mode),
        grid=(m // tm,),
        in_specs=in_specs,
        out_specs=out_specs,
        out_shape=out_shape,
        compiler_params=_cparams(1),
        name="out_proj_" + mode,
    )(*args)


def kernel(x, c, ctx, c_ctx, norm_g, ada_w, ada_b, w_out, fnet_w_in, fnet_w_mix, attn_w_in, attn_sink,
           gmlp_w_in, gmlp_w_s, gmlp_b_s, gmlp_ln_g, gmlp_ln_b, final_g):
    batch, seq, d = x.shape
    ctx_len = ctx.shape[1]
    assert d == D_MODEL and seq % (DFT_NB * 8) == 0 and seq % GRID_W == 0 and batch < 8
    m_lat, m_ctx = batch * seq, batch * ctx_len
    tm_lat = min(1024, seq)
    tm_ctx = ctx_len
    tm_out = 256

    def who_lat(tm):
        return lambda r: (r * tm) // seq

    who_ctx = lambda r: batch

    cvec = jnp.zeros((8, d), F32).at[:batch].set(c).at[batch].set(c_ctx)
    mods = _mods(cvec, ada_w, ada_b).reshape(DEPTH, 8, 3, 1, d)
    norm_g3 = norm_g.reshape(DEPTH, 1, d)
    w_out_b = w_out.astype(BF16)

    xl = x.reshape(m_lat, d)
    xc = ctx.reshape(m_ctx, d)
    hl = _prenorm(xl, norm_g3, mods, 0, who_lat(512), 512)
    hc = _prenorm(xc, norm_g3, mods, 0, who_ctx, tm_ctx)

    cos_np, sin_np = _rope_tables(seq)
    cos_t, sin_t = jnp.asarray(cos_np), jnp.asarray(sin_np)

    def finish(y, x2, layer, who_fn, need_next, is_final):
        if is_final:
            return _out_proj(y, w_out_b, x2, mods, layer, who_fn, "final", final_g, tm_out), None
        if need_next:
            return _out_proj(y, w_out_b, x2, mods, layer, who_fn, "next", norm_g3, tm_out)
        return _out_proj(y, w_out_b, x2, mods, layer, who_fn, "plain", None, tm_out), None

    out = None
    for i in range(DEPTH):
        kind, j = i % 3, i // 3
        need_ctx = i < DEPTH - 1
        last = i == DEPTH - 1
        if kind == 0:
            w = fnet_w_in[j].astype(BF16)
            cs = _fold_mix(fnet_w_mix[j])
            tm_dft = (seq // DFT_NB) * DFT_ROWS
            ar, ai = _proj_channel_dft(hl, w, cs, "fnet_in_u", tm_dft, seq=seq)
            zg = _proj_silu_slabs(hl, w, D_BRANCH, "fnet_in_z", tm_lat)
            y = _position_dft(ar, ai, zg, batch, seq)
            yc = None
            if need_ctx:
                ar, ai = _proj_channel_dft(hc, w, cs, "fnet_in_u_ctx", tm_ctx)
                zg = _proj_silu_slabs(hc, w, D_BRANCH, "fnet_in_z_ctx", tm_ctx)
                yc = _position_dft_dense(ar, ai, zg, batch, ctx_len)
        elif kind == 1:
            w = attn_w_in[j]
            kvw = KV_HEADS * HEAD_DIM
            wq = w[:, :D_BRANCH].astype(BF16)
            wz = w[:, D_BRANCH + 2 * kvw:].astype(BF16)

            def dup(wpart):
                w3 = wpart.reshape(d, KV_HEADS, 1, HEAD_DIM)
                return jnp.broadcast_to(w3, (d, KV_HEADS, 2, HEAD_DIM)).reshape(d, 2 * kvw).astype(BF16)

            wk = dup(w[:, D_BRANCH:D_BRANCH + kvw])
            wv = dup(w[:, D_BRANCH + kvw:D_BRANCH + 2 * kvw])
            sink = attn_sink[j].reshape(KV_HEADS, Q_GROUP)
            scale = HEAD_DIM ** -0.5 * LOG2_E
            q = _proj_rope(hl, wq, 0, D_BRANCH, scale, cos_t, sin_t, seq, "attn_in_q", tm_lat)
            kd = _proj_rope(hl, wk, 0, 2 * kvw, 1.0, cos_t, sin_t, seq, "attn_in_k", tm_lat)
            vd = _proj_simple(hl, wv, 0, 2 * kvw, _epi_cast, "attn_in_v", tm_lat)
            zg = _proj_simple(hl, wz, 0, D_BRANCH, _epi_silu, "attn_in_z", tm_lat)
            kcd = _proj_simple(hc, wk, 0, 2 * kvw, _epi_cast, "attn_in_kc", tm_ctx)
            vcd = _proj_simple(hc, wv, 0, 2 * kvw, _epi_cast, "attn_in_vc", tm_ctx)
            y = _attention(q, kd, vd, kcd, vcd, zg, sink, batch, seq, ctx_len, True)
            yc = None
            if need_ctx:
                qc = _proj_simple(hc, wq, 0, D_BRANCH, functools.partial(_epi_scale_cast, scale),
                                  "attn_in_qc", tm_ctx)
                zgc = _proj_simple(hc, wz, 0, D_BRANCH, _epi_silu, "attn_in_zc", tm_ctx)
                yc = _attention(qc, None, None, kcd, vcd, zgc, sink, batch, ctx_len, ctx_len, False)
        else:
            w = gmlp_w_in[j].astype(BF16)
            ws = gmlp_w_s[j].astype(BF16)

            def gmlp_branch(h, tm):
                gu = _proj_simple(h, w, 0, D_BRANCH, _epi_gelu, "gmlp_in_u", tm)
                gv, mu, rstd = _proj_gelu_stats(h, w, D_BRANCH, min(tm, 512))
                zg = _proj_simple(h, w, 2 * D_BRANCH, D_BRANCH, _epi_silu, "gmlp_in_z", tm)
                return _spatial_gate(gu, gv, zg, mu, rstd, gmlp_ln_g[j], gmlp_ln_b[j], ws, gmlp_b_s[j], tm)

            y = gmlp_branch(hl, tm_lat)
            yc = gmlp_branch(hc, tm_ctx) if need_ctx else None

        res, hl = finish(y, xl, i, who_lat(tm_out), not last, last)
        if last:
            out = res
        else:
            xl = res
        if need_ctx:
            xc, hc = finish(yc, xc, i, who_ctx, i + 1 < DEPTH - 1 or (i + 1) % 3 == 1, False)
    return out.reshape(batch, seq, d)
```

```python
import functools
import math

import numpy as np
import jax
import jax.numpy as jnp
from jax import lax
from jax.experimental import pallas as pl
from jax.experimental.pallas import tpu as pltpu

F32 = jnp.float32
BF16 = jnp.bfloat16

D_MODEL = 2048
D_BRANCH = 4096
DEPTH = 4
GRID_W = 64
FNET_GROUPS = 16
FNET_GROUP_DIM = 256
HEAD_DIM = 64
KV_HEADS = 8
Q_GROUP = 8
ATTN_BLOCK = 128
ROPE_BASE = 10000.0
GMLP_CHUNK = 128
GMLP_GROUPS = 16
EPS = 1e-6
NEG_INF = -1e30
LOG2_E = math.log2(math.e)

LANES = 128
MXU_COLS = 256
PROJ_TN = 1024
DFT_NB = 128
DFT_ROWS = 16
SLABS = D_BRANCH // LANES
VMEM_LIMIT = 56 * 1024 * 1024


def _cparams(n_axes, vmem=VMEM_LIMIT):
    return pltpu.CompilerParams(dimension_semantics=("arbitrary",) * n_axes,
                                vmem_limit_bytes=vmem)


def _silu(z):
    return z / (1.0 + jnp.exp(-z))


def _gelu_tanh(x):
    c = math.sqrt(2.0 / math.pi)
    return 0.5 * x * (1.0 + jnp.tanh(c * (x + 0.044715 * (x * x * x))))


def _mod_rmsnorm(x, g, scale, shift):
    y = x * lax.rsqrt(jnp.mean(x * x, axis=-1, keepdims=True) + EPS) * g
    return y * (1.0 + scale) + shift


def _channel_dft_matrix():
    n = FNET_GROUP_DIM
    k = np.arange(n, dtype=np.float64)
    ang = 2.0 * np.pi * np.outer(k, k) / n
    s = 1.0 / math.sqrt(n)
    return np.concatenate([np.cos(ang) * s, -np.sin(ang) * s], axis=1).astype(np.float32)


def _position_dft_matrices(seq):
    na, nb = seq // DFT_NB, DFT_NB
    a = np.arange(na, dtype=np.float64)
    b = np.arange(nb, dtype=np.float64)
    ang = 2.0 * np.pi * (a[None, None, :] * a[None, :, None] / na + b[:, None, None] * a[None, :, None] / seq)
    mr = np.cos(ang) / math.sqrt(na)
    mi = -np.sin(ang) / math.sqrt(na)
    fa = np.concatenate([np.concatenate([mr, -mi], axis=2), np.concatenate([mi, mr], axis=2)], axis=1)
    fa = np.concatenate([fa[0::2], fa[1::2]], axis=2)
    angb = 2.0 * np.pi * np.outer(b, b) / nb
    fb = np.concatenate([np.cos(angb), np.sin(angb)], axis=1) / math.sqrt(nb)
    return fa.astype(np.float32), fb.astype(np.float32)


def _dense_dft_matrix(n):
    k = np.arange(n, dtype=np.float64)
    ang = 2.0 * np.pi * np.outer(k, k) / n
    return (np.concatenate([np.cos(ang), np.sin(ang)], axis=1) / math.sqrt(n)).astype(np.float32)


def _rope_tables(seq):
    nf = HEAD_DIM // 4
    inv = ROPE_BASE ** (-np.arange(nf, dtype=np.float64) / nf)
    t = np.arange(seq)
    rows = (t // GRID_W).astype(np.float64)
    cols = (t % GRID_W).astype(np.float64)
    parts_c, parts_s = [], []
    for pos in (rows, cols):
        ang = pos[:, None] * inv[None, :]
        parts_c += [np.cos(ang), np.cos(ang)]
        parts_s += [-np.sin(ang), np.sin(ang)]
    cos = np.concatenate(parts_c, axis=1)
    sin = np.concatenate(parts_s, axis=1)
    reps = LANES // HEAD_DIM
    return (np.tile(cos, (1, reps)).astype(np.float32), np.tile(sin, (1, reps)).astype(np.float32))


def _mods_kernel(cv_ref, w_ref, b_ref, o_ref):
    a = _silu(cv_ref[...])
    o_ref[...] = jnp.dot(a, w_ref[...], preferred_element_type=F32,
                         precision=lax.Precision.HIGHEST) + b_ref[...]


def _mods(cvec, ada_w, ada_b):
    depth, d, n3 = ada_w.shape
    tn = 1024
    return pl.pallas_call(
        _mods_kernel,
        grid=(depth, n3 // tn),
        in_specs=[pl.BlockSpec((8, d), lambda i, j: (0, 0)),
                  pl.BlockSpec((None, d, tn), lambda i, j: (i, 0, j)),
                  pl.BlockSpec((None, 1, tn), lambda i, j: (i, 0, j))],
        out_specs=pl.BlockSpec((None, 8, tn), lambda i, j: (i, 0, j)),
        out_shape=jax.ShapeDtypeStruct((depth, 8, n3), F32),
        compiler_params=_cparams(2),
        name="ada_mods",
    )(cvec, ada_w, ada_b.reshape(depth, 1, n3))


def _mod_spec(layer, kind, who_of_row):
    return pl.BlockSpec((None, None, None, 1, D_MODEL),
                        lambda r, *_: (layer, who_of_row(r), kind, 0, 0))


def _row_spec(vec_layer):
    return pl.BlockSpec((None, 1, D_MODEL), lambda r, *_: (vec_layer, 0, 0))


def _prenorm_kernel(x_ref, g_ref, sc_ref, sh_ref, h_ref):
    h_ref[...] = _mod_rmsnorm(x_ref[...], g_ref[...], sc_ref[...], sh_ref[...]).astype(BF16)


def _prenorm(x2, norm_g3, mods, layer, who, tm):
    m = x2.shape[0]
    return pl.pallas_call(
        _prenorm_kernel,
        grid=(m // tm,),
        in_specs=[pl.BlockSpec((tm, D_MODEL), lambda r: (r, 0)),
                  _row_spec(layer), _mod_spec(layer, 1, who), _mod_spec(layer, 0, who)],
        out_specs=pl.BlockSpec((tm, D_MODEL), lambda r: (r, 0)),
        out_shape=jax.ShapeDtypeStruct((m, D_MODEL), BF16),
        compiler_params=_cparams(1),
        name="prenorm",
    )(x2, norm_g3, mods, mods)


def _proj_kernel(epilogue, n_extra, chunk, a_ref, w_ref, *rest):
    if len(a_ref.shape) == 3:
        a_flat = rest[-1]
        rest = rest[:-1]

        @pl.when(pl.program_id(1) == 0)
        def _():
            a_flat[...] = a_ref[...].reshape(a_flat.shape)

        a = a_flat[...]
    else:
        a = a_ref[...]
    for c0 in range(0, w_ref.shape[1], chunk):
        acc = jnp.dot(a, w_ref[:, c0:c0 + chunk].astype(BF16), preferred_element_type=F32)
        epilogue(acc, c0, rest[:n_extra], rest[n_extra:])


def _proj(h, w, col0, ncols, tm, tn, epilogue, extras, extra_specs, out_shapes, out_specs, name,
          w_single_buffer=False, lhs_spec=None, chunk=MXU_COLS):
    k = h.shape[-1]
    m = h.size // k
    off = col0 // tn
    w_kwargs = dict(pipeline_mode=pl.Buffered(1)) if w_single_buffer else {}
    scratch = [pltpu.VMEM((tm, k), h.dtype)] if lhs_spec is not None else []
    if lhs_spec is None:
        lhs_spec = pl.BlockSpec((tm, k), lambda i, j: (i, 0))
    return pl.pallas_call(
        functools.partial(_proj_kernel, epilogue, len(extras), min(chunk, tn)),
        grid=(m // tm, ncols // tn),
        in_specs=[lhs_spec,
                  pl.BlockSpec((k, tn), lambda i, j: (0, j + off), **w_kwargs)] + list(extra_specs),
        out_specs=out_specs,
        out_shape=out_shapes,
        scratch_shapes=scratch,
        compiler_params=_cparams(2),
        name=name,
    )(h, w, *extras)


def _epi_silu(acc, c0, extras, outs):
    outs[0][:, c0:c0 + acc.shape[1]] = _silu(acc).astype(BF16)


def _epi_silu_slabs(acc, c0, extras, outs):
    for t in range(acc.shape[1] // LANES):
        outs[0][c0 // LANES + t] = _silu(acc[:, t * LANES:(t + 1) * LANES]).astype(BF16)


def _epi_gelu(acc, c0, extras, outs):
    outs[0][:, c0:c0 + acc.shape[1]] = _gelu_tanh(acc).astype(BF16)


def _epi_cast(acc, c0, extras, outs):
    outs[0][:, c0:c0 + acc.shape[1]] = acc.astype(BF16)


def _epi_scale_cast(scale, acc, c0, extras, outs):
    outs[0][:, c0:c0 + acc.shape[1]] = (acc * scale).astype(BF16)


def _epi_channel_dft(split_rows, acc, c0, extras, outs):
    cs_ref = extras[0]
    ar_ref, ai_ref = outs
    tm = acc.shape[0]
    for gl in range(acc.shape[1] // FNET_GROUP_DIM):
        g = c0 // FNET_GROUP_DIM + gl
        ub = acc[:, gl * FNET_GROUP_DIM:(gl + 1) * FNET_GROUP_DIM].astype(BF16)
        ab = jnp.dot(ub, cs_ref[g], preferred_element_type=F32)
        if split_rows:
            ab4 = ab.reshape(tm // DFT_ROWS, 2, DFT_ROWS // 2, ab.shape[1])
            ab = jnp.concatenate([ab4[:, hf].reshape(tm // 2, ab.shape[1]) for hf in range(2)], axis=0)
        ar_ref[2 * g] = ab[:, 0:128]
        ar_ref[2 * g + 1] = ab[:, 128:256]
        ai_ref[2 * g] = ab[:, 256:384]
        ai_ref[2 * g + 1] = ab[:, 384:512]


def _epi_rope(scale, acc, c0, extras, outs):
    cos = extras[0][...]
    sin = extras[1][...]
    lane = lax.broadcasted_iota(jnp.int32, (1, LANES), 1)
    first = (lane % 32) < 16
    for t in range(acc.shape[1] // LANES):
        x = acc[:, t * LANES:(t + 1) * LANES]
        partner = jnp.where(first, pltpu.roll(x, LANES - 16, 1), pltpu.roll(x, 16, 1))
        y = x * cos + partner * sin
        if scale != 1.0:
            y = y * scale
        outs[0][:, c0 + t * LANES:c0 + (t + 1) * LANES] = y.astype(BF16)


def _proj_simple(h, w, col0, ncols, epilogue, name, tm, tn=PROJ_TN):
    m = h.shape[0]
    return _proj(h, w, col0, ncols, tm, tn, epilogue, (), (),
                 jax.ShapeDtypeStruct((m, ncols), BF16),
                 pl.BlockSpec((tm, tn), lambda i, j: (i, j)), name)


def _proj_rope(h, w, col0, ncols, scale, cos_t, sin_t, seq, name, tm, tn=PROJ_TN):
    m = h.shape[0]
    per_batch = seq // tm
    tab_spec = pl.BlockSpec((tm, LANES), lambda i, j: (i % per_batch, 0))
    return _proj(h, w, col0, ncols, tm, tn, functools.partial(_epi_rope, scale),
                 (cos_t, sin_t), (tab_spec, tab_spec),
                 jax.ShapeDtypeStruct((m, ncols), BF16),
                 pl.BlockSpec((tm, tn), lambda i, j: (i, j)), name)


def _proj_silu_slabs(h, w, col0, name, tm, tn=PROJ_TN):
    m = h.shape[0]
    return _proj(h, w, col0, D_BRANCH, tm, tn, _epi_silu_slabs, (), (),
                 jax.ShapeDtypeStruct((SLABS, m, LANES), BF16),
                 pl.BlockSpec((tn // LANES, tm, LANES), lambda i, j: (j, i, 0)), name)


def _proj_channel_dft(h, w, cs, name, tm, seq=None, tn=PROJ_TN):
    m = h.shape[0]
    lhs_spec = None
    if seq is not None:
        na, tiles = seq // DFT_NB, DFT_NB // DFT_ROWS
        assert tm == na * DFT_ROWS
        h = h.reshape(m // seq, na, DFT_NB, D_MODEL)
        lhs_spec = pl.BlockSpec((None, na, DFT_ROWS, D_MODEL), lambda i, j: (i // tiles, 0, i % tiles, 0))
    slab_shape = jax.ShapeDtypeStruct((SLABS, m, LANES), F32)
    slab_spec = pl.BlockSpec((tn // LANES, tm, LANES), lambda i, j: (j, i, 0))
    groups = tn // FNET_GROUP_DIM
    cs_spec = pl.BlockSpec((groups,) + cs.shape[1:], lambda i, j: (j, 0, 0))
    return _proj(h, w, 0, D_BRANCH, tm, tn, functools.partial(_epi_channel_dft, seq is not None), (cs,), (cs_spec,),
                 (slab_shape, slab_shape), (slab_spec, slab_spec), name, lhs_spec=lhs_spec,
                 chunk=2 * MXU_COLS)


def _fold_mix_kernel(c_ref, s_ref, wm_ref, o_ref):
    wm = wm_ref[...]
    gd = FNET_GROUP_DIM
    hp = lax.Precision.HIGHEST
    o_ref[:, 0:gd] = jnp.dot(c_ref[...], wm, preferred_element_type=F32, precision=hp).astype(BF16)
    o_ref[:, gd:2 * gd] = jnp.dot(s_ref[...], wm, preferred_element_type=F32, precision=hp).astype(BF16)


def _fold_mix(w_mix):
    gd = FNET_GROUP_DIM
    cs = _channel_dft_matrix()
    mat = pl.BlockSpec((gd, gd), lambda g: (0, 0))
    return pl.pallas_call(
        _fold_mix_kernel,
        grid=(FNET_GROUPS,),
        in_specs=[mat, mat, pl.BlockSpec((None, gd, gd), lambda g: (g, 0, 0))],
        out_specs=pl.BlockSpec((None, gd, 2 * gd), lambda g: (g, 0, 0)),
        out_shape=jax.ShapeDtypeStruct((FNET_GROUPS, gd, 2 * gd), BF16),
        compiler_params=_cparams(1),
        name="fnet_fold_mix",
    )(jnp.asarray(cs[:, :gd]), jnp.asarray(cs[:, gd:]), w_mix)


def _dft_kernel(na, pitch, gpitch, ar_ref, ai_ref, zg_ref, fa_ref, fb_ref, y_ref, er_ref, ei_ref, g_ref):
    half_rows = na * (DFT_ROWS // 2)

    def gather_a(b):
        start = (b // (DFT_ROWS // 2)) * half_rows + b % (DFT_ROWS // 2)
        zr = ar_ref[pl.ds(start, na, stride=DFT_ROWS // 2), :]
        zi = ai_ref[pl.ds(start, na, stride=DFT_ROWS // 2), :]
        return jnp.concatenate([zr, zi], axis=0).astype(BF16)

    def stage_a(pair, carry):
        b0 = 2 * pair
        d0, d1 = gather_a(b0), gather_a(b0 + 1)
        zero = jnp.zeros_like(d0)
        rhs = jnp.concatenate([jnp.concatenate([d0, zero], axis=1),
                               jnp.concatenate([zero, d1], axis=1)], axis=0)
        e = jnp.dot(fa_ref[pair], rhs, preferred_element_type=F32)
        for j in range(2):
            off = pl.multiple_of((b0 + j) * pitch, 8)
            er_ref[pl.ds(off, na), :] = e[:na, j * LANES:(j + 1) * LANES]
            ei_ref[pl.ds(off, na), :] = e[na:, j * LANES:(j + 1) * LANES]
        return carry

    lax.fori_loop(0, DFT_NB // 2, stage_a, 0, unroll=16)

    def gather_b(ka):
        er = er_ref[pl.ds(ka, DFT_NB, stride=pitch), :]
        ei = ei_ref[pl.ds(ka, DFT_NB, stride=pitch), :]
        return jnp.concatenate([er, ei], axis=0).astype(BF16)

    def stage_b(pair, carry):
        ka0 = 2 * pair
        rhs = jnp.concatenate([gather_b(ka0), gather_b(ka0 + 1)], axis=1)
        g = jnp.dot(fb_ref[...], rhs, preferred_element_type=F32)
        for j in range(2):
            off = pl.multiple_of((ka0 + j) * gpitch, 8)
            g_ref[pl.ds(off, DFT_NB), :] = g[:, j * LANES:(j + 1) * LANES]
        return carry

    lax.fori_loop(0, na // 2, stage_b, 0, unroll=8)

    def gate(kb, carry):
        rows = pl.ds(pl.multiple_of(kb * na, na), na)
        g = g_ref[pl.ds(kb, na, stride=gpitch), :]
        y_ref[rows, :] = (g * zg_ref[rows, :].astype(F32)).astype(BF16)
        return carry

    lax.fori_loop(0, DFT_NB, gate, 0, unroll=16)


def _position_dft(ar, ai, zg, batch, seq):
    na = seq // DFT_NB
    pitch = na + 8
    gpitch = DFT_NB + 8
    fa_np, fb_np = _position_dft_matrices(seq)
    fa = jnp.asarray(fa_np).astype(BF16)
    fb = jnp.asarray(fb_np).astype(BF16)
    slab = pl.BlockSpec((None, seq, LANES), lambda s, b: (s, b, 0))
    return pl.pallas_call(
        functools.partial(_dft_kernel, na, pitch, gpitch),
        grid=(SLABS, batch),
        in_specs=[slab, slab, slab,
                  pl.BlockSpec(fa.shape, lambda s, b: (0, 0, 0)),
                  pl.BlockSpec(fb.shape, lambda s, b: (0, 0))],
        out_specs=slab,
        out_shape=jax.ShapeDtypeStruct(ar.shape, BF16),
        scratch_shapes=[pltpu.VMEM((DFT_NB * pitch, LANES), F32),
                        pltpu.VMEM((DFT_NB * pitch, LANES), F32),
                        pltpu.VMEM((na * gpitch, LANES), F32)],
        compiler_params=_cparams(2),
        name="position_dft",
    )(ar, ai, zg, fa, fb)


def _dft_dense_kernel(ar_ref, ai_ref, zg_ref, fd_ref, y_ref):
    d = jnp.concatenate([ar_ref[...], ai_ref[...]], axis=0).astype(BF16)
    g = jnp.dot(fd_ref[...], d, preferred_element_type=F32)
    y_ref[...] = (g * zg_ref[...].astype(F32)).astype(BF16)


def _position_dft_dense(ar, ai, zg, batch, seq):
    fd = jnp.asarray(_dense_dft_matrix(seq)).astype(BF16)
    slab = pl.BlockSpec((None, seq, LANES), lambda s, b: (s, b, 0))
    return pl.pallas_call(
        _dft_dense_kernel,
        grid=(SLABS, batch),
        in_specs=[slab, slab, slab, pl.BlockSpec(fd.shape, lambda s, b: (0, 0))],
        out_specs=slab,
        out_shape=jax.ShapeDtypeStruct(ar.shape, BF16),
        compiler_params=_cparams(2),
        name="position_dft_dense",
    )(ar, ai, zg, fd)


def _attn_kernel(n_band, nblk, sink_ref, q_ref, *refs):
    k_refs = refs[:n_band + 1]
    v_refs = refs[n_band + 1:2 * n_band + 2]
    zg_ref, y_ref = refs[2 * n_band + 2:]
    h = pl.program_id(1)
    i = pl.program_id(2)
    blk = ATTN_BLOCK
    n_pair = Q_GROUP // 2

    lane = lax.broadcasted_iota(jnp.int32, (1, LANES), 1)
    lo = lane < HEAD_DIM
    keys = jnp.concatenate([r[...] for r in k_refs], axis=0)
    vals = jnp.concatenate([r[...] for r in v_refs], axis=0)
    ones = jnp.ones_like(vals)
    v_ext = (jnp.where(lo, vals, ones), jnp.where(lo, ones, vals))
    dn = (((1,), (1,)), ((), ()))
    if n_band:
        r = lax.broadcasted_iota(jnp.int32, (blk, blk), 0)
        c = lax.broadcasted_iota(jnp.int32, (blk, blk), 1)
        prev_ok = c >= r + jnp.where(i > 0, 0, blk)
        next_ok = c <= r - jnp.where(i < nblk - 1, 0, blk)

    q = q_ref[...]
    q4 = jnp.concatenate([q[:, t * LANES:(t + 1) * LANES] for t in range(n_pair)], axis=0)
    qzero = jnp.zeros_like(q4)

    def scores(hd):
        qm = jnp.where(lo, q4, qzero) if hd == 0 else jnp.where(lo, qzero, q4)
        return lax.dot_general(qm, keys, dn, preferred_element_type=F32)

    def finish(hd, s):
        p_rows, sink_rows = [], []
        for t in range(n_pair):
            st = s[t * blk:(t + 1) * blk]
            parts = [st[:, j * blk:(j + 1) * blk] for j in range(st.shape[1] // blk)]
            if n_band:
                parts[0] = jnp.where(prev_ok, parts[0], NEG_INF)
                parts[2] = jnp.where(next_ok, parts[2], NEG_INF)
            sk = sink_ref[h, 2 * t + hd] * LOG2_E
            mx = parts[0]
            for part in parts[1:]:
                mx = jnp.maximum(mx, part)
            mx = jnp.maximum(jnp.max(mx, axis=-1, keepdims=True), sk)
            p_rows.append(jnp.concatenate([jnp.exp2(part - mx).astype(BF16) for part in parts], axis=1))
            sink_rows.append(jnp.exp2(sk - mx))
        p = jnp.concatenate(p_rows, axis=0)
        o_ext = jnp.dot(p, v_ext[hd], preferred_element_type=F32)
        return o_ext, jnp.concatenate(sink_rows, axis=0)

    s_lo, s_hi = scores(0), scores(1)
    o_lo, sink_lo = finish(0, s_lo)
    o_hi, sink_hi = finish(1, s_hi)
    o = jnp.where(lo, o_lo, o_hi)
    denom = jnp.where(lo, pltpu.roll(o_lo, HEAD_DIM, 1) + sink_lo, pltpu.roll(o_hi, HEAD_DIM, 1) + sink_hi)
    res = o / denom
    for t in range(n_pair):
        tile = slice(t * LANES, (t + 1) * LANES)
        y_ref[:, tile] = (res[t * blk:(t + 1) * blk] * zg_ref[:, tile].astype(F32)).astype(BF16)


def _attention(q, kd, vd, kcd, vcd, zg, sink, batch, seq, ctx_len, use_band):
    blk = ATTN_BLOCK
    nblk = seq // blk
    qw = Q_GROUP * HEAD_DIM
    q_spec = pl.BlockSpec((blk, qw), lambda b, h, i: (b * nblk + i, h))
    ctx_spec = pl.BlockSpec((ctx_len, LANES), lambda b, h, i: (b, h))
    if use_band:
        def band(delta):
            return pl.BlockSpec(
                (blk, LANES), lambda b, h, i: (b * nblk + jnp.clip(i + delta, 0, nblk - 1), h))
        k_specs = [band(-1), band(0), band(1), ctx_spec]
        k_args, v_args = [kd, kd, kd, kcd], [vd, vd, vd, vcd]
        n_band = 3
    else:
        k_specs, k_args, v_args, n_band = [ctx_spec], [kcd], [vcd], 0
    return pl.pallas_call(
        functools.partial(_attn_kernel, n_band, nblk),
        grid=(batch, KV_HEADS, nblk),
        in_specs=[pl.BlockSpec(memory_space=pltpu.SMEM), q_spec] + k_specs + k_specs + [q_spec],
        out_specs=q_spec,
        out_shape=jax.ShapeDtypeStruct(q.shape, BF16),
        compiler_params=_cparams(3),
        name="attention_band" if use_band else "attention_ctx",
    )(sink, q, *k_args, *v_args, zg)


def _gelu_stats_kernel(chunk, a_ref, w_ref, gv_ref, mu_ref, rstd_ref):
    a = a_ref[...]
    n = w_ref.shape[1]
    mean = m2 = None
    for idx, c0 in enumerate(range(0, n, chunk)):
        ge = _gelu_tanh(jnp.dot(a, w_ref[:, c0:c0 + chunk], preferred_element_type=F32))
        gv_ref[:, c0:c0 + chunk] = ge.astype(BF16)
        cmean = jnp.mean(ge, axis=-1, keepdims=True)
        d = ge - cmean
        cm2 = jnp.sum(d * d, axis=-1, keepdims=True)
        if idx == 0:
            mean, m2 = cmean, cm2
        else:
            delta = cmean - mean
            mean = mean + delta * (1.0 / (idx + 1))
            m2 = m2 + cm2 + delta * delta * (chunk * idx / (idx + 1))
    mu_ref[...] = mean
    rstd_ref[...] = lax.rsqrt(m2 * (1.0 / n) + EPS)


def _proj_gelu_stats(h, w, col0, tm, chunk=2 * MXU_COLS):
    m, k = h.shape
    n = D_BRANCH
    col = pl.BlockSpec((tm, 1), lambda i: (i, 0))
    stat = jax.ShapeDtypeStruct((m, 1), F32)
    return pl.pallas_call(
        functools.partial(_gelu_stats_kernel, chunk),
        grid=(m // tm,),
        in_specs=[pl.BlockSpec((tm, k), lambda i: (i, 0)),
                  pl.BlockSpec((k, n), lambda i: (0, col0 // n), pipeline_mode=pl.Buffered(1))],
        out_specs=(pl.BlockSpec((tm, n), lambda i: (i, 0)), col, col),
        out_shape=(jax.ShapeDtypeStruct((m, n), BF16), stat, stat),
        compiler_params=_cparams(1),
        name="gmlp_in_v",
    )(h, w)


def _sgu_kernel(n_chunks, n_groups, gu_ref, gv_ref, zg_ref, mu_ref, rstd_ref, lg_ref, lb_ref,
                ws_ref, bs_ref, y_ref):
    gd = D_BRANCH // GMLP_GROUPS
    mu = mu_ref[...]
    rstd = rstd_ref[...]
    for gl in range(n_groups):
        cols = slice(gl * gd, (gl + 1) * gd)
        ws = ws_ref[gl]
        bs = bs_ref[gl]
        lg = lg_ref[:, cols]
        lb = lb_ref[:, cols]
        for c in range(n_chunks):
            rows = slice(c * GMLP_CHUNK, (c + 1) * GMLP_CHUNK)
            vn = ((gv_ref[rows, cols].astype(F32) - mu[rows]) * rstd[rows] * lg + lb).astype(BF16)
            s = jnp.dot(ws, vn, preferred_element_type=F32) + bs
            y_ref[rows, cols] = (gu_ref[rows, cols].astype(F32) * s
                                 * zg_ref[rows, cols].astype(F32)).astype(BF16)


def _spatial_gate(gu, gv, zg, mu, rstd, ln_g, ln_b, w_s, b_s, tm, n_groups=2):
    m = gu.shape[0]
    gd = D_BRANCH // GMLP_GROUPS
    tn = n_groups * gd
    tile = pl.BlockSpec((tm, tn), lambda r, g: (r, g))
    col = pl.BlockSpec((tm, 1), lambda r, g: (r, 0))
    vec = pl.BlockSpec((1, tn), lambda r, g: (0, g))
    return pl.pallas_call(
        functools.partial(_sgu_kernel, tm // GMLP_CHUNK, n_groups),
        grid=(m // tm, GMLP_GROUPS // n_groups),
        in_specs=[tile, tile, tile, col, col, vec, vec,
                  pl.BlockSpec((n_groups, GMLP_CHUNK, GMLP_CHUNK), lambda r, g: (g, 0, 0)),
                  pl.BlockSpec((n_groups, GMLP_CHUNK, 1), lambda r, g: (g, 0, 0))],
        out_specs=tile,
        out_shape=jax.ShapeDtypeStruct(gu.shape, BF16),
        compiler_params=_cparams(2),
        name="gmlp_spatial_gate",
    )(gu, gv, zg, mu, rstd, ln_g.reshape(1, D_BRANCH), ln_b.reshape(1, D_BRANCH),
      w_s, b_s.reshape(GMLP_GROUPS, GMLP_CHUNK, 1))


def _wout_kernel(mode, y_ref, w_ref, x_ref, gate_ref, *refs):
    if len(y_ref.shape) == 3:
        y = jnp.concatenate([y_ref[s] for s in range(y_ref.shape[0])], axis=1)
    else:
        y = y_ref[...]
    acc = jnp.dot(y, w_ref[...], preferred_element_type=F32)
    xn = x_ref[...] + gate_ref[...] * acc
    if mode == "final":
        g_ref, o_ref = refs
        o_ref[...] = xn * lax.rsqrt(jnp.mean(xn * xn, axis=-1, keepdims=True) + EPS) * g_ref[...]
    elif mode == "next":
        g_ref, sc_ref, sh_ref, xo_ref, h_ref = refs
        xo_ref[...] = xn
        h_ref[...] = _mod_rmsnorm(xn, g_ref[...], sc_ref[...], sh_ref[...]).astype(BF16)
    else:
        refs[0][...] = xn


def _out_proj(y, w_out, x2, mods, layer, who, mode, norm_vec, tm):
    m = x2.shape[0]
    row = pl.BlockSpec((tm, D_MODEL), lambda r: (r, 0))
    if y.ndim == 3:
        y_spec = pl.BlockSpec((SLABS, tm, LANES), lambda r: (0, r, 0))
    else:
        y_spec = pl.BlockSpec((tm, D_BRANCH), lambda r: (r, 0))
    in_specs = [y_spec,
                pl.BlockSpec((None, D_BRANCH, D_MODEL), lambda r: (layer, 0, 0),
                             pipeline_mode=pl.Buffered(1)),
                row, _mod_spec(layer, 2, who)]
    args = [y, w_out, x2, mods]
    xs = jax.ShapeDtypeStruct((m, D_MODEL), F32)
    if mode == "final":
        in_specs.append(pl.BlockSpec((1, D_MODEL), lambda r: (0, 0)))
        args.append(norm_vec.reshape(1, D_MODEL))
        out_shape, out_specs = xs, row
    elif mode == "next":
        in_specs += [_row_spec(layer + 1), _mod_spec(layer + 1, 1, who), _mod_spec(layer + 1, 0, who)]
        args += [norm_vec, mods, mods]
        out_shape = (xs, jax.ShapeDtypeStruct((m, D_MODEL), BF16))
        out_specs = (row, row)
    else:
        out_shape, out_specs = xs, row
    return pl.pallas_call(
        functools.partial(_wout_kernel, mode),
        grid=(m // tm,),
        in_specs=in_specs,
        out_specs=out_specs,
        out_shape=out_shape,
        compiler_params=_cparams(1),
        name="out_proj_" + mode,
    )(*args)


def kernel(x, c, ctx, c_ctx, norm_g, ada_w, ada_b, w_out, fnet_w_in, fnet_w_mix, attn_w_in, attn_sink,
           gmlp_w_in, gmlp_w_s, gmlp_b_s, gmlp_ln_g, gmlp_ln_b, final_g):
    batch, seq, d = x.shape
    ctx_len = ctx.shape[1]
    assert d == D_MODEL and seq % (DFT_NB * 8) == 0 and seq % GRID_W == 0 and batch < 8
    m_lat, m_ctx = batch * seq, batch * ctx_len
    tm_lat = min(1024, seq)
    tm_ctx = ctx_len
    tm_out = 256

    def who_lat(tm):
        return lambda r: (r * tm) // seq

    who_ctx = lambda r: batch

    cvec = jnp.zeros((8, d), F32).at[:batch].set(c).at[batch].set(c_ctx)
    mods = _mods(cvec, ada_w, ada_b).reshape(DEPTH, 8, 3, 1, d)
    norm_g3 = norm_g.reshape(DEPTH, 1, d)
    w_out_b = w_out.astype(BF16)

    xl = x.reshape(m_lat, d)
    xc = ctx.reshape(m_ctx, d)
    hl = _prenorm(xl, norm_g3, mods, 0, who_lat(512), 512)
    hc = _prenorm(xc, norm_g3, mods, 0, who_ctx, tm_ctx)

    cos_np, sin_np = _rope_tables(seq)
    cos_t, sin_t = jnp.asarray(cos_np), jnp.asarray(sin_np)

    def finish(y, x2, layer, who_fn, need_next, is_final):
        if is_final:
            return _out_proj(y, w_out_b, x2, mods, layer, who_fn, "final", final_g, tm_out), None
        if need_next:
            return _out_proj(y, w_out_b, x2, mods, layer, who_fn, "next", norm_g3, tm_out)
        return _out_proj(y, w_out_b, x2, mods, layer, who_fn, "plain", None, tm_out), None

    out = None
    for i in range(DEPTH):
        kind, j = i % 3, i // 3
        need_ctx = i < DEPTH - 1
        last = i == DEPTH - 1
        if kind == 0:
            w = fnet_w_in[j]
            cs = _fold_mix(fnet_w_mix[j])
            tm_dft = (seq // DFT_NB) * DFT_ROWS
            ar, ai = _proj_channel_dft(hl, w, cs, "fnet_in_u", tm_dft, seq=seq)
            zg = _proj_silu_slabs(hl, w, D_BRANCH, "fnet_in_z", tm_lat)
            y = _position_dft(ar, ai, zg, batch, seq)
            yc = None
            if need_ctx:
                ar, ai = _proj_channel_dft(hc, w, cs, "fnet_in_u_ctx", tm_ctx)
                zg = _proj_silu_slabs(hc, w, D_BRANCH, "fnet_in_z_ctx", tm_ctx)
                yc = _position_dft_dense(ar, ai, zg, batch, ctx_len)
        elif kind == 1:
            w = attn_w_in[j]
            kvw = KV_HEADS * HEAD_DIM
            z0 = D_BRANCH + 2 * kvw

            def dup(wpart):
                w3 = wpart.reshape(d, KV_HEADS, 1, HEAD_DIM)
                return jnp.broadcast_to(w3, (d, KV_HEADS, 2, HEAD_DIM)).reshape(d, 2 * kvw)

            wk = dup(w[:, D_BRANCH:D_BRANCH + kvw])
            wv = dup(w[:, D_BRANCH + kvw:D_BRANCH + 2 * kvw])
            sink = attn_sink[j].reshape(KV_HEADS, Q_GROUP)
            scale = HEAD_DIM ** -0.5 * LOG2_E
            q = _proj_rope(hl, w, 0, D_BRANCH, scale, cos_t, sin_t, seq, "attn_in_q", tm_lat)
            kd = _proj_rope(hl, wk, 0, 2 * kvw, 1.0, cos_t, sin_t, seq, "attn_in_k", tm_lat)
            vd = _proj_simple(hl, wv, 0, 2 * kvw, _epi_cast, "attn_in_v", tm_lat)
            zg = _proj_simple(hl, w, z0, D_BRANCH, _epi_silu, "attn_in_z", tm_lat)
            kcd = _proj_simple(hc, wk, 0, 2 * kvw, _epi_cast, "attn_in_kc", tm_ctx)
            vcd = _proj_simple(hc, wv, 0, 2 * kvw, _epi_cast, "attn_in_vc", tm_ctx)
            y = _attention(q, kd, vd, kcd, vcd, zg, sink, batch, seq, ctx_len, True)
            yc = None
            if need_ctx:
                qc = _proj_simple(hc, w, 0, D_BRANCH, functools.partial(_epi_scale_cast, scale),
                                  "attn_in_qc", tm_ctx)
                zgc = _proj_simple(hc, w, z0, D_BRANCH, _epi_silu, "attn_in_zc", tm_ctx)
                yc = _attention(qc, None, None, kcd, vcd, zgc, sink, batch, ctx_len, ctx_len, False)
        else:
            w = gmlp_w_in[j]
            w_v = w[:, D_BRANCH:2 * D_BRANCH].astype(BF16)
            ws = gmlp_w_s[j].astype(BF16)

            def gmlp_branch(h, tm):
                gu = _proj_simple(h, w, 0, D_BRANCH, _epi_gelu, "gmlp_in_u", tm)
                gv, mu, rstd = _proj_gelu_stats(h, w_v, 0, min(tm, 512))
                zg = _proj_simple(h, w, 2 * D_BRANCH, D_BRANCH, _epi_silu, "gmlp_in_z", tm)
                return _spatial_gate(gu, gv, zg, mu, rstd, gmlp_ln_g[j], gmlp_ln_b[j], ws, gmlp_b_s[j], tm)

            y = gmlp_branch(hl, tm_lat)
            yc = gmlp_branch(hc, tm_ctx) if need_ctx else None

        res, hl = finish(y, xl, i, who_lat(tm_out), not last, last)
        if last:
            out = res
        else:
            xl = res
        if need_ctx:
            xc, hc = finish(yc, xc, i, who_ctx, i + 1 < DEPTH - 1 or (i + 1) % 3 == 1, False)
    return out.reshape(batch, seq, d)
```

```python
import functools
import math

import numpy as np
import jax
import jax.numpy as jnp
from jax import lax
from jax.experimental import pallas as pl
from jax.experimental.pallas import tpu as pltpu

F32 = jnp.float32
BF16 = jnp.bfloat16

D_MODEL = 2048
D_BRANCH = 4096
DEPTH = 4
GRID_W = 64
FNET_GROUPS = 16
FNET_GROUP_DIM = 256
HEAD_DIM = 64
KV_HEADS = 8
Q_GROUP = 8
ATTN_BLOCK = 128
ROPE_BASE = 10000.0
GMLP_CHUNK = 128
GMLP_GROUPS = 16
EPS = 1e-6
NEG_INF = -1e30
LOG2_E = math.log2(math.e)

LANES = 128
MXU_COLS = 256
PROJ_TN = 1024
DFT_NB = 128
DFT_ROWS = 16
SLABS = D_BRANCH // LANES
VMEM_LIMIT = 56 * 1024 * 1024


def _cparams(n_axes, vmem=VMEM_LIMIT):
    return pltpu.CompilerParams(dimension_semantics=("arbitrary",) * n_axes,
                                vmem_limit_bytes=vmem)


def _silu(z):
    return 0.5 * z * (1.0 + jnp.tanh(0.5 * z))


def _gelu_tanh(x):
    c = math.sqrt(2.0 / math.pi)
    return 0.5 * x * (1.0 + jnp.tanh(c * (x + 0.044715 * (x * x * x))))


def _mod_rmsnorm(x, g, scale, shift):
    y = x * lax.rsqrt(jnp.mean(x * x, axis=-1, keepdims=True) + EPS) * g
    return y * (1.0 + scale) + shift


def _channel_dft_matrix():
    n = FNET_GROUP_DIM
    k = np.arange(n, dtype=np.float64)
    ang = 2.0 * np.pi * np.outer(k, k) / n
    s = 1.0 / math.sqrt(n)
    return np.concatenate([np.cos(ang) * s, -np.sin(ang) * s], axis=1).astype(np.float32)


def _position_dft_matrices(seq):
    na, nb = seq // DFT_NB, DFT_NB
    a = np.arange(na, dtype=np.float64)
    b = np.arange(nb, dtype=np.float64)
    ang = 2.0 * np.pi * (a[None, None, :] * a[None, :, None] / na + b[:, None, None] * a[None, :, None] / seq)
    mr = np.cos(ang) / math.sqrt(na)
    mi = -np.sin(ang) / math.sqrt(na)
    fa = np.concatenate([np.concatenate([mr, -mi], axis=2), np.concatenate([mi, mr], axis=2)], axis=1)
    fa = np.concatenate([fa[0::2], fa[1::2]], axis=2)
    angb = 2.0 * np.pi * np.outer(b, b) / nb
    fb = np.concatenate([np.cos(angb), np.sin(angb)], axis=1) / math.sqrt(nb)
    return fa.astype(np.float32), fb.astype(np.float32)


def _dense_dft_matrix(n):
    k = np.arange(n, dtype=np.float64)
    ang = 2.0 * np.pi * np.outer(k, k) / n
    return (np.concatenate([np.cos(ang), np.sin(ang)], axis=1) / math.sqrt(n)).astype(np.float32)


def _rope_tables(seq):
    nf = HEAD_DIM // 4
    inv = ROPE_BASE ** (-np.arange(nf, dtype=np.float64) / nf)
    t = np.arange(seq)
    rows = (t // GRID_W).astype(np.float64)
    cols = (t % GRID_W).astype(np.float64)
    parts_c, parts_s = [], []
    for pos in (rows, cols):
        ang = pos[:, None] * inv[None, :]
        parts_c += [np.cos(ang), np.cos(ang)]
        parts_s += [-np.sin(ang), np.sin(ang)]
    cos = np.concatenate(parts_c, axis=1)
    sin = np.concatenate(parts_s, axis=1)
    reps = LANES // HEAD_DIM
    return (np.tile(cos, (1, reps)).astype(np.float32), np.tile(sin, (1, reps)).astype(np.float32))


def _mods_kernel(cv_ref, w_ref, b_ref, o_ref):
    a = _silu(cv_ref[...])
    o_ref[...] = jnp.dot(a, w_ref[...], preferred_element_type=F32,
                         precision=lax.Precision.HIGHEST) + b_ref[...]


def _mods(cvec, ada_w, ada_b):
    depth, d, n3 = ada_w.shape
    tn = 1024
    return pl.pallas_call(
        _mods_kernel,
        grid=(depth, n3 // tn),
        in_specs=[pl.BlockSpec((8, d), lambda i, j: (0, 0)),
                  pl.BlockSpec((None, d, tn), lambda i, j: (i, 0, j)),
                  pl.BlockSpec((None, 1, tn), lambda i, j: (i, 0, j))],
        out_specs=pl.BlockSpec((None, 8, tn), lambda i, j: (i, 0, j)),
        out_shape=jax.ShapeDtypeStruct((depth, 8, n3), F32),
        compiler_params=_cparams(2),
        name="ada_mods",
    )(cvec, ada_w, ada_b.reshape(depth, 1, n3))


def _mod_spec(layer, kind, who_of_row):
    return pl.BlockSpec((None, None, None, 1, D_MODEL),
                        lambda r, *_: (layer, who_of_row(r), kind, 0, 0))


def _row_spec(vec_layer):
    return pl.BlockSpec((None, 1, D_MODEL), lambda r, *_: (vec_layer, 0, 0))


def _prenorm_kernel(x_ref, g_ref, sc_ref, sh_ref, h_ref):
    h_ref[...] = _mod_rmsnorm(x_ref[...], g_ref[...], sc_ref[...], sh_ref[...]).astype(BF16)


def _prenorm(x2, norm_g3, mods, layer, who, tm):
    m = x2.shape[0]
    return pl.pallas_call(
        _prenorm_kernel,
        grid=(m // tm,),
        in_specs=[pl.BlockSpec((tm, D_MODEL), lambda r: (r, 0)),
                  _row_spec(layer), _mod_spec(layer, 1, who), _mod_spec(layer, 0, who)],
        out_specs=pl.BlockSpec((tm, D_MODEL), lambda r: (r, 0)),
        out_shape=jax.ShapeDtypeStruct((m, D_MODEL), BF16),
        compiler_params=_cparams(1),
        name="prenorm",
    )(x2, norm_g3, mods, mods)


def _proj_kernel(epilogue, n_extra, chunk, a_ref, w_ref, *rest):
    if len(a_ref.shape) == 3:
        a_flat = rest[-1]
        rest = rest[:-1]

        @pl.when(pl.program_id(1) == 0)
        def _():
            a_flat[...] = a_ref[...].reshape(a_flat.shape)

        a = a_flat[...]
    else:
        a = a_ref[...]
    for c0 in range(0, w_ref.shape[1], chunk):
        acc = jnp.dot(a, w_ref[:, c0:c0 + chunk].astype(BF16), preferred_element_type=F32)
        epilogue(acc, c0, rest[:n_extra], rest[n_extra:])


def _proj(h, w, col0, ncols, tm, tn, epilogue, extras, extra_specs, out_shapes, out_specs, name,
          w_single_buffer=False, lhs_spec=None, chunk=MXU_COLS):
    k = h.shape[-1]
    m = h.size // k
    off = col0 // tn
    w_kwargs = dict(pipeline_mode=pl.Buffered(1)) if w_single_buffer else {}
    scratch = [pltpu.VMEM((tm, k), h.dtype)] if lhs_spec is not None else []
    if lhs_spec is None:
        lhs_spec = pl.BlockSpec((tm, k), lambda i, j: (i, 0))
    return pl.pallas_call(
        functools.partial(_proj_kernel, epilogue, len(extras), min(chunk, tn)),
        grid=(m // tm, ncols // tn),
        in_specs=[lhs_spec,
                  pl.BlockSpec((k, tn), lambda i, j: (0, j + off), **w_kwargs)] + list(extra_specs),
        out_specs=out_specs,
        out_shape=out_shapes,
        scratch_shapes=scratch,
        compiler_params=_cparams(2),
        name=name,
    )(h, w, *extras)


def _epi_silu(acc, c0, extras, outs):
    outs[0][:, c0:c0 + acc.shape[1]] = _silu(acc).astype(BF16)


def _epi_silu_slabs(acc, c0, extras, outs):
    for t in range(acc.shape[1] // LANES):
        outs[0][c0 // LANES + t] = _silu(acc[:, t * LANES:(t + 1) * LANES]).astype(BF16)


def _epi_gelu(acc, c0, extras, outs):
    outs[0][:, c0:c0 + acc.shape[1]] = _gelu_tanh(acc).astype(BF16)


def _epi_cast(acc, c0, extras, outs):
    outs[0][:, c0:c0 + acc.shape[1]] = acc.astype(BF16)


def _epi_scale_cast(scale, acc, c0, extras, outs):
    outs[0][:, c0:c0 + acc.shape[1]] = (acc * scale).astype(BF16)


def _epi_channel_dft(split_rows, acc, c0, extras, outs):
    cs_ref = extras[0]
    ar_ref, ai_ref = outs
    tm = acc.shape[0]
    for gl in range(acc.shape[1] // FNET_GROUP_DIM):
        g = c0 // FNET_GROUP_DIM + gl
        ub = acc[:, gl * FNET_GROUP_DIM:(gl + 1) * FNET_GROUP_DIM].astype(BF16)
        ab = jnp.dot(ub, cs_ref[g], preferred_element_type=F32)
        if split_rows:
            ab4 = ab.reshape(tm // DFT_ROWS, 2, DFT_ROWS // 2, ab.shape[1])
            ab = jnp.concatenate([ab4[:, hf].reshape(tm // 2, ab.shape[1]) for hf in range(2)], axis=0)
        ar_ref[2 * g] = ab[:, 0:128]
        ar_ref[2 * g + 1] = ab[:, 128:256]
        ai_ref[2 * g] = ab[:, 256:384]
        ai_ref[2 * g + 1] = ab[:, 384:512]


def _epi_rope(scale, acc, c0, extras, outs):
    cos = extras[0][...]
    sin = extras[1][...]
    lane = lax.broadcasted_iota(jnp.int32, (1, LANES), 1)
    first = (lane % 32) < 16
    for t in range(acc.shape[1] // LANES):
        x = acc[:, t * LANES:(t + 1) * LANES]
        partner = jnp.where(first, pltpu.roll(x, LANES - 16, 1), pltpu.roll(x, 16, 1))
        y = x * cos + partner * sin
        if scale != 1.0:
            y = y * scale
        outs[0][:, c0 + t * LANES:c0 + (t + 1) * LANES] = y.astype(BF16)


def _proj_simple(h, w, col0, ncols, epilogue, name, tm, tn=PROJ_TN):
    m = h.shape[0]
    return _proj(h, w, col0, ncols, tm, tn, epilogue, (), (),
                 jax.ShapeDtypeStruct((m, ncols), BF16),
                 pl.BlockSpec((tm, tn), lambda i, j: (i, j)), name)


def _proj_rope(h, w, col0, ncols, scale, cos_t, sin_t, seq, name, tm, tn=PROJ_TN):
    m = h.shape[0]
    per_batch = seq // tm
    tab_spec = pl.BlockSpec((tm, LANES), lambda i, j: (i % per_batch, 0))
    return _proj(h, w, col0, ncols, tm, tn, functools.partial(_epi_rope, scale),
                 (cos_t, sin_t), (tab_spec, tab_spec),
                 jax.ShapeDtypeStruct((m, ncols), BF16),
                 pl.BlockSpec((tm, tn), lambda i, j: (i, j)), name)


def _proj_silu_slabs(h, w, col0, name, tm, tn=PROJ_TN):
    m = h.shape[0]
    return _proj(h, w, col0, D_BRANCH, tm, tn, _epi_silu_slabs, (), (),
                 jax.ShapeDtypeStruct((SLABS, m, LANES), BF16),
                 pl.BlockSpec((tn // LANES, tm, LANES), lambda i, j: (j, i, 0)), name)


def _proj_channel_dft(h, w, cs, name, tm, seq=None, tn=PROJ_TN):
    m = h.shape[0]
    lhs_spec = None
    if seq is not None:
        na, tiles = seq // DFT_NB, DFT_NB // DFT_ROWS
        assert tm == na * DFT_ROWS
        h = h.reshape(m // seq, na, DFT_NB, D_MODEL)
        lhs_spec = pl.BlockSpec((None, na, DFT_ROWS, D_MODEL), lambda i, j: (i // tiles, 0, i % tiles, 0))
    slab_shape = jax.ShapeDtypeStruct((SLABS, m, LANES), F32)
    slab_spec = pl.BlockSpec((tn // LANES, tm, LANES), lambda i, j: (j, i, 0))
    groups = tn // FNET_GROUP_DIM
    cs_spec = pl.BlockSpec((groups,) + cs.shape[1:], lambda i, j: (j, 0, 0))
    return _proj(h, w, 0, D_BRANCH, tm, tn, functools.partial(_epi_channel_dft, seq is not None), (cs,), (cs_spec,),
                 (slab_shape, slab_shape), (slab_spec, slab_spec), name, lhs_spec=lhs_spec,
                 chunk=2 * MXU_COLS)


def _fold_mix_kernel(c_ref, s_ref, wm_ref, o_ref):
    wm = wm_ref[...]
    gd = FNET_GROUP_DIM
    hp = lax.Precision.HIGHEST
    o_ref[:, 0:gd] = jnp.dot(c_ref[...], wm, preferred_element_type=F32, precision=hp).astype(BF16)
    o_ref[:, gd:2 * gd] = jnp.dot(s_ref[...], wm, preferred_element_type=F32, precision=hp).astype(BF16)


def _fold_mix(w_mix):
    gd = FNET_GROUP_DIM
    cs = _channel_dft_matrix()
    mat = pl.BlockSpec((gd, gd), lambda g: (0, 0))
    return pl.pallas_call(
        _fold_mix_kernel,
        grid=(FNET_GROUPS,),
        in_specs=[mat, mat, pl.BlockSpec((None, gd, gd), lambda g: (g, 0, 0))],
        out_specs=pl.BlockSpec((None, gd, 2 * gd), lambda g: (g, 0, 0)),
        out_shape=jax.ShapeDtypeStruct((FNET_GROUPS, gd, 2 * gd), BF16),
        compiler_params=_cparams(1),
        name="fnet_fold_mix",
    )(jnp.asarray(cs[:, :gd]), jnp.asarray(cs[:, gd:]), w_mix)


def _dft_kernel(na, pitch, gpitch, ar_ref, ai_ref, zg_ref, fa_ref, fb_ref, y_ref, er_ref, ei_ref, g_ref):
    half_rows = na * (DFT_ROWS // 2)

    def gather_a(b):
        start = (b // (DFT_ROWS // 2)) * half_rows + b % (DFT_ROWS // 2)
        zr = ar_ref[pl.ds(start, na, stride=DFT_ROWS // 2), :]
        zi = ai_ref[pl.ds(start, na, stride=DFT_ROWS // 2), :]
        return jnp.concatenate([zr, zi], axis=0).astype(BF16)

    def stage_a(pair, carry):
        b0 = 2 * pair
        d0, d1 = gather_a(b0), gather_a(b0 + 1)
        zero = jnp.zeros_like(d0)
        rhs = jnp.concatenate([jnp.concatenate([d0, zero], axis=1),
                               jnp.concatenate([zero, d1], axis=1)], axis=0)
        e = jnp.dot(fa_ref[pair], rhs, preferred_element_type=F32)
        for j in range(2):
            off = pl.multiple_of((b0 + j) * pitch, 8)
            er_ref[pl.ds(off, na), :] = e[:na, j * LANES:(j + 1) * LANES]
            ei_ref[pl.ds(off, na), :] = e[na:, j * LANES:(j + 1) * LANES]
        return carry

    lax.fori_loop(0, DFT_NB // 2, stage_a, 0, unroll=16)

    def gather_b(ka):
        er = er_ref[pl.ds(ka, DFT_NB, stride=pitch), :]
        ei = ei_ref[pl.ds(ka, DFT_NB, stride=pitch), :]
        return jnp.concatenate([er, ei], axis=0).astype(BF16)

    def stage_b(pair, carry):
        ka0 = 2 * pair
        rhs = jnp.concatenate([gather_b(ka0), gather_b(ka0 + 1)], axis=1)
        g = jnp.dot(fb_ref[...], rhs, preferred_element_type=F32)
        for j in range(2):
            off = pl.multiple_of((ka0 + j) * gpitch, 8)
            g_ref[pl.ds(off, DFT_NB), :] = g[:, j * LANES:(j + 1) * LANES]
        return carry

    lax.fori_loop(0, na // 2, stage_b, 0, unroll=8)

    def gate(kb, carry):
        rows = pl.ds(pl.multiple_of(kb * na, na), na)
        g = g_ref[pl.ds(kb, na, stride=gpitch), :]
        y_ref[rows, :] = (g * zg_ref[rows, :].astype(F32)).astype(BF16)
        return carry

    lax.fori_loop(0, DFT_NB, gate, 0, unroll=16)


def _position_dft(ar, ai, zg, batch, seq):
    na = seq // DFT_NB
    pitch = na + 8
    gpitch = DFT_NB + 8
    fa_np, fb_np = _position_dft_matrices(seq)
    fa = jnp.asarray(fa_np).astype(BF16)
    fb = jnp.asarray(fb_np).astype(BF16)
    slab = pl.BlockSpec((None, seq, LANES), lambda s, b: (s, b, 0))
    return pl.pallas_call(
        functools.partial(_dft_kernel, na, pitch, gpitch),
        grid=(SLABS, batch),
        in_specs=[slab, slab, slab,
                  pl.BlockSpec(fa.shape, lambda s, b: (0, 0, 0)),
                  pl.BlockSpec(fb.shape, lambda s, b: (0, 0))],
        out_specs=slab,
        out_shape=jax.ShapeDtypeStruct(ar.shape, BF16),
        scratch_shapes=[pltpu.VMEM((DFT_NB * pitch, LANES), F32),
                        pltpu.VMEM((DFT_NB * pitch, LANES), F32),
                        pltpu.VMEM((na * gpitch, LANES), F32)],
        compiler_params=_cparams(2),
        name="position_dft",
    )(ar, ai, zg, fa, fb)


def _dft_dense_kernel(ar_ref, ai_ref, zg_ref, fd_ref, y_ref):
    d = jnp.concatenate([ar_ref[...], ai_ref[...]], axis=0).astype(BF16)
    g = jnp.dot(fd_ref[...], d, preferred_element_type=F32)
    y_ref[...] = (g * zg_ref[...].astype(F32)).astype(BF16)


def _position_dft_dense(ar, ai, zg, batch, seq):
    fd = jnp.asarray(_dense_dft_matrix(seq)).astype(BF16)
    slab = pl.BlockSpec((None, seq, LANES), lambda s, b: (s, b, 0))
    return pl.pallas_call(
        _dft_dense_kernel,
        grid=(SLABS, batch),
        in_specs=[slab, slab, slab, pl.BlockSpec(fd.shape, lambda s, b: (0, 0))],
        out_specs=slab,
        out_shape=jax.ShapeDtypeStruct(ar.shape, BF16),
        compiler_params=_cparams(2),
        name="position_dft_dense",
    )(ar, ai, zg, fd)


def _attn_kernel(n_band, nblk, sink_ref, q_ref, *refs):
    k_refs = refs[:n_band + 1]
    v_refs = refs[n_band + 1:2 * n_band + 2]
    zg_ref, y_ref = refs[2 * n_band + 2:]
    h = pl.program_id(1)
    i = pl.program_id(2)
    blk = ATTN_BLOCK
    n_pair = Q_GROUP // 2

    lane = lax.broadcasted_iota(jnp.int32, (1, LANES), 1)
    lo = lane < HEAD_DIM
    keys = jnp.concatenate([r[...] for r in k_refs], axis=0)
    vals = jnp.concatenate([r[...] for r in v_refs], axis=0)
    ones = jnp.ones_like(vals)
    v_ext = (jnp.where(lo, vals, ones), jnp.where(lo, ones, vals))
    dn = (((1,), (1,)), ((), ()))
    if n_band:
        r = lax.broadcasted_iota(jnp.int32, (blk, blk), 0)
        c = lax.broadcasted_iota(jnp.int32, (blk, blk), 1)
        prev_ok = c >= r + jnp.where(i > 0, 0, blk)
        next_ok = c <= r - jnp.where(i < nblk - 1, 0, blk)

    q = q_ref[...]
    q4 = jnp.concatenate([q[:, t * LANES:(t + 1) * LANES] for t in range(n_pair)], axis=0)
    qzero = jnp.zeros_like(q4)

    def scores(hd):
        qm = jnp.where(lo, q4, qzero) if hd == 0 else jnp.where(lo, qzero, q4)
        return lax.dot_general(qm, keys, dn, preferred_element_type=F32)

    def finish(hd, s):
        p_rows, sink_rows = [], []
        for t in range(n_pair):
            st = s[t * blk:(t + 1) * blk]
            parts = [st[:, j * blk:(j + 1) * blk] for j in range(st.shape[1] // blk)]
            if n_band:
                parts[0] = jnp.where(prev_ok, parts[0], NEG_INF)
                parts[2] = jnp.where(next_ok, parts[2], NEG_INF)
            sk = sink_ref[h, 2 * t + hd] * LOG2_E
            mx = parts[0]
            for part in parts[1:]:
                mx = jnp.maximum(mx, part)
            mx = jnp.maximum(jnp.max(mx, axis=-1, keepdims=True), sk)
            p_rows.append(jnp.concatenate([jnp.exp2(part - mx).astype(BF16) for part in parts], axis=1))
            sink_rows.append(jnp.exp2(sk - mx))
        p = jnp.concatenate(p_rows, axis=0)
        o_ext = jnp.dot(p, v_ext[hd], preferred_element_type=F32)
        return o_ext, jnp.concatenate(sink_rows, axis=0)

    s_lo, s_hi = scores(0), scores(1)
    o_lo, sink_lo = finish(0, s_lo)
    o_hi, sink_hi = finish(1, s_hi)
    o = jnp.where(lo, o_lo, o_hi)
    denom = jnp.where(lo, pltpu.roll(o_lo, HEAD_DIM, 1) + sink_lo, pltpu.roll(o_hi, HEAD_DIM, 1) + sink_hi)
    res = o / denom
    for t in range(n_pair):
        tile = slice(t * LANES, (t + 1) * LANES)
        y_ref[:, tile] = (res[t * blk:(t + 1) * blk] * zg_ref[:, tile].astype(F32)).astype(BF16)


def _attention(q, kd, vd, kcd, vcd, zg, sink, batch, seq, ctx_len, use_band):
    blk = ATTN_BLOCK
    nblk = seq // blk
    qw = Q_GROUP * HEAD_DIM
    q_spec = pl.BlockSpec((blk, qw), lambda b, h, i: (b * nblk + i, h))
    ctx_spec = pl.BlockSpec((ctx_len, LANES), lambda b, h, i: (b, h))
    if use_band:
        def band(delta):
            return pl.BlockSpec(
                (blk, LANES), lambda b, h, i: (b * nblk + jnp.clip(i + delta, 0, nblk - 1), h))
        k_specs = [band(-1), band(0), band(1), ctx_spec]
        k_args, v_args = [kd, kd, kd, kcd], [vd, vd, vd, vcd]
        n_band = 3
    else:
        k_specs, k_args, v_args, n_band = [ctx_spec], [kcd], [vcd], 0
    return pl.pallas_call(
        functools.partial(_attn_kernel, n_band, nblk),
        grid=(batch, KV_HEADS, nblk),
        in_specs=[pl.BlockSpec(memory_space=pltpu.SMEM), q_spec] + k_specs + k_specs + [q_spec],
        out_specs=q_spec,
        out_shape=jax.ShapeDtypeStruct(q.shape, BF16),
        compiler_params=_cparams(3),
        name="attention_band" if use_band else "attention_ctx",
    )(sink, q, *k_args, *v_args, zg)


def _gelu_stats_kernel(chunk, a_ref, w_ref, gv_ref, mu_ref, rstd_ref):
    a = a_ref[...]
    n = w_ref.shape[1]
    mean = m2 = None
    for idx, c0 in enumerate(range(0, n, chunk)):
        ge = _gelu_tanh(jnp.dot(a, w_ref[:, c0:c0 + chunk], preferred_element_type=F32))
        gv_ref[:, c0:c0 + chunk] = ge.astype(BF16)
        cmean = jnp.mean(ge, axis=-1, keepdims=True)
        d = ge - cmean
        cm2 = jnp.sum(d * d, axis=-1, keepdims=True)
        if idx == 0:
            mean, m2 = cmean, cm2
        else:
            delta = cmean - mean
            mean = mean + delta * (1.0 / (idx + 1))
            m2 = m2 + cm2 + delta * delta * (chunk * idx / (idx + 1))
    mu_ref[...] = mean
    rstd_ref[...] = lax.rsqrt(m2 * (1.0 / n) + EPS)


def _proj_gelu_stats(h, w, col0, tm, chunk=2 * MXU_COLS):
    m, k = h.shape
    n = D_BRANCH
    col = pl.BlockSpec((tm, 1), lambda i: (i, 0))
    stat = jax.ShapeDtypeStruct((m, 1), F32)
    return pl.pallas_call(
        functools.partial(_gelu_stats_kernel, chunk),
        grid=(m // tm,),
        in_specs=[pl.BlockSpec((tm, k), lambda i: (i, 0)),
                  pl.BlockSpec((k, n), lambda i: (0, col0 // n), pipeline_mode=pl.Buffered(1))],
        out_specs=(pl.BlockSpec((tm, n), lambda i: (i, 0)), col, col),
        out_shape=(jax.ShapeDtypeStruct((m, n), BF16), stat, stat),
        compiler_params=_cparams(1),
        name="gmlp_in_v",
    )(h, w)


def _sgu_kernel(n_chunks, n_groups, gu_ref, gv_ref, zg_ref, mu_ref, rstd_ref, lg_ref, lb_ref,
                ws_ref, bs_ref, y_ref):
    gd = D_BRANCH // GMLP_GROUPS
    mu = mu_ref[...]
    rstd = rstd_ref[...]
    for gl in range(n_groups):
        cols = slice(gl * gd, (gl + 1) * gd)
        ws = ws_ref[gl]
        bs = bs_ref[gl]
        lg = lg_ref[:, cols]
        lb = lb_ref[:, cols]
        for c in range(n_chunks):
            rows = slice(c * GMLP_CHUNK, (c + 1) * GMLP_CHUNK)
            vn = ((gv_ref[rows, cols].astype(F32) - mu[rows]) * rstd[rows] * lg + lb).astype(BF16)
            s = jnp.dot(ws, vn, preferred_element_type=F32) + bs
            y_ref[rows, cols] = (gu_ref[rows, cols].astype(F32) * s
                                 * zg_ref[rows, cols].astype(F32)).astype(BF16)


def _spatial_gate(gu, gv, zg, mu, rstd, ln_g, ln_b, w_s, b_s, tm, n_groups=2):
    m = gu.shape[0]
    gd = D_BRANCH // GMLP_GROUPS
    tn = n_groups * gd
    tile = pl.BlockSpec((tm, tn), lambda r, g: (r, g))
    col = pl.BlockSpec((tm, 1), lambda r, g: (r, 0))
    vec = pl.BlockSpec((1, tn), lambda r, g: (0, g))
    return pl.pallas_call(
        functools.partial(_sgu_kernel, tm // GMLP_CHUNK, n_groups),
        grid=(m // tm, GMLP_GROUPS // n_groups),
        in_specs=[tile, tile, tile, col, col, vec, vec,
                  pl.BlockSpec((n_groups, GMLP_CHUNK, GMLP_CHUNK), lambda r, g: (g, 0, 0)),
                  pl.BlockSpec((n_groups, GMLP_CHUNK, 1), lambda r, g: (g, 0, 0))],
        out_specs=tile,
        out_shape=jax.ShapeDtypeStruct(gu.shape, BF16),
        compiler_params=_cparams(2),
        name="gmlp_spatial_gate",
    )(gu, gv, zg, mu, rstd, ln_g.reshape(1, D_BRANCH), ln_b.reshape(1, D_BRANCH),
      w_s, b_s.reshape(GMLP_GROUPS, GMLP_CHUNK, 1))


def _wout_kernel(mode, y_ref, w_ref, x_ref, gate_ref, *refs):
    if len(y_ref.shape) == 3:
        y = jnp.concatenate([y_ref[s] for s in range(y_ref.shape[0])], axis=1)
    else:
        y = y_ref[...]
    acc = jnp.dot(y, w_ref[...], preferred_element_type=F32)
    xn = x_ref[...] + gate_ref[...] * acc
    if mode == "final":
        g_ref, o_ref = refs
        o_ref[...] = xn * lax.rsqrt(jnp.mean(xn * xn, axis=-1, keepdims=True) + EPS) * g_ref[...]
    elif mode == "next":
        g_ref, sc_ref, sh_ref, xo_ref, h_ref = refs
        xo_ref[...] = xn
        h_ref[...] = _mod_rmsnorm(xn, g_ref[...], sc_ref[...], sh_ref[...]).astype(BF16)
    else:
        refs[0][...] = xn


def _out_proj(y, w_out, x2, mods, layer, who, mode, norm_vec, tm):
    m = x2.shape[0]
    row = pl.BlockSpec((tm, D_MODEL), lambda r: (r, 0))
    if y.ndim == 3:
        y_spec = pl.BlockSpec((SLABS, tm, LANES), lambda r: (0, r, 0))
    else:
        y_spec = pl.BlockSpec((tm, D_BRANCH), lambda r: (r, 0))
    in_specs = [y_spec,
                pl.BlockSpec((None, D_BRANCH, D_MODEL), lambda r: (layer, 0, 0),
                             pipeline_mode=pl.Buffered(1)),
                row, _mod_spec(layer, 2, who)]
    args = [y, w_out, x2, mods]
    xs = jax.ShapeDtypeStruct((m, D_MODEL), F32)
    if mode == "final":
        in_specs.append(pl.BlockSpec((1, D_MODEL), lambda r: (0, 0)))
        args.append(norm_vec.reshape(1, D_MODEL))
        out_shape, out_specs = xs, row
    elif mode == "next":
        in_specs += [_row_spec(layer + 1), _mod_spec(layer + 1, 1, who), _mod_spec(layer + 1, 0, who)]
        args += [norm_vec, mods, mods]
        out_shape = (xs, jax.ShapeDtypeStruct((m, D_MODEL), BF16))
        out_specs = (row, row)
    else:
        out_shape, out_specs = xs, row
    return pl.pallas_call(
        functools.partial(_wout_kernel, mode),
        grid=(m // tm,),
        in_specs=in_specs,
        out_specs=out_specs,
        out_shape=out_shape,
        compiler_params=_cparams(1),
        name="out_proj_" + mode,
    )(*args)


def kernel(x, c, ctx, c_ctx, norm_g, ada_w, ada_b, w_out, fnet_w_in, fnet_w_mix, attn_w_in, attn_sink,
           gmlp_w_in, gmlp_w_s, gmlp_b_s, gmlp_ln_g, gmlp_ln_b, final_g):
    batch, seq, d = x.shape
    ctx_len = ctx.shape[1]
    assert d == D_MODEL and seq % (DFT_NB * 8) == 0 and seq % GRID_W == 0 and batch < 8
    m_lat, m_ctx = batch * seq, batch * ctx_len
    tm_lat = min(2048, seq)
    tm_ctx = ctx_len
    tm_out = 256

    def who_lat(tm):
        return lambda r: (r * tm) // seq

    who_ctx = lambda r: batch

    cvec = jnp.zeros((8, d), F32).at[:batch].set(c).at[batch].set(c_ctx)
    mods = _mods(cvec, ada_w, ada_b).reshape(DEPTH, 8, 3, 1, d)
    norm_g3 = norm_g.reshape(DEPTH, 1, d)
    w_out_b = w_out.astype(BF16)

    xl = x.reshape(m_lat, d)
    xc = ctx.reshape(m_ctx, d)
    hl = _prenorm(xl, norm_g3, mods, 0, who_lat(512), 512)
    hc = _prenorm(xc, norm_g3, mods, 0, who_ctx, tm_ctx)

    cos_np, sin_np = _rope_tables(seq)
    cos_t, sin_t = jnp.asarray(cos_np), jnp.asarray(sin_np)

    def finish(y, x2, layer, who_fn, need_next, is_final):
        if is_final:
            return _out_proj(y, w_out_b, x2, mods, layer, who_fn, "final", final_g, tm_out), None
        if need_next:
            return _out_proj(y, w_out_b, x2, mods, layer, who_fn, "next", norm_g3, tm_out)
        return _out_proj(y, w_out_b, x2, mods, layer, who_fn, "plain", None, tm_out), None

    out = None
    for i in range(DEPTH):
        kind, j = i % 3, i // 3
        need_ctx = i < DEPTH - 1
        last = i == DEPTH - 1
        if kind == 0:
            w = fnet_w_in[j]
            cs = _fold_mix(fnet_w_mix[j])
            tm_dft = (seq // DFT_NB) * DFT_ROWS
            w_u = w[:, :D_BRANCH].astype(BF16)
            ar, ai = _proj_channel_dft(hl, w_u, cs, "fnet_in_u", tm_dft, seq=seq)
            zg = _proj_silu_slabs(hl, w, D_BRANCH, "fnet_in_z", tm_lat)
            y = _position_dft(ar, ai, zg, batch, seq)
            yc = None
            if need_ctx:
                ar, ai = _proj_channel_dft(hc, w_u, cs, "fnet_in_u_ctx", tm_ctx)
                zg = _proj_silu_slabs(hc, w, D_BRANCH, "fnet_in_z_ctx", tm_ctx)
                yc = _position_dft_dense(ar, ai, zg, batch, ctx_len)
        elif kind == 1:
            w = attn_w_in[j]
            kvw = KV_HEADS * HEAD_DIM
            z0 = D_BRANCH + 2 * kvw

            def dup(wpart):
                w3 = wpart.reshape(d, KV_HEADS, 1, HEAD_DIM)
                return jnp.broadcast_to(w3, (d, KV_HEADS, 2, HEAD_DIM)).reshape(d, 2 * kvw)

            wk = dup(w[:, D_BRANCH:D_BRANCH + kvw])
            wv = dup(w[:, D_BRANCH + kvw:D_BRANCH + 2 * kvw])
            sink = attn_sink[j].reshape(KV_HEADS, Q_GROUP)
            scale = HEAD_DIM ** -0.5 * LOG2_E
            q = _proj_rope(hl, w, 0, D_BRANCH, scale, cos_t, sin_t, seq, "attn_in_q", tm_lat)
            kd = _proj_rope(hl, wk, 0, 2 * kvw, 1.0, cos_t, sin_t, seq, "attn_in_k", tm_lat)
            vd = _proj_simple(hl, wv, 0, 2 * kvw, _epi_cast, "attn_in_v", tm_lat)
            zg = _proj_simple(hl, w, z0, D_BRANCH, _epi_silu, "attn_in_z", tm_lat)
            kcd = _proj_simple(hc, wk, 0, 2 * kvw, _epi_cast, "attn_in_kc", tm_ctx)
            vcd = _proj_simple(hc, wv, 0, 2 * kvw, _epi_cast, "attn_in_vc", tm_ctx)
            y = _attention(q, kd, vd, kcd, vcd, zg, sink, batch, seq, ctx_len, True)
            yc = None
            if need_ctx:
                qc = _proj_simple(hc, w, 0, D_BRANCH, functools.partial(_epi_scale_cast, scale),
                                  "attn_in_qc", tm_ctx)
                zgc = _proj_simple(hc, w, z0, D_BRANCH, _epi_silu, "attn_in_zc", tm_ctx)
                yc = _attention(qc, None, None, kcd, vcd, zgc, sink, batch, ctx_len, ctx_len, False)
        else:
            w = gmlp_w_in[j]
            w_v = w[:, D_BRANCH:2 * D_BRANCH].astype(BF16)
            ws = gmlp_w_s[j].astype(BF16)

            def gmlp_branch(h, tm):
                gu = _proj_simple(h, w, 0, D_BRANCH, _epi_gelu, "gmlp_in_u", tm)
                gv, mu, rstd = _proj_gelu_stats(h, w_v, 0, min(tm, 512))
                zg = _proj_simple(h, w, 2 * D_BRANCH, D_BRANCH, _epi_silu, "gmlp_in_z", tm)
                return _spatial_gate(gu, gv, zg, mu, rstd, gmlp_ln_g[j], gmlp_ln_b[j], ws, gmlp_b_s[j],
                                     min(tm, 1024))

            y = gmlp_branch(hl, tm_lat)
            yc = gmlp_branch(hc, tm_ctx) if need_ctx else None

        res, hl = finish(y, xl, i, who_lat(tm_out), not last, last)
        if last:
            out = res
        else:
            xl = res
        if need_ctx:
            xc, hc = finish(yc, xc, i, who_ctx, i + 1 < DEPTH - 1 or (i + 1) % 3 == 1, False)
    return out.reshape(batch, seq, d)
```

```python
import functools
import math

import numpy as np
import jax
import jax.numpy as jnp
from jax import lax
from jax.experimental import pallas as pl
from jax.experimental.pallas import tpu as pltpu

F32 = jnp.float32
BF16 = jnp.bfloat16

D_MODEL = 2048
D_BRANCH = 4096
DEPTH = 4
GRID_W = 64
FNET_GROUPS = 16
FNET_GROUP_DIM = 256
HEAD_DIM = 64
KV_HEADS = 8
Q_GROUP = 8
ATTN_BLOCK = 128
ROPE_BASE = 10000.0
GMLP_CHUNK = 128
GMLP_GROUPS = 16
EPS = 1e-6
NEG_INF = -1e30
LOG2_E = math.log2(math.e)

LANES = 128
MXU_COLS = 256
PROJ_TN = 1024
DFT_NB = 128
DFT_ROWS = 16
SLABS = D_BRANCH // LANES
VMEM_LIMIT = 56 * 1024 * 1024


def _cparams(n_axes, vmem=VMEM_LIMIT):
    return pltpu.CompilerParams(dimension_semantics=("arbitrary",) * n_axes,
                                vmem_limit_bytes=vmem)


def _silu(z):
    return 0.5 * z * (1.0 + jnp.tanh(0.5 * z))


def _gelu_tanh(x):
    c = math.sqrt(2.0 / math.pi)
    return 0.5 * x * (1.0 + jnp.tanh(c * (x + 0.044715 * (x * x * x))))


def _mod_rmsnorm(x, g, scale, shift):
    y = x * lax.rsqrt(jnp.mean(x * x, axis=-1, keepdims=True) + EPS) * g
    return y * (1.0 + scale) + shift


def _channel_dft_matrix():
    n = FNET_GROUP_DIM
    k = np.arange(n, dtype=np.float64)
    ang = 2.0 * np.pi * np.outer(k, k) / n
    s = 1.0 / math.sqrt(n)
    return np.concatenate([np.cos(ang) * s, -np.sin(ang) * s], axis=1).astype(np.float32)


def _position_dft_matrices(seq):
    na, nb = seq // DFT_NB, DFT_NB
    a = np.arange(na, dtype=np.float64)
    b = np.arange(nb, dtype=np.float64)
    ang = 2.0 * np.pi * (a[None, None, :] * a[None, :, None] / na + b[:, None, None] * a[None, :, None] / seq)
    mr = np.cos(ang) / math.sqrt(na)
    mi = -np.sin(ang) / math.sqrt(na)
    fa = np.concatenate([np.concatenate([mr, -mi], axis=2), np.concatenate([mi, mr], axis=2)], axis=1)
    fa = np.concatenate([fa[0::2], fa[1::2]], axis=2)
    angb = 2.0 * np.pi * np.outer(b, b) / nb
    fb = np.concatenate([np.cos(angb), np.sin(angb)], axis=1) / math.sqrt(nb)
    return fa.astype(np.float32), fb.astype(np.float32)


def _dense_dft_matrix(n):
    k = np.arange(n, dtype=np.float64)
    ang = 2.0 * np.pi * np.outer(k, k) / n
    return (np.concatenate([np.cos(ang), np.sin(ang)], axis=1) / math.sqrt(n)).astype(np.float32)


def _rope_tables(seq):
    nf = HEAD_DIM // 4
    inv = ROPE_BASE ** (-np.arange(nf, dtype=np.float64) / nf)
    t = np.arange(seq)
    rows = (t // GRID_W).astype(np.float64)
    cols = (t % GRID_W).astype(np.float64)
    parts_c, parts_s = [], []
    for pos in (rows, cols):
        ang = pos[:, None] * inv[None, :]
        parts_c += [np.cos(ang), np.cos(ang)]
        parts_s += [-np.sin(ang), np.sin(ang)]
    cos = np.concatenate(parts_c, axis=1)
    sin = np.concatenate(parts_s, axis=1)
    reps = LANES // HEAD_DIM
    return (np.tile(cos, (1, reps)).astype(np.float32), np.tile(sin, (1, reps)).astype(np.float32))


def _mods_kernel(cv_ref, w_ref, b_ref, o_ref):
    a = _silu(cv_ref[...])
    o_ref[...] = jnp.dot(a, w_ref[...], preferred_element_type=F32,
                         precision=lax.Precision.HIGHEST) + b_ref[...]


def _mods(cvec, ada_w, ada_b):
    depth, d, n3 = ada_w.shape
    tn = 1024
    return pl.pallas_call(
        _mods_kernel,
        grid=(depth, n3 // tn),
        in_specs=[pl.BlockSpec((8, d), lambda i, j: (0, 0)),
                  pl.BlockSpec((None, d, tn), lambda i, j: (i, 0, j)),
                  pl.BlockSpec((None, 1, tn), lambda i, j: (i, 0, j))],
        out_specs=pl.BlockSpec((None, 8, tn), lambda i, j: (i, 0, j)),
        out_shape=jax.ShapeDtypeStruct((depth, 8, n3), F32),
        compiler_params=_cparams(2),
        name="ada_mods",
    )(cvec, ada_w, ada_b.reshape(depth, 1, n3))


def _mod_spec(layer, kind, who_of_row):
    return pl.BlockSpec((None, None, None, 1, D_MODEL),
                        lambda r, *_: (layer, who_of_row(r), kind, 0, 0))


def _row_spec(vec_layer):
    return pl.BlockSpec((None, 1, D_MODEL), lambda r, *_: (vec_layer, 0, 0))


def _prenorm_kernel(x_ref, g_ref, sc_ref, sh_ref, h_ref):
    h_ref[...] = _mod_rmsnorm(x_ref[...], g_ref[...], sc_ref[...], sh_ref[...]).astype(BF16)


def _prenorm(x2, norm_g3, mods, layer, who, tm):
    m = x2.shape[0]
    return pl.pallas_call(
        _prenorm_kernel,
        grid=(m // tm,),
        in_specs=[pl.BlockSpec((tm, D_MODEL), lambda r: (r, 0)),
                  _row_spec(layer), _mod_spec(layer, 1, who), _mod_spec(layer, 0, who)],
        out_specs=pl.BlockSpec((tm, D_MODEL), lambda r: (r, 0)),
        out_shape=jax.ShapeDtypeStruct((m, D_MODEL), BF16),
        compiler_params=_cparams(1),
        name="prenorm",
    )(x2, norm_g3, mods, mods)


def _proj_kernel(epilogue, n_extra, chunk, a_ref, w_ref, *rest):
    if len(a_ref.shape) == 3:
        a_flat = rest[-1]
        rest = rest[:-1]

        @pl.when(pl.program_id(1) == 0)
        def _():
            a_flat[...] = a_ref[...].reshape(a_flat.shape)

        a = a_flat[...]
    else:
        a = a_ref[...]
    for c0 in range(0, w_ref.shape[1], chunk):
        acc = jnp.dot(a, w_ref[:, c0:c0 + chunk].astype(BF16), preferred_element_type=F32)
        epilogue(acc, c0, rest[:n_extra], rest[n_extra:])


def _proj(h, w, col0, ncols, tm, tn, epilogue, extras, extra_specs, out_shapes, out_specs, name,
          lhs_spec=None, chunk=MXU_COLS, w_layer=None):
    k = h.shape[-1]
    m = h.size // k
    off = col0 // tn
    scratch = [pltpu.VMEM((tm, k), h.dtype)] if lhs_spec is not None else []
    if lhs_spec is None:
        lhs_spec = pl.BlockSpec((tm, k), lambda i, j: (i, 0))
    if w.ndim == 3:
        w_spec = pl.BlockSpec((None, k, tn), lambda i, j: (w_layer, 0, j + off))
    else:
        w_spec = pl.BlockSpec((k, tn), lambda i, j: (0, j + off))
    return pl.pallas_call(
        functools.partial(_proj_kernel, epilogue, len(extras), min(chunk, tn)),
        grid=(m // tm, ncols // tn),
        in_specs=[lhs_spec, w_spec] + list(extra_specs),
        out_specs=out_specs,
        out_shape=out_shapes,
        scratch_shapes=scratch,
        compiler_params=_cparams(2),
        name=name,
    )(h, w, *extras)


def _epi_silu(acc, c0, extras, outs):
    outs[0][:, c0:c0 + acc.shape[1]] = _silu(acc).astype(BF16)


def _epi_silu_slabs(acc, c0, extras, outs):
    for t in range(acc.shape[1] // LANES):
        outs[0][c0 // LANES + t] = _silu(acc[:, t * LANES:(t + 1) * LANES]).astype(BF16)


def _epi_gelu(acc, c0, extras, outs):
    outs[0][:, c0:c0 + acc.shape[1]] = _gelu_tanh(acc).astype(BF16)


def _epi_cast(acc, c0, extras, outs):
    outs[0][:, c0:c0 + acc.shape[1]] = acc.astype(BF16)


def _epi_scale_cast(scale, acc, c0, extras, outs):
    outs[0][:, c0:c0 + acc.shape[1]] = (acc * scale).astype(BF16)


def _epi_channel_dft(split_rows, acc, c0, extras, outs):
    cs_ref = extras[0]
    ar_ref, ai_ref = outs
    tm = acc.shape[0]
    for gl in range(acc.shape[1] // FNET_GROUP_DIM):
        g = c0 // FNET_GROUP_DIM + gl
        ub = acc[:, gl * FNET_GROUP_DIM:(gl + 1) * FNET_GROUP_DIM].astype(BF16)
        ab = jnp.dot(ub, cs_ref[g], preferred_element_type=F32)
        if split_rows:
            ab4 = ab.reshape(tm // DFT_ROWS, 2, DFT_ROWS // 2, ab.shape[1])
            ab = jnp.concatenate([ab4[:, hf].reshape(tm // 2, ab.shape[1]) for hf in range(2)], axis=0)
        ar_ref[2 * g] = ab[:, 0:128]
        ar_ref[2 * g + 1] = ab[:, 128:256]
        ai_ref[2 * g] = ab[:, 256:384]
        ai_ref[2 * g + 1] = ab[:, 384:512]


def _store_tile(out_ref, col, y, dup_heads):
    if not dup_heads:
        out_ref[:, col:col + LANES] = y.astype(BF16)
        return
    lane = lax.broadcasted_iota(jnp.int32, (1, LANES), 1)
    lo = lane < HEAD_DIM
    swapped = pltpu.roll(y, HEAD_DIM, 1)
    out_ref[:, 2 * col:2 * col + LANES] = jnp.where(lo, y, swapped).astype(BF16)
    out_ref[:, 2 * col + LANES:2 * col + 2 * LANES] = jnp.where(lo, swapped, y).astype(BF16)


def _epi_rope(scale, dup_heads, acc, c0, extras, outs):
    cos = extras[0][...]
    sin = extras[1][...]
    lane = lax.broadcasted_iota(jnp.int32, (1, LANES), 1)
    first = (lane % 32) < 16
    for t in range(acc.shape[1] // LANES):
        x = acc[:, t * LANES:(t + 1) * LANES]
        partner = jnp.where(first, pltpu.roll(x, LANES - 16, 1), pltpu.roll(x, 16, 1))
        y = x * cos + partner * sin
        if scale != 1.0:
            y = y * scale
        _store_tile(outs[0], c0 + t * LANES, y, dup_heads)


def _epi_cast_dup(acc, c0, extras, outs):
    for t in range(acc.shape[1] // LANES):
        _store_tile(outs[0], c0 + t * LANES, acc[:, t * LANES:(t + 1) * LANES], True)


def _proj_simple(h, w, col0, ncols, epilogue, name, tm, tn=PROJ_TN, w_layer=None, out_mult=1):
    m = h.shape[0]
    tn = min(tn, ncols)
    return _proj(h, w, col0, ncols, tm, tn, epilogue, (), (),
                 jax.ShapeDtypeStruct((m, out_mult * ncols), BF16),
                 pl.BlockSpec((tm, out_mult * tn), lambda i, j: (i, j)), name, w_layer=w_layer)


def _proj_rope(h, w, col0, ncols, scale, cos_t, sin_t, seq, name, tm, tn=PROJ_TN, w_layer=None,
               dup_heads=False):
    m = h.shape[0]
    tn = min(tn, ncols)
    per_batch = seq // tm
    out_mult = 2 if dup_heads else 1
    tab_spec = pl.BlockSpec((tm, LANES), lambda i, j: (i % per_batch, 0))
    return _proj(h, w, col0, ncols, tm, tn, functools.partial(_epi_rope, scale, dup_heads),
                 (cos_t, sin_t), (tab_spec, tab_spec),
                 jax.ShapeDtypeStruct((m, out_mult * ncols), BF16),
                 pl.BlockSpec((tm, out_mult * tn), lambda i, j: (i, j)), name, w_layer=w_layer)


def _proj_silu_slabs(h, w, col0, name, tm, tn=PROJ_TN, w_layer=None):
    m = h.shape[0]
    return _proj(h, w, col0, D_BRANCH, tm, tn, _epi_silu_slabs, (), (),
                 jax.ShapeDtypeStruct((SLABS, m, LANES), BF16),
                 pl.BlockSpec((tn // LANES, tm, LANES), lambda i, j: (j, i, 0)), name, w_layer=w_layer)


def _proj_channel_dft(h, w, cs, name, tm, seq=None, tn=PROJ_TN):
    m = h.shape[0]
    lhs_spec = None
    if seq is not None:
        na, tiles = seq // DFT_NB, DFT_NB // DFT_ROWS
        assert tm == na * DFT_ROWS
        h = h.reshape(m // seq, na, DFT_NB, D_MODEL)
        lhs_spec = pl.BlockSpec((None, na, DFT_ROWS, D_MODEL), lambda i, j: (i // tiles, 0, i % tiles, 0))
    slab_shape = jax.ShapeDtypeStruct((SLABS, m, LANES), F32)
    slab_spec = pl.BlockSpec((tn // LANES, tm, LANES), lambda i, j: (j, i, 0))
    groups = tn // FNET_GROUP_DIM
    cs_spec = pl.BlockSpec((groups,) + cs.shape[1:], lambda i, j: (j, 0, 0))
    return _proj(h, w, 0, D_BRANCH, tm, tn, functools.partial(_epi_channel_dft, seq is not None), (cs,), (cs_spec,),
                 (slab_shape, slab_shape), (slab_spec, slab_spec), name, lhs_spec=lhs_spec,
                 chunk=2 * MXU_COLS)


def _fold_mix_kernel(c_ref, s_ref, wm_ref, o_ref):
    wm = wm_ref[...]
    gd = FNET_GROUP_DIM
    hp = lax.Precision.HIGHEST
    o_ref[:, 0:gd] = jnp.dot(c_ref[...], wm, preferred_element_type=F32, precision=hp).astype(BF16)
    o_ref[:, gd:2 * gd] = jnp.dot(s_ref[...], wm, preferred_element_type=F32, precision=hp).astype(BF16)


def _fold_mix(w_mix):
    gd = FNET_GROUP_DIM
    cs = _channel_dft_matrix()
    mat = pl.BlockSpec((gd, gd), lambda g: (0, 0))
    return pl.pallas_call(
        _fold_mix_kernel,
        grid=(FNET_GROUPS,),
        in_specs=[mat, mat, pl.BlockSpec((None, gd, gd), lambda g: (g, 0, 0))],
        out_specs=pl.BlockSpec((None, gd, 2 * gd), lambda g: (g, 0, 0)),
        out_shape=jax.ShapeDtypeStruct((FNET_GROUPS, gd, 2 * gd), BF16),
        compiler_params=_cparams(1),
        name="fnet_fold_mix",
    )(jnp.asarray(cs[:, :gd]), jnp.asarray(cs[:, gd:]), w_mix)


def _dft_kernel(na, pitch, gpitch, ar_ref, ai_ref, zg_ref, fa_ref, fb_ref, y_ref, er_ref, ei_ref, g_ref):
    half_rows = na * (DFT_ROWS // 2)

    def gather_a(b):
        start = (b // (DFT_ROWS // 2)) * half_rows + b % (DFT_ROWS // 2)
        zr = ar_ref[pl.ds(start, na, stride=DFT_ROWS // 2), :]
        zi = ai_ref[pl.ds(start, na, stride=DFT_ROWS // 2), :]
        return jnp.concatenate([zr, zi], axis=0).astype(BF16)

    def stage_a(pair, carry):
        b0 = 2 * pair
        d0, d1 = gather_a(b0), gather_a(b0 + 1)
        zero = jnp.zeros_like(d0)
        rhs = jnp.concatenate([jnp.concatenate([d0, zero], axis=1),
                               jnp.concatenate([zero, d1], axis=1)], axis=0)
        e = jnp.dot(fa_ref[pair], rhs, preferred_element_type=F32)
        for j in range(2):
            off = pl.multiple_of((b0 + j) * pitch, 8)
            er_ref[pl.ds(off, na), :] = e[:na, j * LANES:(j + 1) * LANES]
            ei_ref[pl.ds(off, na), :] = e[na:, j * LANES:(j + 1) * LANES]
        return carry

    lax.fori_loop(0, DFT_NB // 2, stage_a, 0, unroll=16)

    def gather_b(ka):
        er = er_ref[pl.ds(ka, DFT_NB, stride=pitch), :]
        ei = ei_ref[pl.ds(ka, DFT_NB, stride=pitch), :]
        return jnp.concatenate([er, ei], axis=0).astype(BF16)

    def stage_b(pair, carry):
        ka0 = 2 * pair
        rhs = jnp.concatenate([gather_b(ka0), gather_b(ka0 + 1)], axis=1)
        g = jnp.dot(fb_ref[...], rhs, preferred_element_type=F32)
        for j in range(2):
            off = pl.multiple_of((ka0 + j) * gpitch, 8)
            g_ref[pl.ds(off, DFT_NB), :] = g[:, j * LANES:(j + 1) * LANES]
        return carry

    lax.fori_loop(0, na // 2, stage_b, 0, unroll=8)

    def gate(kb, carry):
        rows = pl.ds(pl.multiple_of(kb * na, na), na)
        g = g_ref[pl.ds(kb, na, stride=gpitch), :]
        y_ref[rows, :] = (g * zg_ref[rows, :].astype(F32)).astype(BF16)
        return carry

    lax.fori_loop(0, DFT_NB, gate, 0, unroll=16)


def _position_dft(ar, ai, zg, batch, seq):
    na = seq // DFT_NB
    pitch = na + 8
    gpitch = DFT_NB + 8
    fa_np, fb_np = _position_dft_matrices(seq)
    fa = jnp.asarray(fa_np).astype(BF16)
    fb = jnp.asarray(fb_np).astype(BF16)
    slab = pl.BlockSpec((None, seq, LANES), lambda s, b: (s, b, 0))
    return pl.pallas_call(
        functools.partial(_dft_kernel, na, pitch, gpitch),
        grid=(SLABS, batch),
        in_specs=[slab, slab, slab,
                  pl.BlockSpec(fa.shape, lambda s, b: (0, 0, 0)),
                  pl.BlockSpec(fb.shape, lambda s, b: (0, 0))],
        out_specs=slab,
        out_shape=jax.ShapeDtypeStruct(ar.shape, BF16),
        scratch_shapes=[pltpu.VMEM((DFT_NB * pitch, LANES), F32),
                        pltpu.VMEM((DFT_NB * pitch, LANES), F32),
                        pltpu.VMEM((na * gpitch, LANES), F32)],
        compiler_params=_cparams(2),
        name="position_dft",
    )(ar, ai, zg, fa, fb)


def _dft_dense_kernel(ar_ref, ai_ref, zg_ref, fd_ref, y_ref):
    d = jnp.concatenate([ar_ref[...], ai_ref[...]], axis=0).astype(BF16)
    g = jnp.dot(fd_ref[...], d, preferred_element_type=F32)
    y_ref[...] = (g * zg_ref[...].astype(F32)).astype(BF16)


def _position_dft_dense(ar, ai, zg, batch, seq):
    fd = jnp.asarray(_dense_dft_matrix(seq)).astype(BF16)
    slab = pl.BlockSpec((None, seq, LANES), lambda s, b: (s, b, 0))
    return pl.pallas_call(
        _dft_dense_kernel,
        grid=(SLABS, batch),
        in_specs=[slab, slab, slab, pl.BlockSpec(fd.shape, lambda s, b: (0, 0))],
        out_specs=slab,
        out_shape=jax.ShapeDtypeStruct(ar.shape, BF16),
        compiler_params=_cparams(2),
        name="position_dft_dense",
    )(ar, ai, zg, fd)


def _attn_kernel(n_band, nblk, sink_ref, q_ref, *refs):
    k_refs = refs[:n_band + 1]
    v_refs = refs[n_band + 1:2 * n_band + 2]
    zg_ref, y_ref = refs[2 * n_band + 2:]
    h = pl.program_id(1)
    i = pl.program_id(2)
    blk = ATTN_BLOCK
    n_pair = Q_GROUP // 2

    lane = lax.broadcasted_iota(jnp.int32, (1, LANES), 1)
    lo = lane < HEAD_DIM
    keys = jnp.concatenate([r[...] for r in k_refs], axis=0)
    vals = jnp.concatenate([r[...] for r in v_refs], axis=0)
    ones = jnp.ones_like(vals)
    v_ext = (jnp.where(lo, vals, ones), jnp.where(lo, ones, vals))
    dn = (((1,), (1,)), ((), ()))
    if n_band:
        r = lax.broadcasted_iota(jnp.int32, (blk, blk), 0)
        c = lax.broadcasted_iota(jnp.int32, (blk, blk), 1)
        prev_ok = c >= r + jnp.where(i > 0, 0, blk)
        next_ok = c <= r - jnp.where(i < nblk - 1, 0, blk)

    q = q_ref[...]
    q4 = jnp.concatenate([q[:, t * LANES:(t + 1) * LANES] for t in range(n_pair)], axis=0)
    qzero = jnp.zeros_like(q4)

    def scores(hd):
        qm = jnp.where(lo, q4, qzero) if hd == 0 else jnp.where(lo, qzero, q4)
        return lax.dot_general(qm, keys, dn, preferred_element_type=F32)

    def finish(hd, s):
        p_rows, sink_rows = [], []
        for t in range(n_pair):
            st = s[t * blk:(t + 1) * blk]
            parts = [st[:, j * blk:(j + 1) * blk] for j in range(st.shape[1] // blk)]
            if n_band:
                parts[0] = jnp.where(prev_ok, parts[0], NEG_INF)
                parts[2] = jnp.where(next_ok, parts[2], NEG_INF)
            sk = sink_ref[h, 2 * t + hd] * LOG2_E
            mx = parts[0]
            for part in parts[1:]:
                mx = jnp.maximum(mx, part)
            mx = jnp.maximum(jnp.max(mx, axis=-1, keepdims=True), sk)
            p_rows.append(jnp.concatenate([jnp.exp2(part - mx).astype(BF16) for part in parts], axis=1))
            sink_rows.append(jnp.exp2(sk - mx))
        p = jnp.concatenate(p_rows, axis=0)
        o_ext = jnp.dot(p, v_ext[hd], preferred_element_type=F32)
        return o_ext, jnp.concatenate(sink_rows, axis=0)

    s_lo, s_hi = scores(0), scores(1)
    o_lo, sink_lo = finish(0, s_lo)
    o_hi, sink_hi = finish(1, s_hi)
    o = jnp.where(lo, o_lo, o_hi)
    denom = jnp.where(lo, pltpu.roll(o_lo, HEAD_DIM, 1) + sink_lo, pltpu.roll(o_hi, HEAD_DIM, 1) + sink_hi)
    res = o / denom
    for t in range(n_pair):
        tile = slice(t * LANES, (t + 1) * LANES)
        y_ref[:, tile] = (res[t * blk:(t + 1) * blk] * zg_ref[:, tile].astype(F32)).astype(BF16)


def _attention(q, kd, vd, kcd, vcd, zg, sink, batch, seq, ctx_len, use_band):
    blk = ATTN_BLOCK
    nblk = seq // blk
    qw = Q_GROUP * HEAD_DIM
    q_spec = pl.BlockSpec((blk, qw), lambda b, h, i: (b * nblk + i, h))
    ctx_spec = pl.BlockSpec((ctx_len, LANES), lambda b, h, i: (b, h))
    if use_band:
        def band(delta):
            return pl.BlockSpec(
                (blk, LANES), lambda b, h, i: (b * nblk + jnp.clip(i + delta, 0, nblk - 1), h))
        k_specs = [band(-1), band(0), band(1), ctx_spec]
        k_args, v_args = [kd, kd, kd, kcd], [vd, vd, vd, vcd]
        n_band = 3
    else:
        k_specs, k_args, v_args, n_band = [ctx_spec], [kcd], [vcd], 0
    return pl.pallas_call(
        functools.partial(_attn_kernel, n_band, nblk),
        grid=(batch, KV_HEADS, nblk),
        in_specs=[pl.BlockSpec(memory_space=pltpu.SMEM), q_spec] + k_specs + k_specs + [q_spec],
        out_specs=q_spec,
        out_shape=jax.ShapeDtypeStruct(q.shape, BF16),
        compiler_params=_cparams(3),
        name="attention_band" if use_band else "attention_ctx",
    )(sink, q, *k_args, *v_args, zg)


def _gelu_stats_kernel(chunk, a_ref, w_ref, gv_ref, mu_ref, rstd_ref):
    a = a_ref[...]
    n = w_ref.shape[1]
    mean = m2 = None
    for idx, c0 in enumerate(range(0, n, chunk)):
        ge = _gelu_tanh(jnp.dot(a, w_ref[:, c0:c0 + chunk], preferred_element_type=F32))
        gv_ref[:, c0:c0 + chunk] = ge.astype(BF16)
        cmean = jnp.mean(ge, axis=-1, keepdims=True)
        d = ge - cmean
        cm2 = jnp.sum(d * d, axis=-1, keepdims=True)
        if idx == 0:
            mean, m2 = cmean, cm2
        else:
            delta = cmean - mean
            mean = mean + delta * (1.0 / (idx + 1))
            m2 = m2 + cm2 + delta * delta * (chunk * idx / (idx + 1))
    mu_ref[...] = mean
    rstd_ref[...] = lax.rsqrt(m2 * (1.0 / n) + EPS)


def _proj_gelu_stats(h, w, col0, tm, chunk=2 * MXU_COLS):
    m, k = h.shape
    n = D_BRANCH
    col = pl.BlockSpec((tm, 1), lambda i: (i, 0))
    stat = jax.ShapeDtypeStruct((m, 1), F32)
    return pl.pallas_call(
        functools.partial(_gelu_stats_kernel, chunk),
        grid=(m // tm,),
        in_specs=[pl.BlockSpec((tm, k), lambda i: (i, 0)),
                  pl.BlockSpec((k, n), lambda i: (0, col0 // n), pipeline_mode=pl.Buffered(1))],
        out_specs=(pl.BlockSpec((tm, n), lambda i: (i, 0)), col, col),
        out_shape=(jax.ShapeDtypeStruct((m, n), BF16), stat, stat),
        compiler_params=_cparams(1),
        name="gmlp_in_v",
    )(h, w)


def _sgu_kernel(n_chunks, n_groups, gu_ref, gv_ref, zg_ref, mu_ref, rstd_ref, lg_ref, lb_ref,
                ws_ref, bs_ref, y_ref):
    gd = D_BRANCH // GMLP_GROUPS
    mu = mu_ref[...]
    rstd = rstd_ref[...]
    for gl in range(n_groups):
        cols = slice(gl * gd, (gl + 1) * gd)
        ws = ws_ref[gl]
        bs = bs_ref[gl]
        lg = lg_ref[:, cols]
        lb = lb_ref[:, cols]
        for c in range(n_chunks):
            rows = slice(c * GMLP_CHUNK, (c + 1) * GMLP_CHUNK)
            vn = ((gv_ref[rows, cols].astype(F32) - mu[rows]) * rstd[rows] * lg + lb).astype(BF16)
            s = jnp.dot(ws, vn, preferred_element_type=F32) + bs
            y_ref[rows, cols] = (gu_ref[rows, cols].astype(F32) * s
                                 * zg_ref[rows, cols].astype(F32)).astype(BF16)


def _spatial_gate(gu, gv, zg, mu, rstd, ln_g, ln_b, w_s, b_s, tm, n_groups=2):
    m = gu.shape[0]
    gd = D_BRANCH // GMLP_GROUPS
    tn = n_groups * gd
    tile = pl.BlockSpec((tm, tn), lambda r, g: (r, g))
    col = pl.BlockSpec((tm, 1), lambda r, g: (r, 0))
    vec = pl.BlockSpec((1, tn), lambda r, g: (0, g))
    return pl.pallas_call(
        functools.partial(_sgu_kernel, tm // GMLP_CHUNK, n_groups),
        grid=(m // tm, GMLP_GROUPS // n_groups),
        in_specs=[tile, tile, tile, col, col, vec, vec,
                  pl.BlockSpec((n_groups, GMLP_CHUNK, GMLP_CHUNK), lambda r, g: (g, 0, 0)),
                  pl.BlockSpec((n_groups, GMLP_CHUNK, 1), lambda r, g: (g, 0, 0))],
        out_specs=tile,
        out_shape=jax.ShapeDtypeStruct(gu.shape, BF16),
        compiler_params=_cparams(2),
        name="gmlp_spatial_gate",
    )(gu, gv, zg, mu, rstd, ln_g.reshape(1, D_BRANCH), ln_b.reshape(1, D_BRANCH),
      w_s, b_s.reshape(GMLP_GROUPS, GMLP_CHUNK, 1))


def _wout_kernel(mode, y_ref, w_ref, x_ref, gate_ref, *refs):
    if len(y_ref.shape) == 3:
        y = jnp.concatenate([y_ref[s] for s in range(y_ref.shape[0])], axis=1)
    else:
        y = y_ref[...]
    acc = jnp.dot(y, w_ref[...], preferred_element_type=F32)
    xn = x_ref[...] + gate_ref[...] * acc
    if mode == "final":
        g_ref, o_ref = refs
        o_ref[...] = xn * lax.rsqrt(jnp.mean(xn * xn, axis=-1, keepdims=True) + EPS) * g_ref[...]
    elif mode == "next":
        g_ref, sc_ref, sh_ref, xo_ref, h_ref = refs
        xo_ref[...] = xn
        h_ref[...] = _mod_rmsnorm(xn, g_ref[...], sc_ref[...], sh_ref[...]).astype(BF16)
    else:
        refs[0][...] = xn


def _out_proj(y, w_out, x2, mods, layer, who, mode, norm_vec, tm):
    m = x2.shape[0]
    row = pl.BlockSpec((tm, D_MODEL), lambda r: (r, 0))
    if y.ndim == 3:
        y_spec = pl.BlockSpec((SLABS, tm, LANES), lambda r: (0, r, 0))
    else:
        y_spec = pl.BlockSpec((tm, D_BRANCH), lambda r: (r, 0))
    in_specs = [y_spec,
                pl.BlockSpec((None, D_BRANCH, D_MODEL), lambda r: (layer, 0, 0),
                             pipeline_mode=pl.Buffered(1)),
                row, _mod_spec(layer, 2, who)]
    args = [y, w_out, x2, mods]
    xs = jax.ShapeDtypeStruct((m, D_MODEL), F32)
    if mode == "final":
        in_specs.append(pl.BlockSpec((1, D_MODEL), lambda r: (0, 0)))
        args.append(norm_vec.reshape(1, D_MODEL))
        out_shape, out_specs = xs, row
    elif mode == "next":
        in_specs += [_row_spec(layer + 1), _mod_spec(layer + 1, 1, who), _mod_spec(layer + 1, 0, who)]
        args += [norm_vec, mods, mods]
        out_shape = (xs, jax.ShapeDtypeStruct((m, D_MODEL), BF16))
        out_specs = (row, row)
    else:
        out_shape, out_specs = xs, row
    return pl.pallas_call(
        functools.partial(_wout_kernel, mode),
        grid=(m // tm,),
        in_specs=in_specs,
        out_specs=out_specs,
        out_shape=out_shape,
        compiler_params=_cparams(1),
        name="out_proj_" + mode,
    )(*args)


def kernel(x, c, ctx, c_ctx, norm_g, ada_w, ada_b, w_out, fnet_w_in, fnet_w_mix, attn_w_in, attn_sink,
           gmlp_w_in, gmlp_w_s, gmlp_b_s, gmlp_ln_g, gmlp_ln_b, final_g):
    batch, seq, d = x.shape
    ctx_len = ctx.shape[1]
    assert d == D_MODEL and seq % (DFT_NB * 8) == 0 and seq % GRID_W == 0 and batch < 8
    m_lat, m_ctx = batch * seq, batch * ctx_len
    tm_lat = min(2048, seq)
    tm_ctx = ctx_len
    tm_out = 256

    def who_lat(tm):
        return lambda r: (r * tm) // seq

    who_ctx = lambda r: batch

    cvec = jnp.zeros((8, d), F32).at[:batch].set(c).at[batch].set(c_ctx)
    mods = _mods(cvec, ada_w, ada_b).reshape(DEPTH, 8, 3, 1, d)
    norm_g3 = norm_g.reshape(DEPTH, 1, d)
    w_out_b = w_out.astype(BF16)

    xl = x.reshape(m_lat, d)
    xc = ctx.reshape(m_ctx, d)
    hl = _prenorm(xl, norm_g3, mods, 0, who_lat(512), 512)
    hc = _prenorm(xc, norm_g3, mods, 0, who_ctx, tm_ctx)

    cos_np, sin_np = _rope_tables(seq)
    cos_t, sin_t = jnp.asarray(cos_np), jnp.asarray(sin_np)

    def finish(y, x2, layer, who_fn, need_next, is_final):
        if is_final:
            return _out_proj(y, w_out_b, x2, mods, layer, who_fn, "final", final_g, tm_out), None
        if need_next:
            return _out_proj(y, w_out_b, x2, mods, layer, who_fn, "next", norm_g3, tm_out)
        return _out_proj(y, w_out_b, x2, mods, layer, who_fn, "plain", None, tm_out), None

    out = None
    for i in range(DEPTH):
        kind, j = i % 3, i // 3
        need_ctx = i < DEPTH - 1
        last = i == DEPTH - 1
        if kind == 0:
            cs = _fold_mix(fnet_w_mix[j])
            tm_dft = (seq // DFT_NB) * DFT_ROWS
            w_u = fnet_w_in[j, :, :D_BRANCH].astype(BF16)
            ar, ai = _proj_channel_dft(hl, w_u, cs, "fnet_in_u", tm_dft, seq=seq)
            zg = _proj_silu_slabs(hl, fnet_w_in, D_BRANCH, "fnet_in_z", tm_lat, w_layer=j)
            y = _position_dft(ar, ai, zg, batch, seq)
            yc = None
            if need_ctx:
                ar, ai = _proj_channel_dft(hc, w_u, cs, "fnet_in_u_ctx", tm_ctx)
                zg = _proj_silu_slabs(hc, fnet_w_in, D_BRANCH, "fnet_in_z_ctx", tm_ctx, w_layer=j)
                yc = _position_dft_dense(ar, ai, zg, batch, ctx_len)
        elif kind == 1:
            kvw = KV_HEADS * HEAD_DIM
            k0, v0, z0 = D_BRANCH, D_BRANCH + kvw, D_BRANCH + 2 * kvw
            sink = attn_sink[j].reshape(KV_HEADS, Q_GROUP)
            scale = HEAD_DIM ** -0.5 * LOG2_E
            q = _proj_rope(hl, attn_w_in, 0, D_BRANCH, scale, cos_t, sin_t, seq, "attn_in_q", tm_lat, w_layer=j)
            kd = _proj_rope(hl, attn_w_in, k0, kvw, 1.0, cos_t, sin_t, seq, "attn_in_k", tm_lat, w_layer=j,
                            dup_heads=True)
            vd = _proj_simple(hl, attn_w_in, v0, kvw, _epi_cast_dup, "attn_in_v", tm_lat, w_layer=j, out_mult=2)
            zg = _proj_simple(hl, attn_w_in, z0, D_BRANCH, _epi_silu, "attn_in_z", tm_lat, w_layer=j)
            kcd = _proj_simple(hc, attn_w_in, k0, kvw, _epi_cast_dup, "attn_in_kc", tm_ctx, w_layer=j, out_mult=2)
            vcd = _proj_simple(hc, attn_w_in, v0, kvw, _epi_cast_dup, "attn_in_vc", tm_ctx, w_layer=j, out_mult=2)
            y = _attention(q, kd, vd, kcd, vcd, zg, sink, batch, seq, ctx_len, True)
            yc = None
            if need_ctx:
                qc = _proj_simple(hc, attn_w_in, 0, D_BRANCH, functools.partial(_epi_scale_cast, scale),
                                  "attn_in_qc", tm_ctx, w_layer=j)
                zgc = _proj_simple(hc, attn_w_in, z0, D_BRANCH, _epi_silu, "attn_in_zc", tm_ctx, w_layer=j)
                yc = _attention(qc, None, None, kcd, vcd, zgc, sink, batch, ctx_len, ctx_len, False)
        else:
            w_v = gmlp_w_in[j, :, D_BRANCH:2 * D_BRANCH].astype(BF16)
            ws = gmlp_w_s[j].astype(BF16)

            def gmlp_branch(h, tm):
                gu = _proj_simple(h, gmlp_w_in, 0, D_BRANCH, _epi_gelu, "gmlp_in_u", tm, w_layer=j)
                gv, mu, rstd = _proj_gelu_stats(h, w_v, 0, min(tm, 512))
                zg = _proj_simple(h, gmlp_w_in, 2 * D_BRANCH, D_BRANCH, _epi_silu, "gmlp_in_z", tm, w_layer=j)
                return _spatial_gate(gu, gv, zg, mu, rstd, gmlp_ln_g[j], gmlp_ln_b[j], ws, gmlp_b_s[j],
                                     min(tm, 1024))

            y = gmlp_branch(hl, tm_lat)
            yc = gmlp_branch(hc, tm_ctx) if need_ctx else None

        res, hl = finish(y, xl, i, who_lat(tm_out), not last, last)
        if last:
            out = res
        else:
            xl = res
        if need_ctx:
            xc, hc = finish(yc, xc, i, who_ctx, i + 1 < DEPTH - 1 or (i + 1) % 3 == 1, False)
    return out.reshape(batch, seq, d)
```

```python
import functools
import math

import numpy as np
import jax
import jax.numpy as jnp
from jax import lax
from jax.experimental import pallas as pl
from jax.experimental.pallas import tpu as pltpu

F32 = jnp.float32
BF16 = jnp.bfloat16

D_MODEL = 2048
D_BRANCH = 4096
DEPTH = 4
GRID_W = 64
FNET_GROUPS = 16
FNET_GROUP_DIM = 256
HEAD_DIM = 64
KV_HEADS = 8
Q_GROUP = 8
ATTN_BLOCK = 128
ATTN_HEADS_PER_STEP = 4
ROPE_BASE = 10000.0
GMLP_CHUNK = 128
GMLP_GROUPS = 16
EPS = 1e-6
NEG_INF = -1e30
LOG2_E = math.log2(math.e)

LANES = 128
MXU_COLS = 256
PROJ_TN = 1024
DFT_NB = 128
DFT_ROWS = 16
SLABS = D_BRANCH // LANES
VMEM_LIMIT = 56 * 1024 * 1024


def _cparams(n_axes, vmem=VMEM_LIMIT):
    return pltpu.CompilerParams(dimension_semantics=("arbitrary",) * n_axes,
                                vmem_limit_bytes=vmem)


def _silu(z):
    return 0.5 * z * (1.0 + jnp.tanh(0.5 * z))


def _gelu_tanh(x):
    c = math.sqrt(2.0 / math.pi)
    return 0.5 * x * (1.0 + jnp.tanh(c * (x + 0.044715 * (x * x * x))))


def _mod_rmsnorm(x, g, scale, shift):
    y = x * lax.rsqrt(jnp.mean(x * x, axis=-1, keepdims=True) + EPS) * g
    return y * (1.0 + scale) + shift


def _channel_dft_matrix():
    n = FNET_GROUP_DIM
    k = np.arange(n, dtype=np.float64)
    ang = 2.0 * np.pi * np.outer(k, k) / n
    s = 1.0 / math.sqrt(n)
    return np.concatenate([np.cos(ang) * s, -np.sin(ang) * s], axis=1).astype(np.float32)


def _position_dft_matrices(seq):
    na, nb = seq // DFT_NB, DFT_NB
    a = np.arange(na, dtype=np.float64)
    b = np.arange(nb, dtype=np.float64)
    ang = 2.0 * np.pi * (a[None, None, :] * a[None, :, None] / na + b[:, None, None] * a[None, :, None] / seq)
    mr = np.cos(ang) / math.sqrt(na)
    mi = -np.sin(ang) / math.sqrt(na)
    fa = np.concatenate([np.concatenate([mr, -mi], axis=2), np.concatenate([mi, mr], axis=2)], axis=1)
    fa = np.concatenate([fa[0::2], fa[1::2]], axis=2)
    angb = 2.0 * np.pi * np.outer(b, b) / nb
    fb = np.concatenate([np.cos(angb), np.sin(angb)], axis=1) / math.sqrt(nb)
    return fa.astype(np.float32), fb.astype(np.float32)


def _dense_dft_matrix(n):
    k = np.arange(n, dtype=np.float64)
    ang = 2.0 * np.pi * np.outer(k, k) / n
    return (np.concatenate([np.cos(ang), np.sin(ang)], axis=1) / math.sqrt(n)).astype(np.float32)


def _rope_tables(seq):
    nf = HEAD_DIM // 4
    inv = ROPE_BASE ** (-np.arange(nf, dtype=np.float64) / nf)
    t = np.arange(seq)
    rows = (t // GRID_W).astype(np.float64)
    cols = (t % GRID_W).astype(np.float64)
    parts_c, parts_s = [], []
    for pos in (rows, cols):
        ang = pos[:, None] * inv[None, :]
        parts_c += [np.cos(ang), np.cos(ang)]
        parts_s += [-np.sin(ang), np.sin(ang)]
    cos = np.concatenate(parts_c, axis=1)
    sin = np.concatenate(parts_s, axis=1)
    reps = LANES // HEAD_DIM
    return (np.tile(cos, (1, reps)).astype(np.float32), np.tile(sin, (1, reps)).astype(np.float32))


def _mods_kernel(cv_ref, w_ref, b_ref, o_ref):
    a = _silu(cv_ref[...])
    o_ref[...] = jnp.dot(a, w_ref[...], preferred_element_type=F32,
                         precision=lax.Precision.HIGHEST) + b_ref[...]


def _mods(cvec, ada_w, ada_b):
    depth, d, n3 = ada_w.shape
    tn = 1024
    return pl.pallas_call(
        _mods_kernel,
        grid=(depth, n3 // tn),
        in_specs=[pl.BlockSpec((8, d), lambda i, j: (0, 0)),
                  pl.BlockSpec((None, d, tn), lambda i, j: (i, 0, j)),
                  pl.BlockSpec((None, 1, tn), lambda i, j: (i, 0, j))],
        out_specs=pl.BlockSpec((None, 8, tn), lambda i, j: (i, 0, j)),
        out_shape=jax.ShapeDtypeStruct((depth, 8, n3), F32),
        compiler_params=_cparams(2),
        name="ada_mods",
    )(cvec, ada_w, ada_b.reshape(depth, 1, n3))


def _mod_spec(layer, kind, who_of_row):
    return pl.BlockSpec((None, None, None, 1, D_MODEL),
                        lambda r, *_: (layer, who_of_row(r), kind, 0, 0))


def _row_spec(vec_layer):
    return pl.BlockSpec((None, 1, D_MODEL), lambda r, *_: (vec_layer, 0, 0))


def _prenorm_kernel(x_ref, g_ref, sc_ref, sh_ref, h_ref):
    h_ref[...] = _mod_rmsnorm(x_ref[...], g_ref[...], sc_ref[...], sh_ref[...]).astype(BF16)


def _prenorm(x2, norm_g3, mods, layer, who, tm):
    m = x2.shape[0]
    return pl.pallas_call(
        _prenorm_kernel,
        grid=(m // tm,),
        in_specs=[pl.BlockSpec((tm, D_MODEL), lambda r: (r, 0)),
                  _row_spec(layer), _mod_spec(layer, 1, who), _mod_spec(layer, 0, who)],
        out_specs=pl.BlockSpec((tm, D_MODEL), lambda r: (r, 0)),
        out_shape=jax.ShapeDtypeStruct((m, D_MODEL), BF16),
        compiler_params=_cparams(1),
        name="prenorm",
    )(x2, norm_g3, mods, mods)


def _proj_kernel(epilogue, n_extra, chunk, a_ref, w_ref, *rest):
    if len(a_ref.shape) == 3:
        a_flat = rest[-1]
        rest = rest[:-1]

        @pl.when(pl.program_id(1) == 0)
        def _():
            a_flat[...] = a_ref[...].reshape(a_flat.shape)

        a = a_flat[...]
    else:
        a = a_ref[...]
    for c0 in range(0, w_ref.shape[1], chunk):
        acc = jnp.dot(a, w_ref[:, c0:c0 + chunk].astype(BF16), preferred_element_type=F32)
        epilogue(acc, c0, rest[:n_extra], rest[n_extra:])


def _proj(h, w, col0, ncols, tm, tn, epilogue, extras, extra_specs, out_shapes, out_specs, name,
          lhs_spec=None, chunk=MXU_COLS, w_layer=None):
    k = h.shape[-1]
    m = h.size // k
    off = col0 // tn
    scratch = [pltpu.VMEM((tm, k), h.dtype)] if lhs_spec is not None else []
    if lhs_spec is None:
        lhs_spec = pl.BlockSpec((tm, k), lambda i, j: (i, 0))
    if w.ndim == 3:
        w_spec = pl.BlockSpec((None, k, tn), lambda i, j: (w_layer, 0, j + off))
    else:
        w_spec = pl.BlockSpec((k, tn), lambda i, j: (0, j + off))
    return pl.pallas_call(
        functools.partial(_proj_kernel, epilogue, len(extras), min(chunk, tn)),
        grid=(m // tm, ncols // tn),
        in_specs=[lhs_spec, w_spec] + list(extra_specs),
        out_specs=out_specs,
        out_shape=out_shapes,
        scratch_shapes=scratch,
        compiler_params=_cparams(2),
        name=name,
    )(h, w, *extras)


def _epi_silu(acc, c0, extras, outs):
    outs[0][:, c0:c0 + acc.shape[1]] = _silu(acc).astype(BF16)


def _epi_silu_slabs(acc, c0, extras, outs):
    for t in range(acc.shape[1] // LANES):
        outs[0][c0 // LANES + t] = _silu(acc[:, t * LANES:(t + 1) * LANES]).astype(BF16)


def _epi_gelu(acc, c0, extras, outs):
    outs[0][:, c0:c0 + acc.shape[1]] = _gelu_tanh(acc).astype(BF16)


def _epi_cast(acc, c0, extras, outs):
    outs[0][:, c0:c0 + acc.shape[1]] = acc.astype(BF16)


def _epi_scale_cast(scale, acc, c0, extras, outs):
    outs[0][:, c0:c0 + acc.shape[1]] = (acc * scale).astype(BF16)


def _epi_channel_dft(split_rows, acc, c0, extras, outs):
    cs_ref = extras[0]
    ar_ref, ai_ref = outs
    tm = acc.shape[0]
    for gl in range(acc.shape[1] // FNET_GROUP_DIM):
        g = c0 // FNET_GROUP_DIM + gl
        ub = acc[:, gl * FNET_GROUP_DIM:(gl + 1) * FNET_GROUP_DIM].astype(BF16)
        ab = jnp.dot(ub, cs_ref[g], preferred_element_type=F32)
        if split_rows:
            ab4 = ab.reshape(tm // DFT_ROWS, 2, DFT_ROWS // 2, ab.shape[1])
            ab = jnp.concatenate([ab4[:, hf].reshape(tm // 2, ab.shape[1]) for hf in range(2)], axis=0)
        ar_ref[2 * g] = ab[:, 0:128]
        ar_ref[2 * g + 1] = ab[:, 128:256]
        ai_ref[2 * g] = ab[:, 256:384]
        ai_ref[2 * g + 1] = ab[:, 384:512]


def _store_tile(out_ref, col, y, dup_heads):
    if not dup_heads:
        out_ref[:, col:col + LANES] = y.astype(BF16)
        return
    lane = lax.broadcasted_iota(jnp.int32, (1, LANES), 1)
    lo = lane < HEAD_DIM
    swapped = pltpu.roll(y, HEAD_DIM, 1)
    out_ref[:, 2 * col:2 * col + LANES] = jnp.where(lo, y, swapped).astype(BF16)
    out_ref[:, 2 * col + LANES:2 * col + 2 * LANES] = jnp.where(lo, swapped, y).astype(BF16)


def _epi_rope(scale, dup_heads, acc, c0, extras, outs):
    cos = extras[0][...]
    sin = extras[1][...]
    lane = lax.broadcasted_iota(jnp.int32, (1, LANES), 1)
    first = (lane % 32) < 16
    for t in range(acc.shape[1] // LANES):
        x = acc[:, t * LANES:(t + 1) * LANES]
        partner = jnp.where(first, pltpu.roll(x, LANES - 16, 1), pltpu.roll(x, 16, 1))
        y = x * cos + partner * sin
        if scale != 1.0:
            y = y * scale
        _store_tile(outs[0], c0 + t * LANES, y, dup_heads)


def _epi_cast_dup(acc, c0, extras, outs):
    for t in range(acc.shape[1] // LANES):
        _store_tile(outs[0], c0 + t * LANES, acc[:, t * LANES:(t + 1) * LANES], True)


def _proj_simple(h, w, col0, ncols, epilogue, name, tm, tn=PROJ_TN, w_layer=None, out_mult=1):
    m = h.shape[0]
    tn = min(tn, ncols)
    return _proj(h, w, col0, ncols, tm, tn, epilogue, (), (),
                 jax.ShapeDtypeStruct((m, out_mult * ncols), BF16),
                 pl.BlockSpec((tm, out_mult * tn), lambda i, j: (i, j)), name, w_layer=w_layer)


def _proj_rope(h, w, col0, ncols, scale, cos_t, sin_t, seq, name, tm, tn=PROJ_TN, w_layer=None,
               dup_heads=False):
    m = h.shape[0]
    tn = min(tn, ncols)
    per_batch = seq // tm
    out_mult = 2 if dup_heads else 1
    tab_spec = pl.BlockSpec((tm, LANES), lambda i, j: (i % per_batch, 0))
    return _proj(h, w, col0, ncols, tm, tn, functools.partial(_epi_rope, scale, dup_heads),
                 (cos_t, sin_t), (tab_spec, tab_spec),
                 jax.ShapeDtypeStruct((m, out_mult * ncols), BF16),
                 pl.BlockSpec((tm, out_mult * tn), lambda i, j: (i, j)), name, w_layer=w_layer)


def _proj_silu_slabs(h, w, col0, name, tm, tn=PROJ_TN, w_layer=None):
    m = h.shape[0]
    return _proj(h, w, col0, D_BRANCH, tm, tn, _epi_silu_slabs, (), (),
                 jax.ShapeDtypeStruct((SLABS, m, LANES), BF16),
                 pl.BlockSpec((tn // LANES, tm, LANES), lambda i, j: (j, i, 0)), name, w_layer=w_layer)


def _proj_channel_dft(h, w, cs, name, tm, seq=None, tn=PROJ_TN):
    m = h.shape[0]
    lhs_spec = None
    if seq is not None:
        na, tiles = seq // DFT_NB, DFT_NB // DFT_ROWS
        assert tm == na * DFT_ROWS
        h = h.reshape(m // seq, na, DFT_NB, D_MODEL)
        lhs_spec = pl.BlockSpec((None, na, DFT_ROWS, D_MODEL), lambda i, j: (i // tiles, 0, i % tiles, 0))
    slab_shape = jax.ShapeDtypeStruct((SLABS, m, LANES), F32)
    slab_spec = pl.BlockSpec((tn // LANES, tm, LANES), lambda i, j: (j, i, 0))
    groups = tn // FNET_GROUP_DIM
    cs_spec = pl.BlockSpec((groups,) + cs.shape[1:], lambda i, j: (j, 0, 0))
    return _proj(h, w, 0, D_BRANCH, tm, tn, functools.partial(_epi_channel_dft, seq is not None), (cs,), (cs_spec,),
                 (slab_shape, slab_shape), (slab_spec, slab_spec), name, lhs_spec=lhs_spec,
                 chunk=2 * MXU_COLS)


def _fold_mix_kernel(c_ref, s_ref, wm_ref, o_ref):
    wm = wm_ref[...]
    gd = FNET_GROUP_DIM
    hp = lax.Precision.HIGHEST
    o_ref[:, 0:gd] = jnp.dot(c_ref[...], wm, preferred_element_type=F32, precision=hp).astype(BF16)
    o_ref[:, gd:2 * gd] = jnp.dot(s_ref[...], wm, preferred_element_type=F32, precision=hp).astype(BF16)


def _fold_mix(w_mix):
    gd = FNET_GROUP_DIM
    cs = _channel_dft_matrix()
    mat = pl.BlockSpec((gd, gd), lambda g: (0, 0))
    return pl.pallas_call(
        _fold_mix_kernel,
        grid=(FNET_GROUPS,),
        in_specs=[mat, mat, pl.BlockSpec((None, gd, gd), lambda g: (g, 0, 0))],
        out_specs=pl.BlockSpec((None, gd, 2 * gd), lambda g: (g, 0, 0)),
        out_shape=jax.ShapeDtypeStruct((FNET_GROUPS, gd, 2 * gd), BF16),
        compiler_params=_cparams(1),
        name="fnet_fold_mix",
    )(jnp.asarray(cs[:, :gd]), jnp.asarray(cs[:, gd:]), w_mix)


def _dft_kernel(na, pitch, gpitch, ar_ref, ai_ref, zg_ref, fa_ref, fb_ref, y_ref, er_ref, ei_ref, g_ref):
    half_rows = na * (DFT_ROWS // 2)

    def gather_a(b):
        start = (b // (DFT_ROWS // 2)) * half_rows + b % (DFT_ROWS // 2)
        zr = ar_ref[pl.ds(start, na, stride=DFT_ROWS // 2), :]
        zi = ai_ref[pl.ds(start, na, stride=DFT_ROWS // 2), :]
        return jnp.concatenate([zr, zi], axis=0).astype(BF16)

    def stage_a(pair, carry):
        b0 = 2 * pair
        d0, d1 = gather_a(b0), gather_a(b0 + 1)
        zero = jnp.zeros_like(d0)
        rhs = jnp.concatenate([jnp.concatenate([d0, zero], axis=1),
                               jnp.concatenate([zero, d1], axis=1)], axis=0)
        e = jnp.dot(fa_ref[pair], rhs, preferred_element_type=F32)
        for j in range(2):
            off = pl.multiple_of((b0 + j) * pitch, 8)
            er_ref[pl.ds(off, na), :] = e[:na, j * LANES:(j + 1) * LANES]
            ei_ref[pl.ds(off, na), :] = e[na:, j * LANES:(j + 1) * LANES]
        return carry

    lax.fori_loop(0, DFT_NB // 2, stage_a, 0, unroll=16)

    def gather_b(ka):
        er = er_ref[pl.ds(ka, DFT_NB, stride=pitch), :]
        ei = ei_ref[pl.ds(ka, DFT_NB, stride=pitch), :]
        return jnp.concatenate([er, ei], axis=0).astype(BF16)

    def stage_b(pair, carry):
        ka0 = 2 * pair
        rhs = jnp.concatenate([gather_b(ka0), gather_b(ka0 + 1)], axis=1)
        g = jnp.dot(fb_ref[...], rhs, preferred_element_type=F32)
        for j in range(2):
            off = pl.multiple_of((ka0 + j) * gpitch, 8)
            g_ref[pl.ds(off, DFT_NB), :] = g[:, j * LANES:(j + 1) * LANES]
        return carry

    lax.fori_loop(0, na // 2, stage_b, 0, unroll=8)

    def gate(kb, carry):
        rows = pl.ds(pl.multiple_of(kb * na, na), na)
        g = g_ref[pl.ds(kb, na, stride=gpitch), :]
        y_ref[rows, :] = (g * zg_ref[rows, :].astype(F32)).astype(BF16)
        return carry

    lax.fori_loop(0, DFT_NB, gate, 0, unroll=16)


def _position_dft(ar, ai, zg, batch, seq):
    na = seq // DFT_NB
    pitch = na + 8
    gpitch = DFT_NB + 8
    fa_np, fb_np = _position_dft_matrices(seq)
    fa = jnp.asarray(fa_np).astype(BF16)
    fb = jnp.asarray(fb_np).astype(BF16)
    slab = pl.BlockSpec((None, seq, LANES), lambda s, b: (s, b, 0))
    return pl.pallas_call(
        functools.partial(_dft_kernel, na, pitch, gpitch),
        grid=(SLABS, batch),
        in_specs=[slab, slab, slab,
                  pl.BlockSpec(fa.shape, lambda s, b: (0, 0, 0)),
                  pl.BlockSpec(fb.shape, lambda s, b: (0, 0))],
        out_specs=slab,
        out_shape=jax.ShapeDtypeStruct(ar.shape, BF16),
        scratch_shapes=[pltpu.VMEM((DFT_NB * pitch, LANES), F32),
                        pltpu.VMEM((DFT_NB * pitch, LANES), F32),
                        pltpu.VMEM((na * gpitch, LANES), F32)],
        compiler_params=_cparams(2),
        name="position_dft",
    )(ar, ai, zg, fa, fb)


def _dft_dense_kernel(ar_ref, ai_ref, zg_ref, fd_ref, y_ref):
    d = jnp.concatenate([ar_ref[...], ai_ref[...]], axis=0).astype(BF16)
    g = jnp.dot(fd_ref[...], d, preferred_element_type=F32)
    y_ref[...] = (g * zg_ref[...].astype(F32)).astype(BF16)


def _position_dft_dense(ar, ai, zg, batch, seq):
    fd = jnp.asarray(_dense_dft_matrix(seq)).astype(BF16)
    slab = pl.BlockSpec((None, seq, LANES), lambda s, b: (s, b, 0))
    return pl.pallas_call(
        _dft_dense_kernel,
        grid=(SLABS, batch),
        in_specs=[slab, slab, slab, pl.BlockSpec(fd.shape, lambda s, b: (0, 0))],
        out_specs=slab,
        out_shape=jax.ShapeDtypeStruct(ar.shape, BF16),
        compiler_params=_cparams(2),
        name="position_dft_dense",
    )(ar, ai, zg, fd)


def _attn_kernel(n_band, nblk, sink_ref, q_ref, *refs):
    k_refs = refs[:n_band + 1]
    v_refs = refs[n_band + 1:2 * n_band + 2]
    zg_ref, y_ref = refs[2 * n_band + 2:]
    i = pl.program_id(2)
    blk = ATTN_BLOCK
    n_pair = Q_GROUP // 2
    qw = Q_GROUP * HEAD_DIM

    lane = lax.broadcasted_iota(jnp.int32, (1, LANES), 1)
    lo = lane < HEAD_DIM
    dn = (((1,), (1,)), ((), ()))
    if n_band:
        r = lax.broadcasted_iota(jnp.int32, (blk, blk), 0)
        c = lax.broadcasted_iota(jnp.int32, (blk, blk), 1)
        prev_ok = c >= r + jnp.where(i > 0, 0, blk)
        next_ok = c <= r - jnp.where(i < nblk - 1, 0, blk)

    def scores(hh):
        kl = slice(hh * LANES, (hh + 1) * LANES)
        keys = jnp.concatenate([ref[:, kl] for ref in k_refs], axis=0)
        q4 = jnp.concatenate([q_ref[:, hh * qw + t * LANES:hh * qw + (t + 1) * LANES]
                              for t in range(n_pair)], axis=0)
        qzero = jnp.zeros_like(q4)
        return [lax.dot_general(qm, keys, dn, preferred_element_type=F32)
                for qm in (jnp.where(lo, q4, qzero), jnp.where(lo, qzero, q4))]

    def finish(hh, s_both):
        h = pl.program_id(1) * ATTN_HEADS_PER_STEP + hh
        kl = slice(hh * LANES, (hh + 1) * LANES)
        vals = jnp.concatenate([ref[:, kl] for ref in v_refs], axis=0)
        ones = jnp.ones_like(vals)
        v_ext = (jnp.where(lo, vals, ones), jnp.where(lo, ones, vals))
        o_ext, sink_term = [], []
        for hd, s in enumerate(s_both):
            p_rows, sink_rows = [], []
            for t in range(n_pair):
                st = s[t * blk:(t + 1) * blk]
                parts = [st[:, j * blk:(j + 1) * blk] for j in range(st.shape[1] // blk)]
                if n_band:
                    parts[0] = jnp.where(prev_ok, parts[0], NEG_INF)
                    parts[2] = jnp.where(next_ok, parts[2], NEG_INF)
                sk = sink_ref[h, 2 * t + hd] * LOG2_E
                mx = parts[0]
                for part in parts[1:]:
                    mx = jnp.maximum(mx, part)
                mx = jnp.maximum(jnp.max(mx, axis=-1, keepdims=True), sk)
                p_rows.append(jnp.concatenate([jnp.exp2(part - mx).astype(BF16) for part in parts], axis=1))
                sink_rows.append(jnp.exp2(sk - mx))
            p = jnp.concatenate(p_rows, axis=0)
            o_ext.append(jnp.dot(p, v_ext[hd], preferred_element_type=F32))
            sink_term.append(jnp.concatenate(sink_rows, axis=0))
        o = jnp.where(lo, o_ext[0], o_ext[1])
        denom = jnp.where(lo, pltpu.roll(o_ext[0], HEAD_DIM, 1) + sink_term[0],
                          pltpu.roll(o_ext[1], HEAD_DIM, 1) + sink_term[1])
        res = o / denom
        for t in range(n_pair):
            tile = slice(hh * qw + t * LANES, hh * qw + (t + 1) * LANES)
            y_ref[:, tile] = (res[t * blk:(t + 1) * blk] * zg_ref[:, tile].astype(F32)).astype(BF16)

    pending = scores(0)
    for hh in range(ATTN_HEADS_PER_STEP):
        upcoming = scores(hh + 1) if hh + 1 < ATTN_HEADS_PER_STEP else None
        finish(hh, pending)
        pending = upcoming


def _attention(q, kd, vd, kcd, vcd, zg, sink, batch, seq, ctx_len, use_band):
    blk = ATTN_BLOCK
    nblk = seq // blk
    hps = ATTN_HEADS_PER_STEP
    qw = hps * Q_GROUP * HEAD_DIM
    kw = hps * LANES
    q_spec = pl.BlockSpec((blk, qw), lambda b, h, i: (b * nblk + i, h))
    ctx_spec = pl.BlockSpec((ctx_len, kw), lambda b, h, i: (b, h))
    if use_band:
        def band(delta):
            return pl.BlockSpec(
                (blk, kw), lambda b, h, i: (b * nblk + jnp.clip(i + delta, 0, nblk - 1), h))
        k_specs = [band(-1), band(0), band(1), ctx_spec]
        k_args, v_args = [kd, kd, kd, kcd], [vd, vd, vd, vcd]
        n_band = 3
    else:
        k_specs, k_args, v_args, n_band = [ctx_spec], [kcd], [vcd], 0
    return pl.pallas_call(
        functools.partial(_attn_kernel, n_band, nblk),
        grid=(batch, KV_HEADS // hps, nblk),
        in_specs=[pl.BlockSpec(memory_space=pltpu.SMEM), q_spec] + k_specs + k_specs + [q_spec],
        out_specs=q_spec,
        out_shape=jax.ShapeDtypeStruct(q.shape, BF16),
        compiler_params=_cparams(3),
        name="attention_band" if use_band else "attention_ctx",
    )(sink, q, *k_args, *v_args, zg)


def _gelu_stats_kernel(chunk, a_ref, w_ref, gv_ref, mu_ref, rstd_ref):
    a = a_ref[...]
    n = w_ref.shape[1]
    mean = m2 = None
    for idx, c0 in enumerate(range(0, n, chunk)):
        ge = _gelu_tanh(jnp.dot(a, w_ref[:, c0:c0 + chunk], preferred_element_type=F32))
        gv_ref[:, c0:c0 + chunk] = ge.astype(BF16)
        cmean = jnp.mean(ge, axis=-1, keepdims=True)
        d = ge - cmean
        cm2 = jnp.sum(d * d, axis=-1, keepdims=True)
        if idx == 0:
            mean, m2 = cmean, cm2
        else:
            delta = cmean - mean
            mean = mean + delta * (1.0 / (idx + 1))
            m2 = m2 + cm2 + delta * delta * (chunk * idx / (idx + 1))
    mu_ref[...] = mean
    rstd_ref[...] = lax.rsqrt(m2 * (1.0 / n) + EPS)


def _proj_gelu_stats(h, w, col0, tm, chunk=2 * MXU_COLS):
    m, k = h.shape
    n = D_BRANCH
    col = pl.BlockSpec((tm, 1), lambda i: (i, 0))
    stat = jax.ShapeDtypeStruct((m, 1), F32)
    return pl.pallas_call(
        functools.partial(_gelu_stats_kernel, chunk),
        grid=(m // tm,),
        in_specs=[pl.BlockSpec((tm, k), lambda i: (i, 0)),
                  pl.BlockSpec((k, n), lambda i: (0, col0 // n), pipeline_mode=pl.Buffered(1))],
        out_specs=(pl.BlockSpec((tm, n), lambda i: (i, 0)), col, col),
        out_shape=(jax.ShapeDtypeStruct((m, n), BF16), stat, stat),
        compiler_params=_cparams(1),
        name="gmlp_in_v",
    )(h, w)


def _sgu_kernel(n_chunks, n_groups, gu_ref, gv_ref, zg_ref, mu_ref, rstd_ref, lg_ref, lb_ref,
                ws_ref, bs_ref, y_ref):
    gd = D_BRANCH // GMLP_GROUPS
    mu = mu_ref[...]
    rstd = rstd_ref[...]
    for gl in range(n_groups):
        cols = slice(gl * gd, (gl + 1) * gd)
        ws = ws_ref[gl]
        bs = bs_ref[gl]
        lg = lg_ref[:, cols]
        lb = lb_ref[:, cols]
        for c in range(n_chunks):
            rows = slice(c * GMLP_CHUNK, (c + 1) * GMLP_CHUNK)
            vn = ((gv_ref[rows, cols].astype(F32) - mu[rows]) * rstd[rows] * lg + lb).astype(BF16)
            s = jnp.dot(ws, vn, preferred_element_type=F32) + bs
            y_ref[rows, cols] = (gu_ref[rows, cols].astype(F32) * s
                                 * zg_ref[rows, cols].astype(F32)).astype(BF16)


def _spatial_gate(gu, gv, zg, mu, rstd, ln_g, ln_b, w_s, b_s, tm, n_groups=2):
    m = gu.shape[0]
    gd = D_BRANCH // GMLP_GROUPS
    tn = n_groups * gd
    tile = pl.BlockSpec((tm, tn), lambda r, g: (r, g))
    col = pl.BlockSpec((tm, 1), lambda r, g: (r, 0))
    vec = pl.BlockSpec((1, tn), lambda r, g: (0, g))
    return pl.pallas_call(
        functools.partial(_sgu_kernel, tm // GMLP_CHUNK, n_groups),
        grid=(m // tm, GMLP_GROUPS // n_groups),
        in_specs=[tile, tile, tile, col, col, vec, vec,
                  pl.BlockSpec((n_groups, GMLP_CHUNK, GMLP_CHUNK), lambda r, g: (g, 0, 0)),
                  pl.BlockSpec((n_groups, GMLP_CHUNK, 1), lambda r, g: (g, 0, 0))],
        out_specs=tile,
        out_shape=jax.ShapeDtypeStruct(gu.shape, BF16),
        compiler_params=_cparams(2),
        name="gmlp_spatial_gate",
    )(gu, gv, zg, mu, rstd, ln_g.reshape(1, D_BRANCH), ln_b.reshape(1, D_BRANCH),
      w_s, b_s.reshape(GMLP_GROUPS, GMLP_CHUNK, 1))


def _wout_kernel(mode, y_ref, w_ref, x_ref, gate_ref, *refs):
    if len(y_ref.shape) == 3:
        y = jnp.concatenate([y_ref[s] for s in range(y_ref.shape[0])], axis=1)
    else:
        y = y_ref[...]
    acc = jnp.dot(y, w_ref[...], preferred_element_type=F32)
    xn = x_ref[...] + gate_ref[...] * acc
    if mode == "final":
        g_ref, o_ref = refs
        o_ref[...] = xn * lax.rsqrt(jnp.mean(xn * xn, axis=-1, keepdims=True) + EPS) * g_ref[...]
    elif mode == "next":
        g_ref, sc_ref, sh_ref, xo_ref, h_ref = refs
        xo_ref[...] = xn
        h_ref[...] = _mod_rmsnorm(xn, g_ref[...], sc_ref[...], sh_ref[...]).astype(BF16)
    else:
        refs[0][...] = xn


def _out_proj(y, w_out, x2, mods, layer, who, mode, norm_vec, tm):
    m = x2.shape[0]
    row = pl.BlockSpec((tm, D_MODEL), lambda r: (r, 0))
    if y.ndim == 3:
        y_spec = pl.BlockSpec((SLABS, tm, LANES), lambda r: (0, r, 0))
    else:
        y_spec = pl.BlockSpec((tm, D_BRANCH), lambda r: (r, 0))
    in_specs = [y_spec,
                pl.BlockSpec((None, D_BRANCH, D_MODEL), lambda r: (layer, 0, 0),
                             pipeline_mode=pl.Buffered(1)),
                row, _mod_spec(layer, 2, who)]
    args = [y, w_out, x2, mods]
    xs = jax.ShapeDtypeStruct((m, D_MODEL), F32)
    if mode == "final":
        in_specs.append(pl.BlockSpec((1, D_MODEL), lambda r: (0, 0)))
        args.append(norm_vec.reshape(1, D_MODEL))
        out_shape, out_specs = xs, row
    elif mode == "next":
        in_specs += [_row_spec(layer + 1), _mod_spec(layer + 1, 1, who), _mod_spec(layer + 1, 0, who)]
        args += [norm_vec, mods, mods]
        out_shape = (xs, jax.ShapeDtypeStruct((m, D_MODEL), BF16))
        out_specs = (row, row)
    else:
        out_shape, out_specs = xs, row
    return pl.pallas_call(
        functools.partial(_wout_kernel, mode),
        grid=(m // tm,),
        in_specs=in_specs,
        out_specs=out_specs,
        out_shape=out_shape,
        compiler_params=_cparams(1),
        name="out_proj_" + mode,
    )(*args)


def kernel(x, c, ctx, c_ctx, norm_g, ada_w, ada_b, w_out, fnet_w_in, fnet_w_mix, attn_w_in, attn_sink,
           gmlp_w_in, gmlp_w_s, gmlp_b_s, gmlp_ln_g, gmlp_ln_b, final_g):
    batch, seq, d = x.shape
    ctx_len = ctx.shape[1]
    assert d == D_MODEL and seq % (DFT_NB * 8) == 0 and seq % GRID_W == 0 and batch < 8
    m_lat, m_ctx = batch * seq, batch * ctx_len
    tm_lat = min(2048, seq)
    tm_ctx = ctx_len
    tm_out = 256

    def who_lat(tm):
        return lambda r: (r * tm) // seq

    who_ctx = lambda r: batch

    cvec = jnp.zeros((8, d), F32).at[:batch].set(c).at[batch].set(c_ctx)
    mods = _mods(cvec, ada_w, ada_b).reshape(DEPTH, 8, 3, 1, d)
    norm_g3 = norm_g.reshape(DEPTH, 1, d)
    w_out_b = w_out.astype(BF16)

    xl = x.reshape(m_lat, d)
    xc = ctx.reshape(m_ctx, d)
    hl = _prenorm(xl, norm_g3, mods, 0, who_lat(512), 512)
    hc = _prenorm(xc, norm_g3, mods, 0, who_ctx, tm_ctx)

    cos_np, sin_np = _rope_tables(seq)
    cos_t, sin_t = jnp.asarray(cos_np), jnp.asarray(sin_np)

    def finish(y, x2, layer, who_fn, need_next, is_final):
        if is_final:
            return _out_proj(y, w_out_b, x2, mods, layer, who_fn, "final", final_g, tm_out), None
        if need_next:
            return _out_proj(y, w_out_b, x2, mods, layer, who_fn, "next", norm_g3, tm_out)
        return _out_proj(y, w_out_b, x2, mods, layer, who_fn, "plain", None, tm_out), None

    out = None
    for i in range(DEPTH):
        kind, j = i % 3, i // 3
        need_ctx = i < DEPTH - 1
        last = i == DEPTH - 1
        if kind == 0:
            cs = _fold_mix(fnet_w_mix[j])
            tm_dft = (seq // DFT_NB) * DFT_ROWS
            w_u = fnet_w_in[j, :, :D_BRANCH].astype(BF16)
            ar, ai = _proj_channel_dft(hl, w_u, cs, "fnet_in_u", tm_dft, seq=seq)
            zg = _proj_silu_slabs(hl, fnet_w_in, D_BRANCH, "fnet_in_z", tm_lat, w_layer=j)
            y = _position_dft(ar, ai, zg, batch, seq)
            yc = None
            if need_ctx:
                ar, ai = _proj_channel_dft(hc, w_u, cs, "fnet_in_u_ctx", tm_ctx)
                zg = _proj_silu_slabs(hc, fnet_w_in, D_BRANCH, "fnet_in_z_ctx", tm_ctx, w_layer=j)
                yc = _position_dft_dense(ar, ai, zg, batch, ctx_len)
        elif kind == 1:
            kvw = KV_HEADS * HEAD_DIM
            k0, v0, z0 = D_BRANCH, D_BRANCH + kvw, D_BRANCH + 2 * kvw
            sink = attn_sink[j].reshape(KV_HEADS, Q_GROUP)
            scale = HEAD_DIM ** -0.5 * LOG2_E
            q = _proj_rope(hl, attn_w_in, 0, D_BRANCH, scale, cos_t, sin_t, seq, "attn_in_q", tm_lat, w_layer=j)
            kd = _proj_rope(hl, attn_w_in, k0, kvw, 1.0, cos_t, sin_t, seq, "attn_in_k", tm_lat, w_layer=j,
                            dup_heads=True)
            vd = _proj_simple(hl, attn_w_in, v0, kvw, _epi_cast_dup, "attn_in_v", tm_lat, w_layer=j, out_mult=2)
            zg = _proj_simple(hl, attn_w_in, z0, D_BRANCH, _epi_silu, "attn_in_z", tm_lat, w_layer=j)
            kcd = _proj_simple(hc, attn_w_in, k0, kvw, _epi_cast_dup, "attn_in_kc", tm_ctx, w_layer=j, out_mult=2)
            vcd = _proj_simple(hc, attn_w_in, v0, kvw, _epi_cast_dup, "attn_in_vc", tm_ctx, w_layer=j, out_mult=2)
            y = _attention(q, kd, vd, kcd, vcd, zg, sink, batch, seq, ctx_len, True)
            yc = None
            if need_ctx:
                qc = _proj_simple(hc, attn_w_in, 0, D_BRANCH, functools.partial(_epi_scale_cast, scale),
                                  "attn_in_qc", tm_ctx, w_layer=j)
                zgc = _proj_simple(hc, attn_w_in, z0, D_BRANCH, _epi_silu, "attn_in_zc", tm_ctx, w_layer=j)
                yc = _attention(qc, None, None, kcd, vcd, zgc, sink, batch, ctx_len, ctx_len, False)
        else:
            w_v = gmlp_w_in[j, :, D_BRANCH:2 * D_BRANCH].astype(BF16)
            ws = gmlp_w_s[j].astype(BF16)

            def gmlp_branch(h, tm):
                gu = _proj_simple(h, gmlp_w_in, 0, D_BRANCH, _epi_gelu, "gmlp_in_u", tm, w_layer=j)
                gv, mu, rstd = _proj_gelu_stats(h, w_v, 0, min(tm, 512))
                zg = _proj_simple(h, gmlp_w_in, 2 * D_BRANCH, D_BRANCH, _epi_silu, "gmlp_in_z", tm, w_layer=j)
                return _spatial_gate(gu, gv, zg, mu, rstd, gmlp_ln_g[j], gmlp_ln_b[j], ws, gmlp_b_s[j],
                                     min(tm, 1024))

            y = gmlp_branch(hl, tm_lat)
            yc = gmlp_branch(hc, tm_ctx) if need_ctx else None

        res, hl = finish(y, xl, i, who_lat(tm_out), not last, last)
        if last:
            out = res
        else:
            xl = res
        if need_ctx:
            xc, hc = finish(yc, xc, i, who_ctx, i + 1 < DEPTH - 1 or (i + 1) % 3 == 1, False)
    return out.reshape(batch, seq, d)
```

```python
import functools
import math

import numpy as np
import jax
import jax.numpy as jnp
from jax import lax
from jax.experimental import pallas as pl
from jax.experimental.pallas import tpu as pltpu

F32 = jnp.float32
BF16 = jnp.bfloat16

D_MODEL = 2048
D_BRANCH = 4096
DEPTH = 4
GRID_W = 64
FNET_GROUPS = 16
FNET_GROUP_DIM = 256
HEAD_DIM = 64
KV_HEADS = 8
Q_GROUP = 8
ATTN_BLOCK = 128
ATTN_HEADS_PER_STEP = 4
ROPE_BASE = 10000.0
GMLP_CHUNK = 128
GMLP_GROUPS = 16
EPS = 1e-6
NEG_INF = -1e30
LOG2_E = math.log2(math.e)

LANES = 128
MXU_COLS = 256
PROJ_TN = 1024
DFT_NB = 128
DFT_ROWS = 16
SLABS = D_BRANCH // LANES
VMEM_LIMIT = 56 * 1024 * 1024


def _cparams(n_axes, vmem=VMEM_LIMIT):
    return pltpu.CompilerParams(dimension_semantics=("arbitrary",) * n_axes,
                                vmem_limit_bytes=vmem)


def _silu(z):
    return 0.5 * z * (1.0 + jnp.tanh(0.5 * z))


def _gelu_tanh(x):
    c = math.sqrt(2.0 / math.pi)
    return 0.5 * x * (1.0 + jnp.tanh(c * (x + 0.044715 * (x * x * x))))


def _mod_rmsnorm(x, g, scale, shift):
    y = x * lax.rsqrt(jnp.mean(x * x, axis=-1, keepdims=True) + EPS) * g
    return y * (1.0 + scale) + shift


def _channel_dft_matrix():
    n = FNET_GROUP_DIM
    k = np.arange(n, dtype=np.float64)
    ang = 2.0 * np.pi * np.outer(k, k) / n
    s = 1.0 / math.sqrt(n)
    return np.concatenate([np.cos(ang) * s, -np.sin(ang) * s], axis=1).astype(np.float32)


def _position_dft_matrices(seq):
    na, nb = seq // DFT_NB, DFT_NB
    a = np.arange(na, dtype=np.float64)
    b = np.arange(nb, dtype=np.float64)
    ang = 2.0 * np.pi * (a[None, None, :] * a[None, :, None] / na + b[:, None, None] * a[None, :, None] / seq)
    mr = np.cos(ang) / math.sqrt(na)
    mi = -np.sin(ang) / math.sqrt(na)
    fa = np.concatenate([np.concatenate([mr, -mi], axis=2), np.concatenate([mi, mr], axis=2)], axis=1)
    fa = np.concatenate([fa[0::2], fa[1::2]], axis=2)
    angb = 2.0 * np.pi * np.outer(b, b) / nb
    fb = np.concatenate([np.cos(angb), np.sin(angb)], axis=1) / math.sqrt(nb)
    return fa.astype(np.float32), fb.astype(np.float32)


def _dense_dft_matrix(n):
    k = np.arange(n, dtype=np.float64)
    ang = 2.0 * np.pi * np.outer(k, k) / n
    return (np.concatenate([np.cos(ang), np.sin(ang)], axis=1) / math.sqrt(n)).astype(np.float32)


def _rope_tables(seq, scale=1.0):
    nf = HEAD_DIM // 4
    inv = ROPE_BASE ** (-np.arange(nf, dtype=np.float64) / nf)
    t = np.arange(seq)
    rows = (t // GRID_W).astype(np.float64)
    cols = (t % GRID_W).astype(np.float64)
    parts_c, parts_s = [], []
    for pos in (rows, cols):
        ang = pos[:, None] * inv[None, :]
        parts_c += [np.cos(ang), np.cos(ang)]
        parts_s += [-np.sin(ang), np.sin(ang)]
    cos = np.concatenate(parts_c, axis=1) * scale
    sin = np.concatenate(parts_s, axis=1) * scale
    reps = LANES // HEAD_DIM
    return (np.tile(cos, (1, reps)).astype(np.float32), np.tile(sin, (1, reps)).astype(np.float32))


def _mods_kernel(cv_ref, w_ref, b_ref, o_ref):
    a = _silu(cv_ref[...])
    a_hi = a.astype(BF16)
    a_lo = (a - a_hi.astype(F32)).astype(BF16)
    w = w_ref[...].astype(BF16)
    o_ref[...] = (jnp.dot(a_hi, w, preferred_element_type=F32)
                  + jnp.dot(a_lo, w, preferred_element_type=F32)) + b_ref[...]


def _mods(cvec, ada_w, ada_b):
    depth, d, n3 = ada_w.shape
    tn = 2048
    return pl.pallas_call(
        _mods_kernel,
        grid=(depth, n3 // tn),
        in_specs=[pl.BlockSpec((8, d), lambda i, j: (0, 0)),
                  pl.BlockSpec((None, d, tn), lambda i, j: (i, 0, j)),
                  pl.BlockSpec((None, 1, tn), lambda i, j: (i, 0, j))],
        out_specs=pl.BlockSpec((None, 8, tn), lambda i, j: (i, 0, j)),
        out_shape=jax.ShapeDtypeStruct((depth, 8, n3), F32),
        compiler_params=_cparams(2),
        name="ada_mods",
    )(cvec, ada_w, ada_b.reshape(depth, 1, n3))


def _mod_spec(layer, kind, who_of_row):
    return pl.BlockSpec((None, None, None, 1, D_MODEL),
                        lambda r, *_: (layer, who_of_row(r), kind, 0, 0))


def _row_spec(vec_layer):
    return pl.BlockSpec((None, 1, D_MODEL), lambda r, *_: (vec_layer, 0, 0))


def _prenorm_kernel(x_ref, g_ref, sc_ref, sh_ref, h_ref):
    h_ref[...] = _mod_rmsnorm(x_ref[...], g_ref[...], sc_ref[...], sh_ref[...]).astype(BF16)


def _prenorm(x2, norm_g3, mods, layer, who, tm):
    m = x2.shape[0]
    return pl.pallas_call(
        _prenorm_kernel,
        grid=(m // tm,),
        in_specs=[pl.BlockSpec((tm, D_MODEL), lambda r: (r, 0)),
                  _row_spec(layer), _mod_spec(layer, 1, who), _mod_spec(layer, 0, who)],
        out_specs=pl.BlockSpec((tm, D_MODEL), lambda r: (r, 0)),
        out_shape=jax.ShapeDtypeStruct((m, D_MODEL), BF16),
        compiler_params=_cparams(1),
        name="prenorm",
    )(x2, norm_g3, mods, mods)


def _proj_kernel(epilogue, n_extra, chunk, a_ref, w_ref, *rest):
    if len(a_ref.shape) == 3:
        a_flat = rest[-1]
        rest = rest[:-1]

        @pl.when(pl.program_id(1) == 0)
        def _():
            a_flat[...] = a_ref[...].reshape(a_flat.shape)

        a = a_flat[...]
    else:
        a = a_ref[...]
    for c0 in range(0, w_ref.shape[1], chunk):
        acc = jnp.dot(a, w_ref[:, c0:c0 + chunk].astype(BF16), preferred_element_type=F32)
        epilogue(acc, c0, rest[:n_extra], rest[n_extra:])


def _proj(h, w, col0, ncols, tm, tn, epilogue, extras, extra_specs, out_shapes, out_specs, name,
          lhs_spec=None, chunk=MXU_COLS, w_layer=None):
    k = h.shape[-1]
    m = h.size // k
    off = col0 // tn
    scratch = [pltpu.VMEM((tm, k), h.dtype)] if lhs_spec is not None else []
    if lhs_spec is None:
        lhs_spec = pl.BlockSpec((tm, k), lambda i, j: (i, 0))
    if w.ndim == 3:
        w_spec = pl.BlockSpec((None, k, tn), lambda i, j: (w_layer, 0, j + off))
    else:
        w_spec = pl.BlockSpec((k, tn), lambda i, j: (0, j + off))
    return pl.pallas_call(
        functools.partial(_proj_kernel, epilogue, len(extras), min(chunk, tn)),
        grid=(m // tm, ncols // tn),
        in_specs=[lhs_spec, w_spec] + list(extra_specs),
        out_specs=out_specs,
        out_shape=out_shapes,
        scratch_shapes=scratch,
        compiler_params=_cparams(2),
        name=name,
    )(h, w, *extras)


def _epi_silu(acc, c0, extras, outs):
    outs[0][:, c0:c0 + acc.shape[1]] = _silu(acc).astype(BF16)


def _epi_silu_slabs(acc, c0, extras, outs):
    for t in range(acc.shape[1] // LANES):
        outs[0][c0 // LANES + t] = _silu(acc[:, t * LANES:(t + 1) * LANES]).astype(BF16)


def _epi_gelu(acc, c0, extras, outs):
    outs[0][:, c0:c0 + acc.shape[1]] = _gelu_tanh(acc).astype(BF16)


def _epi_cast(acc, c0, extras, outs):
    outs[0][:, c0:c0 + acc.shape[1]] = acc.astype(BF16)


def _epi_scale_cast(scale, acc, c0, extras, outs):
    outs[0][:, c0:c0 + acc.shape[1]] = (acc * scale).astype(BF16)


def _epi_channel_dft(split_rows, acc, c0, extras, outs):
    cs_ref = extras[0]
    ar_ref, ai_ref = outs
    tm = acc.shape[0]
    for gl in range(acc.shape[1] // FNET_GROUP_DIM):
        g = c0 // FNET_GROUP_DIM + gl
        ub = acc[:, gl * FNET_GROUP_DIM:(gl + 1) * FNET_GROUP_DIM].astype(BF16)
        ab = jnp.dot(ub, cs_ref[g], preferred_element_type=F32)
        if split_rows:
            ab4 = ab.reshape(tm // DFT_ROWS, 2, DFT_ROWS // 2, ab.shape[1])
            ab = jnp.concatenate([ab4[:, hf].reshape(tm // 2, ab.shape[1]) for hf in range(2)], axis=0)
        ar_ref[2 * g] = ab[:, 0:128]
        ar_ref[2 * g + 1] = ab[:, 128:256]
        ai_ref[2 * g] = ab[:, 256:384]
        ai_ref[2 * g + 1] = ab[:, 384:512]


def _store_tile(out_ref, col, y, dup_heads):
    if not dup_heads:
        out_ref[:, col:col + LANES] = y.astype(BF16)
        return
    lane = lax.broadcasted_iota(jnp.int32, (1, LANES), 1)
    lo = lane < HEAD_DIM
    swapped = pltpu.roll(y, HEAD_DIM, 1)
    out_ref[:, 2 * col:2 * col + LANES] = jnp.where(lo, y, swapped).astype(BF16)
    out_ref[:, 2 * col + LANES:2 * col + 2 * LANES] = jnp.where(lo, swapped, y).astype(BF16)


def _epi_rope(dup_heads, acc, c0, extras, outs):
    cos = extras[0][...]
    sin = extras[1][...]
    lane = lax.broadcasted_iota(jnp.int32, (1, LANES), 1)
    first = (lane % 32) < 16
    for t in range(acc.shape[1] // LANES):
        x = acc[:, t * LANES:(t + 1) * LANES]
        partner = jnp.where(first, pltpu.roll(x, LANES - 16, 1), pltpu.roll(x, 16, 1))
        _store_tile(outs[0], c0 + t * LANES, x * cos + partner * sin, dup_heads)


def _epi_cast_dup(acc, c0, extras, outs):
    for t in range(acc.shape[1] // LANES):
        _store_tile(outs[0], c0 + t * LANES, acc[:, t * LANES:(t + 1) * LANES], True)


def _proj_simple(h, w, col0, ncols, epilogue, name, tm, tn=PROJ_TN, w_layer=None, out_mult=1):
    m = h.shape[0]
    tn = min(tn, ncols)
    return _proj(h, w, col0, ncols, tm, tn, epilogue, (), (),
                 jax.ShapeDtypeStruct((m, out_mult * ncols), BF16),
                 pl.BlockSpec((tm, out_mult * tn), lambda i, j: (i, j)), name, w_layer=w_layer)


def _proj_rope(h, w, col0, ncols, cos_t, sin_t, seq, name, tm, tn=PROJ_TN, w_layer=None,
               dup_heads=False):
    m = h.shape[0]
    tn = min(tn, ncols)
    per_batch = seq // tm
    out_mult = 2 if dup_heads else 1
    tab_spec = pl.BlockSpec((tm, LANES), lambda i, j: (i % per_batch, 0))
    return _proj(h, w, col0, ncols, tm, tn, functools.partial(_epi_rope, dup_heads),
                 (cos_t, sin_t), (tab_spec, tab_spec),
                 jax.ShapeDtypeStruct((m, out_mult * ncols), BF16),
                 pl.BlockSpec((tm, out_mult * tn), lambda i, j: (i, j)), name, w_layer=w_layer)


def _proj_silu_slabs(h, w, col0, name, tm, tn=PROJ_TN, w_layer=None):
    m = h.shape[0]
    return _proj(h, w, col0, D_BRANCH, tm, tn, _epi_silu_slabs, (), (),
                 jax.ShapeDtypeStruct((SLABS, m, LANES), BF16),
                 pl.BlockSpec((tn // LANES, tm, LANES), lambda i, j: (j, i, 0)), name, w_layer=w_layer)


def _proj_channel_dft(h, w, cs, name, tm, seq=None, tn=PROJ_TN):
    m = h.shape[0]
    lhs_spec = None
    if seq is not None:
        na, tiles = seq // DFT_NB, DFT_NB // DFT_ROWS
        assert tm == na * DFT_ROWS
        h = h.reshape(m // seq, na, DFT_NB, D_MODEL)
        lhs_spec = pl.BlockSpec((None, na, DFT_ROWS, D_MODEL), lambda i, j: (i // tiles, 0, i % tiles, 0))
    slab_shape = jax.ShapeDtypeStruct((SLABS, m, LANES), F32)
    slab_spec = pl.BlockSpec((tn // LANES, tm, LANES), lambda i, j: (j, i, 0))
    groups = tn // FNET_GROUP_DIM
    cs_spec = pl.BlockSpec((groups,) + cs.shape[1:], lambda i, j: (j, 0, 0))
    return _proj(h, w, 0, D_BRANCH, tm, tn, functools.partial(_epi_channel_dft, seq is not None), (cs,), (cs_spec,),
                 (slab_shape, slab_shape), (slab_spec, slab_spec), name, lhs_spec=lhs_spec,
                 chunk=2 * MXU_COLS)


def _fold_mix_kernel(c_ref, s_ref, wm_ref, o_ref):
    wm = wm_ref[...]
    gd = FNET_GROUP_DIM
    hp = lax.Precision.HIGHEST
    o_ref[:, 0:gd] = jnp.dot(c_ref[...], wm, preferred_element_type=F32, precision=hp).astype(BF16)
    o_ref[:, gd:2 * gd] = jnp.dot(s_ref[...], wm, preferred_element_type=F32, precision=hp).astype(BF16)


def _fold_mix(w_mix):
    gd = FNET_GROUP_DIM
    cs = _channel_dft_matrix()
    mat = pl.BlockSpec((gd, gd), lambda g: (0, 0))
    return pl.pallas_call(
        _fold_mix_kernel,
        grid=(FNET_GROUPS,),
        in_specs=[mat, mat, pl.BlockSpec((None, gd, gd), lambda g: (g, 0, 0))],
        out_specs=pl.BlockSpec((None, gd, 2 * gd), lambda g: (g, 0, 0)),
        out_shape=jax.ShapeDtypeStruct((FNET_GROUPS, gd, 2 * gd), BF16),
        compiler_params=_cparams(1),
        name="fnet_fold_mix",
    )(jnp.asarray(cs[:, :gd]), jnp.asarray(cs[:, gd:]), w_mix)


def _dft_kernel(na, pitch, gpitch, ar_ref, ai_ref, zg_ref, fa_ref, fb_ref, y_ref, er_ref, ei_ref, g_ref):
    half_rows = na * (DFT_ROWS // 2)

    def gather_a(b):
        start = (b // (DFT_ROWS // 2)) * half_rows + b % (DFT_ROWS // 2)
        zr = ar_ref[pl.ds(start, na, stride=DFT_ROWS // 2), :]
        zi = ai_ref[pl.ds(start, na, stride=DFT_ROWS // 2), :]
        return jnp.concatenate([zr, zi], axis=0).astype(BF16)

    def stage_a(pair, carry):
        b0 = 2 * pair
        d0, d1 = gather_a(b0), gather_a(b0 + 1)
        zero = jnp.zeros_like(d0)
        rhs = jnp.concatenate([jnp.concatenate([d0, zero], axis=1),
                               jnp.concatenate([zero, d1], axis=1)], axis=0)
        e = jnp.dot(fa_ref[pair], rhs, preferred_element_type=F32)
        for j in range(2):
            off = pl.multiple_of((b0 + j) * pitch, 8)
            er_ref[pl.ds(off, na), :] = e[:na, j * LANES:(j + 1) * LANES]
            ei_ref[pl.ds(off, na), :] = e[na:, j * LANES:(j + 1) * LANES]
        return carry

    lax.fori_loop(0, DFT_NB // 2, stage_a, 0, unroll=16)

    def gather_b(ka):
        er = er_ref[pl.ds(ka, DFT_NB, stride=pitch), :]
        ei = ei_ref[pl.ds(ka, DFT_NB, stride=pitch), :]
        return jnp.concatenate([er, ei], axis=0).astype(BF16)

    def stage_b(pair, carry):
        ka0 = 2 * pair
        rhs = jnp.concatenate([gather_b(ka0), gather_b(ka0 + 1)], axis=1)
        g = jnp.dot(fb_ref[...], rhs, preferred_element_type=F32)
        for j in range(2):
            off = pl.multiple_of((ka0 + j) * gpitch, 8)
            g_ref[pl.ds(off, DFT_NB), :] = g[:, j * LANES:(j + 1) * LANES]
        return carry

    lax.fori_loop(0, na // 2, stage_b, 0, unroll=8)

    def gate(kb, carry):
        rows = pl.ds(pl.multiple_of(kb * na, na), na)
        g = g_ref[pl.ds(kb, na, stride=gpitch), :]
        y_ref[rows, :] = (g * zg_ref[rows, :].astype(F32)).astype(BF16)
        return carry

    lax.fori_loop(0, DFT_NB, gate, 0, unroll=16)


def _position_dft(ar, ai, zg, batch, seq):
    na = seq // DFT_NB
    pitch = na + 8
    gpitch = DFT_NB + 8
    fa_np, fb_np = _position_dft_matrices(seq)
    fa = jnp.asarray(fa_np).astype(BF16)
    fb = jnp.asarray(fb_np).astype(BF16)
    slab = pl.BlockSpec((None, seq, LANES), lambda s, b: (s, b, 0))
    return pl.pallas_call(
        functools.partial(_dft_kernel, na, pitch, gpitch),
        grid=(SLABS, batch),
        in_specs=[slab, slab, slab,
                  pl.BlockSpec(fa.shape, lambda s, b: (0, 0, 0)),
                  pl.BlockSpec(fb.shape, lambda s, b: (0, 0))],
        out_specs=slab,
        out_shape=jax.ShapeDtypeStruct(ar.shape, BF16),
        scratch_shapes=[pltpu.VMEM((DFT_NB * pitch, LANES), F32),
                        pltpu.VMEM((DFT_NB * pitch, LANES), F32),
                        pltpu.VMEM((na * gpitch, LANES), F32)],
        compiler_params=_cparams(2),
        name="position_dft",
    )(ar, ai, zg, fa, fb)


def _dft_dense_kernel(ar_ref, ai_ref, zg_ref, fd_ref, y_ref):
    d = jnp.concatenate([ar_ref[...], ai_ref[...]], axis=0).astype(BF16)
    g = jnp.dot(fd_ref[...], d, preferred_element_type=F32)
    y_ref[...] = (g * zg_ref[...].astype(F32)).astype(BF16)


def _position_dft_dense(ar, ai, zg, batch, seq):
    fd = jnp.asarray(_dense_dft_matrix(seq)).astype(BF16)
    slab = pl.BlockSpec((None, seq, LANES), lambda s, b: (s, b, 0))
    return pl.pallas_call(
        _dft_dense_kernel,
        grid=(SLABS, batch),
        in_specs=[slab, slab, slab, pl.BlockSpec(fd.shape, lambda s, b: (0, 0))],
        out_specs=slab,
        out_shape=jax.ShapeDtypeStruct(ar.shape, BF16),
        compiler_params=_cparams(2),
        name="position_dft_dense",
    )(ar, ai, zg, fd)


def _attn_kernel(n_band, nblk, sink_ref, q_ref, *refs):
    k_refs = refs[:n_band + 1]
    v_refs = refs[n_band + 1:2 * n_band + 2]
    zg_ref, y_ref = refs[2 * n_band + 2:]
    i = pl.program_id(2)
    blk = ATTN_BLOCK
    n_pair = Q_GROUP // 2
    qw = Q_GROUP * HEAD_DIM

    lane = lax.broadcasted_iota(jnp.int32, (1, LANES), 1)
    lo = lane < HEAD_DIM
    dn = (((1,), (1,)), ((), ()))
    if n_band:
        r = lax.broadcasted_iota(jnp.int32, (blk, blk), 0)
        c = lax.broadcasted_iota(jnp.int32, (blk, blk), 1)
        prev_ok = c >= r + jnp.where(i > 0, 0, blk)
        next_ok = c <= r - jnp.where(i < nblk - 1, 0, blk)

    def scores(hh):
        kl = slice(hh * LANES, (hh + 1) * LANES)
        keys = jnp.concatenate([ref[:, kl] for ref in k_refs], axis=0)
        q4 = jnp.concatenate([q_ref[:, hh * qw + t * LANES:hh * qw + (t + 1) * LANES]
                              for t in range(n_pair)], axis=0)
        qzero = jnp.zeros_like(q4)
        return [lax.dot_general(qm, keys, dn, preferred_element_type=F32)
                for qm in (jnp.where(lo, q4, qzero), jnp.where(lo, qzero, q4))]

    def finish(hh, s_both):
        h = pl.program_id(1) * ATTN_HEADS_PER_STEP + hh
        kl = slice(hh * LANES, (hh + 1) * LANES)
        vals = jnp.concatenate([ref[:, kl] for ref in v_refs], axis=0)
        ones = jnp.ones_like(vals)
        v_ext = (jnp.where(lo, vals, ones), jnp.where(lo, ones, vals))
        o_ext, sink_term = [], []
        for hd, s in enumerate(s_both):
            p_rows, sink_rows = [], []
            for t in range(n_pair):
                st = s[t * blk:(t + 1) * blk]
                parts = [st[:, j * blk:(j + 1) * blk] for j in range(st.shape[1] // blk)]
                if n_band:
                    parts[0] = jnp.where(prev_ok, parts[0], NEG_INF)
                    parts[2] = jnp.where(next_ok, parts[2], NEG_INF)
                sk = sink_ref[h, 2 * t + hd] * LOG2_E
                mx = parts[0]
                for part in parts[1:]:
                    mx = jnp.maximum(mx, part)
                mx = jnp.maximum(jnp.max(mx, axis=-1, keepdims=True), sk)
                p_rows.append(jnp.concatenate([jnp.exp2(part - mx).astype(BF16) for part in parts], axis=1))
                sink_rows.append(jnp.exp2(sk - mx))
            p = jnp.concatenate(p_rows, axis=0)
            o_ext.append(jnp.dot(p, v_ext[hd], preferred_element_type=F32))
            sink_term.append(jnp.concatenate(sink_rows, axis=0))
        o = jnp.where(lo, o_ext[0], o_ext[1])
        denom = jnp.where(lo, pltpu.roll(o_ext[0], HEAD_DIM, 1) + sink_term[0],
                          pltpu.roll(o_ext[1], HEAD_DIM, 1) + sink_term[1])
        res = o / denom
        for t in range(n_pair):
            tile = slice(hh * qw + t * LANES, hh * qw + (t + 1) * LANES)
            y_ref[:, tile] = (res[t * blk:(t + 1) * blk] * zg_ref[:, tile].astype(F32)).astype(BF16)

    pending = scores(0)
    for hh in range(ATTN_HEADS_PER_STEP):
        upcoming = scores(hh + 1) if hh + 1 < ATTN_HEADS_PER_STEP else None
        finish(hh, pending)
        pending = upcoming


def _attention(q, kd, vd, kcd, vcd, zg, sink, batch, seq, ctx_len, use_band):
    blk = ATTN_BLOCK
    nblk = seq // blk
    hps = ATTN_HEADS_PER_STEP
    qw = hps * Q_GROUP * HEAD_DIM
    kw = hps * LANES
    q_spec = pl.BlockSpec((blk, qw), lambda b, h, i: (b * nblk + i, h))
    ctx_spec = pl.BlockSpec((ctx_len, kw), lambda b, h, i: (b, h))
    if use_band:
        def band(delta):
            return pl.BlockSpec(
                (blk, kw), lambda b, h, i: (b * nblk + jnp.clip(i + delta, 0, nblk - 1), h))
        k_specs = [band(-1), band(0), band(1), ctx_spec]
        k_args, v_args = [kd, kd, kd, kcd], [vd, vd, vd, vcd]
        n_band = 3
    else:
        k_specs, k_args, v_args, n_band = [ctx_spec], [kcd], [vcd], 0
    return pl.pallas_call(
        functools.partial(_attn_kernel, n_band, nblk),
        grid=(batch, KV_HEADS // hps, nblk),
        in_specs=[pl.BlockSpec(memory_space=pltpu.SMEM), q_spec] + k_specs + k_specs + [q_spec],
        out_specs=q_spec,
        out_shape=jax.ShapeDtypeStruct(q.shape, BF16),
        compiler_params=_cparams(3),
        name="attention_band" if use_band else "attention_ctx",
    )(sink, q, *k_args, *v_args, zg)


def _gelu_stats_kernel(chunk, a_ref, w_ref, gv_ref, mu_ref, rstd_ref):
    a = a_ref[...]
    n = w_ref.shape[1]
    mean = m2 = None
    for idx, c0 in enumerate(range(0, n, chunk)):
        ge = _gelu_tanh(jnp.dot(a, w_ref[:, c0:c0 + chunk], preferred_element_type=F32))
        gv_ref[:, c0:c0 + chunk] = ge.astype(BF16)
        cmean = jnp.mean(ge, axis=-1, keepdims=True)
        d = ge - cmean
        cm2 = jnp.sum(d * d, axis=-1, keepdims=True)
        if idx == 0:
            mean, m2 = cmean, cm2
        else:
            delta = cmean - mean
            mean = mean + delta * (1.0 / (idx + 1))
            m2 = m2 + cm2 + delta * delta * (chunk * idx / (idx + 1))
    mu_ref[...] = mean
    rstd_ref[...] = lax.rsqrt(m2 * (1.0 / n) + EPS)


def _proj_gelu_stats(h, w, col0, tm, chunk=2 * MXU_COLS):
    m, k = h.shape
    n = D_BRANCH
    col = pl.BlockSpec((tm, 1), lambda i: (i, 0))
    stat = jax.ShapeDtypeStruct((m, 1), F32)
    return pl.pallas_call(
        functools.partial(_gelu_stats_kernel, chunk),
        grid=(m // tm,),
        in_specs=[pl.BlockSpec((tm, k), lambda i: (i, 0)),
                  pl.BlockSpec((k, n), lambda i: (0, col0 // n), pipeline_mode=pl.Buffered(1))],
        out_specs=(pl.BlockSpec((tm, n), lambda i: (i, 0)), col, col),
        out_shape=(jax.ShapeDtypeStruct((m, n), BF16), stat, stat),
        compiler_params=_cparams(1),
        name="gmlp_in_v",
    )(h, w)


def _sgu_kernel(n_chunks, n_groups, gu_ref, gv_ref, zg_ref, mu_ref, rstd_ref, lg_ref, lb_ref,
                ws_ref, bs_ref, y_ref):
    gd = D_BRANCH // GMLP_GROUPS
    mu = mu_ref[...]
    rstd = rstd_ref[...]
    for gl in range(n_groups):
        cols = slice(gl * gd, (gl + 1) * gd)
        ws = ws_ref[gl]
        bs = bs_ref[gl]
        lg = lg_ref[:, cols]
        lb = lb_ref[:, cols]
        for c in range(n_chunks):
            rows = slice(c * GMLP_CHUNK, (c + 1) * GMLP_CHUNK)
            vn = ((gv_ref[rows, cols].astype(F32) - mu[rows]) * rstd[rows] * lg + lb).astype(BF16)
            s = jnp.dot(ws, vn, preferred_element_type=F32) + bs
            y_ref[rows, cols] = (gu_ref[rows, cols].astype(F32) * s
                                 * zg_ref[rows, cols].astype(F32)).astype(BF16)


def _spatial_gate(gu, gv, zg, mu, rstd, ln_g, ln_b, w_s, b_s, tm, n_groups=2):
    m = gu.shape[0]
    gd = D_BRANCH // GMLP_GROUPS
    tn = n_groups * gd
    tile = pl.BlockSpec((tm, tn), lambda r, g: (r, g))
    col = pl.BlockSpec((tm, 1), lambda r, g: (r, 0))
    vec = pl.BlockSpec((1, tn), lambda r, g: (0, g))
    return pl.pallas_call(
        functools.partial(_sgu_kernel, tm // GMLP_CHUNK, n_groups),
        grid=(m // tm, GMLP_GROUPS // n_groups),
        in_specs=[tile, tile, tile, col, col, vec, vec,
                  pl.BlockSpec((n_groups, GMLP_CHUNK, GMLP_CHUNK), lambda r, g: (g, 0, 0)),
                  pl.BlockSpec((n_groups, GMLP_CHUNK, 1), lambda r, g: (g, 0, 0))],
        out_specs=tile,
        out_shape=jax.ShapeDtypeStruct(gu.shape, BF16),
        compiler_params=_cparams(2),
        name="gmlp_spatial_gate",
    )(gu, gv, zg, mu, rstd, ln_g.reshape(1, D_BRANCH), ln_b.reshape(1, D_BRANCH),
      w_s, b_s.reshape(GMLP_GROUPS, GMLP_CHUNK, 1))


def _wout_kernel(mode, y_ref, w_ref, x_ref, gate_ref, *refs):
    if len(y_ref.shape) == 3:
        y = jnp.concatenate([y_ref[s] for s in range(y_ref.shape[0])], axis=1)
    else:
        y = y_ref[...]
    acc = jnp.dot(y, w_ref[...], preferred_element_type=F32)
    xn = x_ref[...] + gate_ref[...] * acc
    if mode == "final":
        g_ref, o_ref = refs
        o_ref[...] = xn * lax.rsqrt(jnp.mean(xn * xn, axis=-1, keepdims=True) + EPS) * g_ref[...]
    elif mode == "next":
        g_ref, sc_ref, sh_ref, xo_ref, h_ref = refs
        xo_ref[...] = xn
        h_ref[...] = _mod_rmsnorm(xn, g_ref[...], sc_ref[...], sh_ref[...]).astype(BF16)
    else:
        refs[0][...] = xn


def _out_proj(y, w_out, x2, mods, layer, who, mode, norm_vec, tm):
    m = x2.shape[0]
    row = pl.BlockSpec((tm, D_MODEL), lambda r: (r, 0))
    if y.ndim == 3:
        y_spec = pl.BlockSpec((SLABS, tm, LANES), lambda r: (0, r, 0))
    else:
        y_spec = pl.BlockSpec((tm, D_BRANCH), lambda r: (r, 0))
    in_specs = [y_spec,
                pl.BlockSpec((None, D_BRANCH, D_MODEL), lambda r: (layer, 0, 0),
                             pipeline_mode=pl.Buffered(1)),
                row, _mod_spec(layer, 2, who)]
    args = [y, w_out, x2, mods]
    xs = jax.ShapeDtypeStruct((m, D_MODEL), F32)
    if mode == "final":
        in_specs.append(pl.BlockSpec((1, D_MODEL), lambda r: (0, 0)))
        args.append(norm_vec.reshape(1, D_MODEL))
        out_shape, out_specs = xs, row
    elif mode == "next":
        in_specs += [_row_spec(layer + 1), _mod_spec(layer + 1, 1, who), _mod_spec(layer + 1, 0, who)]
        args += [norm_vec, mods, mods]
        out_shape = (xs, jax.ShapeDtypeStruct((m, D_MODEL), BF16))
        out_specs = (row, row)
    else:
        out_shape, out_specs = xs, row
    return pl.pallas_call(
        functools.partial(_wout_kernel, mode),
        grid=(m // tm,),
        in_specs=in_specs,
        out_specs=out_specs,
        out_shape=out_shape,
        compiler_params=_cparams(1),
        name="out_proj_" + mode,
    )(*args)


def kernel(x, c, ctx, c_ctx, norm_g, ada_w, ada_b, w_out, fnet_w_in, fnet_w_mix, attn_w_in, attn_sink,
           gmlp_w_in, gmlp_w_s, gmlp_b_s, gmlp_ln_g, gmlp_ln_b, final_g):
    batch, seq, d = x.shape
    ctx_len = ctx.shape[1]
    assert d == D_MODEL and seq % (DFT_NB * 8) == 0 and seq % GRID_W == 0 and batch < 8
    m_lat, m_ctx = batch * seq, batch * ctx_len
    tm_lat = min(2048, seq)
    tm_ctx = ctx_len
    tm_out = 256

    def who_lat(tm):
        return lambda r: (r * tm) // seq

    who_ctx = lambda r: batch

    cvec = jnp.zeros((8, d), F32).at[:batch].set(c).at[batch].set(c_ctx)
    mods = _mods(cvec, ada_w, ada_b).reshape(DEPTH, 8, 3, 1, d)
    norm_g3 = norm_g.reshape(DEPTH, 1, d)
    w_out_b = w_out.astype(BF16)

    xl = x.reshape(m_lat, d)
    xc = ctx.reshape(m_ctx, d)
    hl = _prenorm(xl, norm_g3, mods, 0, who_lat(512), 512)
    hc = _prenorm(xc, norm_g3, mods, 0, who_ctx, tm_ctx)


    def finish(y, x2, layer, who_fn, need_next, is_final):
        if is_final:
            return _out_proj(y, w_out_b, x2, mods, layer, who_fn, "final", final_g, tm_out), None
        if need_next:
            return _out_proj(y, w_out_b, x2, mods, layer, who_fn, "next", norm_g3, tm_out)
        return _out_proj(y, w_out_b, x2, mods, layer, who_fn, "plain", None, tm_out), None

    out = None
    for i in range(DEPTH):
        kind, j = i % 3, i // 3
        need_ctx = i < DEPTH - 1
        last = i == DEPTH - 1
        if kind == 0:
            cs = _fold_mix(fnet_w_mix[j])
            tm_dft = (seq // DFT_NB) * DFT_ROWS
            w_u = fnet_w_in[j, :, :D_BRANCH].astype(BF16)
            ar, ai = _proj_channel_dft(hl, w_u, cs, "fnet_in_u", tm_dft, seq=seq)
            zg = _proj_silu_slabs(hl, fnet_w_in, D_BRANCH, "fnet_in_z", tm_lat, w_layer=j)
            y = _position_dft(ar, ai, zg, batch, seq)
            yc = None
            if need_ctx:
                ar, ai = _proj_channel_dft(hc, w_u, cs, "fnet_in_u_ctx", tm_ctx)
                zg = _proj_silu_slabs(hc, fnet_w_in, D_BRANCH, "fnet_in_z_ctx", tm_ctx, w_layer=j)
                yc = _position_dft_dense(ar, ai, zg, batch, ctx_len)
        elif kind == 1:
            kvw = KV_HEADS * HEAD_DIM
            k0, v0, z0 = D_BRANCH, D_BRANCH + kvw, D_BRANCH + 2 * kvw
            sink = attn_sink[j].reshape(KV_HEADS, Q_GROUP)
            scale = HEAD_DIM ** -0.5 * LOG2_E
            q_tabs = [jnp.asarray(t) for t in _rope_tables(seq, scale)]
            k_tabs = [jnp.asarray(t) for t in _rope_tables(seq)]
            q = _proj_rope(hl, attn_w_in, 0, D_BRANCH, *q_tabs, seq, "attn_in_q", tm_lat, w_layer=j)
            kd = _proj_rope(hl, attn_w_in, k0, kvw, *k_tabs, seq, "attn_in_k", tm_lat, w_layer=j,
                            dup_heads=True)
            vd = _proj_simple(hl, attn_w_in, v0, kvw, _epi_cast_dup, "attn_in_v", tm_lat, w_layer=j, out_mult=2)
            zg = _proj_simple(hl, attn_w_in, z0, D_BRANCH, _epi_silu, "attn_in_z", tm_lat, w_layer=j)
            kcd = _proj_simple(hc, attn_w_in, k0, kvw, _epi_cast_dup, "attn_in_kc", tm_ctx, w_layer=j, out_mult=2)
            vcd = _proj_simple(hc, attn_w_in, v0, kvw, _epi_cast_dup, "attn_in_vc", tm_ctx, w_layer=j, out_mult=2)
            y = _attention(q, kd, vd, kcd, vcd, zg, sink, batch, seq, ctx_len, True)
            yc = None
            if need_ctx:
                qc = _proj_simple(hc, attn_w_in, 0, D_BRANCH, functools.partial(_epi_scale_cast, scale),
                                  "attn_in_qc", tm_ctx, w_layer=j)
                zgc = _proj_simple(hc, attn_w_in, z0, D_BRANCH, _epi_silu, "attn_in_zc", tm_ctx, w_layer=j)
                yc = _attention(qc, None, None, kcd, vcd, zgc, sink, batch, ctx_len, ctx_len, False)
        else:
            w_v = gmlp_w_in[j, :, D_BRANCH:2 * D_BRANCH].astype(BF16)
            ws = gmlp_w_s[j].astype(BF16)

            def gmlp_branch(h, tm):
                gu = _proj_simple(h, gmlp_w_in, 0, D_BRANCH, _epi_gelu, "gmlp_in_u", tm, w_layer=j)
                gv, mu, rstd = _proj_gelu_stats(h, w_v, 0, min(tm, 512))
                zg = _proj_simple(h, gmlp_w_in, 2 * D_BRANCH, D_BRANCH, _epi_silu, "gmlp_in_z", tm, w_layer=j)
                return _spatial_gate(gu, gv, zg, mu, rstd, gmlp_ln_g[j], gmlp_ln_b[j], ws, gmlp_b_s[j],
                                     min(tm, 1024))

            y = gmlp_branch(hl, tm_lat)
            yc = gmlp_branch(hc, tm_ctx) if need_ctx else None

        res, hl = finish(y, xl, i, who_lat(tm_out), not last, last)
        if last:
            out = res
        else:
            xl = res
        if need_ctx:
            xc, hc = finish(yc, xc, i, who_ctx, i + 1 < DEPTH - 1 or (i + 1) % 3 == 1, False)
    return out.reshape(batch, seq, d)
```

```python
import functools
import math

import numpy as np
import jax
import jax.numpy as jnp
from jax import lax
from jax.experimental import pallas as pl
from jax.experimental.pallas import tpu as pltpu

F32 = jnp.float32
BF16 = jnp.bfloat16

D_MODEL = 2048
D_BRANCH = 4096
DEPTH = 4
GRID_W = 64
FNET_GROUPS = 16
FNET_GROUP_DIM = 256
HEAD_DIM = 64
KV_HEADS = 8
Q_GROUP = 8
ATTN_BLOCK = 128
ATTN_HEADS_PER_STEP = 8
ROPE_BASE = 10000.0
GMLP_CHUNK = 128
GMLP_GROUPS = 16
EPS = 1e-6
NEG_INF = -1e30
LOG2_E = math.log2(math.e)

LANES = 128
MXU_COLS = 256
PROJ_TN = 1024
DFT_NB = 128
DFT_ROWS = 16
SLABS = D_BRANCH // LANES
VMEM_LIMIT = 56 * 1024 * 1024


def _cparams(n_axes, vmem=VMEM_LIMIT):
    return pltpu.CompilerParams(dimension_semantics=("arbitrary",) * n_axes,
                                vmem_limit_bytes=vmem)


def _silu(z):
    return 0.5 * z * (1.0 + jnp.tanh(0.5 * z))


def _gelu_tanh(x):
    c = math.sqrt(2.0 / math.pi)
    return 0.5 * x * (1.0 + jnp.tanh(c * (x + 0.044715 * (x * x * x))))


def _mod_rmsnorm(x, g, scale, shift):
    y = x * lax.rsqrt(jnp.mean(x * x, axis=-1, keepdims=True) + EPS) * g
    return y * (1.0 + scale) + shift


def _channel_dft_matrix():
    n = FNET_GROUP_DIM
    k = np.arange(n, dtype=np.float64)
    ang = 2.0 * np.pi * np.outer(k, k) / n
    s = 1.0 / math.sqrt(n)
    return np.concatenate([np.cos(ang) * s, -np.sin(ang) * s], axis=1).astype(np.float32)


def _position_dft_matrices(seq):
    na, nb = seq // DFT_NB, DFT_NB
    a = np.arange(na, dtype=np.float64)
    b = np.arange(nb, dtype=np.float64)
    ang = 2.0 * np.pi * (a[None, None, :] * a[None, :, None] / na + b[:, None, None] * a[None, :, None] / seq)
    mr = np.cos(ang) / math.sqrt(na)
    mi = -np.sin(ang) / math.sqrt(na)
    fa = np.concatenate([np.concatenate([mr, -mi], axis=2), np.concatenate([mi, mr], axis=2)], axis=1)
    fa = np.concatenate([fa[0::2], fa[1::2]], axis=2)
    angb = 2.0 * np.pi * np.outer(b, b) / nb
    fb = np.concatenate([np.cos(angb), np.sin(angb)], axis=1) / math.sqrt(nb)
    return fa.astype(np.float32), fb.astype(np.float32)


def _dense_dft_matrix(n):
    k = np.arange(n, dtype=np.float64)
    ang = 2.0 * np.pi * np.outer(k, k) / n
    return (np.concatenate([np.cos(ang), np.sin(ang)], axis=1) / math.sqrt(n)).astype(np.float32)


def _rope_tables(seq, scale=1.0):
    nf = HEAD_DIM // 4
    inv = ROPE_BASE ** (-np.arange(nf, dtype=np.float64) / nf)
    t = np.arange(seq)
    rows = (t // GRID_W).astype(np.float64)
    cols = (t % GRID_W).astype(np.float64)
    parts_c, parts_s = [], []
    for pos in (rows, cols):
        ang = pos[:, None] * inv[None, :]
        parts_c += [np.cos(ang), np.cos(ang)]
        parts_s += [-np.sin(ang), np.sin(ang)]
    cos = np.concatenate(parts_c, axis=1) * scale
    sin = np.concatenate(parts_s, axis=1) * scale
    reps = LANES // HEAD_DIM
    return (np.tile(cos, (1, reps)).astype(np.float32), np.tile(sin, (1, reps)).astype(np.float32))


def _mods_kernel(cv_ref, w_ref, b_ref, o_ref):
    a = _silu(cv_ref[...])
    a_hi = a.astype(BF16)
    a_lo = (a - a_hi.astype(F32)).astype(BF16)
    w = w_ref[...].astype(BF16)
    o_ref[...] = (jnp.dot(a_hi, w, preferred_element_type=F32)
                  + jnp.dot(a_lo, w, preferred_element_type=F32)) + b_ref[...]


def _mods(cvec, ada_w, ada_b):
    depth, d, n3 = ada_w.shape
    tn = 2048
    return pl.pallas_call(
        _mods_kernel,
        grid=(depth, n3 // tn),
        in_specs=[pl.BlockSpec((8, d), lambda i, j: (0, 0)),
                  pl.BlockSpec((None, d, tn), lambda i, j: (i, 0, j)),
                  pl.BlockSpec((None, 1, tn), lambda i, j: (i, 0, j))],
        out_specs=pl.BlockSpec((None, 8, tn), lambda i, j: (i, 0, j)),
        out_shape=jax.ShapeDtypeStruct((depth, 8, n3), F32),
        compiler_params=_cparams(2),
        name="ada_mods",
    )(cvec, ada_w, ada_b.reshape(depth, 1, n3))


def _mod_spec(layer, kind, who_of_row):
    return pl.BlockSpec((None, None, None, 1, D_MODEL),
                        lambda r, *_: (layer, who_of_row(r), kind, 0, 0))


def _row_spec(vec_layer):
    return pl.BlockSpec((None, 1, D_MODEL), lambda r, *_: (vec_layer, 0, 0))


def _prenorm_kernel(x_ref, g_ref, sc_ref, sh_ref, h_ref):
    h_ref[...] = _mod_rmsnorm(x_ref[...], g_ref[...], sc_ref[...], sh_ref[...]).astype(BF16)


def _prenorm(x2, norm_g3, mods, layer, who, tm):
    m = x2.shape[0]
    return pl.pallas_call(
        _prenorm_kernel,
        grid=(m // tm,),
        in_specs=[pl.BlockSpec((tm, D_MODEL), lambda r: (r, 0)),
                  _row_spec(layer), _mod_spec(layer, 1, who), _mod_spec(layer, 0, who)],
        out_specs=pl.BlockSpec((tm, D_MODEL), lambda r: (r, 0)),
        out_shape=jax.ShapeDtypeStruct((m, D_MODEL), BF16),
        compiler_params=_cparams(1),
        name="prenorm",
    )(x2, norm_g3, mods, mods)


def _proj_kernel(epilogue, n_extra, chunk, n_w, a_ref, *rest):
    w_refs, rest = rest[:n_w], rest[n_w:]
    if len(a_ref.shape) == 3:
        a_flat = rest[-1]
        rest = rest[:-1]

        @pl.when(pl.program_id(1) == 0)
        def _():
            a_flat[...] = a_ref[...].reshape(a_flat.shape)

        a = a_flat[...]
    else:
        a = a_ref[...]
    for c0 in range(0, w_refs[0].shape[1], chunk):
        accs = [jnp.dot(a, w_ref[:, c0:c0 + chunk].astype(BF16), preferred_element_type=F32)
                for w_ref in w_refs]
        epilogue(accs[0] if n_w == 1 else accs, c0, rest[:n_extra], rest[n_extra:])


def _proj(h, w, col0, ncols, tm, tn, epilogue, extras, extra_specs, out_shapes, out_specs, name,
          lhs_spec=None, chunk=MXU_COLS, w_layer=None):
    k = h.shape[-1]
    m = h.size // k
    col0s = col0 if isinstance(col0, tuple) else (col0,)
    scratch = [pltpu.VMEM((tm, k), h.dtype)] if lhs_spec is not None else []
    if lhs_spec is None:
        lhs_spec = pl.BlockSpec((tm, k), lambda i, j: (i, 0))

    def w_spec(off):
        if w.ndim == 3:
            return pl.BlockSpec((None, k, tn), lambda i, j: (w_layer, 0, j + off))
        return pl.BlockSpec((k, tn), lambda i, j: (0, j + off))

    return pl.pallas_call(
        functools.partial(_proj_kernel, epilogue, len(extras), min(chunk, tn), len(col0s)),
        grid=(m // tm, ncols // tn),
        in_specs=[lhs_spec] + [w_spec(c // tn) for c in col0s] + list(extra_specs),
        out_specs=out_specs,
        out_shape=out_shapes,
        scratch_shapes=scratch,
        compiler_params=_cparams(2),
        name=name,
    )(h, *([w] * len(col0s)), *extras)


def _epi_silu(acc, c0, extras, outs):
    outs[0][:, c0:c0 + acc.shape[1]] = _silu(acc).astype(BF16)


def _epi_silu_slabs(acc, c0, extras, outs):
    for t in range(acc.shape[1] // LANES):
        outs[0][c0 // LANES + t] = _silu(acc[:, t * LANES:(t + 1) * LANES]).astype(BF16)


def _epi_gelu_times_silu(accs, c0, extras, outs):
    acc_u, acc_z = accs
    outs[0][:, c0:c0 + acc_u.shape[1]] = (_gelu_tanh(acc_u) * _silu(acc_z)).astype(BF16)


def _epi_cast(acc, c0, extras, outs):
    outs[0][:, c0:c0 + acc.shape[1]] = acc.astype(BF16)


def _epi_scale_cast(scale, acc, c0, extras, outs):
    outs[0][:, c0:c0 + acc.shape[1]] = (acc * scale).astype(BF16)


def _epi_channel_dft(split_rows, acc, c0, extras, outs):
    cs_ref = extras[0]
    ar_ref, ai_ref = outs
    tm = acc.shape[0]
    for gl in range(acc.shape[1] // FNET_GROUP_DIM):
        g = c0 // FNET_GROUP_DIM + gl
        ub = acc[:, gl * FNET_GROUP_DIM:(gl + 1) * FNET_GROUP_DIM].astype(BF16)
        ab = jnp.dot(ub, cs_ref[g], preferred_element_type=F32)
        if split_rows:
            ab4 = ab.reshape(tm // DFT_ROWS, 2, DFT_ROWS // 2, ab.shape[1])
            ab = jnp.concatenate([ab4[:, hf].reshape(tm // 2, ab.shape[1]) for hf in range(2)], axis=0)
        ar_ref[2 * g] = ab[:, 0:128]
        ar_ref[2 * g + 1] = ab[:, 128:256]
        ai_ref[2 * g] = ab[:, 256:384]
        ai_ref[2 * g + 1] = ab[:, 384:512]


def _store_tile(out_ref, col, y, dup_heads):
    if not dup_heads:
        out_ref[:, col:col + LANES] = y.astype(BF16)
        return
    lane = lax.broadcasted_iota(jnp.int32, (1, LANES), 1)
    lo = lane < HEAD_DIM
    swapped = pltpu.roll(y, HEAD_DIM, 1)
    out_ref[:, 2 * col:2 * col + LANES] = jnp.where(lo, y, swapped).astype(BF16)
    out_ref[:, 2 * col + LANES:2 * col + 2 * LANES] = jnp.where(lo, swapped, y).astype(BF16)


def _epi_rope(dup_heads, acc, c0, extras, outs):
    cos = extras[0][...]
    sin = extras[1][...]
    lane = lax.broadcasted_iota(jnp.int32, (1, LANES), 1)
    first = (lane % 32) < 16
    for t in range(acc.shape[1] // LANES):
        x = acc[:, t * LANES:(t + 1) * LANES]
        partner = jnp.where(first, pltpu.roll(x, LANES - 16, 1), pltpu.roll(x, 16, 1))
        _store_tile(outs[0], c0 + t * LANES, x * cos + partner * sin, dup_heads)


def _epi_cast_dup(acc, c0, extras, outs):
    for t in range(acc.shape[1] // LANES):
        _store_tile(outs[0], c0 + t * LANES, acc[:, t * LANES:(t + 1) * LANES], True)


def _proj_simple(h, w, col0, ncols, epilogue, name, tm, tn=PROJ_TN, w_layer=None, out_mult=1):
    m = h.shape[0]
    tn = min(tn, ncols)
    return _proj(h, w, col0, ncols, tm, tn, epilogue, (), (),
                 jax.ShapeDtypeStruct((m, out_mult * ncols), BF16),
                 pl.BlockSpec((tm, out_mult * tn), lambda i, j: (i, j)), name, w_layer=w_layer)


def _proj_rope(h, w, col0, ncols, cos_t, sin_t, seq, name, tm, tn=PROJ_TN, w_layer=None,
               dup_heads=False):
    m = h.shape[0]
    tn = min(tn, ncols)
    per_batch = seq // tm
    out_mult = 2 if dup_heads else 1
    tab_spec = pl.BlockSpec((tm, LANES), lambda i, j: (i % per_batch, 0))
    return _proj(h, w, col0, ncols, tm, tn, functools.partial(_epi_rope, dup_heads),
                 (cos_t, sin_t), (tab_spec, tab_spec),
                 jax.ShapeDtypeStruct((m, out_mult * ncols), BF16),
                 pl.BlockSpec((tm, out_mult * tn), lambda i, j: (i, j)), name, w_layer=w_layer)


def _proj_silu_slabs(h, w, col0, name, tm, tn=PROJ_TN, w_layer=None):
    m = h.shape[0]
    return _proj(h, w, col0, D_BRANCH, tm, tn, _epi_silu_slabs, (), (),
                 jax.ShapeDtypeStruct((SLABS, m, LANES), BF16),
                 pl.BlockSpec((tn // LANES, tm, LANES), lambda i, j: (j, i, 0)), name, w_layer=w_layer)


def _proj_channel_dft(h, w, cs, name, tm, seq=None, tn=PROJ_TN):
    m = h.shape[0]
    lhs_spec = None
    if seq is not None:
        na, tiles = seq // DFT_NB, DFT_NB // DFT_ROWS
        assert tm == na * DFT_ROWS
        h = h.reshape(m // seq, na, DFT_NB, D_MODEL)
        lhs_spec = pl.BlockSpec((None, na, DFT_ROWS, D_MODEL), lambda i, j: (i // tiles, 0, i % tiles, 0))
    slab_shape = jax.ShapeDtypeStruct((SLABS, m, LANES), F32)
    slab_spec = pl.BlockSpec((tn // LANES, tm, LANES), lambda i, j: (j, i, 0))
    groups = tn // FNET_GROUP_DIM
    cs_spec = pl.BlockSpec((groups,) + cs.shape[1:], lambda i, j: (j, 0, 0))
    return _proj(h, w, 0, D_BRANCH, tm, tn, functools.partial(_epi_channel_dft, seq is not None), (cs,), (cs_spec,),
                 (slab_shape, slab_shape), (slab_spec, slab_spec), name, lhs_spec=lhs_spec,
                 chunk=2 * MXU_COLS)


def _fold_mix_kernel(c_ref, s_ref, wm_ref, o_ref):
    wm = wm_ref[...]
    gd = FNET_GROUP_DIM
    hp = lax.Precision.HIGHEST
    o_ref[:, 0:gd] = jnp.dot(c_ref[...], wm, preferred_element_type=F32, precision=hp).astype(BF16)
    o_ref[:, gd:2 * gd] = jnp.dot(s_ref[...], wm, preferred_element_type=F32, precision=hp).astype(BF16)


def _fold_mix(w_mix):
    gd = FNET_GROUP_DIM
    cs = _channel_dft_matrix()
    mat = pl.BlockSpec((gd, gd), lambda g: (0, 0))
    return pl.pallas_call(
        _fold_mix_kernel,
        grid=(FNET_GROUPS,),
        in_specs=[mat, mat, pl.BlockSpec((None, gd, gd), lambda g: (g, 0, 0))],
        out_specs=pl.BlockSpec((None, gd, 2 * gd), lambda g: (g, 0, 0)),
        out_shape=jax.ShapeDtypeStruct((FNET_GROUPS, gd, 2 * gd), BF16),
        compiler_params=_cparams(1),
        name="fnet_fold_mix",
    )(jnp.asarray(cs[:, :gd]), jnp.asarray(cs[:, gd:]), w_mix)


def _dft_kernel(na, pitch, gpitch, ar_ref, ai_ref, zg_ref, fa_ref, fb_ref, y_ref, er_ref, ei_ref, g_ref):
    half_rows = na * (DFT_ROWS // 2)

    def gather_a(b):
        start = (b // (DFT_ROWS // 2)) * half_rows + b % (DFT_ROWS // 2)
        zr = ar_ref[pl.ds(start, na, stride=DFT_ROWS // 2), :]
        zi = ai_ref[pl.ds(start, na, stride=DFT_ROWS // 2), :]
        return jnp.concatenate([zr, zi], axis=0).astype(BF16)

    def stage_a(pair, carry):
        b0 = 2 * pair
        d0, d1 = gather_a(b0), gather_a(b0 + 1)
        zero = jnp.zeros_like(d0)
        rhs = jnp.concatenate([jnp.concatenate([d0, zero], axis=1),
                               jnp.concatenate([zero, d1], axis=1)], axis=0)
        e = jnp.dot(fa_ref[pair], rhs, preferred_element_type=F32)
        for j in range(2):
            off = pl.multiple_of((b0 + j) * pitch, 8)
            er_ref[pl.ds(off, na), :] = e[:na, j * LANES:(j + 1) * LANES]
            ei_ref[pl.ds(off, na), :] = e[na:, j * LANES:(j + 1) * LANES]
        return carry

    lax.fori_loop(0, DFT_NB // 2, stage_a, 0, unroll=32)

    def gather_b(ka):
        er = er_ref[pl.ds(ka, DFT_NB, stride=pitch), :]
        ei = ei_ref[pl.ds(ka, DFT_NB, stride=pitch), :]
        return jnp.concatenate([er, ei], axis=0).astype(BF16)

    def stage_b(pair, carry):
        ka0 = 2 * pair
        rhs = jnp.concatenate([gather_b(ka0), gather_b(ka0 + 1)], axis=1)
        g = jnp.dot(fb_ref[...], rhs, preferred_element_type=F32)
        for j in range(2):
            off = pl.multiple_of((ka0 + j) * gpitch, 8)
            g_ref[pl.ds(off, DFT_NB), :] = g[:, j * LANES:(j + 1) * LANES]
        return carry

    lax.fori_loop(0, na // 2, stage_b, 0, unroll=16)

    def gate(kb, carry):
        rows = pl.ds(pl.multiple_of(kb * na, na), na)
        g = g_ref[pl.ds(kb, na, stride=gpitch), :]
        y_ref[rows, :] = (g * zg_ref[rows, :].astype(F32)).astype(BF16)
        return carry

    lax.fori_loop(0, DFT_NB, gate, 0, unroll=16)


def _position_dft(ar, ai, zg, batch, seq):
    na = seq // DFT_NB
    pitch = na + 8
    gpitch = DFT_NB + 8
    fa_np, fb_np = _position_dft_matrices(seq)
    fa = jnp.asarray(fa_np).astype(BF16)
    fb = jnp.asarray(fb_np).astype(BF16)
    slab = pl.BlockSpec((None, seq, LANES), lambda s, b: (s, b, 0))
    return pl.pallas_call(
        functools.partial(_dft_kernel, na, pitch, gpitch),
        grid=(SLABS, batch),
        in_specs=[slab, slab, slab,
                  pl.BlockSpec(fa.shape, lambda s, b: (0, 0, 0)),
                  pl.BlockSpec(fb.shape, lambda s, b: (0, 0))],
        out_specs=slab,
        out_shape=jax.ShapeDtypeStruct(ar.shape, BF16),
        scratch_shapes=[pltpu.VMEM((DFT_NB * pitch, LANES), F32),
                        pltpu.VMEM((DFT_NB * pitch, LANES), F32),
                        pltpu.VMEM((na * gpitch, LANES), F32)],
        compiler_params=_cparams(2),
        name="position_dft",
    )(ar, ai, zg, fa, fb)


def _dft_dense_kernel(ar_ref, ai_ref, zg_ref, fd_ref, y_ref):
    d = jnp.concatenate([ar_ref[...], ai_ref[...]], axis=0).astype(BF16)
    g = jnp.dot(fd_ref[...], d, preferred_element_type=F32)
    y_ref[...] = (g * zg_ref[...].astype(F32)).astype(BF16)


def _position_dft_dense(ar, ai, zg, batch, seq):
    fd = jnp.asarray(_dense_dft_matrix(seq)).astype(BF16)
    slab = pl.BlockSpec((None, seq, LANES), lambda s, b: (s, b, 0))
    return pl.pallas_call(
        _dft_dense_kernel,
        grid=(SLABS, batch),
        in_specs=[slab, slab, slab, pl.BlockSpec(fd.shape, lambda s, b: (0, 0))],
        out_specs=slab,
        out_shape=jax.ShapeDtypeStruct(ar.shape, BF16),
        compiler_params=_cparams(2),
        name="position_dft_dense",
    )(ar, ai, zg, fd)


def _attn_kernel(n_band, nblk, sink_ref, q_ref, *refs):
    k_refs = refs[:n_band + 1]
    v_refs = refs[n_band + 1:2 * n_band + 2]
    zg_ref, y_ref = refs[2 * n_band + 2:]
    i = pl.program_id(2)
    blk = ATTN_BLOCK
    n_pair = Q_GROUP // 2
    qw = Q_GROUP * HEAD_DIM

    lane = lax.broadcasted_iota(jnp.int32, (1, LANES), 1)
    lo = lane < HEAD_DIM
    dn = (((1,), (1,)), ((), ()))
    if n_band:
        r = lax.broadcasted_iota(jnp.int32, (blk, blk), 0)
        c = lax.broadcasted_iota(jnp.int32, (blk, blk), 1)
        prev_ok = c >= r + jnp.where(i > 0, 0, blk)
        next_ok = c <= r - jnp.where(i < nblk - 1, 0, blk)

    def scores(hh):
        kl = slice(hh * LANES, (hh + 1) * LANES)
        keys = jnp.concatenate([ref[:, kl] for ref in k_refs], axis=0)
        q4 = jnp.concatenate([q_ref[:, hh * qw + t * LANES:hh * qw + (t + 1) * LANES]
                              for t in range(n_pair)], axis=0)
        qzero = jnp.zeros_like(q4)
        return [lax.dot_general(qm, keys, dn, preferred_element_type=F32)
                for qm in (jnp.where(lo, q4, qzero), jnp.where(lo, qzero, q4))]

    def finish(hh, s_both):
        h = pl.program_id(1) * ATTN_HEADS_PER_STEP + hh
        kl = slice(hh * LANES, (hh + 1) * LANES)
        vals = jnp.concatenate([ref[:, kl] for ref in v_refs], axis=0)
        ones = jnp.ones_like(vals)
        v_ext = (jnp.where(lo, vals, ones), jnp.where(lo, ones, vals))
        o_ext, sink_term = [], []
        for hd, s in enumerate(s_both):
            p_rows, sink_rows = [], []
            for t in range(n_pair):
                st = s[t * blk:(t + 1) * blk]
                parts = [st[:, j * blk:(j + 1) * blk] for j in range(st.shape[1] // blk)]
                if n_band:
                    parts[0] = jnp.where(prev_ok, parts[0], NEG_INF)
                    parts[2] = jnp.where(next_ok, parts[2], NEG_INF)
                sk = sink_ref[h, 2 * t + hd] * LOG2_E
                mx = parts[0]
                for part in parts[1:]:
                    mx = jnp.maximum(mx, part)
                mx = jnp.maximum(jnp.max(mx, axis=-1, keepdims=True), sk)
                p_rows.append(jnp.concatenate([jnp.exp2(part - mx).astype(BF16) for part in parts], axis=1))
                sink_rows.append(jnp.exp2(sk - mx))
            p = jnp.concatenate(p_rows, axis=0)
            o_ext.append(jnp.dot(p, v_ext[hd], preferred_element_type=F32))
            sink_term.append(jnp.concatenate(sink_rows, axis=0))
        o = jnp.where(lo, o_ext[0], o_ext[1])
        denom = jnp.where(lo, pltpu.roll(o_ext[0], HEAD_DIM, 1) + sink_term[0],
                          pltpu.roll(o_ext[1], HEAD_DIM, 1) + sink_term[1])
        res = o / denom
        for t in range(n_pair):
            tile = slice(hh * qw + t * LANES, hh * qw + (t + 1) * LANES)
            y_ref[:, tile] = (res[t * blk:(t + 1) * blk] * zg_ref[:, tile].astype(F32)).astype(BF16)

    pending = scores(0)
    for hh in range(ATTN_HEADS_PER_STEP):
        upcoming = scores(hh + 1) if hh + 1 < ATTN_HEADS_PER_STEP else None
        finish(hh, pending)
        pending = upcoming


def _attention(q, kd, vd, kcd, vcd, zg, sink, batch, seq, ctx_len, use_band):
    blk = ATTN_BLOCK
    nblk = seq // blk
    hps = ATTN_HEADS_PER_STEP
    qw = hps * Q_GROUP * HEAD_DIM
    kw = hps * LANES
    q_spec = pl.BlockSpec((blk, qw), lambda b, h, i: (b * nblk + i, h))
    ctx_spec = pl.BlockSpec((ctx_len, kw), lambda b, h, i: (b, h))
    if use_band:
        def band(delta):
            return pl.BlockSpec(
                (blk, kw), lambda b, h, i: (b * nblk + jnp.clip(i + delta, 0, nblk - 1), h))
        k_specs = [band(-1), band(0), band(1), ctx_spec]
        k_args, v_args = [kd, kd, kd, kcd], [vd, vd, vd, vcd]
        n_band = 3
    else:
        k_specs, k_args, v_args, n_band = [ctx_spec], [kcd], [vcd], 0
    return pl.pallas_call(
        functools.partial(_attn_kernel, n_band, nblk),
        grid=(batch, KV_HEADS // hps, nblk),
        in_specs=[pl.BlockSpec(memory_space=pltpu.SMEM), q_spec] + k_specs + k_specs + [q_spec],
        out_specs=q_spec,
        out_shape=jax.ShapeDtypeStruct(q.shape, BF16),
        compiler_params=_cparams(3),
        name="attention_band" if use_band else "attention_ctx",
    )(sink, q, *k_args, *v_args, zg)


def _gelu_stats_kernel(chunk, a_ref, w_ref, gv_ref, mu_ref, rstd_ref):
    a = a_ref[...]
    n = w_ref.shape[1]
    mean = m2 = None
    for idx, c0 in enumerate(range(0, n, chunk)):
        ge = _gelu_tanh(jnp.dot(a, w_ref[:, c0:c0 + chunk], preferred_element_type=F32))
        gv_ref[:, c0:c0 + chunk] = ge.astype(BF16)
        cmean = jnp.mean(ge, axis=-1, keepdims=True)
        d = ge - cmean
        cm2 = jnp.sum(d * d, axis=-1, keepdims=True)
        if idx == 0:
            mean, m2 = cmean, cm2
        else:
            delta = cmean - mean
            mean = mean + delta * (1.0 / (idx + 1))
            m2 = m2 + cm2 + delta * delta * (chunk * idx / (idx + 1))
    mu_ref[...] = mean
    rstd_ref[...] = lax.rsqrt(m2 * (1.0 / n) + EPS)


def _proj_gelu_stats(h, w, col0, tm, chunk=2 * MXU_COLS):
    m, k = h.shape
    n = D_BRANCH
    col = pl.BlockSpec((tm, 1), lambda i: (i, 0))
    stat = jax.ShapeDtypeStruct((m, 1), F32)
    return pl.pallas_call(
        functools.partial(_gelu_stats_kernel, chunk),
        grid=(m // tm,),
        in_specs=[pl.BlockSpec((tm, k), lambda i: (i, 0)),
                  pl.BlockSpec((k, n), lambda i: (0, col0 // n), pipeline_mode=pl.Buffered(1))],
        out_specs=(pl.BlockSpec((tm, n), lambda i: (i, 0)), col, col),
        out_shape=(jax.ShapeDtypeStruct((m, n), BF16), stat, stat),
        compiler_params=_cparams(1),
        name="gmlp_in_v",
    )(h, w)


def _sgu_kernel(n_chunks, n_groups, uz_ref, gv_ref, mu_ref, rstd_ref, lg_ref, lb_ref,
                ws_ref, bs_ref, y_ref):
    gd = D_BRANCH // GMLP_GROUPS
    mu = mu_ref[...]
    rstd = rstd_ref[...]
    for gl in range(n_groups):
        cols = slice(gl * gd, (gl + 1) * gd)
        ws = ws_ref[gl]
        bs = bs_ref[gl]
        lg = lg_ref[:, cols]
        lb = lb_ref[:, cols]
        for c in range(n_chunks):
            rows = slice(c * GMLP_CHUNK, (c + 1) * GMLP_CHUNK)
            vn = ((gv_ref[rows, cols].astype(F32) - mu[rows]) * rstd[rows] * lg + lb).astype(BF16)
            s = jnp.dot(ws, vn, preferred_element_type=F32) + bs
            y_ref[rows, cols] = (uz_ref[rows, cols].astype(F32) * s).astype(BF16)


def _spatial_gate(uz, gv, mu, rstd, ln_g, ln_b, w_s, b_s, tm, n_groups=2):
    m = uz.shape[0]
    gd = D_BRANCH // GMLP_GROUPS
    tn = n_groups * gd
    tile = pl.BlockSpec((tm, tn), lambda r, g: (r, g))
    col = pl.BlockSpec((tm, 1), lambda r, g: (r, 0))
    vec = pl.BlockSpec((1, tn), lambda r, g: (0, g))
    return pl.pallas_call(
        functools.partial(_sgu_kernel, tm // GMLP_CHUNK, n_groups),
        grid=(m // tm, GMLP_GROUPS // n_groups),
        in_specs=[tile, tile, col, col, vec, vec,
                  pl.BlockSpec((n_groups, GMLP_CHUNK, GMLP_CHUNK), lambda r, g: (g, 0, 0)),
                  pl.BlockSpec((n_groups, GMLP_CHUNK, 1), lambda r, g: (g, 0, 0))],
        out_specs=tile,
        out_shape=jax.ShapeDtypeStruct(uz.shape, BF16),
        compiler_params=_cparams(2),
        name="gmlp_spatial_gate",
    )(uz, gv, mu, rstd, ln_g.reshape(1, D_BRANCH), ln_b.reshape(1, D_BRANCH),
      w_s, b_s.reshape(GMLP_GROUPS, GMLP_CHUNK, 1))


def _wout_kernel(mode, y_ref, w_ref, x_ref, gate_ref, *refs):
    if len(y_ref.shape) == 3:
        y = jnp.concatenate([y_ref[s] for s in range(y_ref.shape[0])], axis=1)
    else:
        y = y_ref[...]
    acc = jnp.dot(y, w_ref[...], preferred_element_type=F32)
    xn = x_ref[...] + gate_ref[...] * acc
    if mode == "final":
        g_ref, o_ref = refs
        o_ref[...] = xn * lax.rsqrt(jnp.mean(xn * xn, axis=-1, keepdims=True) + EPS) * g_ref[...]
    elif mode == "next":
        g_ref, sc_ref, sh_ref, xo_ref, h_ref = refs
        xo_ref[...] = xn
        h_ref[...] = _mod_rmsnorm(xn, g_ref[...], sc_ref[...], sh_ref[...]).astype(BF16)
    else:
        refs[0][...] = xn


def _out_proj(y, w_out, x2, mods, layer, who, mode, norm_vec, tm):
    m = x2.shape[0]
    row = pl.BlockSpec((tm, D_MODEL), lambda r: (r, 0))
    if y.ndim == 3:
        y_spec = pl.BlockSpec((SLABS, tm, LANES), lambda r: (0, r, 0))
    else:
        y_spec = pl.BlockSpec((tm, D_BRANCH), lambda r: (r, 0))
    in_specs = [y_spec,
                pl.BlockSpec((None, D_BRANCH, D_MODEL), lambda r: (layer, 0, 0),
                             pipeline_mode=pl.Buffered(1)),
                row, _mod_spec(layer, 2, who)]
    args = [y, w_out, x2, mods]
    xs = jax.ShapeDtypeStruct((m, D_MODEL), F32)
    if mode == "final":
        in_specs.append(pl.BlockSpec((1, D_MODEL), lambda r: (0, 0)))
        args.append(norm_vec.reshape(1, D_MODEL))
        out_shape, out_specs = xs, row
    elif mode == "next":
        in_specs += [_row_spec(layer + 1), _mod_spec(layer + 1, 1, who), _mod_spec(layer + 1, 0, who)]
        args += [norm_vec, mods, mods]
        out_shape = (xs, jax.ShapeDtypeStruct((m, D_MODEL), BF16))
        out_specs = (row, row)
    else:
        out_shape, out_specs = xs, row
    return pl.pallas_call(
        functools.partial(_wout_kernel, mode),
        grid=(m // tm,),
        in_specs=in_specs,
        out_specs=out_specs,
        out_shape=out_shape,
        compiler_params=_cparams(1),
        name="out_proj_" + mode,
    )(*args)


def kernel(x, c, ctx, c_ctx, norm_g, ada_w, ada_b, w_out, fnet_w_in, fnet_w_mix, attn_w_in, attn_sink,
           gmlp_w_in, gmlp_w_s, gmlp_b_s, gmlp_ln_g, gmlp_ln_b, final_g):
    batch, seq, d = x.shape
    ctx_len = ctx.shape[1]
    assert d == D_MODEL and seq % (DFT_NB * 8) == 0 and seq % GRID_W == 0 and batch < 8
    m_lat, m_ctx = batch * seq, batch * ctx_len
    tm_lat = min(2048, seq)
    tm_ctx = ctx_len
    tm_out = 256

    def who_lat(tm):
        return lambda r: (r * tm) // seq

    who_ctx = lambda r: batch

    cvec = jnp.zeros((8, d), F32).at[:batch].set(c).at[batch].set(c_ctx)
    mods = _mods(cvec, ada_w, ada_b).reshape(DEPTH, 8, 3, 1, d)
    norm_g3 = norm_g.reshape(DEPTH, 1, d)
    w_out_b = w_out.astype(BF16)

    xl = x.reshape(m_lat, d)
    xc = ctx.reshape(m_ctx, d)
    hl = _prenorm(xl, norm_g3, mods, 0, who_lat(512), 512)
    hc = _prenorm(xc, norm_g3, mods, 0, who_ctx, tm_ctx)


    def finish(y, x2, layer, who_fn, need_next, is_final):
        if is_final:
            return _out_proj(y, w_out_b, x2, mods, layer, who_fn, "final", final_g, tm_out), None
        if need_next:
            return _out_proj(y, w_out_b, x2, mods, layer, who_fn, "next", norm_g3, tm_out)
        return _out_proj(y, w_out_b, x2, mods, layer, who_fn, "plain", None, tm_out), None

    out = None
    for i in range(DEPTH):
        kind, j = i % 3, i // 3
        need_ctx = i < DEPTH - 1
        last = i == DEPTH - 1
        if kind == 0:
            cs = _fold_mix(fnet_w_mix[j])
            tm_dft = (seq // DFT_NB) * DFT_ROWS
            w_u = fnet_w_in[j, :, :D_BRANCH].astype(BF16)
            ar, ai = _proj_channel_dft(hl, w_u, cs, "fnet_in_u", tm_dft, seq=seq)
            zg = _proj_silu_slabs(hl, fnet_w_in, D_BRANCH, "fnet_in_z", tm_lat, w_layer=j)
            y = _position_dft(ar, ai, zg, batch, seq)
            yc = None
            if need_ctx:
                ar, ai = _proj_channel_dft(hc, w_u, cs, "fnet_in_u_ctx", tm_ctx)
                zg = _proj_silu_slabs(hc, fnet_w_in, D_BRANCH, "fnet_in_z_ctx", tm_ctx, w_layer=j)
                yc = _position_dft_dense(ar, ai, zg, batch, ctx_len)
        elif kind == 1:
            kvw = KV_HEADS * HEAD_DIM
            k0, v0, z0 = D_BRANCH, D_BRANCH + kvw, D_BRANCH + 2 * kvw
            sink = attn_sink[j].reshape(KV_HEADS, Q_GROUP)
            scale = HEAD_DIM ** -0.5 * LOG2_E
            q_tabs = [jnp.asarray(t) for t in _rope_tables(seq, scale)]
            k_tabs = [jnp.asarray(t) for t in _rope_tables(seq)]
            q = _proj_rope(hl, attn_w_in, 0, D_BRANCH, *q_tabs, seq, "attn_in_q", tm_lat, w_layer=j)
            kd = _proj_rope(hl, attn_w_in, k0, kvw, *k_tabs, seq, "attn_in_k", tm_lat, w_layer=j,
                            dup_heads=True)
            vd = _proj_simple(hl, attn_w_in, v0, kvw, _epi_cast_dup, "attn_in_v", tm_lat, w_layer=j, out_mult=2)
            zg = _proj_simple(hl, attn_w_in, z0, D_BRANCH, _epi_silu, "attn_in_z", tm_lat, w_layer=j)
            kcd = _proj_simple(hc, attn_w_in, k0, kvw, _epi_cast_dup, "attn_in_kc", tm_ctx, w_layer=j, out_mult=2)
            vcd = _proj_simple(hc, attn_w_in, v0, kvw, _epi_cast_dup, "attn_in_vc", tm_ctx, w_layer=j, out_mult=2)
            y = _attention(q, kd, vd, kcd, vcd, zg, sink, batch, seq, ctx_len, True)
            yc = None
            if need_ctx:
                qc = _proj_simple(hc, attn_w_in, 0, D_BRANCH, functools.partial(_epi_scale_cast, scale),
                                  "attn_in_qc", tm_ctx, w_layer=j)
                zgc = _proj_simple(hc, attn_w_in, z0, D_BRANCH, _epi_silu, "attn_in_zc", tm_ctx, w_layer=j)
                yc = _attention(qc, None, None, kcd, vcd, zgc, sink, batch, ctx_len, ctx_len, False)
        else:
            w_v = gmlp_w_in[j, :, D_BRANCH:2 * D_BRANCH].astype(BF16)
            ws = gmlp_w_s[j].astype(BF16)

            def gmlp_branch(h, tm):
                uz = _proj_simple(h, gmlp_w_in, (0, 2 * D_BRANCH), D_BRANCH, _epi_gelu_times_silu,
                                  "gmlp_in_uz", tm, tn=PROJ_TN // 2, w_layer=j)
                gv, mu, rstd = _proj_gelu_stats(h, w_v, 0, min(tm, 256))
                return _spatial_gate(uz, gv, mu, rstd, gmlp_ln_g[j], gmlp_ln_b[j], ws, gmlp_b_s[j],
                                     min(tm, 1024))

            y = gmlp_branch(hl, tm_lat)
            yc = gmlp_branch(hc, tm_ctx) if need_ctx else None

        res, hl = finish(y, xl, i, who_lat(tm_out), not last, last)
        if last:
            out = res
        else:
            xl = res
        if need_ctx:
            xc, hc = finish(yc, xc, i, who_ctx, i + 1 < DEPTH - 1 or (i + 1) % 3 == 1, False)
    return out.reshape(batch, seq, d)
```

```python
import functools
import math

import numpy as np
import jax
import jax.numpy as jnp
from jax import lax
from jax.experimental import pallas as pl
from jax.experimental.pallas import tpu as pltpu

F32 = jnp.float32
BF16 = jnp.bfloat16

D_MODEL = 2048
D_BRANCH = 4096
DEPTH = 4
GRID_W = 64
FNET_GROUPS = 16
FNET_GROUP_DIM = 256
HEAD_DIM = 64
KV_HEADS = 8
Q_GROUP = 8
ATTN_BLOCK = 128
ATTN_HEADS_PER_STEP = 8
ROPE_BASE = 10000.0
GMLP_CHUNK = 128
GMLP_GROUPS = 16
EPS = 1e-6
NEG_INF = -1e30
LOG2_E = math.log2(math.e)

LANES = 128
MXU_COLS = 256
PROJ_TN = 1024
DFT_NB = 128
DFT_ROWS = 16
SLABS = D_BRANCH // LANES
VMEM_LIMIT = 56 * 1024 * 1024


def _cparams(n_axes, vmem=VMEM_LIMIT):
    return pltpu.CompilerParams(dimension_semantics=("arbitrary",) * n_axes,
                                vmem_limit_bytes=vmem)


def _silu(z):
    return 0.5 * z * (1.0 + jnp.tanh(0.5 * z))


def _gelu_tanh(x):
    c = math.sqrt(2.0 / math.pi)
    return 0.5 * x * (1.0 + jnp.tanh(c * (x + 0.044715 * (x * x * x))))


def _mod_rmsnorm(x, g, scale, shift):
    y = x * lax.rsqrt(jnp.mean(x * x, axis=-1, keepdims=True) + EPS) * g
    return y * (1.0 + scale) + shift


def _channel_dft_matrix():
    n = FNET_GROUP_DIM
    k = np.arange(n, dtype=np.float64)
    ang = 2.0 * np.pi * np.outer(k, k) / n
    s = 1.0 / math.sqrt(n)
    return np.concatenate([np.cos(ang) * s, -np.sin(ang) * s], axis=1).astype(np.float32)


def _position_dft_matrices(seq):
    na, nb = seq // DFT_NB, DFT_NB
    a = np.arange(na, dtype=np.float64)
    b = np.arange(nb, dtype=np.float64)
    ang = 2.0 * np.pi * (a[None, None, :] * a[None, :, None] / na + b[:, None, None] * a[None, :, None] / seq)
    mr = np.cos(ang) / math.sqrt(na)
    mi = -np.sin(ang) / math.sqrt(na)
    fa = np.concatenate([np.concatenate([mr, -mi], axis=2), np.concatenate([mi, mr], axis=2)], axis=1)
    fa = np.concatenate([fa[0::2], fa[1::2]], axis=2)
    angb = 2.0 * np.pi * np.outer(b, b) / nb
    fb = np.concatenate([np.cos(angb), np.sin(angb)], axis=1) / math.sqrt(nb)
    return fa.astype(np.float32), fb.astype(np.float32)


def _dense_dft_matrix(n):
    k = np.arange(n, dtype=np.float64)
    ang = 2.0 * np.pi * np.outer(k, k) / n
    return (np.concatenate([np.cos(ang), np.sin(ang)], axis=1) / math.sqrt(n)).astype(np.float32)


def _rope_tables(seq, scale=1.0):
    nf = HEAD_DIM // 4
    inv = ROPE_BASE ** (-np.arange(nf, dtype=np.float64) / nf)
    t = np.arange(seq)
    rows = (t // GRID_W).astype(np.float64)
    cols = (t % GRID_W).astype(np.float64)
    parts_c, parts_s = [], []
    for pos in (rows, cols):
        ang = pos[:, None] * inv[None, :]
        parts_c += [np.cos(ang), np.cos(ang)]
        parts_s += [-np.sin(ang), np.sin(ang)]
    cos = np.concatenate(parts_c, axis=1) * scale
    sin = np.concatenate(parts_s, axis=1) * scale
    reps = LANES // HEAD_DIM
    return (np.tile(cos, (1, reps)).astype(np.float32), np.tile(sin, (1, reps)).astype(np.float32))


def _mods_kernel(cv_ref, w_ref, b_ref, o_ref):
    a = _silu(cv_ref[...])
    a_hi = a.astype(BF16)
    a_lo = (a - a_hi.astype(F32)).astype(BF16)
    w = w_ref[...].astype(BF16)
    o_ref[...] = (jnp.dot(a_hi, w, preferred_element_type=F32)
                  + jnp.dot(a_lo, w, preferred_element_type=F32)) + b_ref[...]


def _mods(cvec, ada_w, ada_b):
    depth, d, n3 = ada_w.shape
    tn = 2048
    return pl.pallas_call(
        _mods_kernel,
        grid=(depth, n3 // tn),
        in_specs=[pl.BlockSpec((8, d), lambda i, j: (0, 0)),
                  pl.BlockSpec((None, d, tn), lambda i, j: (i, 0, j)),
                  pl.BlockSpec((None, 1, tn), lambda i, j: (i, 0, j))],
        out_specs=pl.BlockSpec((None, 8, tn), lambda i, j: (i, 0, j)),
        out_shape=jax.ShapeDtypeStruct((depth, 8, n3), F32),
        compiler_params=_cparams(2),
        name="ada_mods",
    )(cvec, ada_w, ada_b.reshape(depth, 1, n3))


def _mod_spec(layer, kind, who_of_row):
    return pl.BlockSpec((None, None, None, 1, D_MODEL),
                        lambda r, *_: (layer, who_of_row(r), kind, 0, 0))


def _row_spec(vec_layer):
    return pl.BlockSpec((None, 1, D_MODEL), lambda r, *_: (vec_layer, 0, 0))


def _prenorm_kernel(x_ref, g_ref, sc_ref, sh_ref, h_ref):
    h_ref[...] = _mod_rmsnorm(x_ref[...], g_ref[...], sc_ref[...], sh_ref[...]).astype(BF16)


def _prenorm(x2, norm_g3, mods, layer, who, tm):
    m = x2.shape[0]
    return pl.pallas_call(
        _prenorm_kernel,
        grid=(m // tm,),
        in_specs=[pl.BlockSpec((tm, D_MODEL), lambda r: (r, 0)),
                  _row_spec(layer), _mod_spec(layer, 1, who), _mod_spec(layer, 0, who)],
        out_specs=pl.BlockSpec((tm, D_MODEL), lambda r: (r, 0)),
        out_shape=jax.ShapeDtypeStruct((m, D_MODEL), BF16),
        compiler_params=_cparams(1),
        name="prenorm",
    )(x2, norm_g3, mods, mods)


def _proj_kernel(epilogue, n_extra, chunk, n_w, a_ref, *rest):
    w_refs, rest = rest[:n_w], rest[n_w:]
    if len(a_ref.shape) == 3:
        a_flat = rest[-1]
        rest = rest[:-1]

        @pl.when(pl.program_id(1) == 0)
        def _():
            a_flat[...] = a_ref[...].reshape(a_flat.shape)

        a = a_flat[...]
    else:
        a = a_ref[...]
    for c0 in range(0, w_refs[0].shape[1], chunk):
        accs = [jnp.dot(a, w_ref[:, c0:c0 + chunk].astype(BF16), preferred_element_type=F32)
                for w_ref in w_refs]
        epilogue(accs[0] if n_w == 1 else accs, c0, rest[:n_extra], rest[n_extra:])


def _proj(h, w, col0, ncols, tm, tn, epilogue, extras, extra_specs, out_shapes, out_specs, name,
          lhs_spec=None, chunk=MXU_COLS, w_layer=None):
    k = h.shape[-1]
    m = h.size // k
    col0s = col0 if isinstance(col0, tuple) else (col0,)
    scratch = [pltpu.VMEM((tm, k), h.dtype)] if lhs_spec is not None else []
    if lhs_spec is None:
        lhs_spec = pl.BlockSpec((tm, k), lambda i, j: (i, 0))

    def w_spec(off):
        if w.ndim == 3:
            return pl.BlockSpec((None, k, tn), lambda i, j: (w_layer, 0, j + off))
        return pl.BlockSpec((k, tn), lambda i, j: (0, j + off))

    return pl.pallas_call(
        functools.partial(_proj_kernel, epilogue, len(extras), min(chunk, tn), len(col0s)),
        grid=(m // tm, ncols // tn),
        in_specs=[lhs_spec] + [w_spec(c // tn) for c in col0s] + list(extra_specs),
        out_specs=out_specs,
        out_shape=out_shapes,
        scratch_shapes=scratch,
        compiler_params=_cparams(2),
        name=name,
    )(h, *([w] * len(col0s)), *extras)


def _epi_silu(acc, c0, extras, outs):
    outs[0][:, c0:c0 + acc.shape[1]] = _silu(acc).astype(BF16)


def _epi_silu_slabs(acc, c0, extras, outs):
    for t in range(acc.shape[1] // LANES):
        outs[0][c0 // LANES + t] = _silu(acc[:, t * LANES:(t + 1) * LANES]).astype(BF16)


def _epi_gelu_times_silu(accs, c0, extras, outs):
    acc_u, acc_z = accs
    outs[0][:, c0:c0 + acc_u.shape[1]] = (_gelu_tanh(acc_u) * _silu(acc_z)).astype(BF16)


def _epi_cast(acc, c0, extras, outs):
    outs[0][:, c0:c0 + acc.shape[1]] = acc.astype(BF16)


def _epi_scale_cast(scale, acc, c0, extras, outs):
    outs[0][:, c0:c0 + acc.shape[1]] = (acc * scale).astype(BF16)


def _epi_channel_dft(split_rows, acc, c0, extras, outs):
    cs_ref = extras[0]
    ar_ref, ai_ref = outs
    tm = acc.shape[0]
    for gl in range(acc.shape[1] // FNET_GROUP_DIM):
        g = c0 // FNET_GROUP_DIM + gl
        ub = acc[:, gl * FNET_GROUP_DIM:(gl + 1) * FNET_GROUP_DIM].astype(BF16)
        ab = jnp.dot(ub, cs_ref[g], preferred_element_type=F32)
        if split_rows:
            ab4 = ab.reshape(tm // DFT_ROWS, 2, DFT_ROWS // 2, ab.shape[1])
            ab = jnp.concatenate([ab4[:, hf].reshape(tm // 2, ab.shape[1]) for hf in range(2)], axis=0)
        ar_ref[2 * g] = ab[:, 0:128]
        ar_ref[2 * g + 1] = ab[:, 128:256]
        ai_ref[2 * g] = ab[:, 256:384]
        ai_ref[2 * g + 1] = ab[:, 384:512]


def _store_tile(out_ref, col, y, dup_heads):
    if not dup_heads:
        out_ref[:, col:col + LANES] = y.astype(BF16)
        return
    lane = lax.broadcasted_iota(jnp.int32, (1, LANES), 1)
    lo = lane < HEAD_DIM
    swapped = pltpu.roll(y, HEAD_DIM, 1)
    out_ref[:, 2 * col:2 * col + LANES] = jnp.where(lo, y, swapped).astype(BF16)
    out_ref[:, 2 * col + LANES:2 * col + 2 * LANES] = jnp.where(lo, swapped, y).astype(BF16)


def _epi_rope(dup_heads, acc, c0, extras, outs):
    cos = extras[0][...]
    sin = extras[1][...]
    lane = lax.broadcasted_iota(jnp.int32, (1, LANES), 1)
    first = (lane % 32) < 16
    for t in range(acc.shape[1] // LANES):
        x = acc[:, t * LANES:(t + 1) * LANES]
        partner = jnp.where(first, pltpu.roll(x, LANES - 16, 1), pltpu.roll(x, 16, 1))
        _store_tile(outs[0], c0 + t * LANES, x * cos + partner * sin, dup_heads)


def _epi_cast_dup(acc, c0, extras, outs):
    for t in range(acc.shape[1] // LANES):
        _store_tile(outs[0], c0 + t * LANES, acc[:, t * LANES:(t + 1) * LANES], True)


def _proj_simple(h, w, col0, ncols, epilogue, name, tm, tn=PROJ_TN, w_layer=None, out_mult=1):
    m = h.shape[0]
    tn = min(tn, ncols)
    return _proj(h, w, col0, ncols, tm, tn, epilogue, (), (),
                 jax.ShapeDtypeStruct((m, out_mult * ncols), BF16),
                 pl.BlockSpec((tm, out_mult * tn), lambda i, j: (i, j)), name, w_layer=w_layer)


def _proj_rope(h, w, col0, ncols, cos_t, sin_t, seq, name, tm, tn=PROJ_TN, w_layer=None,
               dup_heads=False):
    m = h.shape[0]
    tn = min(tn, ncols)
    per_batch = seq // tm
    out_mult = 2 if dup_heads else 1
    tab_spec = pl.BlockSpec((tm, LANES), lambda i, j: (i % per_batch, 0))
    return _proj(h, w, col0, ncols, tm, tn, functools.partial(_epi_rope, dup_heads),
                 (cos_t, sin_t), (tab_spec, tab_spec),
                 jax.ShapeDtypeStruct((m, out_mult * ncols), BF16),
                 pl.BlockSpec((tm, out_mult * tn), lambda i, j: (i, j)), name, w_layer=w_layer)


def _proj_silu_slabs(h, w, col0, name, tm, tn=PROJ_TN, w_layer=None):
    m = h.shape[0]
    return _proj(h, w, col0, D_BRANCH, tm, tn, _epi_silu_slabs, (), (),
                 jax.ShapeDtypeStruct((SLABS, m, LANES), BF16),
                 pl.BlockSpec((tn // LANES, tm, LANES), lambda i, j: (j, i, 0)), name, w_layer=w_layer)


def _proj_channel_dft(h, w, cs, name, tm, seq=None, tn=PROJ_TN):
    m = h.shape[0]
    lhs_spec = None
    if seq is not None:
        na, tiles = seq // DFT_NB, DFT_NB // DFT_ROWS
        assert tm == na * DFT_ROWS
        h = h.reshape(m // seq, na, DFT_NB, D_MODEL)
        lhs_spec = pl.BlockSpec((None, na, DFT_ROWS, D_MODEL), lambda i, j: (i // tiles, 0, i % tiles, 0))
    slab_shape = jax.ShapeDtypeStruct((SLABS, m, LANES), F32)
    slab_spec = pl.BlockSpec((tn // LANES, tm, LANES), lambda i, j: (j, i, 0))
    groups = tn // FNET_GROUP_DIM
    cs_spec = pl.BlockSpec((groups,) + cs.shape[1:], lambda i, j: (j, 0, 0))
    return _proj(h, w, 0, D_BRANCH, tm, tn, functools.partial(_epi_channel_dft, seq is not None), (cs,), (cs_spec,),
                 (slab_shape, slab_shape), (slab_spec, slab_spec), name, lhs_spec=lhs_spec,
                 chunk=2 * MXU_COLS)


def _fold_mix_kernel(c_ref, s_ref, wm_ref, o_ref):
    wm = wm_ref[...]
    gd = FNET_GROUP_DIM
    hp = lax.Precision.HIGHEST
    o_ref[:, 0:gd] = jnp.dot(c_ref[...], wm, preferred_element_type=F32, precision=hp).astype(BF16)
    o_ref[:, gd:2 * gd] = jnp.dot(s_ref[...], wm, preferred_element_type=F32, precision=hp).astype(BF16)


def _fold_mix(w_mix):
    gd = FNET_GROUP_DIM
    cs = _channel_dft_matrix()
    mat = pl.BlockSpec((gd, gd), lambda g: (0, 0))
    return pl.pallas_call(
        _fold_mix_kernel,
        grid=(FNET_GROUPS,),
        in_specs=[mat, mat, pl.BlockSpec((None, gd, gd), lambda g: (g, 0, 0))],
        out_specs=pl.BlockSpec((None, gd, 2 * gd), lambda g: (g, 0, 0)),
        out_shape=jax.ShapeDtypeStruct((FNET_GROUPS, gd, 2 * gd), BF16),
        compiler_params=_cparams(1),
        name="fnet_fold_mix",
    )(jnp.asarray(cs[:, :gd]), jnp.asarray(cs[:, gd:]), w_mix)


def _dft_kernel(na, pitch, gpitch, ar_ref, ai_ref, zg_ref, fa_ref, fb_ref, y_ref, er_ref, ei_ref, g_ref):
    half_rows = na * (DFT_ROWS // 2)

    def gather_a(b):
        start = (b // (DFT_ROWS // 2)) * half_rows + b % (DFT_ROWS // 2)
        zr = ar_ref[pl.ds(start, na, stride=DFT_ROWS // 2), :]
        zi = ai_ref[pl.ds(start, na, stride=DFT_ROWS // 2), :]
        return jnp.concatenate([zr, zi], axis=0).astype(BF16)

    def stage_a(pair, carry):
        b0 = 2 * pair
        d0, d1 = gather_a(b0), gather_a(b0 + 1)
        zero = jnp.zeros_like(d0)
        rhs = jnp.concatenate([jnp.concatenate([d0, zero], axis=1),
                               jnp.concatenate([zero, d1], axis=1)], axis=0)
        e = jnp.dot(fa_ref[pair], rhs, preferred_element_type=F32)
        for j in range(2):
            off = pl.multiple_of((b0 + j) * pitch, 8)
            er_ref[pl.ds(off, na), :] = e[:na, j * LANES:(j + 1) * LANES]
            ei_ref[pl.ds(off, na), :] = e[na:, j * LANES:(j + 1) * LANES]
        return carry

    lax.fori_loop(0, DFT_NB // 2, stage_a, 0, unroll=64)

    def gather_b(ka):
        er = er_ref[pl.ds(ka, DFT_NB, stride=pitch), :]
        ei = ei_ref[pl.ds(ka, DFT_NB, stride=pitch), :]
        return jnp.concatenate([er, ei], axis=0).astype(BF16)

    def stage_b(pair, carry):
        ka0 = 2 * pair
        rhs = jnp.concatenate([gather_b(ka0), gather_b(ka0 + 1)], axis=1)
        g = jnp.dot(fb_ref[...], rhs, preferred_element_type=F32)
        for j in range(2):
            off = pl.multiple_of((ka0 + j) * gpitch, 8)
            g_ref[pl.ds(off, DFT_NB), :] = g[:, j * LANES:(j + 1) * LANES]
        return carry

    lax.fori_loop(0, na // 2, stage_b, 0, unroll=32)

    def gate(kb, carry):
        rows = pl.ds(pl.multiple_of(kb * na, na), na)
        g = g_ref[pl.ds(kb, na, stride=gpitch), :]
        y_ref[rows, :] = (g * zg_ref[rows, :].astype(F32)).astype(BF16)
        return carry

    lax.fori_loop(0, DFT_NB, gate, 0, unroll=32)


def _position_dft(ar, ai, zg, batch, seq):
    na = seq // DFT_NB
    pitch = na + 8
    gpitch = DFT_NB + 8
    fa_np, fb_np = _position_dft_matrices(seq)
    fa = jnp.asarray(fa_np).astype(BF16)
    fb = jnp.asarray(fb_np).astype(BF16)
    slab = pl.BlockSpec((None, seq, LANES), lambda s, b: (s, b, 0))
    return pl.pallas_call(
        functools.partial(_dft_kernel, na, pitch, gpitch),
        grid=(SLABS, batch),
        in_specs=[slab, slab, slab,
                  pl.BlockSpec(fa.shape, lambda s, b: (0, 0, 0)),
                  pl.BlockSpec(fb.shape, lambda s, b: (0, 0))],
        out_specs=slab,
        out_shape=jax.ShapeDtypeStruct(ar.shape, BF16),
        scratch_shapes=[pltpu.VMEM((DFT_NB * pitch, LANES), F32),
                        pltpu.VMEM((DFT_NB * pitch, LANES), F32),
                        pltpu.VMEM((na * gpitch, LANES), F32)],
        compiler_params=_cparams(2),
        name="position_dft",
    )(ar, ai, zg, fa, fb)


def _dft_dense_kernel(ar_ref, ai_ref, zg_ref, fd_ref, y_ref):
    d = jnp.concatenate([ar_ref[...], ai_ref[...]], axis=0).astype(BF16)
    g = jnp.dot(fd_ref[...], d, preferred_element_type=F32)
    y_ref[...] = (g * zg_ref[...].astype(F32)).astype(BF16)


def _position_dft_dense(ar, ai, zg, batch, seq):
    fd = jnp.asarray(_dense_dft_matrix(seq)).astype(BF16)
    slab = pl.BlockSpec((None, seq, LANES), lambda s, b: (s, b, 0))
    return pl.pallas_call(
        _dft_dense_kernel,
        grid=(SLABS, batch),
        in_specs=[slab, slab, slab, pl.BlockSpec(fd.shape, lambda s, b: (0, 0))],
        out_specs=slab,
        out_shape=jax.ShapeDtypeStruct(ar.shape, BF16),
        compiler_params=_cparams(2),
        name="position_dft_dense",
    )(ar, ai, zg, fd)


def _attn_kernel(n_band, nblk, sink_ref, q_ref, *refs):
    k_refs = refs[:n_band + 1]
    v_refs = refs[n_band + 1:2 * n_band + 2]
    zg_ref, y_ref = refs[2 * n_band + 2:]
    i = pl.program_id(2)
    blk = ATTN_BLOCK
    n_pair = Q_GROUP // 2
    qw = Q_GROUP * HEAD_DIM

    lane = lax.broadcasted_iota(jnp.int32, (1, LANES), 1)
    lo = lane < HEAD_DIM
    dn = (((1,), (1,)), ((), ()))
    if n_band:
        r = lax.broadcasted_iota(jnp.int32, (blk, blk), 0)
        c = lax.broadcasted_iota(jnp.int32, (blk, blk), 1)
        prev_ok = c >= r + jnp.where(i > 0, 0, blk)
        next_ok = c <= r - jnp.where(i < nblk - 1, 0, blk)

    def scores(hh):
        kl = slice(hh * LANES, (hh + 1) * LANES)
        keys = jnp.concatenate([ref[:, kl] for ref in k_refs], axis=0)
        q4 = jnp.concatenate([q_ref[:, hh * qw + t * LANES:hh * qw + (t + 1) * LANES]
                              for t in range(n_pair)], axis=0)
        qzero = jnp.zeros_like(q4)
        return [lax.dot_general(qm, keys, dn, preferred_element_type=F32)
                for qm in (jnp.where(lo, q4, qzero), jnp.where(lo, qzero, q4))]

    def finish(hh, s_both):
        h = pl.program_id(1) * ATTN_HEADS_PER_STEP + hh
        kl = slice(hh * LANES, (hh + 1) * LANES)
        vals = jnp.concatenate([ref[:, kl] for ref in v_refs], axis=0)
        ones = jnp.ones_like(vals)
        v_ext = (jnp.where(lo, vals, ones), jnp.where(lo, ones, vals))
        o_ext, sink_term = [], []
        for hd, s in enumerate(s_both):
            p_rows, sink_rows = [], []
            for t in range(n_pair):
                st = s[t * blk:(t + 1) * blk]
                parts = [st[:, j * blk:(j + 1) * blk] for j in range(st.shape[1] // blk)]
                if n_band:
                    parts[0] = jnp.where(prev_ok, parts[0], NEG_INF)
                    parts[2] = jnp.where(next_ok, parts[2], NEG_INF)
                sk = sink_ref[h, 2 * t + hd] * LOG2_E
                mx = parts[0]
                for part in parts[1:]:
                    mx = jnp.maximum(mx, part)
                mx = jnp.maximum(jnp.max(mx, axis=-1, keepdims=True), sk)
                p_rows.append(jnp.concatenate([jnp.exp2(part - mx).astype(BF16) for part in parts], axis=1))
                sink_rows.append(jnp.exp2(sk - mx))
            p = jnp.concatenate(p_rows, axis=0)
            o_ext.append(jnp.dot(p, v_ext[hd], preferred_element_type=F32))
            sink_term.append(jnp.concatenate(sink_rows, axis=0))
        o = jnp.where(lo, o_ext[0], o_ext[1])
        denom = jnp.where(lo, pltpu.roll(o_ext[0], HEAD_DIM, 1) + sink_term[0],
                          pltpu.roll(o_ext[1], HEAD_DIM, 1) + sink_term[1])
        res = o / denom
        for t in range(n_pair):
            tile = slice(hh * qw + t * LANES, hh * qw + (t + 1) * LANES)
            y_ref[:, tile] = (res[t * blk:(t + 1) * blk] * zg_ref[:, tile].astype(F32)).astype(BF16)

    pending = scores(0)
    for hh in range(ATTN_HEADS_PER_STEP):
        upcoming = scores(hh + 1) if hh + 1 < ATTN_HEADS_PER_STEP else None
        finish(hh, pending)
        pending = upcoming


def _attention(q, kd, vd, kcd, vcd, zg, sink, batch, seq, ctx_len, use_band):
    blk = ATTN_BLOCK
    nblk = seq // blk
    hps = ATTN_HEADS_PER_STEP
    qw = hps * Q_GROUP * HEAD_DIM
    kw = hps * LANES
    q_spec = pl.BlockSpec((blk, qw), lambda b, h, i: (b * nblk + i, h))
    ctx_spec = pl.BlockSpec((ctx_len, kw), lambda b, h, i: (b, h))
    if use_band:
        def band(delta):
            return pl.BlockSpec(
                (blk, kw), lambda b, h, i: (b * nblk + jnp.clip(i + delta, 0, nblk - 1), h))
        k_specs = [band(-1), band(0), band(1), ctx_spec]
        k_args, v_args = [kd, kd, kd, kcd], [vd, vd, vd, vcd]
        n_band = 3
    else:
        k_specs, k_args, v_args, n_band = [ctx_spec], [kcd], [vcd], 0
    return pl.pallas_call(
        functools.partial(_attn_kernel, n_band, nblk),
        grid=(batch, KV_HEADS // hps, nblk),
        in_specs=[pl.BlockSpec(memory_space=pltpu.SMEM), q_spec] + k_specs + k_specs + [q_spec],
        out_specs=q_spec,
        out_shape=jax.ShapeDtypeStruct(q.shape, BF16),
        compiler_params=_cparams(3),
        name="attention_band" if use_band else "attention_ctx",
    )(sink, q, *k_args, *v_args, zg)


def _gelu_stats_kernel(chunk, a_ref, w_ref, gv_ref, mu_ref, rstd_ref):
    a = a_ref[...]
    n = w_ref.shape[1]
    mean = m2 = None
    for idx, c0 in enumerate(range(0, n, chunk)):
        ge = _gelu_tanh(jnp.dot(a, w_ref[:, c0:c0 + chunk], preferred_element_type=F32))
        gv_ref[:, c0:c0 + chunk] = ge.astype(BF16)
        cmean = jnp.mean(ge, axis=-1, keepdims=True)
        d = ge - cmean
        cm2 = jnp.sum(d * d, axis=-1, keepdims=True)
        if idx == 0:
            mean, m2 = cmean, cm2
        else:
            delta = cmean - mean
            mean = mean + delta * (1.0 / (idx + 1))
            m2 = m2 + cm2 + delta * delta * (chunk * idx / (idx + 1))
    mu_ref[...] = mean
    rstd_ref[...] = lax.rsqrt(m2 * (1.0 / n) + EPS)


def _proj_gelu_stats(h, w, col0, tm, chunk=2 * MXU_COLS):
    m, k = h.shape
    n = D_BRANCH
    col = pl.BlockSpec((tm, 1), lambda i: (i, 0))
    stat = jax.ShapeDtypeStruct((m, 1), F32)
    return pl.pallas_call(
        functools.partial(_gelu_stats_kernel, chunk),
        grid=(m // tm,),
        in_specs=[pl.BlockSpec((tm, k), lambda i: (i, 0)),
                  pl.BlockSpec((k, n), lambda i: (0, col0 // n), pipeline_mode=pl.Buffered(1))],
        out_specs=(pl.BlockSpec((tm, n), lambda i: (i, 0)), col, col),
        out_shape=(jax.ShapeDtypeStruct((m, n), BF16), stat, stat),
        compiler_params=_cparams(1),
        name="gmlp_in_v",
    )(h, w)


def _sgu_kernel(n_chunks, n_groups, uz_ref, gv_ref, mu_ref, rstd_ref, lg_ref, lb_ref,
                ws_ref, bs_ref, y_ref):
    gd = D_BRANCH // GMLP_GROUPS
    mu = mu_ref[...]
    rstd = rstd_ref[...]
    for gl in range(n_groups):
        cols = slice(gl * gd, (gl + 1) * gd)
        ws = ws_ref[gl]
        bs = bs_ref[gl]
        lg = lg_ref[:, cols]
        lb = lb_ref[:, cols]
        for c in range(n_chunks):
            rows = slice(c * GMLP_CHUNK, (c + 1) * GMLP_CHUNK)
            vn = ((gv_ref[rows, cols].astype(F32) - mu[rows]) * rstd[rows] * lg + lb).astype(BF16)
            s = jnp.dot(ws, vn, preferred_element_type=F32) + bs
            y_ref[rows, cols] = (uz_ref[rows, cols].astype(F32) * s).astype(BF16)


def _spatial_gate(uz, gv, mu, rstd, ln_g, ln_b, w_s, b_s, tm, n_groups=4):
    m = uz.shape[0]
    gd = D_BRANCH // GMLP_GROUPS
    tn = n_groups * gd
    tile = pl.BlockSpec((tm, tn), lambda r, g: (r, g))
    col = pl.BlockSpec((tm, 1), lambda r, g: (r, 0))
    vec = pl.BlockSpec((1, tn), lambda r, g: (0, g))
    return pl.pallas_call(
        functools.partial(_sgu_kernel, tm // GMLP_CHUNK, n_groups),
        grid=(m // tm, GMLP_GROUPS // n_groups),
        in_specs=[tile, tile, col, col, vec, vec,
                  pl.BlockSpec((n_groups, GMLP_CHUNK, GMLP_CHUNK), lambda r, g: (g, 0, 0)),
                  pl.BlockSpec((n_groups, GMLP_CHUNK, 1), lambda r, g: (g, 0, 0))],
        out_specs=tile,
        out_shape=jax.ShapeDtypeStruct(uz.shape, BF16),
        compiler_params=_cparams(2),
        name="gmlp_spatial_gate",
    )(uz, gv, mu, rstd, ln_g.reshape(1, D_BRANCH), ln_b.reshape(1, D_BRANCH),
      w_s, b_s.reshape(GMLP_GROUPS, GMLP_CHUNK, 1))


def _wout_kernel(mode, y_ref, w_ref, x_ref, gate_ref, *refs):
    if len(y_ref.shape) == 3:
        y = jnp.concatenate([y_ref[s] for s in range(y_ref.shape[0])], axis=1)
    else:
        y = y_ref[...]
    acc = jnp.dot(y, w_ref[...], preferred_element_type=F32)
    xn = x_ref[...] + gate_ref[...] * acc
    if mode == "final":
        g_ref, o_ref = refs
        o_ref[...] = xn * lax.rsqrt(jnp.mean(xn * xn, axis=-1, keepdims=True) + EPS) * g_ref[...]
    elif mode == "next":
        g_ref, sc_ref, sh_ref, xo_ref, h_ref = refs
        xo_ref[...] = xn
        h_ref[...] = _mod_rmsnorm(xn, g_ref[...], sc_ref[...], sh_ref[...]).astype(BF16)
    else:
        refs[0][...] = xn


def _out_proj(y, w_out, x2, mods, layer, who, mode, norm_vec, tm):
    m = x2.shape[0]
    row = pl.BlockSpec((tm, D_MODEL), lambda r: (r, 0))
    if y.ndim == 3:
        y_spec = pl.BlockSpec((SLABS, tm, LANES), lambda r: (0, r, 0))
    else:
        y_spec = pl.BlockSpec((tm, D_BRANCH), lambda r: (r, 0))
    in_specs = [y_spec,
                pl.BlockSpec((None, D_BRANCH, D_MODEL), lambda r: (layer, 0, 0),
                             pipeline_mode=pl.Buffered(1)),
                row, _mod_spec(layer, 2, who)]
    args = [y, w_out, x2, mods]
    xs = jax.ShapeDtypeStruct((m, D_MODEL), F32)
    if mode == "final":
        in_specs.append(pl.BlockSpec((1, D_MODEL), lambda r: (0, 0)))
        args.append(norm_vec.reshape(1, D_MODEL))
        out_shape, out_specs = xs, row
    elif mode == "next":
        in_specs += [_row_spec(layer + 1), _mod_spec(layer + 1, 1, who), _mod_spec(layer + 1, 0, who)]
        args += [norm_vec, mods, mods]
        out_shape = (xs, jax.ShapeDtypeStruct((m, D_MODEL), BF16))
        out_specs = (row, row)
    else:
        out_shape, out_specs = xs, row
    return pl.pallas_call(
        functools.partial(_wout_kernel, mode),
        grid=(m // tm,),
        in_specs=in_specs,
        out_specs=out_specs,
        out_shape=out_shape,
        compiler_params=_cparams(1),
        name="out_proj_" + mode,
    )(*args)


def kernel(x, c, ctx, c_ctx, norm_g, ada_w, ada_b, w_out, fnet_w_in, fnet_w_mix, attn_w_in, attn_sink,
           gmlp_w_in, gmlp_w_s, gmlp_b_s, gmlp_ln_g, gmlp_ln_b, final_g):
    batch, seq, d = x.shape
    ctx_len = ctx.shape[1]
    assert d == D_MODEL and seq % (DFT_NB * 8) == 0 and seq % GRID_W == 0 and batch < 8
    m_lat, m_ctx = batch * seq, batch * ctx_len
    tm_lat = min(2048, seq)
    tm_ctx = ctx_len
    tm_out = 256

    def who_lat(tm):
        return lambda r: (r * tm) // seq

    who_ctx = lambda r: batch

    cvec = jnp.zeros((8, d), F32).at[:batch].set(c).at[batch].set(c_ctx)
    mods = _mods(cvec, ada_w, ada_b).reshape(DEPTH, 8, 3, 1, d)
    norm_g3 = norm_g.reshape(DEPTH, 1, d)
    w_out_b = w_out.astype(BF16)

    xl = x.reshape(m_lat, d)
    xc = ctx.reshape(m_ctx, d)
    hl = _prenorm(xl, norm_g3, mods, 0, who_lat(512), 512)
    hc = _prenorm(xc, norm_g3, mods, 0, who_ctx, tm_ctx)


    def finish(y, x2, layer, who_fn, need_next, is_final):
        if is_final:
            return _out_proj(y, w_out_b, x2, mods, layer, who_fn, "final", final_g, tm_out), None
        if need_next:
            return _out_proj(y, w_out_b, x2, mods, layer, who_fn, "next", norm_g3, tm_out)
        return _out_proj(y, w_out_b, x2, mods, layer, who_fn, "plain", None, tm_out), None

    out = None
    for i in range(DEPTH):
        kind, j = i % 3, i // 3
        need_ctx = i < DEPTH - 1
        last = i == DEPTH - 1
        if kind == 0:
            cs = _fold_mix(fnet_w_mix[j])
            tm_dft = (seq // DFT_NB) * DFT_ROWS
            w_u = fnet_w_in[j, :, :D_BRANCH].astype(BF16)
            ar, ai = _proj_channel_dft(hl, w_u, cs, "fnet_in_u", tm_dft, seq=seq)
            zg = _proj_silu_slabs(hl, fnet_w_in, D_BRANCH, "fnet_in_z", tm_lat, w_layer=j)
            y = _position_dft(ar, ai, zg, batch, seq)
            yc = None
            if need_ctx:
                ar, ai = _proj_channel_dft(hc, w_u, cs, "fnet_in_u_ctx", tm_ctx)
                zg = _proj_silu_slabs(hc, fnet_w_in, D_BRANCH, "fnet_in_z_ctx", tm_ctx, w_layer=j)
                yc = _position_dft_dense(ar, ai, zg, batch, ctx_len)
        elif kind == 1:
            kvw = KV_HEADS * HEAD_DIM
            k0, v0, z0 = D_BRANCH, D_BRANCH + kvw, D_BRANCH + 2 * kvw
            sink = attn_sink[j].reshape(KV_HEADS, Q_GROUP)
            scale = HEAD_DIM ** -0.5 * LOG2_E
            q_tabs = [jnp.asarray(t) for t in _rope_tables(seq, scale)]
            k_tabs = [jnp.asarray(t) for t in _rope_tables(seq)]
            q = _proj_rope(hl, attn_w_in, 0, D_BRANCH, *q_tabs, seq, "attn_in_q", tm_lat, w_layer=j)
            kd = _proj_rope(hl, attn_w_in, k0, kvw, *k_tabs, seq, "attn_in_k", tm_lat, w_layer=j,
                            dup_heads=True)
            vd = _proj_simple(hl, attn_w_in, v0, kvw, _epi_cast_dup, "attn_in_v", tm_lat, w_layer=j, out_mult=2)
            zg = _proj_simple(hl, attn_w_in, z0, D_BRANCH, _epi_silu, "attn_in_z", tm_lat, w_layer=j)
            kcd = _proj_simple(hc, attn_w_in, k0, kvw, _epi_cast_dup, "attn_in_kc", tm_ctx, w_layer=j, out_mult=2)
            vcd = _proj_simple(hc, attn_w_in, v0, kvw, _epi_cast_dup, "attn_in_vc", tm_ctx, w_layer=j, out_mult=2)
            y = _attention(q, kd, vd, kcd, vcd, zg, sink, batch, seq, ctx_len, True)
            yc = None
            if need_ctx:
                qc = _proj_simple(hc, attn_w_in, 0, D_BRANCH, functools.partial(_epi_scale_cast, scale),
                                  "attn_in_qc", tm_ctx, w_layer=j)
                zgc = _proj_simple(hc, attn_w_in, z0, D_BRANCH, _epi_silu, "attn_in_zc", tm_ctx, w_layer=j)
                yc = _attention(qc, None, None, kcd, vcd, zgc, sink, batch, ctx_len, ctx_len, False)
        else:
            w_v = gmlp_w_in[j, :, D_BRANCH:2 * D_BRANCH].astype(BF16)
            ws = gmlp_w_s[j].astype(BF16)

            def gmlp_branch(h, tm):
                uz = _proj_simple(h, gmlp_w_in, (0, 2 * D_BRANCH), D_BRANCH, _epi_gelu_times_silu,
                                  "gmlp_in_uz", tm, tn=PROJ_TN // 2, w_layer=j)
                gv, mu, rstd = _proj_gelu_stats(h, w_v, 0, min(tm, 256))
                return _spatial_gate(uz, gv, mu, rstd, gmlp_ln_g[j], gmlp_ln_b[j], ws, gmlp_b_s[j],
                                     min(tm, 1024))

            y = gmlp_branch(hl, tm_lat)
            yc = gmlp_branch(hc, tm_ctx) if need_ctx else None

        res, hl = finish(y, xl, i, who_lat(tm_out), not last, last)
        if last:
            out = res
        else:
            xl = res
        if need_ctx:
            xc, hc = finish(yc, xc, i, who_ctx, i + 1 < DEPTH - 1 or (i + 1) % 3 == 1, False)
    return out.reshape(batch, seq, d)
```

```python
import functools
import math

import numpy as np
import jax
import jax.numpy as jnp
from jax import lax
from jax.experimental import pallas as pl
from jax.experimental.pallas import tpu as pltpu

F32 = jnp.float32
BF16 = jnp.bfloat16

D_MODEL = 2048
D_BRANCH = 4096
DEPTH = 4
GRID_W = 64
FNET_GROUPS = 16
FNET_GROUP_DIM = 256
HEAD_DIM = 64
KV_HEADS = 8
Q_GROUP = 8
ATTN_BLOCK = 128
ATTN_HEADS_PER_STEP = 8
ROPE_BASE = 10000.0
GMLP_CHUNK = 128
GMLP_GROUPS = 16
EPS = 1e-6
NEG_INF = -1e30
LOG2_E = math.log2(math.e)

LANES = 128
MXU_COLS = 256
PROJ_TN = 1024
DFT_NB = 128
DFT_ROWS = 16
SLABS = D_BRANCH // LANES
VMEM_LIMIT = 56 * 1024 * 1024


def _cparams(n_axes, vmem=VMEM_LIMIT):
    return pltpu.CompilerParams(dimension_semantics=("arbitrary",) * n_axes,
                                vmem_limit_bytes=vmem)


def _silu(z):
    return 0.5 * z * (1.0 + jnp.tanh(0.5 * z))


def _gelu_tanh(x):
    c = math.sqrt(2.0 / math.pi)
    return 0.5 * x * (1.0 + jnp.tanh(c * (x + 0.044715 * (x * x * x))))


def _mod_rmsnorm(x, g, scale, shift):
    y = x * lax.rsqrt(jnp.mean(x * x, axis=-1, keepdims=True) + EPS) * g
    return y * (1.0 + scale) + shift


def _channel_dft_matrix():
    n = FNET_GROUP_DIM
    k = np.arange(n, dtype=np.float64)
    ang = 2.0 * np.pi * np.outer(k, k) / n
    s = 1.0 / math.sqrt(n)
    return np.concatenate([np.cos(ang) * s, -np.sin(ang) * s], axis=1).astype(np.float32)


def _position_dft_matrices(seq):
    na, nb = seq // DFT_NB, DFT_NB
    a = np.arange(na, dtype=np.float64)
    b = np.arange(nb, dtype=np.float64)
    ang = 2.0 * np.pi * (a[None, None, :] * a[None, :, None] / na + b[:, None, None] * a[None, :, None] / seq)
    mr = np.cos(ang) / math.sqrt(na)
    mi = -np.sin(ang) / math.sqrt(na)
    fa = np.concatenate([np.concatenate([mr, -mi], axis=2), np.concatenate([mi, mr], axis=2)], axis=1)
    fa = np.concatenate([fa[0::2], fa[1::2]], axis=2)
    angb = 2.0 * np.pi * np.outer(b, b) / nb
    fb = np.concatenate([np.cos(angb), np.sin(angb)], axis=1) / math.sqrt(nb)
    return fa.astype(np.float32), fb.astype(np.float32)


def _dense_dft_matrix(n):
    k = np.arange(n, dtype=np.float64)
    ang = 2.0 * np.pi * np.outer(k, k) / n
    return (np.concatenate([np.cos(ang), np.sin(ang)], axis=1) / math.sqrt(n)).astype(np.float32)


def _rope_tables(seq, scale=1.0):
    nf = HEAD_DIM // 4
    inv = ROPE_BASE ** (-np.arange(nf, dtype=np.float64) / nf)
    t = np.arange(seq)
    rows = (t // GRID_W).astype(np.float64)
    cols = (t % GRID_W).astype(np.float64)
    parts_c, parts_s = [], []
    for pos in (rows, cols):
        ang = pos[:, None] * inv[None, :]
        parts_c += [np.cos(ang), np.cos(ang)]
        parts_s += [-np.sin(ang), np.sin(ang)]
    cos = np.concatenate(parts_c, axis=1) * scale
    sin = np.concatenate(parts_s, axis=1) * scale
    reps = LANES // HEAD_DIM
    return (np.tile(cos, (1, reps)).astype(np.float32), np.tile(sin, (1, reps)).astype(np.float32))


def _mods_kernel(cv_ref, w_ref, b_ref, o_ref):
    a = _silu(cv_ref[...])
    a_hi = a.astype(BF16)
    a_lo = (a - a_hi.astype(F32)).astype(BF16)
    w = w_ref[...].astype(BF16)
    o_ref[...] = (jnp.dot(a_hi, w, preferred_element_type=F32)
                  + jnp.dot(a_lo, w, preferred_element_type=F32)) + b_ref[...]


def _mods(cvec, ada_w, ada_b):
    depth, d, n3 = ada_w.shape
    tn = 2048
    return pl.pallas_call(
        _mods_kernel,
        grid=(depth, n3 // tn),
        in_specs=[pl.BlockSpec((8, d), lambda i, j: (0, 0)),
                  pl.BlockSpec((None, d, tn), lambda i, j: (i, 0, j)),
                  pl.BlockSpec((None, 1, tn), lambda i, j: (i, 0, j))],
        out_specs=pl.BlockSpec((None, 8, tn), lambda i, j: (i, 0, j)),
        out_shape=jax.ShapeDtypeStruct((depth, 8, n3), F32),
        compiler_params=_cparams(2),
        name="ada_mods",
    )(cvec, ada_w, ada_b.reshape(depth, 1, n3))


def _mod_spec(layer, kind, who_of_row):
    return pl.BlockSpec((None, None, None, 1, D_MODEL),
                        lambda r, *_: (layer, who_of_row(r), kind, 0, 0))


def _row_spec(vec_layer):
    return pl.BlockSpec((None, 1, D_MODEL), lambda r, *_: (vec_layer, 0, 0))


def _prenorm_kernel(x_ref, g_ref, sc_ref, sh_ref, h_ref):
    h_ref[...] = _mod_rmsnorm(x_ref[...], g_ref[...], sc_ref[...], sh_ref[...]).astype(BF16)


def _prenorm(x2, norm_g3, mods, layer, who, tm):
    m = x2.shape[0]
    return pl.pallas_call(
        _prenorm_kernel,
        grid=(m // tm,),
        in_specs=[pl.BlockSpec((tm, D_MODEL), lambda r: (r, 0)),
                  _row_spec(layer), _mod_spec(layer, 1, who), _mod_spec(layer, 0, who)],
        out_specs=pl.BlockSpec((tm, D_MODEL), lambda r: (r, 0)),
        out_shape=jax.ShapeDtypeStruct((m, D_MODEL), BF16),
        compiler_params=_cparams(1),
        name="prenorm",
    )(x2, norm_g3, mods, mods)


def _proj_kernel(epilogue, n_extra, chunk, n_w, a_ref, *rest):
    w_refs, rest = rest[:n_w], rest[n_w:]
    if len(a_ref.shape) == 3:
        a_flat = rest[-1]
        rest = rest[:-1]

        @pl.when(pl.program_id(1) == 0)
        def _():
            a_flat[...] = a_ref[...].reshape(a_flat.shape)

        a = a_flat[...]
    else:
        a = a_ref[...]
    for c0 in range(0, w_refs[0].shape[1], chunk):
        accs = [jnp.dot(a, w_ref[:, c0:c0 + chunk].astype(BF16), preferred_element_type=F32)
                for w_ref in w_refs]
        epilogue(accs[0] if n_w == 1 else accs, c0, rest[:n_extra], rest[n_extra:])


def _proj(h, w, col0, ncols, tm, tn, epilogue, extras, extra_specs, out_shapes, out_specs, name,
          lhs_spec=None, chunk=MXU_COLS, w_layer=None):
    k = h.shape[-1]
    m = h.size // k
    col0s = col0 if isinstance(col0, tuple) else (col0,)
    scratch = [pltpu.VMEM((tm, k), h.dtype)] if lhs_spec is not None else []
    if lhs_spec is None:
        lhs_spec = pl.BlockSpec((tm, k), lambda i, j: (i, 0))

    def w_spec(off):
        if w.ndim == 3:
            return pl.BlockSpec((None, k, tn), lambda i, j: (w_layer, 0, j + off))
        return pl.BlockSpec((k, tn), lambda i, j: (0, j + off))

    return pl.pallas_call(
        functools.partial(_proj_kernel, epilogue, len(extras), min(chunk, tn), len(col0s)),
        grid=(m // tm, ncols // tn),
        in_specs=[lhs_spec] + [w_spec(c // tn) for c in col0s] + list(extra_specs),
        out_specs=out_specs,
        out_shape=out_shapes,
        scratch_shapes=scratch,
        compiler_params=_cparams(2),
        name=name,
    )(h, *([w] * len(col0s)), *extras)


def _epi_silu(acc, c0, extras, outs):
    outs[0][:, c0:c0 + acc.shape[1]] = _silu(acc).astype(BF16)


def _epi_silu_slabs(acc, c0, extras, outs):
    for t in range(acc.shape[1] // LANES):
        outs[0][c0 // LANES + t] = _silu(acc[:, t * LANES:(t + 1) * LANES]).astype(BF16)


def _epi_gelu_times_silu(accs, c0, extras, outs):
    acc_u, acc_z = accs
    outs[0][:, c0:c0 + acc_u.shape[1]] = (_gelu_tanh(acc_u) * _silu(acc_z)).astype(BF16)


def _epi_cast(acc, c0, extras, outs):
    outs[0][:, c0:c0 + acc.shape[1]] = acc.astype(BF16)


def _epi_scale_cast(scale, acc, c0, extras, outs):
    outs[0][:, c0:c0 + acc.shape[1]] = (acc * scale).astype(BF16)


def _epi_channel_dft(split_rows, acc, c0, extras, outs):
    cs_ref = extras[0]
    ar_ref, ai_ref = outs
    tm = acc.shape[0]
    for gl in range(acc.shape[1] // FNET_GROUP_DIM):
        g = c0 // FNET_GROUP_DIM + gl
        ub = acc[:, gl * FNET_GROUP_DIM:(gl + 1) * FNET_GROUP_DIM].astype(BF16)
        ab = jnp.dot(ub, cs_ref[g], preferred_element_type=F32)
        if split_rows:
            ab4 = ab.reshape(tm // DFT_ROWS, 2, DFT_ROWS // 2, ab.shape[1])
            ab = jnp.concatenate([ab4[:, hf].reshape(tm // 2, ab.shape[1]) for hf in range(2)], axis=0)
        ab = ab.astype(BF16)
        ar_ref[2 * g] = ab[:, 0:128]
        ar_ref[2 * g + 1] = ab[:, 128:256]
        ai_ref[2 * g] = ab[:, 256:384]
        ai_ref[2 * g + 1] = ab[:, 384:512]


def _store_tile(out_ref, col, y, dup_heads):
    if not dup_heads:
        out_ref[:, col:col + LANES] = y.astype(BF16)
        return
    lane = lax.broadcasted_iota(jnp.int32, (1, LANES), 1)
    lo = lane < HEAD_DIM
    swapped = pltpu.roll(y, HEAD_DIM, 1)
    out_ref[:, 2 * col:2 * col + LANES] = jnp.where(lo, y, swapped).astype(BF16)
    out_ref[:, 2 * col + LANES:2 * col + 2 * LANES] = jnp.where(lo, swapped, y).astype(BF16)


def _epi_rope(dup_heads, acc, c0, extras, outs):
    cos = extras[0][...]
    sin = extras[1][...]
    lane = lax.broadcasted_iota(jnp.int32, (1, LANES), 1)
    first = (lane % 32) < 16
    for t in range(acc.shape[1] // LANES):
        x = acc[:, t * LANES:(t + 1) * LANES]
        partner = jnp.where(first, pltpu.roll(x, LANES - 16, 1), pltpu.roll(x, 16, 1))
        _store_tile(outs[0], c0 + t * LANES, x * cos + partner * sin, dup_heads)


def _epi_cast_dup(acc, c0, extras, outs):
    for t in range(acc.shape[1] // LANES):
        _store_tile(outs[0], c0 + t * LANES, acc[:, t * LANES:(t + 1) * LANES], True)


def _proj_simple(h, w, col0, ncols, epilogue, name, tm, tn=PROJ_TN, w_layer=None, out_mult=1):
    m = h.shape[0]
    tn = min(tn, ncols)
    return _proj(h, w, col0, ncols, tm, tn, epilogue, (), (),
                 jax.ShapeDtypeStruct((m, out_mult * ncols), BF16),
                 pl.BlockSpec((tm, out_mult * tn), lambda i, j: (i, j)), name, w_layer=w_layer)


def _proj_rope(h, w, col0, ncols, cos_t, sin_t, seq, name, tm, tn=PROJ_TN, w_layer=None,
               dup_heads=False):
    m = h.shape[0]
    tn = min(tn, ncols)
    per_batch = seq // tm
    out_mult = 2 if dup_heads else 1
    tab_spec = pl.BlockSpec((tm, LANES), lambda i, j: (i % per_batch, 0))
    return _proj(h, w, col0, ncols, tm, tn, functools.partial(_epi_rope, dup_heads),
                 (cos_t, sin_t), (tab_spec, tab_spec),
                 jax.ShapeDtypeStruct((m, out_mult * ncols), BF16),
                 pl.BlockSpec((tm, out_mult * tn), lambda i, j: (i, j)), name, w_layer=w_layer)


def _proj_silu_slabs(h, w, col0, name, tm, tn=PROJ_TN, w_layer=None):
    m = h.shape[0]
    return _proj(h, w, col0, D_BRANCH, tm, tn, _epi_silu_slabs, (), (),
                 jax.ShapeDtypeStruct((SLABS, m, LANES), BF16),
                 pl.BlockSpec((tn // LANES, tm, LANES), lambda i, j: (j, i, 0)), name, w_layer=w_layer)


def _proj_channel_dft(h, w, cs, name, tm, seq=None, tn=PROJ_TN):
    m = h.shape[0]
    lhs_spec = None
    if seq is not None:
        na, tiles = seq // DFT_NB, DFT_NB // DFT_ROWS
        assert tm == na * DFT_ROWS
        h = h.reshape(m // seq, na, DFT_NB, D_MODEL)
        lhs_spec = pl.BlockSpec((None, na, DFT_ROWS, D_MODEL), lambda i, j: (i // tiles, 0, i % tiles, 0))
    slab_shape = jax.ShapeDtypeStruct((SLABS, m, LANES), BF16)
    slab_spec = pl.BlockSpec((tn // LANES, tm, LANES), lambda i, j: (j, i, 0))
    groups = tn // FNET_GROUP_DIM
    cs_spec = pl.BlockSpec((groups,) + cs.shape[1:], lambda i, j: (j, 0, 0))
    return _proj(h, w, 0, D_BRANCH, tm, tn, functools.partial(_epi_channel_dft, seq is not None), (cs,), (cs_spec,),
                 (slab_shape, slab_shape), (slab_spec, slab_spec), name, lhs_spec=lhs_spec,
                 chunk=2 * MXU_COLS)


def _fold_mix_kernel(c_ref, s_ref, wm_ref, o_ref):
    wm = wm_ref[...]
    gd = FNET_GROUP_DIM
    hp = lax.Precision.HIGHEST
    o_ref[:, 0:gd] = jnp.dot(c_ref[...], wm, preferred_element_type=F32, precision=hp).astype(BF16)
    o_ref[:, gd:2 * gd] = jnp.dot(s_ref[...], wm, preferred_element_type=F32, precision=hp).astype(BF16)


def _fold_mix(w_mix):
    gd = FNET_GROUP_DIM
    cs = _channel_dft_matrix()
    mat = pl.BlockSpec((gd, gd), lambda g: (0, 0))
    return pl.pallas_call(
        _fold_mix_kernel,
        grid=(FNET_GROUPS,),
        in_specs=[mat, mat, pl.BlockSpec((None, gd, gd), lambda g: (g, 0, 0))],
        out_specs=pl.BlockSpec((None, gd, 2 * gd), lambda g: (g, 0, 0)),
        out_shape=jax.ShapeDtypeStruct((FNET_GROUPS, gd, 2 * gd), BF16),
        compiler_params=_cparams(1),
        name="fnet_fold_mix",
    )(jnp.asarray(cs[:, :gd]), jnp.asarray(cs[:, gd:]), w_mix)


def _dft_kernel(na, pitch, gpitch, ar_ref, ai_ref, zg_ref, fa_ref, fb_ref, y_ref,
                xr_ref, xi_ref, er_ref, ei_ref, g_ref):
    half_rows = na * (DFT_ROWS // 2)
    xr_ref[...] = ar_ref[...].astype(F32)
    xi_ref[...] = ai_ref[...].astype(F32)

    def gather_a(b):
        start = (b // (DFT_ROWS // 2)) * half_rows + b % (DFT_ROWS // 2)
        zr = xr_ref[pl.ds(start, na, stride=DFT_ROWS // 2), :]
        zi = xi_ref[pl.ds(start, na, stride=DFT_ROWS // 2), :]
        return jnp.concatenate([zr, zi], axis=0).astype(BF16)

    def stage_a(pair, carry):
        b0 = 2 * pair
        d0, d1 = gather_a(b0), gather_a(b0 + 1)
        zero = jnp.zeros_like(d0)
        rhs = jnp.concatenate([jnp.concatenate([d0, zero], axis=1),
                               jnp.concatenate([zero, d1], axis=1)], axis=0)
        e = jnp.dot(fa_ref[pair], rhs, preferred_element_type=F32)
        for j in range(2):
            off = pl.multiple_of((b0 + j) * pitch, 8)
            er_ref[pl.ds(off, na), :] = e[:na, j * LANES:(j + 1) * LANES]
            ei_ref[pl.ds(off, na), :] = e[na:, j * LANES:(j + 1) * LANES]
        return carry

    lax.fori_loop(0, DFT_NB // 2, stage_a, 0, unroll=64)

    def gather_b(ka):
        er = er_ref[pl.ds(ka, DFT_NB, stride=pitch), :]
        ei = ei_ref[pl.ds(ka, DFT_NB, stride=pitch), :]
        return jnp.concatenate([er, ei], axis=0).astype(BF16)

    def stage_b(pair, carry):
        ka0 = 2 * pair
        rhs = jnp.concatenate([gather_b(ka0), gather_b(ka0 + 1)], axis=1)
        g = jnp.dot(fb_ref[...], rhs, preferred_element_type=F32)
        for j in range(2):
            off = pl.multiple_of((ka0 + j) * gpitch, 8)
            g_ref[pl.ds(off, DFT_NB), :] = g[:, j * LANES:(j + 1) * LANES]
        return carry

    lax.fori_loop(0, na // 2, stage_b, 0, unroll=32)

    def gate(kb, carry):
        rows = pl.ds(pl.multiple_of(kb * na, na), na)
        g = g_ref[pl.ds(kb, na, stride=gpitch), :]
        y_ref[rows, :] = (g * zg_ref[rows, :].astype(F32)).astype(BF16)
        return carry

    lax.fori_loop(0, DFT_NB, gate, 0, unroll=32)


def _position_dft(ar, ai, zg, batch, seq):
    na = seq // DFT_NB
    pitch = na + 8
    gpitch = DFT_NB + 8
    fa_np, fb_np = _position_dft_matrices(seq)
    fa = jnp.asarray(fa_np).astype(BF16)
    fb = jnp.asarray(fb_np).astype(BF16)
    slab = pl.BlockSpec((None, seq, LANES), lambda s, b: (s, b, 0))
    return pl.pallas_call(
        functools.partial(_dft_kernel, na, pitch, gpitch),
        grid=(SLABS, batch),
        in_specs=[slab, slab, slab,
                  pl.BlockSpec(fa.shape, lambda s, b: (0, 0, 0)),
                  pl.BlockSpec(fb.shape, lambda s, b: (0, 0))],
        out_specs=slab,
        out_shape=jax.ShapeDtypeStruct(ar.shape, BF16),
        scratch_shapes=[pltpu.VMEM((seq, LANES), F32),
                        pltpu.VMEM((seq, LANES), F32),
                        pltpu.VMEM((DFT_NB * pitch, LANES), F32),
                        pltpu.VMEM((DFT_NB * pitch, LANES), F32),
                        pltpu.VMEM((na * gpitch, LANES), F32)],
        compiler_params=_cparams(2),
        name="position_dft",
    )(ar, ai, zg, fa, fb)


def _dft_dense_kernel(ar_ref, ai_ref, zg_ref, fd_ref, y_ref):
    d = jnp.concatenate([ar_ref[...], ai_ref[...]], axis=0).astype(BF16)
    g = jnp.dot(fd_ref[...], d, preferred_element_type=F32)
    y_ref[...] = (g * zg_ref[...].astype(F32)).astype(BF16)


def _position_dft_dense(ar, ai, zg, batch, seq):
    fd = jnp.asarray(_dense_dft_matrix(seq)).astype(BF16)
    slab = pl.BlockSpec((None, seq, LANES), lambda s, b: (s, b, 0))
    return pl.pallas_call(
        _dft_dense_kernel,
        grid=(SLABS, batch),
        in_specs=[slab, slab, slab, pl.BlockSpec(fd.shape, lambda s, b: (0, 0))],
        out_specs=slab,
        out_shape=jax.ShapeDtypeStruct(ar.shape, BF16),
        compiler_params=_cparams(2),
        name="position_dft_dense",
    )(ar, ai, zg, fd)


def _attn_kernel(n_band, nblk, sink_ref, q_ref, *refs):
    k_refs = refs[:n_band + 1]
    v_refs = refs[n_band + 1:2 * n_band + 2]
    zg_ref, y_ref = refs[2 * n_band + 2:]
    i = pl.program_id(2)
    blk = ATTN_BLOCK
    n_pair = Q_GROUP // 2
    qw = Q_GROUP * HEAD_DIM

    lane = lax.broadcasted_iota(jnp.int32, (1, LANES), 1)
    lo = lane < HEAD_DIM
    dn = (((1,), (1,)), ((), ()))
    if n_band:
        r = lax.broadcasted_iota(jnp.int32, (blk, blk), 0)
        c = lax.broadcasted_iota(jnp.int32, (blk, blk), 1)
        prev_ok = c >= r + jnp.where(i > 0, 0, blk)
        next_ok = c <= r - jnp.where(i < nblk - 1, 0, blk)

    def scores(hh):
        kl = slice(hh * LANES, (hh + 1) * LANES)
        keys = jnp.concatenate([ref[:, kl] for ref in k_refs], axis=0)
        q4 = jnp.concatenate([q_ref[:, hh * qw + t * LANES:hh * qw + (t + 1) * LANES]
                              for t in range(n_pair)], axis=0)
        qzero = jnp.zeros_like(q4)
        return [lax.dot_general(qm, keys, dn, preferred_element_type=F32)
                for qm in (jnp.where(lo, q4, qzero), jnp.where(lo, qzero, q4))]

    def finish(hh, s_both):
        h = pl.program_id(1) * ATTN_HEADS_PER_STEP + hh
        kl = slice(hh * LANES, (hh + 1) * LANES)
        vals = jnp.concatenate([ref[:, kl] for ref in v_refs], axis=0)
        ones = jnp.ones_like(vals)
        v_ext = (jnp.where(lo, vals, ones), jnp.where(lo, ones, vals))
        o_ext, sink_term = [], []
        for hd, s in enumerate(s_both):
            p_rows, sink_rows = [], []
            for t in range(n_pair):
                st = s[t * blk:(t + 1) * blk]
                parts = [st[:, j * blk:(j + 1) * blk] for j in range(st.shape[1] // blk)]
                if n_band:
                    parts[0] = jnp.where(prev_ok, parts[0], NEG_INF)
                    parts[2] = jnp.where(next_ok, parts[2], NEG_INF)
                sk = sink_ref[h, 2 * t + hd] * LOG2_E
                mx = parts[0]
                for part in parts[1:]:
                    mx = jnp.maximum(mx, part)
                mx = jnp.maximum(jnp.max(mx, axis=-1, keepdims=True), sk)
                p_rows.append(jnp.concatenate([jnp.exp2(part - mx).astype(BF16) for part in parts], axis=1))
                sink_rows.append(jnp.exp2(sk - mx))
            p = jnp.concatenate(p_rows, axis=0)
            o_ext.append(jnp.dot(p, v_ext[hd], preferred_element_type=F32))
            sink_term.append(jnp.concatenate(sink_rows, axis=0))
        o = jnp.where(lo, o_ext[0], o_ext[1])
        denom = jnp.where(lo, pltpu.roll(o_ext[0], HEAD_DIM, 1) + sink_term[0],
                          pltpu.roll(o_ext[1], HEAD_DIM, 1) + sink_term[1])
        res = o / denom
        for t in range(n_pair):
            tile = slice(hh * qw + t * LANES, hh * qw + (t + 1) * LANES)
            y_ref[:, tile] = (res[t * blk:(t + 1) * blk] * zg_ref[:, tile].astype(F32)).astype(BF16)

    pending = scores(0)
    for hh in range(ATTN_HEADS_PER_STEP):
        upcoming = scores(hh + 1) if hh + 1 < ATTN_HEADS_PER_STEP else None
        finish(hh, pending)
        pending = upcoming


def _attention(q, kd, vd, kcd, vcd, zg, sink, batch, seq, ctx_len, use_band):
    blk = ATTN_BLOCK
    nblk = seq // blk
    hps = ATTN_HEADS_PER_STEP
    qw = hps * Q_GROUP * HEAD_DIM
    kw = hps * LANES
    q_spec = pl.BlockSpec((blk, qw), lambda b, h, i: (b * nblk + i, h))
    ctx_spec = pl.BlockSpec((ctx_len, kw), lambda b, h, i: (b, h))
    if use_band:
        def band(delta):
            return pl.BlockSpec(
                (blk, kw), lambda b, h, i: (b * nblk + jnp.clip(i + delta, 0, nblk - 1), h))
        k_specs = [band(-1), band(0), band(1), ctx_spec]
        k_args, v_args = [kd, kd, kd, kcd], [vd, vd, vd, vcd]
        n_band = 3
    else:
        k_specs, k_args, v_args, n_band = [ctx_spec], [kcd], [vcd], 0
    return pl.pallas_call(
        functools.partial(_attn_kernel, n_band, nblk),
        grid=(batch, KV_HEADS // hps, nblk),
        in_specs=[pl.BlockSpec(memory_space=pltpu.SMEM), q_spec] + k_specs + k_specs + [q_spec],
        out_specs=q_spec,
        out_shape=jax.ShapeDtypeStruct(q.shape, BF16),
        compiler_params=_cparams(3),
        name="attention_band" if use_band else "attention_ctx",
    )(sink, q, *k_args, *v_args, zg)


def _gelu_stats_kernel(chunk, a_ref, w_ref, gv_ref, mu_ref, rstd_ref):
    a = a_ref[...]
    n = w_ref.shape[1]
    mean = m2 = None
    for idx, c0 in enumerate(range(0, n, chunk)):
        ge = _gelu_tanh(jnp.dot(a, w_ref[:, c0:c0 + chunk], preferred_element_type=F32))
        gv_ref[:, c0:c0 + chunk] = ge.astype(BF16)
        cmean = jnp.mean(ge, axis=-1, keepdims=True)
        d = ge - cmean
        cm2 = jnp.sum(d * d, axis=-1, keepdims=True)
        if idx == 0:
            mean, m2 = cmean, cm2
        else:
            delta = cmean - mean
            mean = mean + delta * (1.0 / (idx + 1))
            m2 = m2 + cm2 + delta * delta * (chunk * idx / (idx + 1))
    mu_ref[...] = mean
    rstd_ref[...] = lax.rsqrt(m2 * (1.0 / n) + EPS)


def _proj_gelu_stats(h, w, col0, tm, chunk=2 * MXU_COLS):
    m, k = h.shape
    n = D_BRANCH
    col = pl.BlockSpec((tm, 1), lambda i: (i, 0))
    stat = jax.ShapeDtypeStruct((m, 1), F32)
    return pl.pallas_call(
        functools.partial(_gelu_stats_kernel, chunk),
        grid=(m // tm,),
        in_specs=[pl.BlockSpec((tm, k), lambda i: (i, 0)),
                  pl.BlockSpec((k, n), lambda i: (0, col0 // n), pipeline_mode=pl.Buffered(1))],
        out_specs=(pl.BlockSpec((tm, n), lambda i: (i, 0)), col, col),
        out_shape=(jax.ShapeDtypeStruct((m, n), BF16), stat, stat),
        compiler_params=_cparams(1),
        name="gmlp_in_v",
    )(h, w)


def _sgu_kernel(n_chunks, n_groups, uz_ref, gv_ref, mu_ref, rstd_ref, lg_ref, lb_ref,
                ws_ref, bs_ref, y_ref):
    gd = D_BRANCH // GMLP_GROUPS
    mu = mu_ref[...]
    rstd = rstd_ref[...]
    for gl in range(n_groups):
        cols = slice(gl * gd, (gl + 1) * gd)
        ws = ws_ref[gl]
        bs = bs_ref[gl]
        lg = lg_ref[:, cols]
        lb = lb_ref[:, cols]
        for c in range(n_chunks):
            rows = slice(c * GMLP_CHUNK, (c + 1) * GMLP_CHUNK)
            vn = ((gv_ref[rows, cols].astype(F32) - mu[rows]) * rstd[rows] * lg + lb).astype(BF16)
            s = jnp.dot(ws, vn, preferred_element_type=F32) + bs
            y_ref[rows, cols] = (uz_ref[rows, cols].astype(F32) * s).astype(BF16)


def _spatial_gate(uz, gv, mu, rstd, ln_g, ln_b, w_s, b_s, tm, n_groups=4):
    m = uz.shape[0]
    gd = D_BRANCH // GMLP_GROUPS
    tn = n_groups * gd
    tile = pl.BlockSpec((tm, tn), lambda r, g: (r, g))
    col = pl.BlockSpec((tm, 1), lambda r, g: (r, 0))
    vec = pl.BlockSpec((1, tn), lambda r, g: (0, g))
    return pl.pallas_call(
        functools.partial(_sgu_kernel, tm // GMLP_CHUNK, n_groups),
        grid=(m // tm, GMLP_GROUPS // n_groups),
        in_specs=[tile, tile, col, col, vec, vec,
                  pl.BlockSpec((n_groups, GMLP_CHUNK, GMLP_CHUNK), lambda r, g: (g, 0, 0)),
                  pl.BlockSpec((n_groups, GMLP_CHUNK, 1), lambda r, g: (g, 0, 0))],
        out_specs=tile,
        out_shape=jax.ShapeDtypeStruct(uz.shape, BF16),
        compiler_params=_cparams(2),
        name="gmlp_spatial_gate",
    )(uz, gv, mu, rstd, ln_g.reshape(1, D_BRANCH), ln_b.reshape(1, D_BRANCH),
      w_s, b_s.reshape(GMLP_GROUPS, GMLP_CHUNK, 1))


def _wout_kernel(mode, y_ref, w_ref, x_ref, gate_ref, *refs):
    if len(y_ref.shape) == 3:
        y = jnp.concatenate([y_ref[s] for s in range(y_ref.shape[0])], axis=1)
    else:
        y = y_ref[...]
    acc = jnp.dot(y, w_ref[...], preferred_element_type=F32)
    xn = x_ref[...] + gate_ref[...] * acc
    if mode == "final":
        g_ref, o_ref = refs
        o_ref[...] = xn * lax.rsqrt(jnp.mean(xn * xn, axis=-1, keepdims=True) + EPS) * g_ref[...]
    elif mode == "next":
        g_ref, sc_ref, sh_ref, xo_ref, h_ref = refs
        xo_ref[...] = xn
        h_ref[...] = _mod_rmsnorm(xn, g_ref[...], sc_ref[...], sh_ref[...]).astype(BF16)
    else:
        refs[0][...] = xn


def _out_proj(y, w_out, x2, mods, layer, who, mode, norm_vec, tm):
    m = x2.shape[0]
    row = pl.BlockSpec((tm, D_MODEL), lambda r: (r, 0))
    if y.ndim == 3:
        y_spec = pl.BlockSpec((SLABS, tm, LANES), lambda r: (0, r, 0))
    else:
        y_spec = pl.BlockSpec((tm, D_BRANCH), lambda r: (r, 0))
    in_specs = [y_spec,
                pl.BlockSpec((None, D_BRANCH, D_MODEL), lambda r: (layer, 0, 0),
                             pipeline_mode=pl.Buffered(1)),
                row, _mod_spec(layer, 2, who)]
    args = [y, w_out, x2, mods]
    xs = jax.ShapeDtypeStruct((m, D_MODEL), F32)
    if mode == "final":
        in_specs.append(pl.BlockSpec((1, D_MODEL), lambda r: (0, 0)))
        args.append(norm_vec.reshape(1, D_MODEL))
        out_shape, out_specs = xs, row
    elif mode == "next":
        in_specs += [_row_spec(layer + 1), _mod_spec(layer + 1, 1, who), _mod_spec(layer + 1, 0, who)]
        args += [norm_vec, mods, mods]
        out_shape = (xs, jax.ShapeDtypeStruct((m, D_MODEL), BF16))
        out_specs = (row, row)
    else:
        out_shape, out_specs = xs, row
    return pl.pallas_call(
        functools.partial(_wout_kernel, mode),
        grid=(m // tm,),
        in_specs=in_specs,
        out_specs=out_specs,
        out_shape=out_shape,
        compiler_params=_cparams(1),
        name="out_proj_" + mode,
    )(*args)


def kernel(x, c, ctx, c_ctx, norm_g, ada_w, ada_b, w_out, fnet_w_in, fnet_w_mix, attn_w_in, attn_sink,
           gmlp_w_in, gmlp_w_s, gmlp_b_s, gmlp_ln_g, gmlp_ln_b, final_g):
    batch, seq, d = x.shape
    ctx_len = ctx.shape[1]
    assert d == D_MODEL and seq % (DFT_NB * 8) == 0 and seq % GRID_W == 0 and batch < 8
    m_lat, m_ctx = batch * seq, batch * ctx_len
    tm_lat = min(2048, seq)
    tm_ctx = ctx_len
    tm_out = 256

    def who_lat(tm):
        return lambda r: (r * tm) // seq

    who_ctx = lambda r: batch

    cvec = jnp.zeros((8, d), F32).at[:batch].set(c).at[batch].set(c_ctx)
    mods = _mods(cvec, ada_w, ada_b).reshape(DEPTH, 8, 3, 1, d)
    norm_g3 = norm_g.reshape(DEPTH, 1, d)
    w_out_b = w_out.astype(BF16)

    xl = x.reshape(m_lat, d)
    xc = ctx.reshape(m_ctx, d)
    hl = _prenorm(xl, norm_g3, mods, 0, who_lat(512), 512)
    hc = _prenorm(xc, norm_g3, mods, 0, who_ctx, tm_ctx)


    def finish(y, x2, layer, who_fn, need_next, is_final):
        if is_final:
            return _out_proj(y, w_out_b, x2, mods, layer, who_fn, "final", final_g, tm_out), None
        if need_next:
            return _out_proj(y, w_out_b, x2, mods, layer, who_fn, "next", norm_g3, tm_out)
        return _out_proj(y, w_out_b, x2, mods, layer, who_fn, "plain", None, tm_out), None

    out = None
    for i in range(DEPTH):
        kind, j = i % 3, i // 3
        need_ctx = i < DEPTH - 1
        last = i == DEPTH - 1
        if kind == 0:
            cs = _fold_mix(fnet_w_mix[j])
            tm_dft = (seq // DFT_NB) * DFT_ROWS
            w_u = fnet_w_in[j, :, :D_BRANCH].astype(BF16)
            ar, ai = _proj_channel_dft(hl, w_u, cs, "fnet_in_u", tm_dft, seq=seq)
            zg = _proj_silu_slabs(hl, fnet_w_in, D_BRANCH, "fnet_in_z", tm_lat, w_layer=j)
            y = _position_dft(ar, ai, zg, batch, seq)
            yc = None
            if need_ctx:
                ar, ai = _proj_channel_dft(hc, w_u, cs, "fnet_in_u_ctx", tm_ctx)
                zg = _proj_silu_slabs(hc, fnet_w_in, D_BRANCH, "fnet_in_z_ctx", tm_ctx, w_layer=j)
                yc = _position_dft_dense(ar, ai, zg, batch, ctx_len)
        elif kind == 1:
            kvw = KV_HEADS * HEAD_DIM
            k0, v0, z0 = D_BRANCH, D_BRANCH + kvw, D_BRANCH + 2 * kvw
            sink = attn_sink[j].reshape(KV_HEADS, Q_GROUP)
            scale = HEAD_DIM ** -0.5 * LOG2_E
            q_tabs = [jnp.asarray(t) for t in _rope_tables(seq, scale)]
            k_tabs = [jnp.asarray(t) for t in _rope_tables(seq)]
            q = _proj_rope(hl, attn_w_in, 0, D_BRANCH, *q_tabs, seq, "attn_in_q", tm_lat, w_layer=j)
            kd = _proj_rope(hl, attn_w_in, k0, kvw, *k_tabs, seq, "attn_in_k", tm_lat, w_layer=j,
                            dup_heads=True)
            vd = _proj_simple(hl, attn_w_in, v0, kvw, _epi_cast_dup, "attn_in_v", tm_lat, w_layer=j, out_mult=2)
            zg = _proj_simple(hl, attn_w_in, z0, D_BRANCH, _epi_silu, "attn_in_z", tm_lat, w_layer=j)
            kcd = _proj_simple(hc, attn_w_in, k0, kvw, _epi_cast_dup, "attn_in_kc", tm_ctx, w_layer=j, out_mult=2)
            vcd = _proj_simple(hc, attn_w_in, v0, kvw, _epi_cast_dup, "attn_in_vc", tm_ctx, w_layer=j, out_mult=2)
            y = _attention(q, kd, vd, kcd, vcd, zg, sink, batch, seq, ctx_len, True)
            yc = None
            if need_ctx:
                qc = _proj_simple(hc, attn_w_in, 0, D_BRANCH, functools.partial(_epi_scale_cast, scale),
                                  "attn_in_qc", tm_ctx, w_layer=j)
                zgc = _proj_simple(hc, attn_w_in, z0, D_BRANCH, _epi_silu, "attn_in_zc", tm_ctx, w_layer=j)
                yc = _attention(qc, None, None, kcd, vcd, zgc, sink, batch, ctx_len, ctx_len, False)
        else:
            w_v = gmlp_w_in[j, :, D_BRANCH:2 * D_BRANCH].astype(BF16)
            ws = gmlp_w_s[j].astype(BF16)

            def gmlp_branch(h, tm):
                uz = _proj_simple(h, gmlp_w_in, (0, 2 * D_BRANCH), D_BRANCH, _epi_gelu_times_silu,
                                  "gmlp_in_uz", tm, tn=PROJ_TN // 2, w_layer=j)
                gv, mu, rstd = _proj_gelu_stats(h, w_v, 0, min(tm, 256))
                return _spatial_gate(uz, gv, mu, rstd, gmlp_ln_g[j], gmlp_ln_b[j], ws, gmlp_b_s[j],
                                     min(tm, 256), n_groups=GMLP_GROUPS)

            y = gmlp_branch(hl, tm_lat)
            yc = gmlp_branch(hc, tm_ctx) if need_ctx else None

        res, hl = finish(y, xl, i, who_lat(tm_out), not last, last)
        if last:
            out = res
        else:
            xl = res
        if need_ctx:
            xc, hc = finish(yc, xc, i, who_ctx, i + 1 < DEPTH - 1 or (i + 1) % 3 == 1, False)
    return out.reshape(batch, seq, d)
```

```python
import functools
import math

import numpy as np
import jax
import jax.numpy as jnp
from jax import lax
from jax.experimental import pallas as pl
from jax.experimental.pallas import tpu as pltpu

F32 = jnp.float32
BF16 = jnp.bfloat16

D_MODEL = 2048
D_BRANCH = 4096
DEPTH = 4
GRID_W = 64
FNET_GROUPS = 16
FNET_GROUP_DIM = 256
HEAD_DIM = 64
KV_HEADS = 8
Q_GROUP = 8
ATTN_BLOCK = 128
ATTN_HEADS_PER_STEP = 8
ROPE_BASE = 10000.0
GMLP_CHUNK = 128
GMLP_GROUPS = 16
EPS = 1e-6
NEG_INF = -1e30
LOG2_E = math.log2(math.e)

LANES = 128
MXU_COLS = 256
PROJ_TN = 1024
DFT_NB = 128
DFT_ROWS = 16
SLABS = D_BRANCH // LANES
VMEM_LIMIT = 56 * 1024 * 1024


def _cparams(n_axes, vmem=VMEM_LIMIT):
    return pltpu.CompilerParams(dimension_semantics=("arbitrary",) * n_axes,
                                vmem_limit_bytes=vmem)


def _silu(z):
    return 0.5 * z * (1.0 + jnp.tanh(0.5 * z))


def _gelu_tanh(x):
    c = math.sqrt(2.0 / math.pi)
    return 0.5 * x * (1.0 + jnp.tanh(c * (x + 0.044715 * (x * x * x))))


def _mod_rmsnorm(x, g, scale, shift):
    y = x * lax.rsqrt(jnp.mean(x * x, axis=-1, keepdims=True) + EPS) * g
    return y * (1.0 + scale) + shift


def _channel_dft_matrix():
    n = FNET_GROUP_DIM
    k = np.arange(n, dtype=np.float64)
    ang = 2.0 * np.pi * np.outer(k, k) / n
    s = 1.0 / math.sqrt(n)
    return np.concatenate([np.cos(ang) * s, -np.sin(ang) * s], axis=1).astype(np.float32)


def _position_dft_matrices(seq):
    na, nb = seq // DFT_NB, DFT_NB
    a = np.arange(na, dtype=np.float64)
    b = np.arange(nb, dtype=np.float64)
    ang = 2.0 * np.pi * (a[None, None, :] * a[None, :, None] / na + b[:, None, None] * a[None, :, None] / seq)
    mr = np.cos(ang) / math.sqrt(na)
    mi = -np.sin(ang) / math.sqrt(na)
    fa = np.concatenate([np.concatenate([mr, -mi], axis=2), np.concatenate([mi, mr], axis=2)], axis=1)
    fa = np.concatenate([fa[0::2], fa[1::2]], axis=2)
    angb = 2.0 * np.pi * np.outer(b, b) / nb
    fb = np.concatenate([np.cos(angb), np.sin(angb)], axis=1) / math.sqrt(nb)
    return fa.astype(np.float32), fb.astype(np.float32)


def _dense_dft_matrix(n):
    k = np.arange(n, dtype=np.float64)
    ang = 2.0 * np.pi * np.outer(k, k) / n
    return (np.concatenate([np.cos(ang), np.sin(ang)], axis=1) / math.sqrt(n)).astype(np.float32)


def _rope_tables(seq, scale=1.0):
    nf = HEAD_DIM // 4
    inv = ROPE_BASE ** (-np.arange(nf, dtype=np.float64) / nf)
    t = np.arange(seq)
    rows = (t // GRID_W).astype(np.float64)
    cols = (t % GRID_W).astype(np.float64)
    parts_c, parts_s = [], []
    for pos in (rows, cols):
        ang = pos[:, None] * inv[None, :]
        parts_c += [np.cos(ang), np.cos(ang)]
        parts_s += [-np.sin(ang), np.sin(ang)]
    cos = np.concatenate(parts_c, axis=1) * scale
    sin = np.concatenate(parts_s, axis=1) * scale
    reps = LANES // HEAD_DIM
    return (np.tile(cos, (1, reps)).astype(np.float32), np.tile(sin, (1, reps)).astype(np.float32))


def _mods_kernel(cv_ref, w_ref, b_ref, o_ref):
    a = _silu(cv_ref[...])
    a_hi = a.astype(BF16)
    a_lo = (a - a_hi.astype(F32)).astype(BF16)
    w = w_ref[...].astype(BF16)
    part = jnp.dot(a_hi, w, preferred_element_type=F32) + jnp.dot(a_lo, w, preferred_element_type=F32)

    @pl.when(pl.program_id(1) == 0)
    def _():
        o_ref[...] = part + b_ref[...]

    @pl.when(pl.program_id(1) > 0)
    def _():
        o_ref[...] += part


def _mods(cvec, ada_w, ada_b):
    depth, d, n3 = ada_w.shape
    tk = 512
    return pl.pallas_call(
        _mods_kernel,
        grid=(depth, d // tk),
        in_specs=[pl.BlockSpec((8, tk), lambda i, k: (0, k)),
                  pl.BlockSpec((None, tk, n3), lambda i, k: (i, k, 0)),
                  pl.BlockSpec((None, 1, n3), lambda i, k: (i, 0, 0))],
        out_specs=pl.BlockSpec((None, 8, n3), lambda i, k: (i, 0, 0)),
        out_shape=jax.ShapeDtypeStruct((depth, 8, n3), F32),
        compiler_params=_cparams(2),
        name="ada_mods",
    )(cvec, ada_w, ada_b.reshape(depth, 1, n3))


def _mod_spec(layer, kind, who_of_row):
    return pl.BlockSpec((None, None, None, 1, D_MODEL),
                        lambda r, *_: (layer, who_of_row(r), kind, 0, 0))


def _row_spec(vec_layer):
    return pl.BlockSpec((None, 1, D_MODEL), lambda r, *_: (vec_layer, 0, 0))


def _prenorm_kernel(x_ref, g_ref, sc_ref, sh_ref, h_ref):
    h_ref[...] = _mod_rmsnorm(x_ref[...], g_ref[...], sc_ref[...], sh_ref[...]).astype(BF16)


def _prenorm(x2, norm_g3, mods, layer, who, tm):
    m = x2.shape[0]
    return pl.pallas_call(
        _prenorm_kernel,
        grid=(m // tm,),
        in_specs=[pl.BlockSpec((tm, D_MODEL), lambda r: (r, 0)),
                  _row_spec(layer), _mod_spec(layer, 1, who), _mod_spec(layer, 0, who)],
        out_specs=pl.BlockSpec((tm, D_MODEL), lambda r: (r, 0)),
        out_shape=jax.ShapeDtypeStruct((m, D_MODEL), BF16),
        compiler_params=_cparams(1),
        name="prenorm",
    )(x2, norm_g3, mods, mods)


def _proj_kernel(epilogue, n_extra, chunk, n_w, a_ref, *rest):
    w_refs, rest = rest[:n_w], rest[n_w:]
    if len(a_ref.shape) == 3:
        a_flat = rest[-1]
        rest = rest[:-1]

        @pl.when(pl.program_id(1) == 0)
        def _():
            a_flat[...] = a_ref[...].reshape(a_flat.shape)

        a = a_flat[...]
    else:
        a = a_ref[...]
    for c0 in range(0, w_refs[0].shape[1], chunk):
        accs = [jnp.dot(a, w_ref[:, c0:c0 + chunk].astype(BF16), preferred_element_type=F32)
                for w_ref in w_refs]
        epilogue(accs[0] if n_w == 1 else accs, c0, rest[:n_extra], rest[n_extra:])


def _proj(h, w, col0, ncols, tm, tn, epilogue, extras, extra_specs, out_shapes, out_specs, name,
          lhs_spec=None, chunk=MXU_COLS, w_layer=None):
    k = h.shape[-1]
    m = h.size // k
    col0s = col0 if isinstance(col0, tuple) else (col0,)
    scratch = [pltpu.VMEM((tm, k), h.dtype)] if lhs_spec is not None else []
    if lhs_spec is None:
        lhs_spec = pl.BlockSpec((tm, k), lambda i, j: (i, 0))

    def w_spec(off):
        if w.ndim == 3:
            return pl.BlockSpec((None, k, tn), lambda i, j: (w_layer, 0, j + off))
        return pl.BlockSpec((k, tn), lambda i, j: (0, j + off))

    return pl.pallas_call(
        functools.partial(_proj_kernel, epilogue, len(extras), min(chunk, tn), len(col0s)),
        grid=(m // tm, ncols // tn),
        in_specs=[lhs_spec] + [w_spec(c // tn) for c in col0s] + list(extra_specs),
        out_specs=out_specs,
        out_shape=out_shapes,
        scratch_shapes=scratch,
        compiler_params=_cparams(2),
        name=name,
    )(h, *([w] * len(col0s)), *extras)


def _epi_silu(acc, c0, extras, outs):
    outs[0][:, c0:c0 + acc.shape[1]] = _silu(acc).astype(BF16)


def _epi_silu_slabs(acc, c0, extras, outs):
    for t in range(acc.shape[1] // LANES):
        outs[0][c0 // LANES + t] = _silu(acc[:, t * LANES:(t + 1) * LANES]).astype(BF16)


def _epi_gelu_times_silu(accs, c0, extras, outs):
    acc_u, acc_z = accs
    outs[0][:, c0:c0 + acc_u.shape[1]] = (_gelu_tanh(acc_u) * _silu(acc_z)).astype(BF16)


def _epi_cast(acc, c0, extras, outs):
    outs[0][:, c0:c0 + acc.shape[1]] = acc.astype(BF16)


def _epi_scale_cast(scale, acc, c0, extras, outs):
    outs[0][:, c0:c0 + acc.shape[1]] = (acc * scale).astype(BF16)


def _epi_channel_dft(split_rows, acc, c0, extras, outs):
    cs_ref = extras[0]
    ar_ref, ai_ref = outs
    tm = acc.shape[0]
    for gl in range(acc.shape[1] // FNET_GROUP_DIM):
        g = c0 // FNET_GROUP_DIM + gl
        ub = acc[:, gl * FNET_GROUP_DIM:(gl + 1) * FNET_GROUP_DIM].astype(BF16)
        ab = jnp.dot(ub, cs_ref[g], preferred_element_type=F32)
        if split_rows:
            ab4 = ab.reshape(tm // DFT_ROWS, 2, DFT_ROWS // 2, ab.shape[1])
            ab = jnp.concatenate([ab4[:, hf].reshape(tm // 2, ab.shape[1]) for hf in range(2)], axis=0)
        ab = ab.astype(BF16)
        ar_ref[2 * g] = ab[:, 0:128]
        ar_ref[2 * g + 1] = ab[:, 128:256]
        ai_ref[2 * g] = ab[:, 256:384]
        ai_ref[2 * g + 1] = ab[:, 384:512]


def _store_tile(out_ref, col, y, dup_heads):
    if not dup_heads:
        out_ref[:, col:col + LANES] = y.astype(BF16)
        return
    lane = lax.broadcasted_iota(jnp.int32, (1, LANES), 1)
    lo = lane < HEAD_DIM
    swapped = pltpu.roll(y, HEAD_DIM, 1)
    out_ref[:, 2 * col:2 * col + LANES] = jnp.where(lo, y, swapped).astype(BF16)
    out_ref[:, 2 * col + LANES:2 * col + 2 * LANES] = jnp.where(lo, swapped, y).astype(BF16)


def _epi_rope(dup_heads, acc, c0, extras, outs):
    cos = extras[0][...]
    sin = extras[1][...]
    lane = lax.broadcasted_iota(jnp.int32, (1, LANES), 1)
    first = (lane % 32) < 16
    for t in range(acc.shape[1] // LANES):
        x = acc[:, t * LANES:(t + 1) * LANES]
        partner = jnp.where(first, pltpu.roll(x, LANES - 16, 1), pltpu.roll(x, 16, 1))
        _store_tile(outs[0], c0 + t * LANES, x * cos + partner * sin, dup_heads)


def _epi_cast_dup(acc, c0, extras, outs):
    for t in range(acc.shape[1] // LANES):
        _store_tile(outs[0], c0 + t * LANES, acc[:, t * LANES:(t + 1) * LANES], True)


def _proj_simple(h, w, col0, ncols, epilogue, name, tm, tn=PROJ_TN, w_layer=None, out_mult=1):
    m = h.shape[0]
    tn = min(tn, ncols)
    return _proj(h, w, col0, ncols, tm, tn, epilogue, (), (),
                 jax.ShapeDtypeStruct((m, out_mult * ncols), BF16),
                 pl.BlockSpec((tm, out_mult * tn), lambda i, j: (i, j)), name, w_layer=w_layer)


def _proj_rope(h, w, col0, ncols, cos_t, sin_t, seq, name, tm, tn=PROJ_TN, w_layer=None,
               dup_heads=False):
    m = h.shape[0]
    tn = min(tn, ncols)
    per_batch = seq // tm
    out_mult = 2 if dup_heads else 1
    tab_spec = pl.BlockSpec((tm, LANES), lambda i, j: (i % per_batch, 0))
    return _proj(h, w, col0, ncols, tm, tn, functools.partial(_epi_rope, dup_heads),
                 (cos_t, sin_t), (tab_spec, tab_spec),
                 jax.ShapeDtypeStruct((m, out_mult * ncols), BF16),
                 pl.BlockSpec((tm, out_mult * tn), lambda i, j: (i, j)), name, w_layer=w_layer)


def _proj_silu_slabs(h, w, col0, name, tm, tn=PROJ_TN, w_layer=None):
    m = h.shape[0]
    return _proj(h, w, col0, D_BRANCH, tm, tn, _epi_silu_slabs, (), (),
                 jax.ShapeDtypeStruct((SLABS, m, LANES), BF16),
                 pl.BlockSpec((tn // LANES, tm, LANES), lambda i, j: (j, i, 0)), name, w_layer=w_layer)


def _proj_channel_dft(h, w, cs, name, tm, seq=None, tn=PROJ_TN):
    m = h.shape[0]
    lhs_spec = None
    if seq is not None:
        na, tiles = seq // DFT_NB, DFT_NB // DFT_ROWS
        assert tm == na * DFT_ROWS
        h = h.reshape(m // seq, na, DFT_NB, D_MODEL)
        lhs_spec = pl.BlockSpec((None, na, DFT_ROWS, D_MODEL), lambda i, j: (i // tiles, 0, i % tiles, 0))
    slab_shape = jax.ShapeDtypeStruct((SLABS, m, LANES), BF16)
    slab_spec = pl.BlockSpec((tn // LANES, tm, LANES), lambda i, j: (j, i, 0))
    groups = tn // FNET_GROUP_DIM
    cs_spec = pl.BlockSpec((groups,) + cs.shape[1:], lambda i, j: (j, 0, 0))
    return _proj(h, w, 0, D_BRANCH, tm, tn, functools.partial(_epi_channel_dft, seq is not None), (cs,), (cs_spec,),
                 (slab_shape, slab_shape), (slab_spec, slab_spec), name, lhs_spec=lhs_spec,
                 chunk=2 * MXU_COLS)


def _fold_mix_kernel(c_ref, s_ref, wm_ref, o_ref):
    wm = wm_ref[...]
    gd = FNET_GROUP_DIM
    hp = lax.Precision.HIGHEST
    o_ref[:, 0:gd] = jnp.dot(c_ref[...], wm, preferred_element_type=F32, precision=hp).astype(BF16)
    o_ref[:, gd:2 * gd] = jnp.dot(s_ref[...], wm, preferred_element_type=F32, precision=hp).astype(BF16)


def _fold_mix(w_mix):
    gd = FNET_GROUP_DIM
    cs = _channel_dft_matrix()
    mat = pl.BlockSpec((gd, gd), lambda g: (0, 0))
    return pl.pallas_call(
        _fold_mix_kernel,
        grid=(FNET_GROUPS,),
        in_specs=[mat, mat, pl.BlockSpec((None, gd, gd), lambda g: (g, 0, 0))],
        out_specs=pl.BlockSpec((None, gd, 2 * gd), lambda g: (g, 0, 0)),
        out_shape=jax.ShapeDtypeStruct((FNET_GROUPS, gd, 2 * gd), BF16),
        compiler_params=_cparams(1),
        name="fnet_fold_mix",
    )(jnp.asarray(cs[:, :gd]), jnp.asarray(cs[:, gd:]), w_mix)


def _dft_kernel(na, pitch, gpitch, ar_ref, ai_ref, zg_ref, fa_ref, fb_ref, y_ref,
                xr_ref, xi_ref, er_ref, ei_ref, g_ref):
    half_rows = na * (DFT_ROWS // 2)
    xr_ref[...] = ar_ref[...].astype(F32)
    xi_ref[...] = ai_ref[...].astype(F32)

    def gather_a(b):
        start = (b // (DFT_ROWS // 2)) * half_rows + b % (DFT_ROWS // 2)
        zr = xr_ref[pl.ds(start, na, stride=DFT_ROWS // 2), :]
        zi = xi_ref[pl.ds(start, na, stride=DFT_ROWS // 2), :]
        return jnp.concatenate([zr, zi], axis=0).astype(BF16)

    def stage_a(pair, carry):
        b0 = 2 * pair
        d0, d1 = gather_a(b0), gather_a(b0 + 1)
        zero = jnp.zeros_like(d0)
        rhs = jnp.concatenate([jnp.concatenate([d0, zero], axis=1),
                               jnp.concatenate([zero, d1], axis=1)], axis=0)
        e = jnp.dot(fa_ref[pair], rhs, preferred_element_type=F32)
        for j in range(2):
            off = pl.multiple_of((b0 + j) * pitch, 8)
            er_ref[pl.ds(off, na), :] = e[:na, j * LANES:(j + 1) * LANES]
            ei_ref[pl.ds(off, na), :] = e[na:, j * LANES:(j + 1) * LANES]
        return carry

    lax.fori_loop(0, DFT_NB // 2, stage_a, 0, unroll=64)

    def gather_b(ka):
        er = er_ref[pl.ds(ka, DFT_NB, stride=pitch), :]
        ei = ei_ref[pl.ds(ka, DFT_NB, stride=pitch), :]
        return jnp.concatenate([er, ei], axis=0).astype(BF16)

    def stage_b(pair, carry):
        ka0 = 2 * pair
        rhs = jnp.concatenate([gather_b(ka0), gather_b(ka0 + 1)], axis=1)
        g = jnp.dot(fb_ref[...], rhs, preferred_element_type=F32)
        for j in range(2):
            off = pl.multiple_of((ka0 + j) * gpitch, 8)
            g_ref[pl.ds(off, DFT_NB), :] = g[:, j * LANES:(j + 1) * LANES]
        return carry

    lax.fori_loop(0, na // 2, stage_b, 0, unroll=32)

    def gate(kb, carry):
        rows = pl.ds(pl.multiple_of(kb * na, na), na)
        g = g_ref[pl.ds(kb, na, stride=gpitch), :]
        y_ref[rows, :] = (g * zg_ref[rows, :].astype(F32)).astype(BF16)
        return carry

    lax.fori_loop(0, DFT_NB, gate, 0, unroll=32)


def _position_dft(ar, ai, zg, batch, seq):
    na = seq // DFT_NB
    pitch = na + 8
    gpitch = DFT_NB + 8
    fa_np, fb_np = _position_dft_matrices(seq)
    fa = jnp.asarray(fa_np).astype(BF16)
    fb = jnp.asarray(fb_np).astype(BF16)
    slab = pl.BlockSpec((None, seq, LANES), lambda s, b: (s, b, 0))
    return pl.pallas_call(
        functools.partial(_dft_kernel, na, pitch, gpitch),
        grid=(SLABS, batch),
        in_specs=[slab, slab, slab,
                  pl.BlockSpec(fa.shape, lambda s, b: (0, 0, 0)),
                  pl.BlockSpec(fb.shape, lambda s, b: (0, 0))],
        out_specs=slab,
        out_shape=jax.ShapeDtypeStruct(ar.shape, BF16),
        scratch_shapes=[pltpu.VMEM((seq, LANES), F32),
                        pltpu.VMEM((seq, LANES), F32),
                        pltpu.VMEM((DFT_NB * pitch, LANES), F32),
                        pltpu.VMEM((DFT_NB * pitch, LANES), F32),
                        pltpu.VMEM((na * gpitch, LANES), F32)],
        compiler_params=_cparams(2),
        name="position_dft",
    )(ar, ai, zg, fa, fb)


def _dft_dense_kernel(ar_ref, ai_ref, zg_ref, fd_ref, y_ref):
    n = ar_ref.shape[0]
    d = jnp.concatenate([jnp.concatenate([ar_ref[s] for s in range(n)], axis=1),
                         jnp.concatenate([ai_ref[s] for s in range(n)], axis=1)], axis=0).astype(BF16)
    g = jnp.dot(fd_ref[...], d, preferred_element_type=F32)
    for s in range(n):
        y_ref[s] = (g[:, s * LANES:(s + 1) * LANES] * zg_ref[s].astype(F32)).astype(BF16)


def _position_dft_dense(ar, ai, zg, batch, seq, slabs_per_step=8):
    fd = jnp.asarray(_dense_dft_matrix(seq)).astype(BF16)
    slab = pl.BlockSpec((slabs_per_step, seq, LANES), lambda s, b: (s, b, 0))
    return pl.pallas_call(
        _dft_dense_kernel,
        grid=(SLABS // slabs_per_step, batch),
        in_specs=[slab, slab, slab, pl.BlockSpec(fd.shape, lambda s, b: (0, 0))],
        out_specs=slab,
        out_shape=jax.ShapeDtypeStruct(ar.shape, BF16),
        compiler_params=_cparams(2),
        name="position_dft_dense",
    )(ar, ai, zg, fd)


def _attn_kernel(n_band, nblk, sink_ref, q_ref, *refs):
    k_refs = refs[:n_band + 1]
    v_refs = refs[n_band + 1:2 * n_band + 2]
    zg_ref, y_ref = refs[2 * n_band + 2:]
    i = pl.program_id(2)
    blk = ATTN_BLOCK
    n_pair = Q_GROUP // 2
    qw = Q_GROUP * HEAD_DIM

    lane = lax.broadcasted_iota(jnp.int32, (1, LANES), 1)
    lo = lane < HEAD_DIM
    dn = (((1,), (1,)), ((), ()))
    if n_band:
        r = lax.broadcasted_iota(jnp.int32, (blk, blk), 0)
        c = lax.broadcasted_iota(jnp.int32, (blk, blk), 1)
        prev_ok = c >= r + jnp.where(i > 0, 0, blk)
        next_ok = c <= r - jnp.where(i < nblk - 1, 0, blk)

    def scores(hh):
        kl = slice(hh * LANES, (hh + 1) * LANES)
        keys = jnp.concatenate([ref[:, kl] for ref in k_refs], axis=0)
        q4 = jnp.concatenate([q_ref[:, hh * qw + t * LANES:hh * qw + (t + 1) * LANES]
                              for t in range(n_pair)], axis=0)
        qzero = jnp.zeros_like(q4)
        return [lax.dot_general(qm, keys, dn, preferred_element_type=F32)
                for qm in (jnp.where(lo, q4, qzero), jnp.where(lo, qzero, q4))]

    def finish(hh, s_both):
        h = pl.program_id(1) * ATTN_HEADS_PER_STEP + hh
        kl = slice(hh * LANES, (hh + 1) * LANES)
        vals = jnp.concatenate([ref[:, kl] for ref in v_refs], axis=0)
        ones = jnp.ones_like(vals)
        v_ext = (jnp.where(lo, vals, ones), jnp.where(lo, ones, vals))
        o_ext, sink_term = [], []
        for hd, s in enumerate(s_both):
            p_rows, sink_rows = [], []
            for t in range(n_pair):
                st = s[t * blk:(t + 1) * blk]
                parts = [st[:, j * blk:(j + 1) * blk] for j in range(st.shape[1] // blk)]
                if n_band:
                    parts[0] = jnp.where(prev_ok, parts[0], NEG_INF)
                    parts[2] = jnp.where(next_ok, parts[2], NEG_INF)
                sk = sink_ref[h, 2 * t + hd] * LOG2_E
                mx = parts[0]
                for part in parts[1:]:
                    mx = jnp.maximum(mx, part)
                mx = jnp.maximum(jnp.max(mx, axis=-1, keepdims=True), sk)
                p_rows.append(jnp.concatenate([jnp.exp2(part - mx).astype(BF16) for part in parts], axis=1))
                sink_rows.append(jnp.exp2(sk - mx))
            p = jnp.concatenate(p_rows, axis=0)
            o_ext.append(jnp.dot(p, v_ext[hd], preferred_element_type=F32))
            sink_term.append(jnp.concatenate(sink_rows, axis=0))
        o = jnp.where(lo, o_ext[0], o_ext[1])
        denom = jnp.where(lo, pltpu.roll(o_ext[0], HEAD_DIM, 1) + sink_term[0],
                          pltpu.roll(o_ext[1], HEAD_DIM, 1) + sink_term[1])
        res = o / denom
        for t in range(n_pair):
            tile = slice(hh * qw + t * LANES, hh * qw + (t + 1) * LANES)
            y_ref[:, tile] = (res[t * blk:(t + 1) * blk] * zg_ref[:, tile].astype(F32)).astype(BF16)

    pending = scores(0)
    for hh in range(ATTN_HEADS_PER_STEP):
        upcoming = scores(hh + 1) if hh + 1 < ATTN_HEADS_PER_STEP else None
        finish(hh, pending)
        pending = upcoming


def _attention(q, kd, vd, kcd, vcd, zg, sink, batch, seq, ctx_len, use_band):
    blk = ATTN_BLOCK
    nblk = seq // blk
    hps = ATTN_HEADS_PER_STEP
    qw = hps * Q_GROUP * HEAD_DIM
    kw = hps * LANES
    q_spec = pl.BlockSpec((blk, qw), lambda b, h, i: (b * nblk + i, h))
    ctx_spec = pl.BlockSpec((ctx_len, kw), lambda b, h, i: (b, h))
    if use_band:
        def band(delta):
            return pl.BlockSpec(
                (blk, kw), lambda b, h, i: (b * nblk + jnp.clip(i + delta, 0, nblk - 1), h))
        k_specs = [band(-1), band(0), band(1), ctx_spec]
        k_args, v_args = [kd, kd, kd, kcd], [vd, vd, vd, vcd]
        n_band = 3
    else:
        k_specs, k_args, v_args, n_band = [ctx_spec], [kcd], [vcd], 0
    return pl.pallas_call(
        functools.partial(_attn_kernel, n_band, nblk),
        grid=(batch, KV_HEADS // hps, nblk),
        in_specs=[pl.BlockSpec(memory_space=pltpu.SMEM), q_spec] + k_specs + k_specs + [q_spec],
        out_specs=q_spec,
        out_shape=jax.ShapeDtypeStruct(q.shape, BF16),
        compiler_params=_cparams(3),
        name="attention_band" if use_band else "attention_ctx",
    )(sink, q, *k_args, *v_args, zg)


def _gelu_stats_kernel(chunk, a_ref, w_ref, gv_ref, mu_ref, rstd_ref):
    a = a_ref[...]
    n = w_ref.shape[1]
    mean = m2 = None
    for idx, c0 in enumerate(range(0, n, chunk)):
        ge = _gelu_tanh(jnp.dot(a, w_ref[:, c0:c0 + chunk], preferred_element_type=F32))
        gv_ref[:, c0:c0 + chunk] = ge.astype(BF16)
        cmean = jnp.mean(ge, axis=-1, keepdims=True)
        d = ge - cmean
        cm2 = jnp.sum(d * d, axis=-1, keepdims=True)
        if idx == 0:
            mean, m2 = cmean, cm2
        else:
            delta = cmean - mean
            mean = mean + delta * (1.0 / (idx + 1))
            m2 = m2 + cm2 + delta * delta * (chunk * idx / (idx + 1))
    mu_ref[...] = mean
    rstd_ref[...] = lax.rsqrt(m2 * (1.0 / n) + EPS)


def _proj_gelu_stats(h, w, col0, tm, chunk=2 * MXU_COLS):
    m, k = h.shape
    n = D_BRANCH
    col = pl.BlockSpec((tm, 1), lambda i: (i, 0))
    stat = jax.ShapeDtypeStruct((m, 1), F32)
    return pl.pallas_call(
        functools.partial(_gelu_stats_kernel, chunk),
        grid=(m // tm,),
        in_specs=[pl.BlockSpec((tm, k), lambda i: (i, 0)),
                  pl.BlockSpec((k, n), lambda i: (0, col0 // n), pipeline_mode=pl.Buffered(1))],
        out_specs=(pl.BlockSpec((tm, n), lambda i: (i, 0)), col, col),
        out_shape=(jax.ShapeDtypeStruct((m, n), BF16), stat, stat),
        compiler_params=_cparams(1),
        name="gmlp_in_v",
    )(h, w)


def _sgu_kernel(n_chunks, n_groups, uz_ref, gv_ref, mu_ref, rstd_ref, lg_ref, lb_ref,
                ws_ref, bs_ref, y_ref):
    gd = D_BRANCH // GMLP_GROUPS
    mu = mu_ref[...]
    rstd = rstd_ref[...]
    for gl in range(n_groups):
        cols = slice(gl * gd, (gl + 1) * gd)
        ws = ws_ref[gl]
        bs = bs_ref[gl]
        lg = lg_ref[:, cols]
        lb = lb_ref[:, cols]
        for c in range(n_chunks):
            rows = slice(c * GMLP_CHUNK, (c + 1) * GMLP_CHUNK)
            vn = ((gv_ref[rows, cols].astype(F32) - mu[rows]) * rstd[rows] * lg + lb).astype(BF16)
            s = jnp.dot(ws, vn, preferred_element_type=F32) + bs
            y_ref[rows, cols] = (uz_ref[rows, cols].astype(F32) * s).astype(BF16)


def _spatial_gate(uz, gv, mu, rstd, ln_g, ln_b, w_s, b_s, tm, n_groups=4):
    m = uz.shape[0]
    gd = D_BRANCH // GMLP_GROUPS
    tn = n_groups * gd
    tile = pl.BlockSpec((tm, tn), lambda r, g: (r, g))
    col = pl.BlockSpec((tm, 1), lambda r, g: (r, 0))
    vec = pl.BlockSpec((1, tn), lambda r, g: (0, g))
    return pl.pallas_call(
        functools.partial(_sgu_kernel, tm // GMLP_CHUNK, n_groups),
        grid=(m // tm, GMLP_GROUPS // n_groups),
        in_specs=[tile, tile, col, col, vec, vec,
                  pl.BlockSpec((n_groups, GMLP_CHUNK, GMLP_CHUNK), lambda r, g: (g, 0, 0)),
                  pl.BlockSpec((n_groups, GMLP_CHUNK, 1), lambda r, g: (g, 0, 0))],
        out_specs=tile,
        out_shape=jax.ShapeDtypeStruct(uz.shape, BF16),
        compiler_params=_cparams(2),
        name="gmlp_spatial_gate",
    )(uz, gv, mu, rstd, ln_g.reshape(1, D_BRANCH), ln_b.reshape(1, D_BRANCH),
      w_s, b_s.reshape(GMLP_GROUPS, GMLP_CHUNK, 1))


def _wout_kernel(mode, y_ref, w_ref, x_ref, gate_ref, *refs):
    if len(y_ref.shape) == 3:
        y = jnp.concatenate([y_ref[s] for s in range(y_ref.shape[0])], axis=1)
    else:
        y = y_ref[...]
    acc = jnp.dot(y, w_ref[...], preferred_element_type=F32)
    xn = x_ref[...] + gate_ref[...] * acc
    if mode == "final":
        g_ref, o_ref = refs
        o_ref[...] = xn * lax.rsqrt(jnp.mean(xn * xn, axis=-1, keepdims=True) + EPS) * g_ref[...]
    elif mode == "next":
        g_ref, sc_ref, sh_ref, xo_ref, h_ref = refs
        xo_ref[...] = xn
        h_ref[...] = _mod_rmsnorm(xn, g_ref[...], sc_ref[...], sh_ref[...]).astype(BF16)
    else:
        refs[0][...] = xn


def _out_proj(y, w_out, x2, mods, layer, who, mode, norm_vec, tm):
    m = x2.shape[0]
    row = pl.BlockSpec((tm, D_MODEL), lambda r: (r, 0))
    if y.ndim == 3:
        y_spec = pl.BlockSpec((SLABS, tm, LANES), lambda r: (0, r, 0))
    else:
        y_spec = pl.BlockSpec((tm, D_BRANCH), lambda r: (r, 0))
    in_specs = [y_spec,
                pl.BlockSpec((None, D_BRANCH, D_MODEL), lambda r: (layer, 0, 0),
                             pipeline_mode=pl.Buffered(1)),
                row, _mod_spec(layer, 2, who)]
    args = [y, w_out, x2, mods]
    xs = jax.ShapeDtypeStruct((m, D_MODEL), F32)
    if mode == "final":
        in_specs.append(pl.BlockSpec((1, D_MODEL), lambda r: (0, 0)))
        args.append(norm_vec.reshape(1, D_MODEL))
        out_shape, out_specs = xs, row
    elif mode == "next":
        in_specs += [_row_spec(layer + 1), _mod_spec(layer + 1, 1, who), _mod_spec(layer + 1, 0, who)]
        args += [norm_vec, mods, mods]
        out_shape = (xs, jax.ShapeDtypeStruct((m, D_MODEL), BF16))
        out_specs = (row, row)
    else:
        out_shape, out_specs = xs, row
    return pl.pallas_call(
        functools.partial(_wout_kernel, mode),
        grid=(m // tm,),
        in_specs=in_specs,
        out_specs=out_specs,
        out_shape=out_shape,
        compiler_params=_cparams(1),
        name="out_proj_" + mode,
    )(*args)


def kernel(x, c, ctx, c_ctx, norm_g, ada_w, ada_b, w_out, fnet_w_in, fnet_w_mix, attn_w_in, attn_sink,
           gmlp_w_in, gmlp_w_s, gmlp_b_s, gmlp_ln_g, gmlp_ln_b, final_g):
    batch, seq, d = x.shape
    ctx_len = ctx.shape[1]
    assert d == D_MODEL and seq % (DFT_NB * 8) == 0 and seq % GRID_W == 0 and batch < 8
    m_lat, m_ctx = batch * seq, batch * ctx_len
    tm_lat = min(2048, seq)
    tm_ctx = m_ctx
    tm_out = 256

    def who_lat(tm):
        return lambda r: (r * tm) // seq

    who_ctx = lambda r: batch

    cvec = jnp.zeros((8, d), F32).at[:batch].set(c).at[batch].set(c_ctx)
    mods = _mods(cvec, ada_w, ada_b).reshape(DEPTH, 8, 3, 1, d)
    norm_g3 = norm_g.reshape(DEPTH, 1, d)
    w_out_b = w_out.astype(BF16)

    xl = x.reshape(m_lat, d)
    xc = ctx.reshape(m_ctx, d)
    hl = _prenorm(xl, norm_g3, mods, 0, who_lat(512), 512)
    hc = _prenorm(xc, norm_g3, mods, 0, who_ctx, tm_ctx)


    def finish(y, x2, layer, who_fn, need_next, is_final):
        if is_final:
            return _out_proj(y, w_out_b, x2, mods, layer, who_fn, "final", final_g, tm_out), None
        if need_next:
            return _out_proj(y, w_out_b, x2, mods, layer, who_fn, "next", norm_g3, tm_out)
        return _out_proj(y, w_out_b, x2, mods, layer, who_fn, "plain", None, tm_out), None

    out = None
    for i in range(DEPTH):
        kind, j = i % 3, i // 3
        need_ctx = i < DEPTH - 1
        last = i == DEPTH - 1
        if kind == 0:
            cs = _fold_mix(fnet_w_mix[j])
            tm_dft = (seq // DFT_NB) * DFT_ROWS
            w_u = fnet_w_in[j, :, :D_BRANCH].astype(BF16)
            ar, ai = _proj_channel_dft(hl, w_u, cs, "fnet_in_u", tm_dft, seq=seq)
            zg = _proj_silu_slabs(hl, fnet_w_in, D_BRANCH, "fnet_in_z", tm_lat, w_layer=j)
            y = _position_dft(ar, ai, zg, batch, seq)
            yc = None
            if need_ctx:
                ar, ai = _proj_channel_dft(hc, w_u, cs, "fnet_in_u_ctx", tm_ctx)
                zg = _proj_silu_slabs(hc, fnet_w_in, D_BRANCH, "fnet_in_z_ctx", tm_ctx, w_layer=j)
                yc = _position_dft_dense(ar, ai, zg, batch, ctx_len)
        elif kind == 1:
            kvw = KV_HEADS * HEAD_DIM
            k0, v0, z0 = D_BRANCH, D_BRANCH + kvw, D_BRANCH + 2 * kvw
            sink = attn_sink[j].reshape(KV_HEADS, Q_GROUP)
            scale = HEAD_DIM ** -0.5 * LOG2_E
            q_tabs = [jnp.asarray(t) for t in _rope_tables(seq, scale)]
            k_tabs = [jnp.asarray(t) for t in _rope_tables(seq)]
            q = _proj_rope(hl, attn_w_in, 0, D_BRANCH, *q_tabs, seq, "attn_in_q", tm_lat, w_layer=j)
            kd = _proj_rope(hl, attn_w_in, k0, kvw, *k_tabs, seq, "attn_in_k", tm_lat, w_layer=j,
                            dup_heads=True)
            vd = _proj_simple(hl, attn_w_in, v0, kvw, _epi_cast_dup, "attn_in_v", tm_lat, w_layer=j, out_mult=2)
            zg = _proj_simple(hl, attn_w_in, z0, D_BRANCH, _epi_silu, "attn_in_z", tm_lat, w_layer=j)
            kcd = _proj_simple(hc, attn_w_in, k0, kvw, _epi_cast_dup, "attn_in_kc", tm_ctx, w_layer=j, out_mult=2)
            vcd = _proj_simple(hc, attn_w_in, v0, kvw, _epi_cast_dup, "attn_in_vc", tm_ctx, w_layer=j, out_mult=2)
            y = _attention(q, kd, vd, kcd, vcd, zg, sink, batch, seq, ctx_len, True)
            yc = None
            if need_ctx:
                qc = _proj_simple(hc, attn_w_in, 0, D_BRANCH, functools.partial(_epi_scale_cast, scale),
                                  "attn_in_qc", tm_ctx, w_layer=j)
                zgc = _proj_simple(hc, attn_w_in, z0, D_BRANCH, _epi_silu, "attn_in_zc", tm_ctx, w_layer=j)
                yc = _attention(qc, None, None, kcd, vcd, zgc, sink, batch, ctx_len, ctx_len, False)
        else:
            w_v = gmlp_w_in[j, :, D_BRANCH:2 * D_BRANCH].astype(BF16)
            ws = gmlp_w_s[j].astype(BF16)

            def gmlp_branch(h, tm):
                uz = _proj_simple(h, gmlp_w_in, (0, 2 * D_BRANCH), D_BRANCH, _epi_gelu_times_silu,
                                  "gmlp_in_uz", tm, tn=PROJ_TN // 2, w_layer=j)
                gv, mu, rstd = _proj_gelu_stats(h, w_v, 0, min(tm, 256))
                return _spatial_gate(uz, gv, mu, rstd, gmlp_ln_g[j], gmlp_ln_b[j], ws, gmlp_b_s[j],
                                     min(tm, 256), n_groups=GMLP_GROUPS)

            y = gmlp_branch(hl, tm_lat)
            yc = gmlp_branch(hc, tm_ctx) if need_ctx else None

        res, hl = finish(y, xl, i, who_lat(tm_out), not last, last)
        if last:
            out = res
        else:
            xl = res
        if need_ctx:
            xc, hc = finish(yc, xc, i, who_ctx, i + 1 < DEPTH - 1 or (i + 1) % 3 == 1, False)
    return out.reshape(batch, seq, d)
```

```python
import functools
import math

import numpy as np
import jax
import jax.numpy as jnp
from jax import lax
from jax.experimental import pallas as pl
from jax.experimental.pallas import tpu as pltpu

F32 = jnp.float32
BF16 = jnp.bfloat16

D_MODEL = 2048
D_BRANCH = 4096
DEPTH = 4
GRID_W = 64
FNET_GROUPS = 16
FNET_GROUP_DIM = 256
HEAD_DIM = 64
KV_HEADS = 8
Q_GROUP = 8
ATTN_BLOCK = 128
ATTN_HEADS_PER_STEP = 8
ROPE_BASE = 10000.0
GMLP_CHUNK = 128
GMLP_GROUPS = 16
EPS = 1e-6
NEG_INF = -1e30
LOG2_E = math.log2(math.e)

LANES = 128
MXU_COLS = 256
PROJ_TN = 1024
PROJ_TM = 2048
OUT_TM = 512
STATS_TM = 256
GATE_TM = 256
NORM_TM = 512
DFT_NB = 128
DFT_ROWS = 16
SLABS = D_BRANCH // LANES
VMEM_LIMIT = 56 * 1024 * 1024


def _cparams(n_axes, vmem=VMEM_LIMIT):
    return pltpu.CompilerParams(dimension_semantics=("arbitrary",) * n_axes,
                                vmem_limit_bytes=vmem)


def _silu(z):
    return 0.5 * z * (1.0 + jnp.tanh(0.5 * z))


def _gelu_tanh(x):
    c = math.sqrt(2.0 / math.pi)
    return 0.5 * x * (1.0 + jnp.tanh(c * (x + 0.044715 * (x * x * x))))


def _mod_rmsnorm(x, g, scale, shift):
    y = x * lax.rsqrt(jnp.mean(x * x, axis=-1, keepdims=True) + EPS) * g
    return y * (1.0 + scale) + shift


def _channel_dft_matrix():
    n = FNET_GROUP_DIM
    k = np.arange(n, dtype=np.float64)
    ang = 2.0 * np.pi * np.outer(k, k) / n
    s = 1.0 / math.sqrt(n)
    return np.concatenate([np.cos(ang) * s, -np.sin(ang) * s], axis=1).astype(np.float32)


def _position_dft_matrices(seq):
    na, nb = seq // DFT_NB, DFT_NB
    a = np.arange(na, dtype=np.float64)
    b = np.arange(nb, dtype=np.float64)
    ang = 2.0 * np.pi * (a[None, None, :] * a[None, :, None] / na + b[:, None, None] * a[None, :, None] / seq)
    mr = np.cos(ang) / math.sqrt(na)
    mi = -np.sin(ang) / math.sqrt(na)
    fa = np.concatenate([np.concatenate([mr, -mi], axis=2), np.concatenate([mi, mr], axis=2)], axis=1)
    fa = np.concatenate([fa[0::2], fa[1::2]], axis=2)
    angb = 2.0 * np.pi * np.outer(b, b) / nb
    fb = np.concatenate([np.cos(angb), np.sin(angb)], axis=1) / math.sqrt(nb)
    return fa.astype(np.float32), fb.astype(np.float32)


def _dense_dft_matrix(n):
    k = np.arange(n, dtype=np.float64)
    ang = 2.0 * np.pi * np.outer(k, k) / n
    return (np.concatenate([np.cos(ang), np.sin(ang)], axis=1) / math.sqrt(n)).astype(np.float32)


def _rope_tables(seq, scale=1.0):
    nf = HEAD_DIM // 4
    inv = ROPE_BASE ** (-np.arange(nf, dtype=np.float64) / nf)
    t = np.arange(seq)
    rows = (t // GRID_W).astype(np.float64)
    cols = (t % GRID_W).astype(np.float64)
    parts_c, parts_s = [], []
    for pos in (rows, cols):
        ang = pos[:, None] * inv[None, :]
        parts_c += [np.cos(ang), np.cos(ang)]
        parts_s += [-np.sin(ang), np.sin(ang)]
    cos = np.concatenate(parts_c, axis=1) * scale
    sin = np.concatenate(parts_s, axis=1) * scale
    reps = LANES // HEAD_DIM
    return (np.tile(cos, (1, reps)).astype(np.float32), np.tile(sin, (1, reps)).astype(np.float32))


def _mods_kernel(cv_ref, w_ref, b_ref, o_ref):
    a = _silu(cv_ref[...])
    a_hi = a.astype(BF16)
    a_lo = (a - a_hi.astype(F32)).astype(BF16)
    w = w_ref[...].astype(BF16)
    part = jnp.dot(a_hi, w, preferred_element_type=F32) + jnp.dot(a_lo, w, preferred_element_type=F32)

    @pl.when(pl.program_id(1) == 0)
    def _():
        o_ref[...] = part + b_ref[...]

    @pl.when(pl.program_id(1) > 0)
    def _():
        o_ref[...] += part


def _mods(cvec, ada_w, ada_b):
    depth, d, n3 = ada_w.shape
    tk = 512
    return pl.pallas_call(
        _mods_kernel,
        grid=(depth, d // tk),
        in_specs=[pl.BlockSpec((8, tk), lambda i, k: (0, k)),
                  pl.BlockSpec((None, tk, n3), lambda i, k: (i, k, 0)),
                  pl.BlockSpec((None, 1, n3), lambda i, k: (i, 0, 0))],
        out_specs=pl.BlockSpec((None, 8, n3), lambda i, k: (i, 0, 0)),
        out_shape=jax.ShapeDtypeStruct((depth, 8, n3), F32),
        compiler_params=_cparams(2),
        name="ada_mods",
    )(cvec, ada_w, ada_b.reshape(depth, 1, n3))


def _mod_spec(layer, kind, who_of_row):
    return pl.BlockSpec((None, None, None, 1, D_MODEL),
                        lambda r, *_: (layer, who_of_row(r), kind, 0, 0))


def _row_spec(vec_layer):
    return pl.BlockSpec((None, 1, D_MODEL), lambda r, *_: (vec_layer, 0, 0))


def _prenorm_kernel(x_ref, g_ref, sc_ref, sh_ref, h_ref):
    h_ref[...] = _mod_rmsnorm(x_ref[...], g_ref[...], sc_ref[...], sh_ref[...]).astype(BF16)


def _prenorm(x2, norm_g3, mods, layer, who, tm):
    m = x2.shape[0]
    return pl.pallas_call(
        _prenorm_kernel,
        grid=(m // tm,),
        in_specs=[pl.BlockSpec((tm, D_MODEL), lambda r: (r, 0)),
                  _row_spec(layer), _mod_spec(layer, 1, who), _mod_spec(layer, 0, who)],
        out_specs=pl.BlockSpec((tm, D_MODEL), lambda r: (r, 0)),
        out_shape=jax.ShapeDtypeStruct((m, D_MODEL), BF16),
        compiler_params=_cparams(1),
        name="prenorm",
    )(x2, norm_g3, mods, mods)


def _proj_kernel(epilogue, n_extra, chunk, n_w, a_ref, *rest):
    w_refs, rest = rest[:n_w], rest[n_w:]
    if len(a_ref.shape) == 3:
        a_flat = rest[-1]
        rest = rest[:-1]

        @pl.when(pl.program_id(1) == 0)
        def _():
            a_flat[...] = a_ref[...].reshape(a_flat.shape)

        a = a_flat[...]
    else:
        a = a_ref[...]
    for c0 in range(0, w_refs[0].shape[1], chunk):
        accs = [jnp.dot(a, w_ref[:, c0:c0 + chunk].astype(BF16), preferred_element_type=F32)
                for w_ref in w_refs]
        epilogue(accs[0] if n_w == 1 else accs, c0, rest[:n_extra], rest[n_extra:])


def _proj(h, w, col0, ncols, tm, tn, epilogue, extras, extra_specs, out_shapes, out_specs, name,
          lhs_spec=None, chunk=MXU_COLS, w_layer=None):
    k = h.shape[-1]
    m = h.size // k
    col0s = col0 if isinstance(col0, tuple) else (col0,)
    scratch = [pltpu.VMEM((tm, k), h.dtype)] if lhs_spec is not None else []
    if lhs_spec is None:
        lhs_spec = pl.BlockSpec((tm, k), lambda i, j: (i, 0))

    def w_spec(off):
        if w.ndim == 3:
            return pl.BlockSpec((None, k, tn), lambda i, j: (w_layer, 0, j + off))
        return pl.BlockSpec((k, tn), lambda i, j: (0, j + off))

    return pl.pallas_call(
        functools.partial(_proj_kernel, epilogue, len(extras), min(chunk, tn), len(col0s)),
        grid=(m // tm, ncols // tn),
        in_specs=[lhs_spec] + [w_spec(c // tn) for c in col0s] + list(extra_specs),
        out_specs=out_specs,
        out_shape=out_shapes,
        scratch_shapes=scratch,
        compiler_params=_cparams(2),
        name=name,
    )(h, *([w] * len(col0s)), *extras)


def _epi_silu(acc, c0, extras, outs):
    outs[0][:, c0:c0 + acc.shape[1]] = _silu(acc).astype(BF16)


def _epi_silu_slabs(acc, c0, extras, outs):
    for t in range(acc.shape[1] // LANES):
        outs[0][c0 // LANES + t] = _silu(acc[:, t * LANES:(t + 1) * LANES]).astype(BF16)


def _epi_gelu_times_silu(accs, c0, extras, outs):
    acc_u, acc_z = accs
    outs[0][:, c0:c0 + acc_u.shape[1]] = (_gelu_tanh(acc_u) * _silu(acc_z)).astype(BF16)


def _epi_scale_cast(scale, acc, c0, extras, outs):
    outs[0][:, c0:c0 + acc.shape[1]] = (acc * scale).astype(BF16)


def _epi_channel_dft(split_rows, acc, c0, extras, outs):
    cs_ref = extras[0]
    ar_ref, ai_ref = outs
    tm = acc.shape[0]
    for gl in range(acc.shape[1] // FNET_GROUP_DIM):
        g = c0 // FNET_GROUP_DIM + gl
        ub = acc[:, gl * FNET_GROUP_DIM:(gl + 1) * FNET_GROUP_DIM].astype(BF16)
        ab = jnp.dot(ub, cs_ref[g], preferred_element_type=F32)
        if split_rows:
            ab4 = ab.reshape(tm // DFT_ROWS, 2, DFT_ROWS // 2, ab.shape[1])
            ab = jnp.concatenate([ab4[:, hf].reshape(tm // 2, ab.shape[1]) for hf in range(2)], axis=0)
        ab = ab.astype(BF16)
        ar_ref[2 * g] = ab[:, 0:128]
        ar_ref[2 * g + 1] = ab[:, 128:256]
        ai_ref[2 * g] = ab[:, 256:384]
        ai_ref[2 * g + 1] = ab[:, 384:512]


def _store_tile(out_ref, col, y, dup_heads):
    if not dup_heads:
        out_ref[:, col:col + LANES] = y.astype(BF16)
        return
    lane = lax.broadcasted_iota(jnp.int32, (1, LANES), 1)
    lo = lane < HEAD_DIM
    swapped = pltpu.roll(y, HEAD_DIM, 1)
    out_ref[:, 2 * col:2 * col + LANES] = jnp.where(lo, y, swapped).astype(BF16)
    out_ref[:, 2 * col + LANES:2 * col + 2 * LANES] = jnp.where(lo, swapped, y).astype(BF16)


def _epi_rope(dup_heads, acc, c0, extras, outs):
    cos = extras[0][...]
    sin = extras[1][...]
    lane = lax.broadcasted_iota(jnp.int32, (1, LANES), 1)
    first = (lane % 32) < 16
    for t in range(acc.shape[1] // LANES):
        x = acc[:, t * LANES:(t + 1) * LANES]
        partner = jnp.where(first, pltpu.roll(x, LANES - 16, 1), pltpu.roll(x, 16, 1))
        _store_tile(outs[0], c0 + t * LANES, x * cos + partner * sin, dup_heads)


def _epi_cast_dup(acc, c0, extras, outs):
    for t in range(acc.shape[1] // LANES):
        _store_tile(outs[0], c0 + t * LANES, acc[:, t * LANES:(t + 1) * LANES], True)


def _proj_simple(h, w, col0, ncols, epilogue, name, tm, tn=PROJ_TN, w_layer=None, out_mult=1):
    m = h.shape[0]
    tn = min(tn, ncols)
    return _proj(h, w, col0, ncols, tm, tn, epilogue, (), (),
                 jax.ShapeDtypeStruct((m, out_mult * ncols), BF16),
                 pl.BlockSpec((tm, out_mult * tn), lambda i, j: (i, j)), name, w_layer=w_layer)


def _proj_rope(h, w, col0, ncols, cos_t, sin_t, seq, name, tm, tn=PROJ_TN, w_layer=None,
               dup_heads=False):
    m = h.shape[0]
    tn = min(tn, ncols)
    per_batch = seq // tm
    out_mult = 2 if dup_heads else 1
    tab_spec = pl.BlockSpec((tm, LANES), lambda i, j: (i % per_batch, 0))
    return _proj(h, w, col0, ncols, tm, tn, functools.partial(_epi_rope, dup_heads),
                 (cos_t, sin_t), (tab_spec, tab_spec),
                 jax.ShapeDtypeStruct((m, out_mult * ncols), BF16),
                 pl.BlockSpec((tm, out_mult * tn), lambda i, j: (i, j)), name, w_layer=w_layer)


def _proj_silu_slabs(h, w, col0, name, tm, tn=PROJ_TN, w_layer=None):
    m = h.shape[0]
    return _proj(h, w, col0, D_BRANCH, tm, tn, _epi_silu_slabs, (), (),
                 jax.ShapeDtypeStruct((SLABS, m, LANES), BF16),
                 pl.BlockSpec((tn // LANES, tm, LANES), lambda i, j: (j, i, 0)), name, w_layer=w_layer)


def _proj_channel_dft(h, w, cs, name, tm, seq=None, tn=PROJ_TN):
    m = h.shape[0]
    lhs_spec = None
    if seq is not None:
        na, tiles = seq // DFT_NB, DFT_NB // DFT_ROWS
        assert tm == na * DFT_ROWS
        h = h.reshape(m // seq, na, DFT_NB, D_MODEL)
        lhs_spec = pl.BlockSpec((None, na, DFT_ROWS, D_MODEL), lambda i, j: (i // tiles, 0, i % tiles, 0))
    slab_shape = jax.ShapeDtypeStruct((SLABS, m, LANES), BF16)
    slab_spec = pl.BlockSpec((tn // LANES, tm, LANES), lambda i, j: (j, i, 0))
    groups = tn // FNET_GROUP_DIM
    cs_spec = pl.BlockSpec((groups,) + cs.shape[1:], lambda i, j: (j, 0, 0))
    return _proj(h, w, 0, D_BRANCH, tm, tn, functools.partial(_epi_channel_dft, seq is not None), (cs,), (cs_spec,),
                 (slab_shape, slab_shape), (slab_spec, slab_spec), name, lhs_spec=lhs_spec,
                 chunk=2 * MXU_COLS)


def _fold_mix_kernel(c_ref, s_ref, wm_ref, o_ref):
    wm = wm_ref[...]
    gd = FNET_GROUP_DIM
    hp = lax.Precision.HIGHEST
    o_ref[:, 0:gd] = jnp.dot(c_ref[...], wm, preferred_element_type=F32, precision=hp).astype(BF16)
    o_ref[:, gd:2 * gd] = jnp.dot(s_ref[...], wm, preferred_element_type=F32, precision=hp).astype(BF16)


def _fold_mix(w_mix):
    gd = FNET_GROUP_DIM
    cs = _channel_dft_matrix()
    mat = pl.BlockSpec((gd, gd), lambda g: (0, 0))
    return pl.pallas_call(
        _fold_mix_kernel,
        grid=(FNET_GROUPS,),
        in_specs=[mat, mat, pl.BlockSpec((None, gd, gd), lambda g: (g, 0, 0))],
        out_specs=pl.BlockSpec((None, gd, 2 * gd), lambda g: (g, 0, 0)),
        out_shape=jax.ShapeDtypeStruct((FNET_GROUPS, gd, 2 * gd), BF16),
        compiler_params=_cparams(1),
        name="fnet_fold_mix",
    )(jnp.asarray(cs[:, :gd]), jnp.asarray(cs[:, gd:]), w_mix)


def _dft_kernel(na, pitch, gpitch, ar_ref, ai_ref, zg_ref, fa_ref, fb_ref, y_ref,
                xr_ref, xi_ref, er_ref, ei_ref, g_ref):
    half_rows = na * (DFT_ROWS // 2)
    xr_ref[...] = ar_ref[...].astype(F32)
    xi_ref[...] = ai_ref[...].astype(F32)

    def gather_a(b):
        start = (b // (DFT_ROWS // 2)) * half_rows + b % (DFT_ROWS // 2)
        zr = xr_ref[pl.ds(start, na, stride=DFT_ROWS // 2), :]
        zi = xi_ref[pl.ds(start, na, stride=DFT_ROWS // 2), :]
        return jnp.concatenate([zr, zi], axis=0).astype(BF16)

    def stage_a(pair, carry):
        b0 = 2 * pair
        d0, d1 = gather_a(b0), gather_a(b0 + 1)
        zero = jnp.zeros_like(d0)
        rhs = jnp.concatenate([jnp.concatenate([d0, zero], axis=1),
                               jnp.concatenate([zero, d1], axis=1)], axis=0)
        e = jnp.dot(fa_ref[pair], rhs, preferred_element_type=F32)
        for j in range(2):
            off = pl.multiple_of((b0 + j) * pitch, 8)
            er_ref[pl.ds(off, na), :] = e[:na, j * LANES:(j + 1) * LANES]
            ei_ref[pl.ds(off, na), :] = e[na:, j * LANES:(j + 1) * LANES]
        return carry

    lax.fori_loop(0, DFT_NB // 2, stage_a, 0, unroll=64)

    def gather_b(ka):
        er = er_ref[pl.ds(ka, DFT_NB, stride=pitch), :]
        ei = ei_ref[pl.ds(ka, DFT_NB, stride=pitch), :]
        return jnp.concatenate([er, ei], axis=0).astype(BF16)

    def stage_b(pair, carry):
        ka0 = 2 * pair
        rhs = jnp.concatenate([gather_b(ka0), gather_b(ka0 + 1)], axis=1)
        g = jnp.dot(fb_ref[...], rhs, preferred_element_type=F32)
        for j in range(2):
            off = pl.multiple_of((ka0 + j) * gpitch, 8)
            g_ref[pl.ds(off, DFT_NB), :] = g[:, j * LANES:(j + 1) * LANES]
        return carry

    lax.fori_loop(0, na // 2, stage_b, 0, unroll=32)

    def gate(kb, carry):
        rows = pl.ds(pl.multiple_of(kb * na, na), na)
        g = g_ref[pl.ds(kb, na, stride=gpitch), :]
        y_ref[rows, :] = (g * zg_ref[rows, :].astype(F32)).astype(BF16)
        return carry

    lax.fori_loop(0, DFT_NB, gate, 0, unroll=32)


def _position_dft(ar, ai, zg, batch, seq):
    na = seq // DFT_NB
    pitch = na + 8
    gpitch = DFT_NB + 8
    fa_np, fb_np = _position_dft_matrices(seq)
    fa = jnp.asarray(fa_np).astype(BF16)
    fb = jnp.asarray(fb_np).astype(BF16)
    slab = pl.BlockSpec((None, seq, LANES), lambda s, b: (s, b, 0))
    return pl.pallas_call(
        functools.partial(_dft_kernel, na, pitch, gpitch),
        grid=(SLABS, batch),
        in_specs=[slab, slab, slab,
                  pl.BlockSpec(fa.shape, lambda s, b: (0, 0, 0)),
                  pl.BlockSpec(fb.shape, lambda s, b: (0, 0))],
        out_specs=slab,
        out_shape=jax.ShapeDtypeStruct(ar.shape, BF16),
        scratch_shapes=[pltpu.VMEM((seq, LANES), F32),
                        pltpu.VMEM((seq, LANES), F32),
                        pltpu.VMEM((DFT_NB * pitch, LANES), F32),
                        pltpu.VMEM((DFT_NB * pitch, LANES), F32),
                        pltpu.VMEM((na * gpitch, LANES), F32)],
        compiler_params=_cparams(2),
        name="position_dft",
    )(ar, ai, zg, fa, fb)


def _dft_dense_kernel(ar_ref, ai_ref, zg_ref, fd_ref, y_ref):
    n = ar_ref.shape[0]
    d = jnp.concatenate([jnp.concatenate([ar_ref[s] for s in range(n)], axis=1),
                         jnp.concatenate([ai_ref[s] for s in range(n)], axis=1)], axis=0).astype(BF16)
    g = jnp.dot(fd_ref[...], d, preferred_element_type=F32)
    for s in range(n):
        y_ref[s] = (g[:, s * LANES:(s + 1) * LANES] * zg_ref[s].astype(F32)).astype(BF16)


def _position_dft_dense(ar, ai, zg, batch, seq, slabs_per_step=8):
    fd = jnp.asarray(_dense_dft_matrix(seq)).astype(BF16)
    slab = pl.BlockSpec((slabs_per_step, seq, LANES), lambda s, b: (s, b, 0))
    return pl.pallas_call(
        _dft_dense_kernel,
        grid=(SLABS // slabs_per_step, batch),
        in_specs=[slab, slab, slab, pl.BlockSpec(fd.shape, lambda s, b: (0, 0))],
        out_specs=slab,
        out_shape=jax.ShapeDtypeStruct(ar.shape, BF16),
        compiler_params=_cparams(2),
        name="position_dft_dense",
    )(ar, ai, zg, fd)


def _attn_kernel(n_band, nblk, sink_ref, q_ref, *refs):
    k_refs = refs[:n_band + 1]
    v_refs = refs[n_band + 1:2 * n_band + 2]
    zg_ref, y_ref = refs[2 * n_band + 2:]
    i = pl.program_id(2)
    blk = ATTN_BLOCK
    n_pair = Q_GROUP // 2
    qw = Q_GROUP * HEAD_DIM

    lane = lax.broadcasted_iota(jnp.int32, (1, LANES), 1)
    lo = lane < HEAD_DIM
    dn = (((1,), (1,)), ((), ()))
    if n_band:
        r = lax.broadcasted_iota(jnp.int32, (blk, blk), 0)
        c = lax.broadcasted_iota(jnp.int32, (blk, blk), 1)
        prev_ok = c >= r + jnp.where(i > 0, 0, blk)
        next_ok = c <= r - jnp.where(i < nblk - 1, 0, blk)

    def scores(hh):
        kl = slice(hh * LANES, (hh + 1) * LANES)
        keys = jnp.concatenate([ref[:, kl] for ref in k_refs], axis=0)
        q4 = jnp.concatenate([q_ref[:, hh * qw + t * LANES:hh * qw + (t + 1) * LANES]
                              for t in range(n_pair)], axis=0)
        qzero = jnp.zeros_like(q4)
        return [lax.dot_general(qm, keys, dn, preferred_element_type=F32)
                for qm in (jnp.where(lo, q4, qzero), jnp.where(lo, qzero, q4))]

    def finish(hh, s_both):
        h = pl.program_id(1) * ATTN_HEADS_PER_STEP + hh
        kl = slice(hh * LANES, (hh + 1) * LANES)
        vals = jnp.concatenate([ref[:, kl] for ref in v_refs], axis=0)
        ones = jnp.ones_like(vals)
        v_ext = (jnp.where(lo, vals, ones), jnp.where(lo, ones, vals))
        o_ext, sink_term = [], []
        for hd, s in enumerate(s_both):
            p_rows, sink_rows = [], []
            for t in range(n_pair):
                st = s[t * blk:(t + 1) * blk]
                parts = [st[:, j * blk:(j + 1) * blk] for j in range(st.shape[1] // blk)]
                if n_band:
                    parts[0] = jnp.where(prev_ok, parts[0], NEG_INF)
                    parts[2] = jnp.where(next_ok, parts[2], NEG_INF)
                sk = sink_ref[h, 2 * t + hd] * LOG2_E
                mx = parts[0]
                for part in parts[1:]:
                    mx = jnp.maximum(mx, part)
                mx = jnp.maximum(jnp.max(mx, axis=-1, keepdims=True), sk)
                p_rows.append(jnp.concatenate([jnp.exp2(part - mx).astype(BF16) for part in parts], axis=1))
                sink_rows.append(jnp.exp2(sk - mx))
            p = jnp.concatenate(p_rows, axis=0)
            o_ext.append(jnp.dot(p, v_ext[hd], preferred_element_type=F32))
            sink_term.append(jnp.concatenate(sink_rows, axis=0))
        o = jnp.where(lo, o_ext[0], o_ext[1])
        denom = jnp.where(lo, pltpu.roll(o_ext[0], HEAD_DIM, 1) + sink_term[0],
                          pltpu.roll(o_ext[1], HEAD_DIM, 1) + sink_term[1])
        res = o / denom
        for t in range(n_pair):
            tile = slice(hh * qw + t * LANES, hh * qw + (t + 1) * LANES)
            y_ref[:, tile] = (res[t * blk:(t + 1) * blk] * zg_ref[:, tile].astype(F32)).astype(BF16)

    pending = scores(0)
    for hh in range(ATTN_HEADS_PER_STEP):
        upcoming = scores(hh + 1) if hh + 1 < ATTN_HEADS_PER_STEP else None
        finish(hh, pending)
        pending = upcoming


def _attention(q, kd, vd, kcd, vcd, zg, sink, batch, seq, ctx_len, use_band):
    blk = ATTN_BLOCK
    nblk = seq // blk
    hps = ATTN_HEADS_PER_STEP
    qw = hps * Q_GROUP * HEAD_DIM
    kw = hps * LANES
    q_spec = pl.BlockSpec((blk, qw), lambda b, h, i: (b * nblk + i, h))
    ctx_spec = pl.BlockSpec((ctx_len, kw), lambda b, h, i: (b, h))
    if use_band:
        def band(delta):
            return pl.BlockSpec(
                (blk, kw), lambda b, h, i: (b * nblk + jnp.clip(i + delta, 0, nblk - 1), h))
        k_specs = [band(-1), band(0), band(1), ctx_spec]
        k_args, v_args = [kd, kd, kd, kcd], [vd, vd, vd, vcd]
        n_band = 3
    else:
        k_specs, k_args, v_args, n_band = [ctx_spec], [kcd], [vcd], 0
    return pl.pallas_call(
        functools.partial(_attn_kernel, n_band, nblk),
        grid=(batch, KV_HEADS // hps, nblk),
        in_specs=[pl.BlockSpec(memory_space=pltpu.SMEM), q_spec] + k_specs + k_specs + [q_spec],
        out_specs=q_spec,
        out_shape=jax.ShapeDtypeStruct(q.shape, BF16),
        compiler_params=_cparams(3),
        name="attention_band" if use_band else "attention_ctx",
    )(sink, q, *k_args, *v_args, zg)


def _gelu_stats_kernel(chunk, a_ref, w_ref, gv_ref, mu_ref, rstd_ref):
    a = a_ref[...]
    n = w_ref.shape[1]
    mean = m2 = None
    for idx, c0 in enumerate(range(0, n, chunk)):
        ge = _gelu_tanh(jnp.dot(a, w_ref[:, c0:c0 + chunk], preferred_element_type=F32))
        gv_ref[:, c0:c0 + chunk] = ge.astype(BF16)
        cmean = jnp.mean(ge, axis=-1, keepdims=True)
        d = ge - cmean
        cm2 = jnp.sum(d * d, axis=-1, keepdims=True)
        if idx == 0:
            mean, m2 = cmean, cm2
        else:
            delta = cmean - mean
            mean = mean + delta * (1.0 / (idx + 1))
            m2 = m2 + cm2 + delta * delta * (chunk * idx / (idx + 1))
    mu_ref[...] = mean
    rstd_ref[...] = lax.rsqrt(m2 * (1.0 / n) + EPS)


def _proj_gelu_stats(h, w, col0, tm, chunk=2 * MXU_COLS):
    m, k = h.shape
    n = D_BRANCH
    col = pl.BlockSpec((tm, 1), lambda i: (i, 0))
    stat = jax.ShapeDtypeStruct((m, 1), F32)
    return pl.pallas_call(
        functools.partial(_gelu_stats_kernel, chunk),
        grid=(m // tm,),
        in_specs=[pl.BlockSpec((tm, k), lambda i: (i, 0)),
                  pl.BlockSpec((k, n), lambda i: (0, col0 // n), pipeline_mode=pl.Buffered(1))],
        out_specs=(pl.BlockSpec((tm, n), lambda i: (i, 0)), col, col),
        out_shape=(jax.ShapeDtypeStruct((m, n), BF16), stat, stat),
        compiler_params=_cparams(1),
        name="gmlp_in_v",
    )(h, w)


def _sgu_kernel(n_chunks, n_groups, uz_ref, gv_ref, mu_ref, rstd_ref, lg_ref, lb_ref,
                ws_ref, bs_ref, y_ref):
    gd = D_BRANCH // GMLP_GROUPS
    mu = mu_ref[...]
    rstd = rstd_ref[...]
    for gl in range(n_groups):
        cols = slice(gl * gd, (gl + 1) * gd)
        ws = ws_ref[gl]
        bs = bs_ref[gl]
        lg = lg_ref[:, cols]
        lb = lb_ref[:, cols]
        for c in range(n_chunks):
            rows = slice(c * GMLP_CHUNK, (c + 1) * GMLP_CHUNK)
            vn = ((gv_ref[rows, cols].astype(F32) - mu[rows]) * rstd[rows] * lg + lb).astype(BF16)
            s = jnp.dot(ws, vn, preferred_element_type=F32) + bs
            y_ref[rows, cols] = (uz_ref[rows, cols].astype(F32) * s).astype(BF16)


def _spatial_gate(uz, gv, mu, rstd, ln_g, ln_b, w_s, b_s, tm, n_groups=4):
    m = uz.shape[0]
    gd = D_BRANCH // GMLP_GROUPS
    tn = n_groups * gd
    tile = pl.BlockSpec((tm, tn), lambda r, g: (r, g))
    col = pl.BlockSpec((tm, 1), lambda r, g: (r, 0))
    vec = pl.BlockSpec((1, tn), lambda r, g: (0, g))
    return pl.pallas_call(
        functools.partial(_sgu_kernel, tm // GMLP_CHUNK, n_groups),
        grid=(m // tm, GMLP_GROUPS // n_groups),
        in_specs=[tile, tile, col, col, vec, vec,
                  pl.BlockSpec((n_groups, GMLP_CHUNK, GMLP_CHUNK), lambda r, g: (g, 0, 0)),
                  pl.BlockSpec((n_groups, GMLP_CHUNK, 1), lambda r, g: (g, 0, 0))],
        out_specs=tile,
        out_shape=jax.ShapeDtypeStruct(uz.shape, BF16),
        compiler_params=_cparams(2),
        name="gmlp_spatial_gate",
    )(uz, gv, mu, rstd, ln_g.reshape(1, D_BRANCH), ln_b.reshape(1, D_BRANCH),
      w_s, b_s.reshape(GMLP_GROUPS, GMLP_CHUNK, 1))


def _wout_kernel(mode, y_ref, w_ref, x_ref, gate_ref, *refs):
    if len(y_ref.shape) == 3:
        y = jnp.concatenate([y_ref[s] for s in range(y_ref.shape[0])], axis=1)
    else:
        y = y_ref[...]
    acc = jnp.dot(y, w_ref[...], preferred_element_type=F32)
    xn = x_ref[...] + gate_ref[...] * acc
    if mode == "final":
        g_ref, o_ref = refs
        o_ref[...] = xn * lax.rsqrt(jnp.mean(xn * xn, axis=-1, keepdims=True) + EPS) * g_ref[...]
    elif mode == "next":
        g_ref, sc_ref, sh_ref, xo_ref, h_ref = refs
        xo_ref[...] = xn
        h_ref[...] = _mod_rmsnorm(xn, g_ref[...], sc_ref[...], sh_ref[...]).astype(BF16)
    else:
        refs[0][...] = xn


def _out_proj(y, w_out, x2, mods, layer, who, mode, norm_vec, tm):
    m = x2.shape[0]
    row = pl.BlockSpec((tm, D_MODEL), lambda r: (r, 0))
    if y.ndim == 3:
        y_spec = pl.BlockSpec((SLABS, tm, LANES), lambda r: (0, r, 0))
    else:
        y_spec = pl.BlockSpec((tm, D_BRANCH), lambda r: (r, 0))
    in_specs = [y_spec,
                pl.BlockSpec((None, D_BRANCH, D_MODEL), lambda r: (layer, 0, 0),
                             pipeline_mode=pl.Buffered(1)),
                row, _mod_spec(layer, 2, who)]
    args = [y, w_out, x2, mods]
    xs = jax.ShapeDtypeStruct((m, D_MODEL), F32)
    if mode == "final":
        in_specs.append(pl.BlockSpec((1, D_MODEL), lambda r: (0, 0)))
        args.append(norm_vec.reshape(1, D_MODEL))
        out_shape, out_specs = xs, row
    elif mode == "next":
        in_specs += [_row_spec(layer + 1), _mod_spec(layer + 1, 1, who), _mod_spec(layer + 1, 0, who)]
        args += [norm_vec, mods, mods]
        out_shape = (xs, jax.ShapeDtypeStruct((m, D_MODEL), BF16))
        out_specs = (row, row)
    else:
        out_shape, out_specs = xs, row
    return pl.pallas_call(
        functools.partial(_wout_kernel, mode),
        grid=(m // tm,),
        in_specs=in_specs,
        out_specs=out_specs,
        out_shape=out_shape,
        compiler_params=_cparams(1),
        name="out_proj_" + mode,
    )(*args)


def kernel(x, c, ctx, c_ctx, norm_g, ada_w, ada_b, w_out, fnet_w_in, fnet_w_mix, attn_w_in, attn_sink,
           gmlp_w_in, gmlp_w_s, gmlp_b_s, gmlp_ln_g, gmlp_ln_b, final_g):
    batch, seq, d = x.shape
    ctx_len = ctx.shape[1]
    assert d == D_MODEL and seq % (DFT_NB * 8) == 0 and seq % GRID_W == 0 and batch < 8
    m_lat, m_ctx = batch * seq, batch * ctx_len
    tm_lat = min(PROJ_TM, seq)
    tm_ctx = m_ctx
    tm_out = OUT_TM

    def who_lat(tm):
        return lambda r: (r * tm) // seq

    who_ctx = lambda r: batch

    cvec = jnp.zeros((8, d), F32).at[:batch].set(c).at[batch].set(c_ctx)
    mods = _mods(cvec, ada_w, ada_b).reshape(DEPTH, 8, 3, 1, d)
    norm_g3 = norm_g.reshape(DEPTH, 1, d)
    w_out_b = w_out.astype(BF16)

    xl = x.reshape(m_lat, d)
    xc = ctx.reshape(m_ctx, d)
    hl = _prenorm(xl, norm_g3, mods, 0, who_lat(NORM_TM), NORM_TM)
    hc = _prenorm(xc, norm_g3, mods, 0, who_ctx, tm_ctx)


    def finish(y, x2, layer, who_fn, need_next, is_final):
        if is_final:
            return _out_proj(y, w_out_b, x2, mods, layer, who_fn, "final", final_g, tm_out), None
        if need_next:
            return _out_proj(y, w_out_b, x2, mods, layer, who_fn, "next", norm_g3, tm_out)
        return _out_proj(y, w_out_b, x2, mods, layer, who_fn, "plain", None, tm_out), None

    out = None
    for i in range(DEPTH):
        kind, j = i % 3, i // 3
        need_ctx = i < DEPTH - 1
        last = i == DEPTH - 1
        if kind == 0:
            cs = _fold_mix(fnet_w_mix[j])
            tm_dft = (seq // DFT_NB) * DFT_ROWS
            w_u = fnet_w_in[j, :, :D_BRANCH].astype(BF16)
            ar, ai = _proj_channel_dft(hl, w_u, cs, "fnet_in_u", tm_dft, seq=seq)
            zg = _proj_silu_slabs(hl, fnet_w_in, D_BRANCH, "fnet_in_z", tm_lat, w_layer=j)
            y = _position_dft(ar, ai, zg, batch, seq)
            yc = None
            if need_ctx:
                ar, ai = _proj_channel_dft(hc, w_u, cs, "fnet_in_u_ctx", tm_ctx)
                zg = _proj_silu_slabs(hc, fnet_w_in, D_BRANCH, "fnet_in_z_ctx", tm_ctx, w_layer=j)
                yc = _position_dft_dense(ar, ai, zg, batch, ctx_len)
        elif kind == 1:
            kvw = KV_HEADS * HEAD_DIM
            k0, v0, z0 = D_BRANCH, D_BRANCH + kvw, D_BRANCH + 2 * kvw
            sink = attn_sink[j].reshape(KV_HEADS, Q_GROUP)
            scale = HEAD_DIM ** -0.5 * LOG2_E
            q_tabs = [jnp.asarray(t) for t in _rope_tables(seq, scale)]
            k_tabs = [jnp.asarray(t) for t in _rope_tables(seq)]
            q = _proj_rope(hl, attn_w_in, 0, D_BRANCH, *q_tabs, seq, "attn_in_q", tm_lat, w_layer=j)
            kd = _proj_rope(hl, attn_w_in, k0, kvw, *k_tabs, seq, "attn_in_k", tm_lat, w_layer=j,
                            dup_heads=True)
            vd = _proj_simple(hl, attn_w_in, v0, kvw, _epi_cast_dup, "attn_in_v", tm_lat, w_layer=j, out_mult=2)
            zg = _proj_simple(hl, attn_w_in, z0, D_BRANCH, _epi_silu, "attn_in_z", tm_lat, w_layer=j)
            kcd = _proj_simple(hc, attn_w_in, k0, kvw, _epi_cast_dup, "attn_in_kc", tm_ctx, w_layer=j, out_mult=2)
            vcd = _proj_simple(hc, attn_w_in, v0, kvw, _epi_cast_dup, "attn_in_vc", tm_ctx, w_layer=j, out_mult=2)
            y = _attention(q, kd, vd, kcd, vcd, zg, sink, batch, seq, ctx_len, True)
            yc = None
            if need_ctx:
                qc = _proj_simple(hc, attn_w_in, 0, D_BRANCH, functools.partial(_epi_scale_cast, scale),
                                  "attn_in_qc", tm_ctx, w_layer=j)
                zgc = _proj_simple(hc, attn_w_in, z0, D_BRANCH, _epi_silu, "attn_in_zc", tm_ctx, w_layer=j)
                yc = _attention(qc, None, None, kcd, vcd, zgc, sink, batch, ctx_len, ctx_len, False)
        else:
            w_v = gmlp_w_in[j, :, D_BRANCH:2 * D_BRANCH].astype(BF16)
            ws = gmlp_w_s[j].astype(BF16)

            def gmlp_branch(h, tm):
                uz = _proj_simple(h, gmlp_w_in, (0, 2 * D_BRANCH), D_BRANCH, _epi_gelu_times_silu,
                                  "gmlp_in_uz", tm, tn=PROJ_TN // 2, w_layer=j)
                gv, mu, rstd = _proj_gelu_stats(h, w_v, 0, min(tm, STATS_TM))
                return _spatial_gate(uz, gv, mu, rstd, gmlp_ln_g[j], gmlp_ln_b[j], ws, gmlp_b_s[j],
                                     min(tm, GATE_TM), n_groups=GMLP_GROUPS)

            y = gmlp_branch(hl, tm_lat)
            yc = gmlp_branch(hc, tm_ctx) if need_ctx else None

        res, hl = finish(y, xl, i, who_lat(tm_out), not last, last)
        if last:
            out = res
        else:
            xl = res
        if need_ctx:
            xc, hc = finish(yc, xc, i, who_ctx, i + 1 < DEPTH - 1 or (i + 1) % 3 == 1, False)
    return out.reshape(batch, seq, d)
```

```python
import functools
import math

import numpy as np
import jax
import jax.numpy as jnp
from jax import lax
from jax.experimental import pallas as pl
from jax.experimental.pallas import tpu as pltpu

F32 = jnp.float32
BF16 = jnp.bfloat16

D_MODEL = 2048
D_BRANCH = 4096
DEPTH = 4
GRID_W = 64
FNET_GROUPS = 16
FNET_GROUP_DIM = 256
HEAD_DIM = 64
KV_HEADS = 8
Q_GROUP = 8
ATTN_BLOCK = 128
ATTN_HEADS_PER_STEP = 8
ROPE_BASE = 10000.0
GMLP_CHUNK = 128
GMLP_GROUPS = 16
EPS = 1e-6
NEG_INF = -1e30
LOG2_E = math.log2(math.e)

LANES = 128
MXU_COLS = 256
PROJ_TN = 1024
PROJ_TM = 2048
OUT_TM = 512
STATS_TM = 256
GATE_TM = 256
NORM_TM = 512
DFT_NB = 128
DFT_ROWS = 16
SLABS = D_BRANCH // LANES
VMEM_LIMIT = 56 * 1024 * 1024


def _cparams(n_axes, vmem=VMEM_LIMIT):
    return pltpu.CompilerParams(dimension_semantics=("arbitrary",) * n_axes,
                                vmem_limit_bytes=vmem)


def _silu(z):
    return 0.5 * z * (1.0 + jnp.tanh(0.5 * z))


def _gelu_tanh(x):
    c = math.sqrt(2.0 / math.pi)
    return 0.5 * x * (1.0 + jnp.tanh(c * (x + 0.044715 * (x * x * x))))


def _mod_rmsnorm(x, g, scale, shift):
    y = x * lax.rsqrt(jnp.mean(x * x, axis=-1, keepdims=True) + EPS) * g
    return y * (1.0 + scale) + shift


def _channel_dft_matrix():
    n = FNET_GROUP_DIM
    k = np.arange(n, dtype=np.float64)
    ang = 2.0 * np.pi * np.outer(k, k) / n
    s = 1.0 / math.sqrt(n)
    return np.concatenate([np.cos(ang) * s, -np.sin(ang) * s], axis=1).astype(np.float32)


def _position_dft_matrices(seq):
    na, nb = seq // DFT_NB, DFT_NB
    a = np.arange(na, dtype=np.float64)
    b = np.arange(nb, dtype=np.float64)
    ang = 2.0 * np.pi * (a[None, None, :] * a[None, :, None] / na + b[:, None, None] * a[None, :, None] / seq)
    mr = np.cos(ang) / math.sqrt(na)
    mi = -np.sin(ang) / math.sqrt(na)
    fa = np.concatenate([np.concatenate([mr, -mi], axis=2), np.concatenate([mi, mr], axis=2)], axis=1)
    fa = np.concatenate([fa[0::2], fa[1::2]], axis=2)
    angb = 2.0 * np.pi * np.outer(b, b) / nb
    fb = np.concatenate([np.cos(angb), np.sin(angb)], axis=1) / math.sqrt(nb)
    return fa.astype(np.float32), fb.astype(np.float32)


def _dense_dft_matrix(n):
    k = np.arange(n, dtype=np.float64)
    ang = 2.0 * np.pi * np.outer(k, k) / n
    return (np.concatenate([np.cos(ang), np.sin(ang)], axis=1) / math.sqrt(n)).astype(np.float32)


def _rope_tables(seq, scale=1.0):
    nf = HEAD_DIM // 4
    inv = ROPE_BASE ** (-np.arange(nf, dtype=np.float64) / nf)
    t = np.arange(seq)
    rows = (t // GRID_W).astype(np.float64)
    cols = (t % GRID_W).astype(np.float64)
    parts_c, parts_s = [], []
    for pos in (rows, cols):
        ang = pos[:, None] * inv[None, :]
        parts_c += [np.cos(ang), np.cos(ang)]
        parts_s += [-np.sin(ang), np.sin(ang)]
    cos = np.concatenate(parts_c, axis=1) * scale
    sin = np.concatenate(parts_s, axis=1) * scale
    reps = LANES // HEAD_DIM
    return (np.tile(cos, (1, reps)).astype(np.float32), np.tile(sin, (1, reps)).astype(np.float32))


def _mods_kernel(cv_ref, w_ref, b_ref, o_ref):
    a = _silu(cv_ref[...])
    a_hi = a.astype(BF16)
    a_lo = (a - a_hi.astype(F32)).astype(BF16)
    w = w_ref[...].astype(BF16)
    part = jnp.dot(a_hi, w, preferred_element_type=F32) + jnp.dot(a_lo, w, preferred_element_type=F32)

    @pl.when(pl.program_id(1) == 0)
    def _():
        o_ref[...] = part + b_ref[...]

    @pl.when(pl.program_id(1) > 0)
    def _():
        o_ref[...] += part


def _mods(cvec, ada_w, ada_b):
    depth, d, n3 = ada_w.shape
    tk = 512
    return pl.pallas_call(
        _mods_kernel,
        grid=(depth, d // tk),
        in_specs=[pl.BlockSpec((8, tk), lambda i, k: (0, k)),
                  pl.BlockSpec((None, tk, n3), lambda i, k: (i, k, 0)),
                  pl.BlockSpec((None, 1, n3), lambda i, k: (i, 0, 0))],
        out_specs=pl.BlockSpec((None, 8, n3), lambda i, k: (i, 0, 0)),
        out_shape=jax.ShapeDtypeStruct((depth, 8, n3), F32),
        compiler_params=_cparams(2),
        name="ada_mods",
    )(cvec, ada_w, ada_b.reshape(depth, 1, n3))


def _mod_spec(layer, kind, who_of_row):
    return pl.BlockSpec((None, None, None, 1, D_MODEL),
                        lambda r, *_: (layer, who_of_row(r), kind, 0, 0))


def _row_spec(vec_layer):
    return pl.BlockSpec((None, 1, D_MODEL), lambda r, *_: (vec_layer, 0, 0))


def _prenorm_kernel(x_ref, g_ref, sc_ref, sh_ref, h_ref):
    h_ref[...] = _mod_rmsnorm(x_ref[...], g_ref[...], sc_ref[...], sh_ref[...]).astype(BF16)


def _prenorm(x2, norm_g3, mods, layer, who, tm):
    m = x2.shape[0]
    return pl.pallas_call(
        _prenorm_kernel,
        grid=(m // tm,),
        in_specs=[pl.BlockSpec((tm, D_MODEL), lambda r: (r, 0)),
                  _row_spec(layer), _mod_spec(layer, 1, who), _mod_spec(layer, 0, who)],
        out_specs=pl.BlockSpec((tm, D_MODEL), lambda r: (r, 0)),
        out_shape=jax.ShapeDtypeStruct((m, D_MODEL), BF16),
        compiler_params=_cparams(1),
        name="prenorm",
    )(x2, norm_g3, mods, mods)


def _proj_kernel(epilogue, n_extra, chunk, n_w, a_ref, *rest):
    w_refs, rest = rest[:n_w], rest[n_w:]
    if len(a_ref.shape) == 3:
        a_flat = rest[-1]
        rest = rest[:-1]

        @pl.when(pl.program_id(1) == 0)
        def _():
            a_flat[...] = a_ref[...].reshape(a_flat.shape)

        a = a_flat[...]
    else:
        a = a_ref[...]
    for c0 in range(0, w_refs[0].shape[1], chunk):
        accs = [jnp.dot(a, w_ref[:, c0:c0 + chunk].astype(BF16), preferred_element_type=F32)
                for w_ref in w_refs]
        epilogue(accs[0] if n_w == 1 else accs, c0, rest[:n_extra], rest[n_extra:])


def _proj(h, w, col0, ncols, tm, tn, epilogue, extras, extra_specs, out_shapes, out_specs, name,
          lhs_spec=None, chunk=MXU_COLS, w_layer=None):
    k = h.shape[-1]
    m = h.size // k
    col0s = col0 if isinstance(col0, tuple) else (col0,)
    scratch = [pltpu.VMEM((tm, k), h.dtype)] if lhs_spec is not None else []
    if lhs_spec is None:
        lhs_spec = pl.BlockSpec((tm, k), lambda i, j: (i, 0))

    def w_spec(off):
        if w.ndim == 3:
            return pl.BlockSpec((None, k, tn), lambda i, j: (w_layer, 0, j + off))
        return pl.BlockSpec((k, tn), lambda i, j: (0, j + off))

    return pl.pallas_call(
        functools.partial(_proj_kernel, epilogue, len(extras), min(chunk, tn), len(col0s)),
        grid=(m // tm, ncols // tn),
        in_specs=[lhs_spec] + [w_spec(c // tn) for c in col0s] + list(extra_specs),
        out_specs=out_specs,
        out_shape=out_shapes,
        scratch_shapes=scratch,
        compiler_params=_cparams(2),
        name=name,
    )(h, *([w] * len(col0s)), *extras)


def _epi_silu(acc, c0, extras, outs):
    outs[0][:, c0:c0 + acc.shape[1]] = _silu(acc).astype(BF16)


def _epi_silu_slabs(acc, c0, extras, outs):
    for t in range(acc.shape[1] // LANES):
        outs[0][c0 // LANES + t] = _silu(acc[:, t * LANES:(t + 1) * LANES]).astype(BF16)


def _epi_gelu_times_silu(accs, c0, extras, outs):
    acc_u, acc_z = accs
    outs[0][:, c0:c0 + acc_u.shape[1]] = (_gelu_tanh(acc_u) * _silu(acc_z)).astype(BF16)


def _epi_scale_cast(scale, acc, c0, extras, outs):
    outs[0][:, c0:c0 + acc.shape[1]] = (acc * scale).astype(BF16)


def _epi_channel_dft(split_rows, acc, c0, extras, outs):
    cs_ref = extras[0]
    ar_ref, ai_ref = outs
    tm = acc.shape[0]
    for gl in range(acc.shape[1] // FNET_GROUP_DIM):
        g = c0 // FNET_GROUP_DIM + gl
        ub = acc[:, gl * FNET_GROUP_DIM:(gl + 1) * FNET_GROUP_DIM].astype(BF16)
        ab = jnp.dot(ub, cs_ref[g], preferred_element_type=F32)
        if split_rows:
            ab4 = ab.reshape(tm // DFT_ROWS, 2, DFT_ROWS // 2, ab.shape[1])
            ab = jnp.concatenate([ab4[:, hf].reshape(tm // 2, ab.shape[1]) for hf in range(2)], axis=0)
        ab = ab.astype(BF16)
        ar_ref[2 * g] = ab[:, 0:128]
        ar_ref[2 * g + 1] = ab[:, 128:256]
        ai_ref[2 * g] = ab[:, 256:384]
        ai_ref[2 * g + 1] = ab[:, 384:512]


def _store_tile(out_ref, col, y, dup_heads):
    if not dup_heads:
        out_ref[:, col:col + LANES] = y.astype(BF16)
        return
    lane = lax.broadcasted_iota(jnp.int32, (1, LANES), 1)
    lo = lane < HEAD_DIM
    swapped = pltpu.roll(y, HEAD_DIM, 1)
    out_ref[:, 2 * col:2 * col + LANES] = jnp.where(lo, y, swapped).astype(BF16)
    out_ref[:, 2 * col + LANES:2 * col + 2 * LANES] = jnp.where(lo, swapped, y).astype(BF16)


def _epi_rope(dup_heads, acc, c0, extras, outs):
    cos = extras[0][...]
    sin = extras[1][...]
    lane = lax.broadcasted_iota(jnp.int32, (1, LANES), 1)
    first = (lane % 32) < 16
    for t in range(acc.shape[1] // LANES):
        x = acc[:, t * LANES:(t + 1) * LANES]
        partner = jnp.where(first, pltpu.roll(x, LANES - 16, 1), pltpu.roll(x, 16, 1))
        _store_tile(outs[0], c0 + t * LANES, x * cos + partner * sin, dup_heads)


def _epi_cast_dup(acc, c0, extras, outs):
    for t in range(acc.shape[1] // LANES):
        _store_tile(outs[0], c0 + t * LANES, acc[:, t * LANES:(t + 1) * LANES], True)


def _proj_simple(h, w, col0, ncols, epilogue, name, tm, tn=PROJ_TN, w_layer=None, out_mult=1):
    m = h.shape[0]
    tn = min(tn, ncols)
    return _proj(h, w, col0, ncols, tm, tn, epilogue, (), (),
                 jax.ShapeDtypeStruct((m, out_mult * ncols), BF16),
                 pl.BlockSpec((tm, out_mult * tn), lambda i, j: (i, j)), name, w_layer=w_layer)


def _proj_rope(h, w, col0, ncols, cos_t, sin_t, seq, name, tm, tn=PROJ_TN, w_layer=None,
               dup_heads=False):
    m = h.shape[0]
    tn = min(tn, ncols)
    per_batch = seq // tm
    out_mult = 2 if dup_heads else 1
    tab_spec = pl.BlockSpec((tm, LANES), lambda i, j: (i % per_batch, 0))
    return _proj(h, w, col0, ncols, tm, tn, functools.partial(_epi_rope, dup_heads),
                 (cos_t, sin_t), (tab_spec, tab_spec),
                 jax.ShapeDtypeStruct((m, out_mult * ncols), BF16),
                 pl.BlockSpec((tm, out_mult * tn), lambda i, j: (i, j)), name, w_layer=w_layer)


def _proj_silu_slabs(h, w, col0, name, tm, tn=PROJ_TN, w_layer=None):
    m = h.shape[0]
    return _proj(h, w, col0, D_BRANCH, tm, tn, _epi_silu_slabs, (), (),
                 jax.ShapeDtypeStruct((SLABS, m, LANES), BF16),
                 pl.BlockSpec((tn // LANES, tm, LANES), lambda i, j: (j, i, 0)), name, w_layer=w_layer)


def _proj_channel_dft(h, w, cs, name, tm, seq=None, tn=PROJ_TN):
    m = h.shape[0]
    lhs_spec = None
    if seq is not None:
        na, tiles = seq // DFT_NB, DFT_NB // DFT_ROWS
        assert tm == na * DFT_ROWS
        h = h.reshape(m // seq, na, DFT_NB, D_MODEL)
        lhs_spec = pl.BlockSpec((None, na, DFT_ROWS, D_MODEL), lambda i, j: (i // tiles, 0, i % tiles, 0))
    slab_shape = jax.ShapeDtypeStruct((SLABS, m, LANES), BF16)
    slab_spec = pl.BlockSpec((tn // LANES, tm, LANES), lambda i, j: (j, i, 0))
    groups = tn // FNET_GROUP_DIM
    cs_spec = pl.BlockSpec((groups,) + cs.shape[1:], lambda i, j: (j, 0, 0))
    return _proj(h, w, 0, D_BRANCH, tm, tn, functools.partial(_epi_channel_dft, seq is not None), (cs,), (cs_spec,),
                 (slab_shape, slab_shape), (slab_spec, slab_spec), name, lhs_spec=lhs_spec,
                 chunk=2 * MXU_COLS)


def _fold_mix_kernel(c_ref, s_ref, wm_ref, o_ref):
    wm = wm_ref[...]
    gd = FNET_GROUP_DIM
    hp = lax.Precision.HIGHEST
    o_ref[:, 0:gd] = jnp.dot(c_ref[...], wm, preferred_element_type=F32, precision=hp).astype(BF16)
    o_ref[:, gd:2 * gd] = jnp.dot(s_ref[...], wm, preferred_element_type=F32, precision=hp).astype(BF16)


def _fold_mix(w_mix):
    gd = FNET_GROUP_DIM
    cs = _channel_dft_matrix()
    mat = pl.BlockSpec((gd, gd), lambda g: (0, 0))
    return pl.pallas_call(
        _fold_mix_kernel,
        grid=(FNET_GROUPS,),
        in_specs=[mat, mat, pl.BlockSpec((None, gd, gd), lambda g: (g, 0, 0))],
        out_specs=pl.BlockSpec((None, gd, 2 * gd), lambda g: (g, 0, 0)),
        out_shape=jax.ShapeDtypeStruct((FNET_GROUPS, gd, 2 * gd), BF16),
        compiler_params=_cparams(1),
        name="fnet_fold_mix",
    )(jnp.asarray(cs[:, :gd]), jnp.asarray(cs[:, gd:]), w_mix)


def _dft_kernel(na, pitch, gpitch, ar_ref, ai_ref, zg_ref, fa_ref, fb_ref, y_ref,
                xr_ref, xi_ref, er_ref, ei_ref, g_ref):
    half_rows = na * (DFT_ROWS // 2)
    xr_ref[...] = ar_ref[...].astype(F32)
    xi_ref[...] = ai_ref[...].astype(F32)

    def gather_a(b):
        start = (b // (DFT_ROWS // 2)) * half_rows + b % (DFT_ROWS // 2)
        zr = xr_ref[pl.ds(start, na, stride=DFT_ROWS // 2), :]
        zi = xi_ref[pl.ds(start, na, stride=DFT_ROWS // 2), :]
        return jnp.concatenate([zr, zi], axis=0).astype(BF16)

    def stage_a(pair, carry):
        b0 = 2 * pair
        d0, d1 = gather_a(b0), gather_a(b0 + 1)
        zero = jnp.zeros_like(d0)
        rhs = jnp.concatenate([jnp.concatenate([d0, zero], axis=1),
                               jnp.concatenate([zero, d1], axis=1)], axis=0)
        e = jnp.dot(fa_ref[pair], rhs, preferred_element_type=F32)
        for j in range(2):
            off = pl.multiple_of((b0 + j) * pitch, 8)
            er_ref[pl.ds(off, na), :] = e[:na, j * LANES:(j + 1) * LANES]
            ei_ref[pl.ds(off, na), :] = e[na:, j * LANES:(j + 1) * LANES]
        return carry

    lax.fori_loop(0, DFT_NB // 2, stage_a, 0, unroll=64)

    def gather_b(ka):
        er = er_ref[pl.ds(ka, DFT_NB, stride=pitch), :]
        ei = ei_ref[pl.ds(ka, DFT_NB, stride=pitch), :]
        return jnp.concatenate([er, ei], axis=0).astype(BF16)

    def stage_b(pair, carry):
        ka0 = 2 * pair
        rhs = jnp.concatenate([gather_b(ka0), gather_b(ka0 + 1)], axis=1)
        g = jnp.dot(fb_ref[...], rhs, preferred_element_type=F32)
        for j in range(2):
            off = pl.multiple_of((ka0 + j) * gpitch, 8)
            g_ref[pl.ds(off, DFT_NB), :] = g[:, j * LANES:(j + 1) * LANES]
        return carry

    lax.fori_loop(0, na // 2, stage_b, 0, unroll=32)

    def gate(kb, carry):
        rows = pl.ds(pl.multiple_of(kb * na, na), na)
        g = g_ref[pl.ds(kb, na, stride=gpitch), :]
        y_ref[rows, :] = (g * zg_ref[rows, :].astype(F32)).astype(BF16)
        return carry

    lax.fori_loop(0, DFT_NB, gate, 0, unroll=32)


def _position_dft(ar, ai, zg, batch, seq):
    na = seq // DFT_NB
    pitch = na + 8
    gpitch = DFT_NB + 8
    fa_np, fb_np = _position_dft_matrices(seq)
    fa = jnp.asarray(fa_np).astype(BF16)
    fb = jnp.asarray(fb_np).astype(BF16)
    slab = pl.BlockSpec((None, seq, LANES), lambda s, b: (s, b, 0))
    return pl.pallas_call(
        functools.partial(_dft_kernel, na, pitch, gpitch),
        grid=(SLABS, batch),
        in_specs=[slab, slab, slab,
                  pl.BlockSpec(fa.shape, lambda s, b: (0, 0, 0)),
                  pl.BlockSpec(fb.shape, lambda s, b: (0, 0))],
        out_specs=slab,
        out_shape=jax.ShapeDtypeStruct(ar.shape, BF16),
        scratch_shapes=[pltpu.VMEM((seq, LANES), F32),
                        pltpu.VMEM((seq, LANES), F32),
                        pltpu.VMEM((DFT_NB * pitch, LANES), F32),
                        pltpu.VMEM((DFT_NB * pitch, LANES), F32),
                        pltpu.VMEM((na * gpitch, LANES), F32)],
        compiler_params=_cparams(2),
        name="position_dft",
    )(ar, ai, zg, fa, fb)


def _dft_dense_kernel(ar_ref, ai_ref, zg_ref, fd_ref, y_ref):
    n = ar_ref.shape[0]
    d = jnp.concatenate([jnp.concatenate([ar_ref[s] for s in range(n)], axis=1),
                         jnp.concatenate([ai_ref[s] for s in range(n)], axis=1)], axis=0).astype(BF16)
    g = jnp.dot(fd_ref[...], d, preferred_element_type=F32)
    for s in range(n):
        y_ref[s] = (g[:, s * LANES:(s + 1) * LANES] * zg_ref[s].astype(F32)).astype(BF16)


def _position_dft_dense(ar, ai, zg, batch, seq, slabs_per_step=8):
    fd = jnp.asarray(_dense_dft_matrix(seq)).astype(BF16)
    slab = pl.BlockSpec((slabs_per_step, seq, LANES), lambda s, b: (s, b, 0))
    return pl.pallas_call(
        _dft_dense_kernel,
        grid=(SLABS // slabs_per_step, batch),
        in_specs=[slab, slab, slab, pl.BlockSpec(fd.shape, lambda s, b: (0, 0))],
        out_specs=slab,
        out_shape=jax.ShapeDtypeStruct(ar.shape, BF16),
        compiler_params=_cparams(2),
        name="position_dft_dense",
    )(ar, ai, zg, fd)


def _attn_kernel(n_band, nblk, sink_ref, q_ref, *refs):
    k_refs = refs[:n_band + 1]
    v_refs = refs[n_band + 1:2 * n_band + 2]
    zg_ref, y_ref = refs[2 * n_band + 2:]
    i = pl.program_id(2)
    blk = ATTN_BLOCK
    n_pair = Q_GROUP // 2
    qw = Q_GROUP * HEAD_DIM

    lane = lax.broadcasted_iota(jnp.int32, (1, LANES), 1)
    lo = lane < HEAD_DIM
    dn = (((1,), (1,)), ((), ()))
    if n_band:
        r = lax.broadcasted_iota(jnp.int32, (blk, blk), 0)
        c = lax.broadcasted_iota(jnp.int32, (blk, blk), 1)
        prev_ok = c >= r + jnp.where(i > 0, 0, blk)
        next_ok = c <= r - jnp.where(i < nblk - 1, 0, blk)

    def scores(hh):
        kl = slice(hh * LANES, (hh + 1) * LANES)
        keys = jnp.concatenate([ref[:, kl] for ref in k_refs], axis=0)
        q4 = jnp.concatenate([q_ref[:, hh * qw + t * LANES:hh * qw + (t + 1) * LANES]
                              for t in range(n_pair)], axis=0)
        qzero = jnp.zeros_like(q4)
        return [lax.dot_general(qm, keys, dn, preferred_element_type=F32)
                for qm in (jnp.where(lo, q4, qzero), jnp.where(lo, qzero, q4))]

    def finish(hh, s_both):
        h = pl.program_id(1) * ATTN_HEADS_PER_STEP + hh
        kl = slice(hh * LANES, (hh + 1) * LANES)
        vals = jnp.concatenate([ref[:, kl] for ref in v_refs], axis=0)
        ones = jnp.ones_like(vals)
        v_ext = (jnp.where(lo, vals, ones), jnp.where(lo, ones, vals))
        o_ext, sink_term = [], []
        for hd, s in enumerate(s_both):
            p_rows, sink_rows = [], []
            for t in range(n_pair):
                st = s[t * blk:(t + 1) * blk]
                parts = [st[:, j * blk:(j + 1) * blk] for j in range(st.shape[1] // blk)]
                if n_band:
                    parts[0] = jnp.where(prev_ok, parts[0], NEG_INF)
                    parts[2] = jnp.where(next_ok, parts[2], NEG_INF)
                sk = sink_ref[h, 2 * t + hd] * LOG2_E
                mx = parts[0]
                for part in parts[1:]:
                    mx = jnp.maximum(mx, part)
                mx = jnp.maximum(jnp.max(mx, axis=-1, keepdims=True), sk)
                p_rows.append(jnp.concatenate([jnp.exp2(part - mx).astype(BF16) for part in parts], axis=1))
                sink_rows.append(jnp.exp2(sk - mx))
            p = jnp.concatenate(p_rows, axis=0)
            o_ext.append(jnp.dot(p, v_ext[hd], preferred_element_type=F32))
            sink_term.append(jnp.concatenate(sink_rows, axis=0))
        o = jnp.where(lo, o_ext[0], o_ext[1])
        denom = jnp.where(lo, pltpu.roll(o_ext[0], HEAD_DIM, 1) + sink_term[0],
                          pltpu.roll(o_ext[1], HEAD_DIM, 1) + sink_term[1])
        res = o / denom
        for t in range(n_pair):
            tile = slice(hh * qw + t * LANES, hh * qw + (t + 1) * LANES)
            y_ref[:, tile] = (res[t * blk:(t + 1) * blk] * zg_ref[:, tile].astype(F32)).astype(BF16)

    pending = scores(0)
    for hh in range(ATTN_HEADS_PER_STEP):
        upcoming = scores(hh + 1) if hh + 1 < ATTN_HEADS_PER_STEP else None
        finish(hh, pending)
        pending = upcoming


def _attention(q, kd, vd, kcd, vcd, zg, sink, batch, seq, ctx_len, use_band):
    blk = ATTN_BLOCK
    nblk = seq // blk
    hps = ATTN_HEADS_PER_STEP
    qw = hps * Q_GROUP * HEAD_DIM
    kw = hps * LANES
    q_spec = pl.BlockSpec((blk, qw), lambda b, h, i: (b * nblk + i, h))
    ctx_spec = pl.BlockSpec((ctx_len, kw), lambda b, h, i: (b, h))
    if use_band:
        def band(delta):
            return pl.BlockSpec(
                (blk, kw), lambda b, h, i: (b * nblk + jnp.clip(i + delta, 0, nblk - 1), h))
        k_specs = [band(-1), band(0), band(1), ctx_spec]
        k_args, v_args = [kd, kd, kd, kcd], [vd, vd, vd, vcd]
        n_band = 3
    else:
        k_specs, k_args, v_args, n_band = [ctx_spec], [kcd], [vcd], 0
    return pl.pallas_call(
        functools.partial(_attn_kernel, n_band, nblk),
        grid=(batch, KV_HEADS // hps, nblk),
        in_specs=[pl.BlockSpec(memory_space=pltpu.SMEM), q_spec] + k_specs + k_specs + [q_spec],
        out_specs=q_spec,
        out_shape=jax.ShapeDtypeStruct(q.shape, BF16),
        compiler_params=_cparams(3),
        name="attention_band" if use_band else "attention_ctx",
    )(sink, q, *k_args, *v_args, zg)


def _gelu_stats_kernel(chunk, a_ref, w_ref, gv_ref, mu_ref, rstd_ref):
    a = a_ref[...]
    n = w_ref.shape[1]
    mean = m2 = None
    for idx, c0 in enumerate(range(0, n, chunk)):
        ge = _gelu_tanh(jnp.dot(a, w_ref[:, c0:c0 + chunk], preferred_element_type=F32))
        gv_ref[:, c0:c0 + chunk] = ge.astype(BF16)
        cmean = jnp.mean(ge, axis=-1, keepdims=True)
        d = ge - cmean
        cm2 = jnp.sum(d * d, axis=-1, keepdims=True)
        if idx == 0:
            mean, m2 = cmean, cm2
        else:
            delta = cmean - mean
            mean = mean + delta * (1.0 / (idx + 1))
            m2 = m2 + cm2 + delta * delta * (chunk * idx / (idx + 1))
    mu_ref[...] = jnp.broadcast_to(mean, mu_ref.shape)
    rstd_ref[...] = jnp.broadcast_to(lax.rsqrt(m2 * (1.0 / n) + EPS), rstd_ref.shape)


def _proj_gelu_stats(h, w, col0, tm, chunk=2 * MXU_COLS):
    m, k = h.shape
    n = D_BRANCH
    col = pl.BlockSpec((tm, LANES), lambda i: (i, 0))
    stat = jax.ShapeDtypeStruct((m, LANES), F32)
    return pl.pallas_call(
        functools.partial(_gelu_stats_kernel, chunk),
        grid=(m // tm,),
        in_specs=[pl.BlockSpec((tm, k), lambda i: (i, 0)),
                  pl.BlockSpec((k, n), lambda i: (0, col0 // n), pipeline_mode=pl.Buffered(1))],
        out_specs=(pl.BlockSpec((tm, n), lambda i: (i, 0)), col, col),
        out_shape=(jax.ShapeDtypeStruct((m, n), BF16), stat, stat),
        compiler_params=_cparams(1),
        name="gmlp_in_v",
    )(h, w)


def _sgu_kernel(n_chunks, n_groups, uz_ref, gv_ref, mu_ref, rstd_ref, lg_ref, lb_ref,
                ws_ref, bs_ref, y_ref):
    gd = D_BRANCH // GMLP_GROUPS
    for c in range(n_chunks):
        rows = slice(c * GMLP_CHUNK, (c + 1) * GMLP_CHUNK)
        mu = jnp.concatenate([mu_ref[rows, :]] * (gd // LANES), axis=1)
        rstd = jnp.concatenate([rstd_ref[rows, :]] * (gd // LANES), axis=1)
        for gl in range(n_groups):
            cols = slice(gl * gd, (gl + 1) * gd)
            ws = ws_ref[gl]
            bs = bs_ref[gl]
            lg = lg_ref[:, cols]
            lb = lb_ref[:, cols]
            vn = ((gv_ref[rows, cols].astype(F32) - mu) * rstd * lg + lb).astype(BF16)
            s = jnp.dot(ws, vn, preferred_element_type=F32) + bs
            y_ref[rows, cols] = (uz_ref[rows, cols].astype(F32) * s).astype(BF16)


def _spatial_gate(uz, gv, mu, rstd, ln_g, ln_b, w_s, b_s, tm, n_groups=4):
    m = uz.shape[0]
    gd = D_BRANCH // GMLP_GROUPS
    tn = n_groups * gd
    tile = pl.BlockSpec((tm, tn), lambda r, g: (r, g))
    col = pl.BlockSpec((tm, LANES), lambda r, g: (r, 0))
    vec = pl.BlockSpec((1, tn), lambda r, g: (0, g))
    return pl.pallas_call(
        functools.partial(_sgu_kernel, tm // GMLP_CHUNK, n_groups),
        grid=(m // tm, GMLP_GROUPS // n_groups),
        in_specs=[tile, tile, col, col, vec, vec,
                  pl.BlockSpec((n_groups, GMLP_CHUNK, GMLP_CHUNK), lambda r, g: (g, 0, 0)),
                  pl.BlockSpec((n_groups, GMLP_CHUNK, 1), lambda r, g: (g, 0, 0))],
        out_specs=tile,
        out_shape=jax.ShapeDtypeStruct(uz.shape, BF16),
        compiler_params=_cparams(2),
        name="gmlp_spatial_gate",
    )(uz, gv, mu, rstd, ln_g.reshape(1, D_BRANCH), ln_b.reshape(1, D_BRANCH),
      w_s, b_s.reshape(GMLP_GROUPS, GMLP_CHUNK, 1))


def _wout_kernel(mode, y_ref, w_ref, x_ref, gate_ref, *refs):
    if len(y_ref.shape) == 3:
        y = jnp.concatenate([y_ref[s] for s in range(y_ref.shape[0])], axis=1)
    else:
        y = y_ref[...]
    acc = jnp.dot(y, w_ref[...], preferred_element_type=F32)
    xn = x_ref[...] + gate_ref[...] * acc
    if mode == "final":
        g_ref, o_ref = refs
        o_ref[...] = xn * lax.rsqrt(jnp.mean(xn * xn, axis=-1, keepdims=True) + EPS) * g_ref[...]
    elif mode == "next":
        g_ref, sc_ref, sh_ref, xo_ref, h_ref = refs
        xo_ref[...] = xn
        h_ref[...] = _mod_rmsnorm(xn, g_ref[...], sc_ref[...], sh_ref[...]).astype(BF16)
    else:
        refs[0][...] = xn


def _out_proj(y, w_out, x2, mods, layer, who, mode, norm_vec, tm):
    m = x2.shape[0]
    row = pl.BlockSpec((tm, D_MODEL), lambda r: (r, 0))
    if y.ndim == 3:
        y_spec = pl.BlockSpec((SLABS, tm, LANES), lambda r: (0, r, 0))
    else:
        y_spec = pl.BlockSpec((tm, D_BRANCH), lambda r: (r, 0))
    in_specs = [y_spec,
                pl.BlockSpec((None, D_BRANCH, D_MODEL), lambda r: (layer, 0, 0),
                             pipeline_mode=pl.Buffered(1)),
                row, _mod_spec(layer, 2, who)]
    args = [y, w_out, x2, mods]
    xs = jax.ShapeDtypeStruct((m, D_MODEL), F32)
    if mode == "final":
        in_specs.append(pl.BlockSpec((1, D_MODEL), lambda r: (0, 0)))
        args.append(norm_vec.reshape(1, D_MODEL))
        out_shape, out_specs = xs, row
    elif mode == "next":
        in_specs += [_row_spec(layer + 1), _mod_spec(layer + 1, 1, who), _mod_spec(layer + 1, 0, who)]
        args += [norm_vec, mods, mods]
        out_shape = (xs, jax.ShapeDtypeStruct((m, D_MODEL), BF16))
        out_specs = (row, row)
    else:
        out_shape, out_specs = xs, row
    return pl.pallas_call(
        functools.partial(_wout_kernel, mode),
        grid=(m // tm,),
        in_specs=in_specs,
        out_specs=out_specs,
        out_shape=out_shape,
        compiler_params=_cparams(1),
        name="out_proj_" + mode,
    )(*args)


def kernel(x, c, ctx, c_ctx, norm_g, ada_w, ada_b, w_out, fnet_w_in, fnet_w_mix, attn_w_in, attn_sink,
           gmlp_w_in, gmlp_w_s, gmlp_b_s, gmlp_ln_g, gmlp_ln_b, final_g):
    batch, seq, d = x.shape
    ctx_len = ctx.shape[1]
    assert d == D_MODEL and seq % (DFT_NB * 8) == 0 and seq % GRID_W == 0 and batch < 8
    m_lat, m_ctx = batch * seq, batch * ctx_len
    tm_lat = min(PROJ_TM, seq)
    tm_ctx = m_ctx
    tm_out = OUT_TM

    def who_lat(tm):
        return lambda r: (r * tm) // seq

    who_ctx = lambda r: batch

    cvec = jnp.zeros((8, d), F32).at[:batch].set(c).at[batch].set(c_ctx)
    mods = _mods(cvec, ada_w, ada_b).reshape(DEPTH, 8, 3, 1, d)
    norm_g3 = norm_g.reshape(DEPTH, 1, d)
    w_out_b = w_out.astype(BF16)

    xl = x.reshape(m_lat, d)
    xc = ctx.reshape(m_ctx, d)
    hl = _prenorm(xl, norm_g3, mods, 0, who_lat(NORM_TM), NORM_TM)
    hc = _prenorm(xc, norm_g3, mods, 0, who_ctx, tm_ctx)


    def finish(y, x2, layer, who_fn, need_next, is_final):
        if is_final:
            return _out_proj(y, w_out_b, x2, mods, layer, who_fn, "final", final_g, tm_out), None
        if need_next:
            return _out_proj(y, w_out_b, x2, mods, layer, who_fn, "next", norm_g3, tm_out)
        return _out_proj(y, w_out_b, x2, mods, layer, who_fn, "plain", None, tm_out), None

    out = None
    for i in range(DEPTH):
        kind, j = i % 3, i // 3
        need_ctx = i < DEPTH - 1
        last = i == DEPTH - 1
        if kind == 0:
            cs = _fold_mix(fnet_w_mix[j])
            tm_dft = (seq // DFT_NB) * DFT_ROWS
            w_u = fnet_w_in[j, :, :D_BRANCH].astype(BF16)
            ar, ai = _proj_channel_dft(hl, w_u, cs, "fnet_in_u", tm_dft, seq=seq)
            zg = _proj_silu_slabs(hl, fnet_w_in, D_BRANCH, "fnet_in_z", tm_lat, w_layer=j)
            y = _position_dft(ar, ai, zg, batch, seq)
            yc = None
            if need_ctx:
                ar, ai = _proj_channel_dft(hc, w_u, cs, "fnet_in_u_ctx", tm_ctx)
                zg = _proj_silu_slabs(hc, fnet_w_in, D_BRANCH, "fnet_in_z_ctx", tm_ctx, w_layer=j)
                yc = _position_dft_dense(ar, ai, zg, batch, ctx_len)
        elif kind == 1:
            kvw = KV_HEADS * HEAD_DIM
            k0, v0, z0 = D_BRANCH, D_BRANCH + kvw, D_BRANCH + 2 * kvw
            sink = attn_sink[j].reshape(KV_HEADS, Q_GROUP)
            scale = HEAD_DIM ** -0.5 * LOG2_E
            q_tabs = [jnp.asarray(t) for t in _rope_tables(seq, scale)]
            k_tabs = [jnp.asarray(t) for t in _rope_tables(seq)]
            q = _proj_rope(hl, attn_w_in, 0, D_BRANCH, *q_tabs, seq, "attn_in_q", tm_lat, w_layer=j)
            kd = _proj_rope(hl, attn_w_in, k0, kvw, *k_tabs, seq, "attn_in_k", tm_lat, w_layer=j,
                            dup_heads=True)
            vd = _proj_simple(hl, attn_w_in, v0, kvw, _epi_cast_dup, "attn_in_v", tm_lat, w_layer=j, out_mult=2)
            zg = _proj_simple(hl, attn_w_in, z0, D_BRANCH, _epi_silu, "attn_in_z", tm_lat, w_layer=j)
            kcd = _proj_simple(hc, attn_w_in, k0, kvw, _epi_cast_dup, "attn_in_kc", tm_ctx, w_layer=j, out_mult=2)
            vcd = _proj_simple(hc, attn_w_in, v0, kvw, _epi_cast_dup, "attn_in_vc", tm_ctx, w_layer=j, out_mult=2)
            y = _attention(q, kd, vd, kcd, vcd, zg, sink, batch, seq, ctx_len, True)
            yc = None
            if need_ctx:
                qc = _proj_simple(hc, attn_w_in, 0, D_BRANCH, functools.partial(_epi_scale_cast, scale),
                                  "attn_in_qc", tm_ctx, w_layer=j)
                zgc = _proj_simple(hc, attn_w_in, z0, D_BRANCH, _epi_silu, "attn_in_zc", tm_ctx, w_layer=j)
                yc = _attention(qc, None, None, kcd, vcd, zgc, sink, batch, ctx_len, ctx_len, False)
        else:
            w_v = gmlp_w_in[j, :, D_BRANCH:2 * D_BRANCH].astype(BF16)
            ws = gmlp_w_s[j].astype(BF16)

            def gmlp_branch(h, tm):
                uz = _proj_simple(h, gmlp_w_in, (0, 2 * D_BRANCH), D_BRANCH, _epi_gelu_times_silu,
                                  "gmlp_in_uz", tm, tn=PROJ_TN // 2, w_layer=j)
                gv, mu, rstd = _proj_gelu_stats(h, w_v, 0, min(tm, STATS_TM))
                return _spatial_gate(uz, gv, mu, rstd, gmlp_ln_g[j], gmlp_ln_b[j], ws, gmlp_b_s[j],
                                     min(tm, GATE_TM), n_groups=GMLP_GROUPS)

            y = gmlp_branch(hl, tm_lat)
            yc = gmlp_branch(hc, tm_ctx) if need_ctx else None

        res, hl = finish(y, xl, i, who_lat(tm_out), not last, last)
        if last:
            out = res
        else:
            xl = res
        if need_ctx:
            xc, hc = finish(yc, xc, i, who_ctx, i + 1 < DEPTH - 1 or (i + 1) % 3 == 1, False)
    return out.reshape(batch, seq, d)
```

```python
import functools
import math

import numpy as np
import jax
import jax.numpy as jnp
from jax import lax
from jax.experimental import pallas as pl
from jax.experimental.pallas import tpu as pltpu

F32 = jnp.float32
BF16 = jnp.bfloat16

D_MODEL = 2048
D_BRANCH = 4096
DEPTH = 4
GRID_W = 64
FNET_GROUPS = 16
FNET_GROUP_DIM = 256
HEAD_DIM = 64
KV_HEADS = 8
Q_GROUP = 8
ATTN_BLOCK = 128
ATTN_HEADS_PER_STEP = 8
ROPE_BASE = 10000.0
GMLP_CHUNK = 128
GMLP_GROUPS = 16
EPS = 1e-6
NEG_INF = -1e30
LOG2_E = math.log2(math.e)

LANES = 128
MXU_COLS = 256
PROJ_TN = 1024
PROJ_TM = 2048
OUT_TM = 512
STATS_TM = 256
GATE_TM = 256
NORM_TM = 1024
DFT_NB = 128
DFT_ROWS = 16
SLABS = D_BRANCH // LANES
VMEM_LIMIT = 56 * 1024 * 1024


def _cparams(n_axes, vmem=VMEM_LIMIT):
    return pltpu.CompilerParams(dimension_semantics=("arbitrary",) * n_axes,
                                vmem_limit_bytes=vmem)


def _silu(z):
    return 0.5 * z * (1.0 + jnp.tanh(0.5 * z))


def _gelu_tanh(x):
    c = math.sqrt(2.0 / math.pi)
    return 0.5 * x * (1.0 + jnp.tanh(c * (x + 0.044715 * (x * x * x))))


def _mod_rmsnorm(x, g, scale, shift):
    y = x * lax.rsqrt(jnp.mean(x * x, axis=-1, keepdims=True) + EPS) * g
    return y * (1.0 + scale) + shift


def _channel_dft_matrix():
    n = FNET_GROUP_DIM
    k = np.arange(n, dtype=np.float64)
    ang = 2.0 * np.pi * np.outer(k, k) / n
    s = 1.0 / math.sqrt(n)
    return np.concatenate([np.cos(ang) * s, -np.sin(ang) * s], axis=1).astype(np.float32)


def _position_dft_matrices(seq):
    na, nb = seq // DFT_NB, DFT_NB
    a = np.arange(na, dtype=np.float64)
    b = np.arange(nb, dtype=np.float64)
    ang = 2.0 * np.pi * (a[None, None, :] * a[None, :, None] / na + b[:, None, None] * a[None, :, None] / seq)
    mr = np.cos(ang) / math.sqrt(na)
    mi = -np.sin(ang) / math.sqrt(na)
    fa = np.concatenate([np.concatenate([mr, -mi], axis=2), np.concatenate([mi, mr], axis=2)], axis=1)
    fa = np.concatenate([fa[0::2], fa[1::2]], axis=2)
    angb = 2.0 * np.pi * np.outer(b, b) / nb
    fb = np.concatenate([np.cos(angb), np.sin(angb)], axis=1) / math.sqrt(nb)
    return fa.astype(np.float32), fb.astype(np.float32)


def _dense_dft_matrix(n):
    k = np.arange(n, dtype=np.float64)
    ang = 2.0 * np.pi * np.outer(k, k) / n
    return (np.concatenate([np.cos(ang), np.sin(ang)], axis=1) / math.sqrt(n)).astype(np.float32)


def _rope_tables(seq, scale=1.0):
    nf = HEAD_DIM // 4
    inv = ROPE_BASE ** (-np.arange(nf, dtype=np.float64) / nf)
    t = np.arange(seq)
    rows = (t // GRID_W).astype(np.float64)
    cols = (t % GRID_W).astype(np.float64)
    parts_c, parts_s = [], []
    for pos in (rows, cols):
        ang = pos[:, None] * inv[None, :]
        parts_c += [np.cos(ang), np.cos(ang)]
        parts_s += [-np.sin(ang), np.sin(ang)]
    cos = np.concatenate(parts_c, axis=1) * scale
    sin = np.concatenate(parts_s, axis=1) * scale
    reps = LANES // HEAD_DIM
    return (np.tile(cos, (1, reps)).astype(np.float32), np.tile(sin, (1, reps)).astype(np.float32))


def _mods_kernel(cv_ref, w_ref, b_ref, o_ref):
    a = _silu(cv_ref[...])
    a_hi = a.astype(BF16)
    a_lo = (a - a_hi.astype(F32)).astype(BF16)
    w = w_ref[...].astype(BF16)
    part = jnp.dot(a_hi, w, preferred_element_type=F32) + jnp.dot(a_lo, w, preferred_element_type=F32)

    @pl.when(pl.program_id(1) == 0)
    def _():
        o_ref[...] = part + b_ref[...]

    @pl.when(pl.program_id(1) > 0)
    def _():
        o_ref[...] += part


def _mods(cvec, ada_w, ada_b):
    depth, d, n3 = ada_w.shape
    tk = 512
    return pl.pallas_call(
        _mods_kernel,
        grid=(depth, d // tk),
        in_specs=[pl.BlockSpec((8, tk), lambda i, k: (0, k)),
                  pl.BlockSpec((None, tk, n3), lambda i, k: (i, k, 0)),
                  pl.BlockSpec((None, 1, n3), lambda i, k: (i, 0, 0))],
        out_specs=pl.BlockSpec((None, 8, n3), lambda i, k: (i, 0, 0)),
        out_shape=jax.ShapeDtypeStruct((depth, 8, n3), F32),
        compiler_params=_cparams(2),
        name="ada_mods",
    )(cvec, ada_w, ada_b.reshape(depth, 1, n3))


def _mod_spec(layer, kind, who_of_row):
    return pl.BlockSpec((None, None, None, 1, D_MODEL),
                        lambda r, *_: (layer, who_of_row(r), kind, 0, 0))


def _row_spec(vec_layer):
    return pl.BlockSpec((None, 1, D_MODEL), lambda r, *_: (vec_layer, 0, 0))


def _prenorm_kernel(x_ref, g_ref, sc_ref, sh_ref, h_ref):
    h_ref[...] = _mod_rmsnorm(x_ref[...], g_ref[...], sc_ref[...], sh_ref[...]).astype(BF16)


def _prenorm(x2, norm_g3, mods, layer, who, tm):
    m = x2.shape[0]
    return pl.pallas_call(
        _prenorm_kernel,
        grid=(m // tm,),
        in_specs=[pl.BlockSpec((tm, D_MODEL), lambda r: (r, 0)),
                  _row_spec(layer), _mod_spec(layer, 1, who), _mod_spec(layer, 0, who)],
        out_specs=pl.BlockSpec((tm, D_MODEL), lambda r: (r, 0)),
        out_shape=jax.ShapeDtypeStruct((m, D_MODEL), BF16),
        compiler_params=_cparams(1),
        name="prenorm",
    )(x2, norm_g3, mods, mods)


def _proj_kernel(epilogue, n_extra, chunk, n_w, a_ref, *rest):
    w_refs, rest = rest[:n_w], rest[n_w:]
    if len(a_ref.shape) == 3:
        a_flat = rest[-1]
        rest = rest[:-1]

        @pl.when(pl.program_id(1) == 0)
        def _():
            a_flat[...] = a_ref[...].reshape(a_flat.shape)

        a = a_flat[...]
    else:
        a = a_ref[...]
    for c0 in range(0, w_refs[0].shape[1], chunk):
        accs = [jnp.dot(a, w_ref[:, c0:c0 + chunk].astype(BF16), preferred_element_type=F32)
                for w_ref in w_refs]
        epilogue(accs[0] if n_w == 1 else accs, c0, rest[:n_extra], rest[n_extra:])


def _proj(h, w, col0, ncols, tm, tn, epilogue, extras, extra_specs, out_shapes, out_specs, name,
          lhs_spec=None, chunk=MXU_COLS, w_layer=None):
    k = h.shape[-1]
    m = h.size // k
    col0s = col0 if isinstance(col0, tuple) else (col0,)
    scratch = [pltpu.VMEM((tm, k), h.dtype)] if lhs_spec is not None else []
    if lhs_spec is None:
        lhs_spec = pl.BlockSpec((tm, k), lambda i, j: (i, 0))

    def w_spec(off):
        if w.ndim == 3:
            return pl.BlockSpec((None, k, tn), lambda i, j: (w_layer, 0, j + off))
        return pl.BlockSpec((k, tn), lambda i, j: (0, j + off))

    return pl.pallas_call(
        functools.partial(_proj_kernel, epilogue, len(extras), min(chunk, tn), len(col0s)),
        grid=(m // tm, ncols // tn),
        in_specs=[lhs_spec] + [w_spec(c // tn) for c in col0s] + list(extra_specs),
        out_specs=out_specs,
        out_shape=out_shapes,
        scratch_shapes=scratch,
        compiler_params=_cparams(2),
        name=name,
    )(h, *([w] * len(col0s)), *extras)


def _epi_silu(acc, c0, extras, outs):
    outs[0][:, c0:c0 + acc.shape[1]] = _silu(acc).astype(BF16)


def _epi_silu_slabs(acc, c0, extras, outs):
    for t in range(acc.shape[1] // LANES):
        outs[0][c0 // LANES + t] = _silu(acc[:, t * LANES:(t + 1) * LANES]).astype(BF16)


def _epi_gelu_times_silu(accs, c0, extras, outs):
    acc_u, acc_z = accs
    outs[0][:, c0:c0 + acc_u.shape[1]] = (_gelu_tanh(acc_u) * _silu(acc_z)).astype(BF16)


def _epi_scale_cast(scale, acc, c0, extras, outs):
    outs[0][:, c0:c0 + acc.shape[1]] = (acc * scale).astype(BF16)


def _epi_channel_dft(split_rows, acc, c0, extras, outs):
    cs_ref = extras[0]
    ar_ref, ai_ref = outs
    tm = acc.shape[0]
    for gl in range(acc.shape[1] // FNET_GROUP_DIM):
        g = c0 // FNET_GROUP_DIM + gl
        ub = acc[:, gl * FNET_GROUP_DIM:(gl + 1) * FNET_GROUP_DIM].astype(BF16)
        ab = jnp.dot(ub, cs_ref[g], preferred_element_type=F32)
        if split_rows:
            ab4 = ab.reshape(tm // DFT_ROWS, 2, DFT_ROWS // 2, ab.shape[1])
            ab = jnp.concatenate([ab4[:, hf].reshape(tm // 2, ab.shape[1]) for hf in range(2)], axis=0)
        ab = ab.astype(BF16)
        ar_ref[2 * g] = ab[:, 0:128]
        ar_ref[2 * g + 1] = ab[:, 128:256]
        ai_ref[2 * g] = ab[:, 256:384]
        ai_ref[2 * g + 1] = ab[:, 384:512]


def _store_tile(out_ref, col, y, dup_heads):
    if not dup_heads:
        out_ref[:, col:col + LANES] = y.astype(BF16)
        return
    lane = lax.broadcasted_iota(jnp.int32, (1, LANES), 1)
    lo = lane < HEAD_DIM
    swapped = pltpu.roll(y, HEAD_DIM, 1)
    out_ref[:, 2 * col:2 * col + LANES] = jnp.where(lo, y, swapped).astype(BF16)
    out_ref[:, 2 * col + LANES:2 * col + 2 * LANES] = jnp.where(lo, swapped, y).astype(BF16)


def _epi_rope(dup_heads, acc, c0, extras, outs):
    cos = extras[0][...]
    sin = extras[1][...]
    lane = lax.broadcasted_iota(jnp.int32, (1, LANES), 1)
    first = (lane % 32) < 16
    for t in range(acc.shape[1] // LANES):
        x = acc[:, t * LANES:(t + 1) * LANES]
        partner = jnp.where(first, pltpu.roll(x, LANES - 16, 1), pltpu.roll(x, 16, 1))
        _store_tile(outs[0], c0 + t * LANES, x * cos + partner * sin, dup_heads)


def _epi_cast_dup(acc, c0, extras, outs):
    for t in range(acc.shape[1] // LANES):
        _store_tile(outs[0], c0 + t * LANES, acc[:, t * LANES:(t + 1) * LANES], True)


def _proj_simple(h, w, col0, ncols, epilogue, name, tm, tn=PROJ_TN, w_layer=None, out_mult=1):
    m = h.shape[0]
    tn = min(tn, ncols)
    return _proj(h, w, col0, ncols, tm, tn, epilogue, (), (),
                 jax.ShapeDtypeStruct((m, out_mult * ncols), BF16),
                 pl.BlockSpec((tm, out_mult * tn), lambda i, j: (i, j)), name, w_layer=w_layer)


def _proj_rope(h, w, col0, ncols, cos_t, sin_t, seq, name, tm, tn=PROJ_TN, w_layer=None,
               dup_heads=False):
    m = h.shape[0]
    tn = min(tn, ncols)
    per_batch = seq // tm
    out_mult = 2 if dup_heads else 1
    tab_spec = pl.BlockSpec((tm, LANES), lambda i, j: (i % per_batch, 0))
    return _proj(h, w, col0, ncols, tm, tn, functools.partial(_epi_rope, dup_heads),
                 (cos_t, sin_t), (tab_spec, tab_spec),
                 jax.ShapeDtypeStruct((m, out_mult * ncols), BF16),
                 pl.BlockSpec((tm, out_mult * tn), lambda i, j: (i, j)), name, w_layer=w_layer)


def _proj_silu_slabs(h, w, col0, name, tm, tn=PROJ_TN, w_layer=None):
    m = h.shape[0]
    return _proj(h, w, col0, D_BRANCH, tm, tn, _epi_silu_slabs, (), (),
                 jax.ShapeDtypeStruct((SLABS, m, LANES), BF16),
                 pl.BlockSpec((tn // LANES, tm, LANES), lambda i, j: (j, i, 0)), name, w_layer=w_layer)


def _proj_channel_dft(h, w, cs, name, tm, seq=None, tn=PROJ_TN):
    m = h.shape[0]
    lhs_spec = None
    if seq is not None:
        na, tiles = seq // DFT_NB, DFT_NB // DFT_ROWS
        assert tm == na * DFT_ROWS
        h = h.reshape(m // seq, na, DFT_NB, D_MODEL)
        lhs_spec = pl.BlockSpec((None, na, DFT_ROWS, D_MODEL), lambda i, j: (i // tiles, 0, i % tiles, 0))
    slab_shape = jax.ShapeDtypeStruct((SLABS, m, LANES), BF16)
    slab_spec = pl.BlockSpec((tn // LANES, tm, LANES), lambda i, j: (j, i, 0))
    groups = tn // FNET_GROUP_DIM
    cs_spec = pl.BlockSpec((groups,) + cs.shape[1:], lambda i, j: (j, 0, 0))
    return _proj(h, w, 0, D_BRANCH, tm, tn, functools.partial(_epi_channel_dft, seq is not None), (cs,), (cs_spec,),
                 (slab_shape, slab_shape), (slab_spec, slab_spec), name, lhs_spec=lhs_spec,
                 chunk=2 * MXU_COLS)


def _fold_mix_kernel(c_ref, s_ref, wm_ref, o_ref):
    gd = FNET_GROUP_DIM
    hp = lax.Precision.HIGHEST
    for g in range(wm_ref.shape[0]):
        wm = wm_ref[g]
        o_ref[g, :, 0:gd] = jnp.dot(c_ref[...], wm, preferred_element_type=F32, precision=hp).astype(BF16)
        o_ref[g, :, gd:2 * gd] = jnp.dot(s_ref[...], wm, preferred_element_type=F32, precision=hp).astype(BF16)


def _fold_mix(w_mix, groups_per_step=4):
    layers = w_mix.shape[0]
    gd = FNET_GROUP_DIM
    cs = _channel_dft_matrix()
    mat = pl.BlockSpec((gd, gd), lambda l, g: (0, 0))
    return pl.pallas_call(
        _fold_mix_kernel,
        grid=(layers, FNET_GROUPS // groups_per_step),
        in_specs=[mat, mat, pl.BlockSpec((None, groups_per_step, gd, gd), lambda l, g: (l, g, 0, 0))],
        out_specs=pl.BlockSpec((None, groups_per_step, gd, 2 * gd), lambda l, g: (l, g, 0, 0)),
        out_shape=jax.ShapeDtypeStruct((layers, FNET_GROUPS, gd, 2 * gd), BF16),
        compiler_params=_cparams(2),
        name="fnet_fold_mix",
    )(jnp.asarray(cs[:, :gd]), jnp.asarray(cs[:, gd:]), w_mix)


def _dft_kernel(na, pitch, gpitch, ar_ref, ai_ref, zg_ref, fa_ref, fb_ref, y_ref,
                xr_ref, xi_ref, er_ref, ei_ref, g_ref):
    half_rows = na * (DFT_ROWS // 2)
    xr_ref[...] = ar_ref[...].astype(F32)
    xi_ref[...] = ai_ref[...].astype(F32)

    def gather_a(b):
        start = (b // (DFT_ROWS // 2)) * half_rows + b % (DFT_ROWS // 2)
        zr = xr_ref[pl.ds(start, na, stride=DFT_ROWS // 2), :]
        zi = xi_ref[pl.ds(start, na, stride=DFT_ROWS // 2), :]
        return jnp.concatenate([zr, zi], axis=0).astype(BF16)

    def stage_a(pair, carry):
        b0 = 2 * pair
        d0, d1 = gather_a(b0), gather_a(b0 + 1)
        zero = jnp.zeros_like(d0)
        rhs = jnp.concatenate([jnp.concatenate([d0, zero], axis=1),
                               jnp.concatenate([zero, d1], axis=1)], axis=0)
        e = jnp.dot(fa_ref[pair], rhs, preferred_element_type=F32)
        for j in range(2):
            off = pl.multiple_of((b0 + j) * pitch, 8)
            er_ref[pl.ds(off, na), :] = e[:na, j * LANES:(j + 1) * LANES]
            ei_ref[pl.ds(off, na), :] = e[na:, j * LANES:(j + 1) * LANES]
        return carry

    lax.fori_loop(0, DFT_NB // 2, stage_a, 0, unroll=64)

    def gather_b(ka):
        er = er_ref[pl.ds(ka, DFT_NB, stride=pitch), :]
        ei = ei_ref[pl.ds(ka, DFT_NB, stride=pitch), :]
        return jnp.concatenate([er, ei], axis=0).astype(BF16)

    def stage_b(pair, carry):
        ka0 = 2 * pair
        rhs = jnp.concatenate([gather_b(ka0), gather_b(ka0 + 1)], axis=1)
        g = jnp.dot(fb_ref[...], rhs, preferred_element_type=F32)
        for j in range(2):
            off = pl.multiple_of((ka0 + j) * gpitch, 8)
            g_ref[pl.ds(off, DFT_NB), :] = g[:, j * LANES:(j + 1) * LANES]
        return carry

    lax.fori_loop(0, na // 2, stage_b, 0, unroll=32)

    def gate(kb, carry):
        rows = pl.ds(pl.multiple_of(kb * na, na), na)
        g = g_ref[pl.ds(kb, na, stride=gpitch), :]
        y_ref[rows, :] = (g * zg_ref[rows, :].astype(F32)).astype(BF16)
        return carry

    lax.fori_loop(0, DFT_NB, gate, 0, unroll=32)


def _position_dft(ar, ai, zg, batch, seq):
    na = seq // DFT_NB
    pitch = na + 8
    gpitch = DFT_NB + 8
    fa_np, fb_np = _position_dft_matrices(seq)
    fa = jnp.asarray(fa_np).astype(BF16)
    fb = jnp.asarray(fb_np).astype(BF16)
    slab = pl.BlockSpec((None, seq, LANES), lambda s, b: (s, b, 0))
    return pl.pallas_call(
        functools.partial(_dft_kernel, na, pitch, gpitch),
        grid=(SLABS, batch),
        in_specs=[slab, slab, slab,
                  pl.BlockSpec(fa.shape, lambda s, b: (0, 0, 0)),
                  pl.BlockSpec(fb.shape, lambda s, b: (0, 0))],
        out_specs=slab,
        out_shape=jax.ShapeDtypeStruct(ar.shape, BF16),
        scratch_shapes=[pltpu.VMEM((seq, LANES), F32),
                        pltpu.VMEM((seq, LANES), F32),
                        pltpu.VMEM((DFT_NB * pitch, LANES), F32),
                        pltpu.VMEM((DFT_NB * pitch, LANES), F32),
                        pltpu.VMEM((na * gpitch, LANES), F32)],
        compiler_params=_cparams(2),
        name="position_dft",
    )(ar, ai, zg, fa, fb)


def _dft_dense_kernel(ar_ref, ai_ref, zg_ref, fd_ref, y_ref):
    n = ar_ref.shape[0]
    d = jnp.concatenate([jnp.concatenate([ar_ref[s] for s in range(n)], axis=1),
                         jnp.concatenate([ai_ref[s] for s in range(n)], axis=1)], axis=0).astype(BF16)
    g = jnp.dot(fd_ref[...], d, preferred_element_type=F32)
    for s in range(n):
        y_ref[s] = (g[:, s * LANES:(s + 1) * LANES] * zg_ref[s].astype(F32)).astype(BF16)


def _position_dft_dense(ar, ai, zg, batch, seq, slabs_per_step=8):
    fd = jnp.asarray(_dense_dft_matrix(seq)).astype(BF16)
    slab = pl.BlockSpec((slabs_per_step, seq, LANES), lambda s, b: (s, b, 0))
    return pl.pallas_call(
        _dft_dense_kernel,
        grid=(SLABS // slabs_per_step, batch),
        in_specs=[slab, slab, slab, pl.BlockSpec(fd.shape, lambda s, b: (0, 0))],
        out_specs=slab,
        out_shape=jax.ShapeDtypeStruct(ar.shape, BF16),
        compiler_params=_cparams(2),
        name="position_dft_dense",
    )(ar, ai, zg, fd)


def _attn_kernel(n_band, nblk, sink_ref, q_ref, *refs):
    k_refs = refs[:n_band + 1]
    v_refs = refs[n_band + 1:2 * n_band + 2]
    zg_ref, y_ref = refs[2 * n_band + 2:]
    i = pl.program_id(2)
    blk = ATTN_BLOCK
    n_pair = Q_GROUP // 2
    qw = Q_GROUP * HEAD_DIM

    lane = lax.broadcasted_iota(jnp.int32, (1, LANES), 1)
    lo = lane < HEAD_DIM
    dn = (((1,), (1,)), ((), ()))
    if n_band:
        r = lax.broadcasted_iota(jnp.int32, (blk, blk), 0)
        c = lax.broadcasted_iota(jnp.int32, (blk, blk), 1)
        prev_ok = c >= r + jnp.where(i > 0, 0, blk)
        next_ok = c <= r - jnp.where(i < nblk - 1, 0, blk)

    def scores(hh):
        kl = slice(hh * LANES, (hh + 1) * LANES)
        keys = jnp.concatenate([ref[:, kl] for ref in k_refs], axis=0)
        q4 = jnp.concatenate([q_ref[:, hh * qw + t * LANES:hh * qw + (t + 1) * LANES]
                              for t in range(n_pair)], axis=0)
        qzero = jnp.zeros_like(q4)
        return [lax.dot_general(qm, keys, dn, preferred_element_type=F32)
                for qm in (jnp.where(lo, q4, qzero), jnp.where(lo, qzero, q4))]

    def finish(hh, s_both):
        h = pl.program_id(1) * ATTN_HEADS_PER_STEP + hh
        kl = slice(hh * LANES, (hh + 1) * LANES)
        vals = jnp.concatenate([ref[:, kl] for ref in v_refs], axis=0)
        ones = jnp.ones_like(vals)
        v_ext = (jnp.where(lo, vals, ones), jnp.where(lo, ones, vals))
        o_ext, sink_term = [], []
        for hd, s in enumerate(s_both):
            p_rows, sink_rows = [], []
            for t in range(n_pair):
                st = s[t * blk:(t + 1) * blk]
                parts = [st[:, j * blk:(j + 1) * blk] for j in range(st.shape[1] // blk)]
                if n_band:
                    parts[0] = jnp.where(prev_ok, parts[0], NEG_INF)
                    parts[2] = jnp.where(next_ok, parts[2], NEG_INF)
                sk = sink_ref[h, 2 * t + hd] * LOG2_E
                mx = parts[0]
                for part in parts[1:]:
                    mx = jnp.maximum(mx, part)
                mx = jnp.maximum(jnp.max(mx, axis=-1, keepdims=True), sk)
                p_rows.append(jnp.concatenate([jnp.exp2(part - mx).astype(BF16) for part in parts], axis=1))
                sink_rows.append(jnp.exp2(sk - mx))
            p = jnp.concatenate(p_rows, axis=0)
            o_ext.append(jnp.dot(p, v_ext[hd], preferred_element_type=F32))
            sink_term.append(jnp.concatenate(sink_rows, axis=0))
        o = jnp.where(lo, o_ext[0], o_ext[1])
        denom = jnp.where(lo, pltpu.roll(o_ext[0], HEAD_DIM, 1) + sink_term[0],
                          pltpu.roll(o_ext[1], HEAD_DIM, 1) + sink_term[1])
        res = o / denom
        for t in range(n_pair):
            tile = slice(hh * qw + t * LANES, hh * qw + (t + 1) * LANES)
            y_ref[:, tile] = (res[t * blk:(t + 1) * blk] * zg_ref[:, tile].astype(F32)).astype(BF16)

    pending = scores(0)
    for hh in range(ATTN_HEADS_PER_STEP):
        upcoming = scores(hh + 1) if hh + 1 < ATTN_HEADS_PER_STEP else None
        finish(hh, pending)
        pending = upcoming


def _attention(q, kd, vd, kcd, vcd, zg, sink, batch, seq, ctx_len, use_band):
    blk = ATTN_BLOCK
    nblk = seq // blk
    hps = ATTN_HEADS_PER_STEP
    qw = hps * Q_GROUP * HEAD_DIM
    kw = hps * LANES
    q_spec = pl.BlockSpec((blk, qw), lambda b, h, i: (b * nblk + i, h))
    ctx_spec = pl.BlockSpec((ctx_len, kw), lambda b, h, i: (b, h))
    if use_band:
        def band(delta):
            return pl.BlockSpec(
                (blk, kw), lambda b, h, i: (b * nblk + jnp.clip(i + delta, 0, nblk - 1), h))
        k_specs = [band(-1), band(0), band(1), ctx_spec]
        k_args, v_args = [kd, kd, kd, kcd], [vd, vd, vd, vcd]
        n_band = 3
    else:
        k_specs, k_args, v_args, n_band = [ctx_spec], [kcd], [vcd], 0
    return pl.pallas_call(
        functools.partial(_attn_kernel, n_band, nblk),
        grid=(batch, KV_HEADS // hps, nblk),
        in_specs=[pl.BlockSpec(memory_space=pltpu.SMEM), q_spec] + k_specs + k_specs + [q_spec],
        out_specs=q_spec,
        out_shape=jax.ShapeDtypeStruct(q.shape, BF16),
        compiler_params=_cparams(3),
        name="attention_band" if use_band else "attention_ctx",
    )(sink, q, *k_args, *v_args, zg)


def _gelu_stats_kernel(chunk, a_ref, w_ref, gv_ref, mu_ref, rstd_ref):
    a = a_ref[...]
    n = w_ref.shape[1]
    mean = m2 = None
    for idx, c0 in enumerate(range(0, n, chunk)):
        ge = _gelu_tanh(jnp.dot(a, w_ref[:, c0:c0 + chunk], preferred_element_type=F32))
        gv_ref[:, c0:c0 + chunk] = ge.astype(BF16)
        cmean = jnp.mean(ge, axis=-1, keepdims=True)
        d = ge - cmean
        cm2 = jnp.sum(d * d, axis=-1, keepdims=True)
        if idx == 0:
            mean, m2 = cmean, cm2
        else:
            delta = cmean - mean
            mean = mean + delta * (1.0 / (idx + 1))
            m2 = m2 + cm2 + delta * delta * (chunk * idx / (idx + 1))
    mu_ref[...] = jnp.broadcast_to(mean, mu_ref.shape)
    rstd_ref[...] = jnp.broadcast_to(lax.rsqrt(m2 * (1.0 / n) + EPS), rstd_ref.shape)


def _proj_gelu_stats(h, w, col0, tm, chunk=2 * MXU_COLS):
    m, k = h.shape
    n = D_BRANCH
    col = pl.BlockSpec((tm, LANES), lambda i: (i, 0))
    stat = jax.ShapeDtypeStruct((m, LANES), F32)
    return pl.pallas_call(
        functools.partial(_gelu_stats_kernel, chunk),
        grid=(m // tm,),
        in_specs=[pl.BlockSpec((tm, k), lambda i: (i, 0)),
                  pl.BlockSpec((k, n), lambda i: (0, col0 // n), pipeline_mode=pl.Buffered(1))],
        out_specs=(pl.BlockSpec((tm, n), lambda i: (i, 0)), col, col),
        out_shape=(jax.ShapeDtypeStruct((m, n), BF16), stat, stat),
        compiler_params=_cparams(1),
        name="gmlp_in_v",
    )(h, w)


def _sgu_kernel(n_chunks, n_groups, uz_ref, gv_ref, mu_ref, rstd_ref, lg_ref, lb_ref,
                ws_ref, bs_ref, y_ref):
    gd = D_BRANCH // GMLP_GROUPS
    for c in range(n_chunks):
        rows = slice(c * GMLP_CHUNK, (c + 1) * GMLP_CHUNK)
        mu = jnp.concatenate([mu_ref[rows, :]] * (gd // LANES), axis=1)
        rstd = jnp.concatenate([rstd_ref[rows, :]] * (gd // LANES), axis=1)
        for gl in range(n_groups):
            cols = slice(gl * gd, (gl + 1) * gd)
            ws = ws_ref[gl]
            bs = bs_ref[gl]
            lg = lg_ref[:, cols]
            lb = lb_ref[:, cols]
            vn = ((gv_ref[rows, cols].astype(F32) - mu) * rstd * lg + lb).astype(BF16)
            s = jnp.dot(ws, vn, preferred_element_type=F32) + bs
            y_ref[rows, cols] = (uz_ref[rows, cols].astype(F32) * s).astype(BF16)


def _spatial_gate(uz, gv, mu, rstd, ln_g, ln_b, w_s, b_s, tm, n_groups=4):
    m = uz.shape[0]
    gd = D_BRANCH // GMLP_GROUPS
    tn = n_groups * gd
    tile = pl.BlockSpec((tm, tn), lambda r, g: (r, g))
    col = pl.BlockSpec((tm, LANES), lambda r, g: (r, 0))
    vec = pl.BlockSpec((1, tn), lambda r, g: (0, g))
    return pl.pallas_call(
        functools.partial(_sgu_kernel, tm // GMLP_CHUNK, n_groups),
        grid=(m // tm, GMLP_GROUPS // n_groups),
        in_specs=[tile, tile, col, col, vec, vec,
                  pl.BlockSpec((n_groups, GMLP_CHUNK, GMLP_CHUNK), lambda r, g: (g, 0, 0)),
                  pl.BlockSpec((n_groups, GMLP_CHUNK, 1), lambda r, g: (g, 0, 0))],
        out_specs=tile,
        out_shape=jax.ShapeDtypeStruct(uz.shape, BF16),
        compiler_params=_cparams(2),
        name="gmlp_spatial_gate",
    )(uz, gv, mu, rstd, ln_g.reshape(1, D_BRANCH), ln_b.reshape(1, D_BRANCH),
      w_s, b_s.reshape(GMLP_GROUPS, GMLP_CHUNK, 1))


def _wout_kernel(mode, y_ref, w_ref, x_ref, gate_ref, *refs):
    if len(y_ref.shape) == 3:
        y = jnp.concatenate([y_ref[s] for s in range(y_ref.shape[0])], axis=1)
    else:
        y = y_ref[...]
    acc = jnp.dot(y, w_ref[...], preferred_element_type=F32)
    xn = x_ref[...] + gate_ref[...] * acc
    if mode == "final":
        g_ref, o_ref = refs
        o_ref[...] = xn * lax.rsqrt(jnp.mean(xn * xn, axis=-1, keepdims=True) + EPS) * g_ref[...]
    elif mode == "next":
        g_ref, sc_ref, sh_ref, xo_ref, h_ref = refs
        xo_ref[...] = xn
        h_ref[...] = _mod_rmsnorm(xn, g_ref[...], sc_ref[...], sh_ref[...]).astype(BF16)
    else:
        refs[0][...] = xn


def _out_proj(y, w_out, x2, mods, layer, who, mode, norm_vec, tm):
    m = x2.shape[0]
    row = pl.BlockSpec((tm, D_MODEL), lambda r: (r, 0))
    if y.ndim == 3:
        y_spec = pl.BlockSpec((SLABS, tm, LANES), lambda r: (0, r, 0))
    else:
        y_spec = pl.BlockSpec((tm, D_BRANCH), lambda r: (r, 0))
    in_specs = [y_spec,
                pl.BlockSpec((None, D_BRANCH, D_MODEL), lambda r: (layer, 0, 0),
                             pipeline_mode=pl.Buffered(1)),
                row, _mod_spec(layer, 2, who)]
    args = [y, w_out, x2, mods]
    xs = jax.ShapeDtypeStruct((m, D_MODEL), F32)
    if mode == "final":
        in_specs.append(pl.BlockSpec((1, D_MODEL), lambda r: (0, 0)))
        args.append(norm_vec.reshape(1, D_MODEL))
        out_shape, out_specs = xs, row
    elif mode == "next":
        in_specs += [_row_spec(layer + 1), _mod_spec(layer + 1, 1, who), _mod_spec(layer + 1, 0, who)]
        args += [norm_vec, mods, mods]
        out_shape = (xs, jax.ShapeDtypeStruct((m, D_MODEL), BF16))
        out_specs = (row, row)
    else:
        out_shape, out_specs = xs, row
    return pl.pallas_call(
        functools.partial(_wout_kernel, mode),
        grid=(m // tm,),
        in_specs=in_specs,
        out_specs=out_specs,
        out_shape=out_shape,
        compiler_params=_cparams(1),
        name="out_proj_" + mode,
    )(*args)


def kernel(x, c, ctx, c_ctx, norm_g, ada_w, ada_b, w_out, fnet_w_in, fnet_w_mix, attn_w_in, attn_sink,
           gmlp_w_in, gmlp_w_s, gmlp_b_s, gmlp_ln_g, gmlp_ln_b, final_g):
    batch, seq, d = x.shape
    ctx_len = ctx.shape[1]
    assert d == D_MODEL and seq % (DFT_NB * 8) == 0 and seq % GRID_W == 0 and batch < 8
    m_lat, m_ctx = batch * seq, batch * ctx_len
    tm_lat = min(PROJ_TM, seq)
    tm_ctx = m_ctx
    tm_out = OUT_TM

    def who_lat(tm):
        return lambda r: (r * tm) // seq

    who_ctx = lambda r: batch

    cvec = jnp.zeros((8, d), F32).at[:batch].set(c).at[batch].set(c_ctx)
    mods = _mods(cvec, ada_w, ada_b).reshape(DEPTH, 8, 3, 1, d)
    norm_g3 = norm_g.reshape(DEPTH, 1, d)
    w_out_b = w_out.astype(BF16)

    xl = x.reshape(m_lat, d)
    xc = ctx.reshape(m_ctx, d)
    hl = _prenorm(xl, norm_g3, mods, 0, who_lat(NORM_TM), NORM_TM)
    hc = _prenorm(xc, norm_g3, mods, 0, who_ctx, tm_ctx)


    def finish(y, x2, layer, who_fn, need_next, is_final):
        if is_final:
            return _out_proj(y, w_out_b, x2, mods, layer, who_fn, "final", final_g, tm_out), None
        if need_next:
            return _out_proj(y, w_out_b, x2, mods, layer, who_fn, "next", norm_g3, tm_out)
        return _out_proj(y, w_out_b, x2, mods, layer, who_fn, "plain", None, tm_out), None

    cs_all = _fold_mix(fnet_w_mix)
    out = None
    for i in range(DEPTH):
        kind, j = i % 3, i // 3
        need_ctx = i < DEPTH - 1
        last = i == DEPTH - 1
        if kind == 0:
            cs = cs_all[j]
            tm_dft = (seq // DFT_NB) * DFT_ROWS
            w_u = fnet_w_in[j, :, :D_BRANCH].astype(BF16)
            ar, ai = _proj_channel_dft(hl, w_u, cs, "fnet_in_u", tm_dft, seq=seq)
            zg = _proj_silu_slabs(hl, fnet_w_in, D_BRANCH, "fnet_in_z", tm_lat, w_layer=j)
            y = _position_dft(ar, ai, zg, batch, seq)
            yc = None
            if need_ctx:
                ar, ai = _proj_channel_dft(hc, w_u, cs, "fnet_in_u_ctx", tm_ctx)
                zg = _proj_silu_slabs(hc, fnet_w_in, D_BRANCH, "fnet_in_z_ctx", tm_ctx, w_layer=j)
                yc = _position_dft_dense(ar, ai, zg, batch, ctx_len)
        elif kind == 1:
            kvw = KV_HEADS * HEAD_DIM
            k0, v0, z0 = D_BRANCH, D_BRANCH + kvw, D_BRANCH + 2 * kvw
            sink = attn_sink[j].reshape(KV_HEADS, Q_GROUP)
            scale = HEAD_DIM ** -0.5 * LOG2_E
            q_tabs = [jnp.asarray(t) for t in _rope_tables(seq, scale)]
            k_tabs = [jnp.asarray(t) for t in _rope_tables(seq)]
            q = _proj_rope(hl, attn_w_in, 0, D_BRANCH, *q_tabs, seq, "attn_in_q", tm_lat, w_layer=j)
            kd = _proj_rope(hl, attn_w_in, k0, kvw, *k_tabs, seq, "attn_in_k", tm_lat, w_layer=j,
                            dup_heads=True)
            vd = _proj_simple(hl, attn_w_in, v0, kvw, _epi_cast_dup, "attn_in_v", tm_lat, w_layer=j, out_mult=2)
            zg = _proj_simple(hl, attn_w_in, z0, D_BRANCH, _epi_silu, "attn_in_z", tm_lat, w_layer=j)
            kcd = _proj_simple(hc, attn_w_in, k0, kvw, _epi_cast_dup, "attn_in_kc", tm_ctx, w_layer=j, out_mult=2)
            vcd = _proj_simple(hc, attn_w_in, v0, kvw, _epi_cast_dup, "attn_in_vc", tm_ctx, w_layer=j, out_mult=2)
            y = _attention(q, kd, vd, kcd, vcd, zg, sink, batch, seq, ctx_len, True)
            yc = None
            if need_ctx:
                qc = _proj_simple(hc, attn_w_in, 0, D_BRANCH, functools.partial(_epi_scale_cast, scale),
                                  "attn_in_qc", tm_ctx, w_layer=j)
                zgc = _proj_simple(hc, attn_w_in, z0, D_BRANCH, _epi_silu, "attn_in_zc", tm_ctx, w_layer=j)
                yc = _attention(qc, None, None, kcd, vcd, zgc, sink, batch, ctx_len, ctx_len, False)
        else:
            w_v = gmlp_w_in[j, :, D_BRANCH:2 * D_BRANCH].astype(BF16)
            ws = gmlp_w_s[j].astype(BF16)

            def gmlp_branch(h, tm):
                uz = _proj_simple(h, gmlp_w_in, (0, 2 * D_BRANCH), D_BRANCH, _epi_gelu_times_silu,
                                  "gmlp_in_uz", tm, tn=PROJ_TN // 2, w_layer=j)
                gv, mu, rstd = _proj_gelu_stats(h, w_v, 0, min(tm, STATS_TM))
                return _spatial_gate(uz, gv, mu, rstd, gmlp_ln_g[j], gmlp_ln_b[j], ws, gmlp_b_s[j],
                                     min(tm, GATE_TM), n_groups=GMLP_GROUPS)

            y = gmlp_branch(hl, tm_lat)
            yc = gmlp_branch(hc, tm_ctx) if need_ctx else None

        res, hl = finish(y, xl, i, who_lat(tm_out), not last, last)
        if last:
            out = res
        else:
            xl = res
        if need_ctx:
            xc, hc = finish(yc, xc, i, who_ctx, i + 1 < DEPTH - 1 or (i + 1) % 3 == 1, False)
    return out.reshape(batch, seq, d)
```

```python
import functools
import math

import numpy as np
import jax
import jax.numpy as jnp
from jax import lax
from jax.experimental import pallas as pl
from jax.experimental.pallas import tpu as pltpu

F32 = jnp.float32
BF16 = jnp.bfloat16

D_MODEL = 2048
D_BRANCH = 4096
DEPTH = 4
GRID_W = 64
FNET_GROUPS = 16
FNET_GROUP_DIM = 256
HEAD_DIM = 64
KV_HEADS = 8
Q_GROUP = 8
ATTN_BLOCK = 128
ATTN_HEADS_PER_STEP = 8
ROPE_BASE = 10000.0
GMLP_CHUNK = 128
GMLP_GROUPS = 16
EPS = 1e-6
NEG_INF = -1e30
LOG2_E = math.log2(math.e)

LANES = 128
MXU_COLS = 256
PROJ_TN = 1024
PROJ_TM = 2048
OUT_TM = 512
STATS_TM = 256
GATE_TM = 256
NORM_TM = 1024
DFT_NB = 128
DFT_ROWS = 16
SLABS = D_BRANCH // LANES
VMEM_LIMIT = 56 * 1024 * 1024


def _cparams(n_axes, vmem=VMEM_LIMIT):
    return pltpu.CompilerParams(dimension_semantics=("arbitrary",) * n_axes,
                                vmem_limit_bytes=vmem)


def _silu(z):
    return 0.5 * z * (1.0 + jnp.tanh(0.5 * z))


def _gelu_tanh(x):
    c = math.sqrt(2.0 / math.pi)
    return 0.5 * x * (1.0 + jnp.tanh(c * (x + 0.044715 * (x * x * x))))


def _mod_rmsnorm(x, g, scale, shift):
    y = x * lax.rsqrt(jnp.mean(x * x, axis=-1, keepdims=True) + EPS) * g
    return y * (1.0 + scale) + shift


def _channel_dft_matrix():
    n = FNET_GROUP_DIM
    k = np.arange(n, dtype=np.float64)
    ang = 2.0 * np.pi * np.outer(k, k) / n
    s = 1.0 / math.sqrt(n)
    return np.concatenate([np.cos(ang) * s, -np.sin(ang) * s], axis=1).astype(np.float32)


def _position_dft_matrices(seq):
    na, nb = seq // DFT_NB, DFT_NB
    a = np.arange(na, dtype=np.float64)
    b = np.arange(nb, dtype=np.float64)
    ang = 2.0 * np.pi * (a[None, None, :] * a[None, :, None] / na + b[:, None, None] * a[None, :, None] / seq)
    mr = np.cos(ang) / math.sqrt(na)
    mi = -np.sin(ang) / math.sqrt(na)
    fa = np.concatenate([np.concatenate([mr, -mi], axis=2), np.concatenate([mi, mr], axis=2)], axis=1)
    fa = np.concatenate([fa[0::2], fa[1::2]], axis=2)
    angb = 2.0 * np.pi * np.outer(b, b) / nb
    fb = np.concatenate([np.cos(angb), np.sin(angb)], axis=1) / math.sqrt(nb)
    return fa.astype(np.float32), fb.astype(np.float32)


def _dense_dft_matrix(n):
    k = np.arange(n, dtype=np.float64)
    ang = 2.0 * np.pi * np.outer(k, k) / n
    return (np.concatenate([np.cos(ang), np.sin(ang)], axis=1) / math.sqrt(n)).astype(np.float32)


def _rope_tables(seq, scale=1.0):
    nf = HEAD_DIM // 4
    inv = ROPE_BASE ** (-np.arange(nf, dtype=np.float64) / nf)
    t = np.arange(seq)
    rows = (t // GRID_W).astype(np.float64)
    cols = (t % GRID_W).astype(np.float64)
    parts_c, parts_s = [], []
    for pos in (rows, cols):
        ang = pos[:, None] * inv[None, :]
        parts_c += [np.cos(ang), np.cos(ang)]
        parts_s += [-np.sin(ang), np.sin(ang)]
    cos = np.concatenate(parts_c, axis=1) * scale
    sin = np.concatenate(parts_s, axis=1) * scale
    reps = LANES // HEAD_DIM
    return (np.tile(cos, (1, reps)).astype(np.float32), np.tile(sin, (1, reps)).astype(np.float32))


def _mods_kernel(cv_ref, w_ref, b_ref, o_ref):
    a = _silu(cv_ref[...])
    a_hi = a.astype(BF16)
    a_lo = (a - a_hi.astype(F32)).astype(BF16)
    w = w_ref[...].astype(BF16)
    part = jnp.dot(a_hi, w, preferred_element_type=F32) + jnp.dot(a_lo, w, preferred_element_type=F32)

    @pl.when(pl.program_id(1) == 0)
    def _():
        o_ref[...] = part + b_ref[...]

    @pl.when(pl.program_id(1) > 0)
    def _():
        o_ref[...] += part


def _mods(cvec, ada_w, ada_b):
    depth, d, n3 = ada_w.shape
    tk = 512
    return pl.pallas_call(
        _mods_kernel,
        grid=(depth, d // tk),
        in_specs=[pl.BlockSpec((8, tk), lambda i, k: (0, k)),
                  pl.BlockSpec((None, tk, n3), lambda i, k: (i, k, 0)),
                  pl.BlockSpec((None, 1, n3), lambda i, k: (i, 0, 0))],
        out_specs=pl.BlockSpec((None, 8, n3), lambda i, k: (i, 0, 0)),
        out_shape=jax.ShapeDtypeStruct((depth, 8, n3), F32),
        compiler_params=_cparams(2),
        name="ada_mods",
    )(cvec, ada_w, ada_b.reshape(depth, 1, n3))


def _mod_spec(layer, kind, who_of_row):
    return pl.BlockSpec((None, None, None, 1, D_MODEL),
                        lambda r, *_: (layer, who_of_row(r), kind, 0, 0))


def _row_spec(vec_layer):
    return pl.BlockSpec((None, 1, D_MODEL), lambda r, *_: (vec_layer, 0, 0))


def _prenorm_kernel(x_ref, g_ref, sc_ref, sh_ref, h_ref):
    h_ref[...] = _mod_rmsnorm(x_ref[...], g_ref[...], sc_ref[...], sh_ref[...]).astype(BF16)


def _prenorm(x2, norm_g3, mods, layer, who, tm):
    m = x2.shape[0]
    return pl.pallas_call(
        _prenorm_kernel,
        grid=(m // tm,),
        in_specs=[pl.BlockSpec((tm, D_MODEL), lambda r: (r, 0)),
                  _row_spec(layer), _mod_spec(layer, 1, who), _mod_spec(layer, 0, who)],
        out_specs=pl.BlockSpec((tm, D_MODEL), lambda r: (r, 0)),
        out_shape=jax.ShapeDtypeStruct((m, D_MODEL), BF16),
        compiler_params=_cparams(1),
        name="prenorm",
    )(x2, norm_g3, mods, mods)


def _proj_kernel(epilogue, n_extra, chunk, n_w, a_ref, *rest):
    w_refs, rest = rest[:n_w], rest[n_w:]
    if len(a_ref.shape) == 3:
        a_flat = rest[-1]
        rest = rest[:-1]

        @pl.when(pl.program_id(1) == 0)
        def _():
            a_flat[...] = a_ref[...].reshape(a_flat.shape)

        a = a_flat[...]
    else:
        a = a_ref[...]
    for c0 in range(0, w_refs[0].shape[1], chunk):
        accs = [jnp.dot(a, w_ref[:, c0:c0 + chunk].astype(BF16), preferred_element_type=F32)
                for w_ref in w_refs]
        epilogue(accs[0] if n_w == 1 else accs, c0, rest[:n_extra], rest[n_extra:])


def _proj(h, w, col0, ncols, tm, tn, epilogue, extras, extra_specs, out_shapes, out_specs, name,
          lhs_spec=None, chunk=MXU_COLS, w_layer=None):
    k = h.shape[-1]
    m = h.size // k
    col0s = col0 if isinstance(col0, tuple) else (col0,)
    scratch = [pltpu.VMEM((tm, k), h.dtype)] if lhs_spec is not None else []
    if lhs_spec is None:
        lhs_spec = pl.BlockSpec((tm, k), lambda i, j: (i, 0))

    def w_spec(off):
        if w.ndim == 3:
            return pl.BlockSpec((None, k, tn), lambda i, j: (w_layer, 0, j + off))
        return pl.BlockSpec((k, tn), lambda i, j: (0, j + off))

    return pl.pallas_call(
        functools.partial(_proj_kernel, epilogue, len(extras), min(chunk, tn), len(col0s)),
        grid=(m // tm, ncols // tn),
        in_specs=[lhs_spec] + [w_spec(c // tn) for c in col0s] + list(extra_specs),
        out_specs=out_specs,
        out_shape=out_shapes,
        scratch_shapes=scratch,
        compiler_params=_cparams(2),
        name=name,
    )(h, *([w] * len(col0s)), *extras)


def _epi_silu(acc, c0, extras, outs):
    outs[0][:, c0:c0 + acc.shape[1]] = _silu(acc).astype(BF16)


def _epi_silu_slabs(acc, c0, extras, outs):
    for t in range(acc.shape[1] // LANES):
        outs[0][c0 // LANES + t] = _silu(acc[:, t * LANES:(t + 1) * LANES]).astype(BF16)


def _epi_gelu_times_silu(accs, c0, extras, outs):
    acc_u, acc_z = accs
    outs[0][:, c0:c0 + acc_u.shape[1]] = (_gelu_tanh(acc_u) * _silu(acc_z)).astype(BF16)


def _epi_scale_cast(scale, acc, c0, extras, outs):
    outs[0][:, c0:c0 + acc.shape[1]] = (acc * scale).astype(BF16)


def _epi_channel_dft(split_rows, acc, c0, extras, outs):
    cs_ref = extras[0]
    ar_ref, ai_ref = outs
    tm = acc.shape[0]
    for gl in range(acc.shape[1] // FNET_GROUP_DIM):
        g = c0 // FNET_GROUP_DIM + gl
        ub = acc[:, gl * FNET_GROUP_DIM:(gl + 1) * FNET_GROUP_DIM].astype(BF16)
        ab = jnp.dot(ub, cs_ref[g], preferred_element_type=F32)
        if split_rows:
            ab4 = ab.reshape(tm // DFT_ROWS, 2, DFT_ROWS // 2, ab.shape[1])
            ab = jnp.concatenate([ab4[:, hf].reshape(tm // 2, ab.shape[1]) for hf in range(2)], axis=0)
        ab = ab.astype(BF16)
        ar_ref[2 * g] = ab[:, 0:128]
        ar_ref[2 * g + 1] = ab[:, 128:256]
        ai_ref[2 * g] = ab[:, 256:384]
        ai_ref[2 * g + 1] = ab[:, 384:512]


def _store_tile(out_ref, col, y, dup_heads):
    if not dup_heads:
        out_ref[:, col:col + LANES] = y.astype(BF16)
        return
    lane = lax.broadcasted_iota(jnp.int32, (1, LANES), 1)
    lo = lane < HEAD_DIM
    swapped = pltpu.roll(y, HEAD_DIM, 1)
    out_ref[:, 2 * col:2 * col + LANES] = jnp.where(lo, y, swapped).astype(BF16)
    out_ref[:, 2 * col + LANES:2 * col + 2 * LANES] = jnp.where(lo, swapped, y).astype(BF16)


def _epi_rope(dup_heads, acc, c0, extras, outs):
    cos = extras[0][...]
    sin = extras[1][...]
    lane = lax.broadcasted_iota(jnp.int32, (1, LANES), 1)
    first = (lane % 32) < 16
    for t in range(acc.shape[1] // LANES):
        x = acc[:, t * LANES:(t + 1) * LANES]
        partner = jnp.where(first, pltpu.roll(x, LANES - 16, 1), pltpu.roll(x, 16, 1))
        _store_tile(outs[0], c0 + t * LANES, x * cos + partner * sin, dup_heads)


def _epi_cast_dup(acc, c0, extras, outs):
    for t in range(acc.shape[1] // LANES):
        _store_tile(outs[0], c0 + t * LANES, acc[:, t * LANES:(t + 1) * LANES], True)


def _epi_rope_k_cast_v(accs, c0, extras, outs):
    _epi_rope(True, accs[0], c0, extras, outs[:1])
    _epi_cast_dup(accs[1], c0, extras, outs[1:])


def _proj_keys_values(h, w, k0, v0, ncols, cos_t, sin_t, seq, name, tm, w_layer):
    m = h.shape[0]
    per_batch = seq // tm
    tab_spec = pl.BlockSpec((tm, LANES), lambda i, j: (i % per_batch, 0))
    shape = jax.ShapeDtypeStruct((m, 2 * ncols), BF16)
    spec = pl.BlockSpec((tm, 2 * ncols), lambda i, j: (i, j))
    return _proj(h, w, (k0, v0), ncols, tm, ncols, _epi_rope_k_cast_v, (cos_t, sin_t), (tab_spec, tab_spec),
                 (shape, shape), (spec, spec), name, w_layer=w_layer)


def _proj_simple(h, w, col0, ncols, epilogue, name, tm, tn=PROJ_TN, w_layer=None, out_mult=1):
    m = h.shape[0]
    tn = min(tn, ncols)
    return _proj(h, w, col0, ncols, tm, tn, epilogue, (), (),
                 jax.ShapeDtypeStruct((m, out_mult * ncols), BF16),
                 pl.BlockSpec((tm, out_mult * tn), lambda i, j: (i, j)), name, w_layer=w_layer)


def _proj_rope(h, w, col0, ncols, cos_t, sin_t, seq, name, tm, tn=PROJ_TN, w_layer=None):
    m = h.shape[0]
    per_batch = seq // tm
    tab_spec = pl.BlockSpec((tm, LANES), lambda i, j: (i % per_batch, 0))
    return _proj(h, w, col0, ncols, tm, tn, functools.partial(_epi_rope, False),
                 (cos_t, sin_t), (tab_spec, tab_spec),
                 jax.ShapeDtypeStruct((m, ncols), BF16),
                 pl.BlockSpec((tm, tn), lambda i, j: (i, j)), name, w_layer=w_layer)


def _proj_silu_slabs(h, w, col0, name, tm, tn=PROJ_TN, w_layer=None):
    m = h.shape[0]
    return _proj(h, w, col0, D_BRANCH, tm, tn, _epi_silu_slabs, (), (),
                 jax.ShapeDtypeStruct((SLABS, m, LANES), BF16),
                 pl.BlockSpec((tn // LANES, tm, LANES), lambda i, j: (j, i, 0)), name, w_layer=w_layer)


def _proj_channel_dft(h, w, cs, name, tm, seq=None, tn=PROJ_TN):
    m = h.shape[0]
    lhs_spec = None
    if seq is not None:
        na, tiles = seq // DFT_NB, DFT_NB // DFT_ROWS
        assert tm == na * DFT_ROWS
        h = h.reshape(m // seq, na, DFT_NB, D_MODEL)
        lhs_spec = pl.BlockSpec((None, na, DFT_ROWS, D_MODEL), lambda i, j: (i // tiles, 0, i % tiles, 0))
    slab_shape = jax.ShapeDtypeStruct((SLABS, m, LANES), BF16)
    slab_spec = pl.BlockSpec((tn // LANES, tm, LANES), lambda i, j: (j, i, 0))
    groups = tn // FNET_GROUP_DIM
    cs_spec = pl.BlockSpec((groups,) + cs.shape[1:], lambda i, j: (j, 0, 0))
    return _proj(h, w, 0, D_BRANCH, tm, tn, functools.partial(_epi_channel_dft, seq is not None), (cs,), (cs_spec,),
                 (slab_shape, slab_shape), (slab_spec, slab_spec), name, lhs_spec=lhs_spec,
                 chunk=2 * MXU_COLS)


def _fold_mix_kernel(c_ref, s_ref, wm_ref, o_ref):
    gd = FNET_GROUP_DIM
    hp = lax.Precision.HIGHEST
    for g in range(wm_ref.shape[0]):
        wm = wm_ref[g]
        o_ref[g, :, 0:gd] = jnp.dot(c_ref[...], wm, preferred_element_type=F32, precision=hp).astype(BF16)
        o_ref[g, :, gd:2 * gd] = jnp.dot(s_ref[...], wm, preferred_element_type=F32, precision=hp).astype(BF16)


def _fold_mix(w_mix, groups_per_step=4):
    layers = w_mix.shape[0]
    gd = FNET_GROUP_DIM
    cs = _channel_dft_matrix()
    mat = pl.BlockSpec((gd, gd), lambda l, g: (0, 0))
    return pl.pallas_call(
        _fold_mix_kernel,
        grid=(layers, FNET_GROUPS // groups_per_step),
        in_specs=[mat, mat, pl.BlockSpec((None, groups_per_step, gd, gd), lambda l, g: (l, g, 0, 0))],
        out_specs=pl.BlockSpec((None, groups_per_step, gd, 2 * gd), lambda l, g: (l, g, 0, 0)),
        out_shape=jax.ShapeDtypeStruct((layers, FNET_GROUPS, gd, 2 * gd), BF16),
        compiler_params=_cparams(2),
        name="fnet_fold_mix",
    )(jnp.asarray(cs[:, :gd]), jnp.asarray(cs[:, gd:]), w_mix)


def _dft_kernel(na, pitch, gpitch, ar_ref, ai_ref, zg_ref, fa_ref, fb_ref, y_ref,
                xr_ref, xi_ref, er_ref, ei_ref, g_ref):
    half_rows = na * (DFT_ROWS // 2)
    xr_ref[...] = ar_ref[...].astype(F32)
    xi_ref[...] = ai_ref[...].astype(F32)

    def gather_a(b):
        start = (b // (DFT_ROWS // 2)) * half_rows + b % (DFT_ROWS // 2)
        zr = xr_ref[pl.ds(start, na, stride=DFT_ROWS // 2), :]
        zi = xi_ref[pl.ds(start, na, stride=DFT_ROWS // 2), :]
        return jnp.concatenate([zr, zi], axis=0).astype(BF16)

    def stage_a(pair, carry):
        b0 = 2 * pair
        d0, d1 = gather_a(b0), gather_a(b0 + 1)
        zero = jnp.zeros_like(d0)
        rhs = jnp.concatenate([jnp.concatenate([d0, zero], axis=1),
                               jnp.concatenate([zero, d1], axis=1)], axis=0)
        e = jnp.dot(fa_ref[pair], rhs, preferred_element_type=F32)
        for j in range(2):
            off = pl.multiple_of((b0 + j) * pitch, 8)
            er_ref[pl.ds(off, na), :] = e[:na, j * LANES:(j + 1) * LANES]
            ei_ref[pl.ds(off, na), :] = e[na:, j * LANES:(j + 1) * LANES]
        return carry

    lax.fori_loop(0, DFT_NB // 2, stage_a, 0, unroll=64)

    def gather_b(ka):
        er = er_ref[pl.ds(ka, DFT_NB, stride=pitch), :]
        ei = ei_ref[pl.ds(ka, DFT_NB, stride=pitch), :]
        return jnp.concatenate([er, ei], axis=0).astype(BF16)

    def stage_b(pair, carry):
        ka0 = 2 * pair
        rhs = jnp.concatenate([gather_b(ka0), gather_b(ka0 + 1)], axis=1)
        g = jnp.dot(fb_ref[...], rhs, preferred_element_type=F32)
        for j in range(2):
            off = pl.multiple_of((ka0 + j) * gpitch, 8)
            g_ref[pl.ds(off, DFT_NB), :] = g[:, j * LANES:(j + 1) * LANES]
        return carry

    lax.fori_loop(0, na // 2, stage_b, 0, unroll=32)

    def gate(kb, carry):
        rows = pl.ds(pl.multiple_of(kb * na, na), na)
        g = g_ref[pl.ds(kb, na, stride=gpitch), :]
        y_ref[rows, :] = (g * zg_ref[rows, :].astype(F32)).astype(BF16)
        return carry

    lax.fori_loop(0, DFT_NB, gate, 0, unroll=32)


def _position_dft(ar, ai, zg, batch, seq):
    na = seq // DFT_NB
    pitch = na + 8
    gpitch = DFT_NB + 8
    fa_np, fb_np = _position_dft_matrices(seq)
    fa = jnp.asarray(fa_np).astype(BF16)
    fb = jnp.asarray(fb_np).astype(BF16)
    slab = pl.BlockSpec((None, seq, LANES), lambda s, b: (s, b, 0))
    return pl.pallas_call(
        functools.partial(_dft_kernel, na, pitch, gpitch),
        grid=(SLABS, batch),
        in_specs=[slab, slab, slab,
                  pl.BlockSpec(fa.shape, lambda s, b: (0, 0, 0)),
                  pl.BlockSpec(fb.shape, lambda s, b: (0, 0))],
        out_specs=slab,
        out_shape=jax.ShapeDtypeStruct(ar.shape, BF16),
        scratch_shapes=[pltpu.VMEM((seq, LANES), F32),
                        pltpu.VMEM((seq, LANES), F32),
                        pltpu.VMEM((DFT_NB * pitch, LANES), F32),
                        pltpu.VMEM((DFT_NB * pitch, LANES), F32),
                        pltpu.VMEM((na * gpitch, LANES), F32)],
        compiler_params=_cparams(2),
        name="position_dft",
    )(ar, ai, zg, fa, fb)


def _dft_dense_kernel(ar_ref, ai_ref, zg_ref, fd_ref, y_ref):
    n = ar_ref.shape[0]
    d = jnp.concatenate([jnp.concatenate([ar_ref[s] for s in range(n)], axis=1),
                         jnp.concatenate([ai_ref[s] for s in range(n)], axis=1)], axis=0).astype(BF16)
    g = jnp.dot(fd_ref[...], d, preferred_element_type=F32)
    for s in range(n):
        y_ref[s] = (g[:, s * LANES:(s + 1) * LANES] * zg_ref[s].astype(F32)).astype(BF16)


def _position_dft_dense(ar, ai, zg, batch, seq, slabs_per_step=8):
    fd = jnp.asarray(_dense_dft_matrix(seq)).astype(BF16)
    slab = pl.BlockSpec((slabs_per_step, seq, LANES), lambda s, b: (s, b, 0))
    return pl.pallas_call(
        _dft_dense_kernel,
        grid=(SLABS // slabs_per_step, batch),
        in_specs=[slab, slab, slab, pl.BlockSpec(fd.shape, lambda s, b: (0, 0))],
        out_specs=slab,
        out_shape=jax.ShapeDtypeStruct(ar.shape, BF16),
        compiler_params=_cparams(2),
        name="position_dft_dense",
    )(ar, ai, zg, fd)


def _attn_kernel(n_band, nblk, sink_ref, q_ref, *refs):
    k_refs = refs[:n_band + 1]
    v_refs = refs[n_band + 1:2 * n_band + 2]
    zg_ref, y_ref = refs[2 * n_band + 2:]
    i = pl.program_id(2)
    blk = ATTN_BLOCK
    n_pair = Q_GROUP // 2
    qw = Q_GROUP * HEAD_DIM

    lane = lax.broadcasted_iota(jnp.int32, (1, LANES), 1)
    lo = lane < HEAD_DIM
    dn = (((1,), (1,)), ((), ()))
    if n_band:
        r = lax.broadcasted_iota(jnp.int32, (blk, blk), 0)
        c = lax.broadcasted_iota(jnp.int32, (blk, blk), 1)
        prev_ok = c >= r + jnp.where(i > 0, 0, blk)
        next_ok = c <= r - jnp.where(i < nblk - 1, 0, blk)

    def scores(hh):
        kl = slice(hh * LANES, (hh + 1) * LANES)
        keys = jnp.concatenate([ref[:, kl] for ref in k_refs], axis=0)
        q4 = jnp.concatenate([q_ref[:, hh * qw + t * LANES:hh * qw + (t + 1) * LANES]
                              for t in range(n_pair)], axis=0)
        qzero = jnp.zeros_like(q4)
        return [lax.dot_general(qm, keys, dn, preferred_element_type=F32)
                for qm in (jnp.where(lo, q4, qzero), jnp.where(lo, qzero, q4))]

    def finish(hh, s_both):
        h = pl.program_id(1) * ATTN_HEADS_PER_STEP + hh
        kl = slice(hh * LANES, (hh + 1) * LANES)
        vals = jnp.concatenate([ref[:, kl] for ref in v_refs], axis=0)
        ones = jnp.ones_like(vals)
        v_ext = (jnp.where(lo, vals, ones), jnp.where(lo, ones, vals))
        o_ext, sink_term = [], []
        for hd, s in enumerate(s_both):
            p_rows, sink_rows = [], []
            for t in range(n_pair):
                st = s[t * blk:(t + 1) * blk]
                parts = [st[:, j * blk:(j + 1) * blk] for j in range(st.shape[1] // blk)]
                if n_band:
                    parts[0] = jnp.where(prev_ok, parts[0], NEG_INF)
                    parts[2] = jnp.where(next_ok, parts[2], NEG_INF)
                sk = sink_ref[h, 2 * t + hd] * LOG2_E
                mx = parts[0]
                for part in parts[1:]:
                    mx = jnp.maximum(mx, part)
                mx = jnp.maximum(jnp.max(mx, axis=-1, keepdims=True), sk)
                p_rows.append(jnp.concatenate([jnp.exp2(part - mx).astype(BF16) for part in parts], axis=1))
                sink_rows.append(jnp.exp2(sk - mx))
            p = jnp.concatenate(p_rows, axis=0)
            o_ext.append(jnp.dot(p, v_ext[hd], preferred_element_type=F32))
            sink_term.append(jnp.concatenate(sink_rows, axis=0))
        o = jnp.where(lo, o_ext[0], o_ext[1])
        denom = jnp.where(lo, pltpu.roll(o_ext[0], HEAD_DIM, 1) + sink_term[0],
                          pltpu.roll(o_ext[1], HEAD_DIM, 1) + sink_term[1])
        res = o / denom
        for t in range(n_pair):
            tile = slice(hh * qw + t * LANES, hh * qw + (t + 1) * LANES)
            y_ref[:, tile] = (res[t * blk:(t + 1) * blk] * zg_ref[:, tile].astype(F32)).astype(BF16)

    pending = scores(0)
    for hh in range(ATTN_HEADS_PER_STEP):
        upcoming = scores(hh + 1) if hh + 1 < ATTN_HEADS_PER_STEP else None
        finish(hh, pending)
        pending = upcoming


def _attention(q, kd, vd, kcd, vcd, zg, sink, batch, seq, ctx_len, use_band):
    blk = ATTN_BLOCK
    nblk = seq // blk
    hps = ATTN_HEADS_PER_STEP
    qw = hps * Q_GROUP * HEAD_DIM
    kw = hps * LANES
    q_spec = pl.BlockSpec((blk, qw), lambda b, h, i: (b * nblk + i, h))
    ctx_spec = pl.BlockSpec((ctx_len, kw), lambda b, h, i: (b, h))
    if use_band:
        def band(delta):
            return pl.BlockSpec(
                (blk, kw), lambda b, h, i: (b * nblk + jnp.clip(i + delta, 0, nblk - 1), h))
        k_specs = [band(-1), band(0), band(1), ctx_spec]
        k_args, v_args = [kd, kd, kd, kcd], [vd, vd, vd, vcd]
        n_band = 3
    else:
        k_specs, k_args, v_args, n_band = [ctx_spec], [kcd], [vcd], 0
    return pl.pallas_call(
        functools.partial(_attn_kernel, n_band, nblk),
        grid=(batch, KV_HEADS // hps, nblk),
        in_specs=[pl.BlockSpec(memory_space=pltpu.SMEM), q_spec] + k_specs + k_specs + [q_spec],
        out_specs=q_spec,
        out_shape=jax.ShapeDtypeStruct(q.shape, BF16),
        compiler_params=_cparams(3),
        name="attention_band" if use_band else "attention_ctx",
    )(sink, q, *k_args, *v_args, zg)


def _gelu_stats_kernel(chunk, a_ref, w_ref, gv_ref, mu_ref, rstd_ref):
    a = a_ref[...]
    n = w_ref.shape[1]
    mean = m2 = None
    for idx, c0 in enumerate(range(0, n, chunk)):
        ge = _gelu_tanh(jnp.dot(a, w_ref[:, c0:c0 + chunk], preferred_element_type=F32))
        gv_ref[:, c0:c0 + chunk] = ge.astype(BF16)
        cmean = jnp.mean(ge, axis=-1, keepdims=True)
        d = ge - cmean
        cm2 = jnp.sum(d * d, axis=-1, keepdims=True)
        if idx == 0:
            mean, m2 = cmean, cm2
        else:
            delta = cmean - mean
            mean = mean + delta * (1.0 / (idx + 1))
            m2 = m2 + cm2 + delta * delta * (chunk * idx / (idx + 1))
    mu_ref[...] = jnp.broadcast_to(mean, mu_ref.shape)
    rstd_ref[...] = jnp.broadcast_to(lax.rsqrt(m2 * (1.0 / n) + EPS), rstd_ref.shape)


def _proj_gelu_stats(h, w, col0, tm, chunk=2 * MXU_COLS):
    m, k = h.shape
    n = D_BRANCH
    col = pl.BlockSpec((tm, LANES), lambda i: (i, 0))
    stat = jax.ShapeDtypeStruct((m, LANES), F32)
    return pl.pallas_call(
        functools.partial(_gelu_stats_kernel, chunk),
        grid=(m // tm,),
        in_specs=[pl.BlockSpec((tm, k), lambda i: (i, 0)),
                  pl.BlockSpec((k, n), lambda i: (0, col0 // n), pipeline_mode=pl.Buffered(1))],
        out_specs=(pl.BlockSpec((tm, n), lambda i: (i, 0)), col, col),
        out_shape=(jax.ShapeDtypeStruct((m, n), BF16), stat, stat),
        compiler_params=_cparams(1),
        name="gmlp_in_v",
    )(h, w)


def _sgu_kernel(n_chunks, n_groups, uz_ref, gv_ref, mu_ref, rstd_ref, lg_ref, lb_ref,
                ws_ref, bs_ref, y_ref):
    gd = D_BRANCH // GMLP_GROUPS
    for c in range(n_chunks):
        rows = slice(c * GMLP_CHUNK, (c + 1) * GMLP_CHUNK)
        mu = jnp.concatenate([mu_ref[rows, :]] * (gd // LANES), axis=1)
        rstd = jnp.concatenate([rstd_ref[rows, :]] * (gd // LANES), axis=1)
        for gl in range(n_groups):
            cols = slice(gl * gd, (gl + 1) * gd)
            ws = ws_ref[gl]
            bs = bs_ref[gl]
            lg = lg_ref[:, cols]
            lb = lb_ref[:, cols]
            vn = ((gv_ref[rows, cols].astype(F32) - mu) * rstd * lg + lb).astype(BF16)
            s = jnp.dot(ws, vn, preferred_element_type=F32) + bs
            y_ref[rows, cols] = (uz_ref[rows, cols].astype(F32) * s).astype(BF16)


def _spatial_gate(uz, gv, mu, rstd, ln_g, ln_b, w_s, b_s, tm, n_groups=4):
    m = uz.shape[0]
    gd = D_BRANCH // GMLP_GROUPS
    tn = n_groups * gd
    tile = pl.BlockSpec((tm, tn), lambda r, g: (r, g))
    col = pl.BlockSpec((tm, LANES), lambda r, g: (r, 0))
    vec = pl.BlockSpec((1, tn), lambda r, g: (0, g))
    return pl.pallas_call(
        functools.partial(_sgu_kernel, tm // GMLP_CHUNK, n_groups),
        grid=(m // tm, GMLP_GROUPS // n_groups),
        in_specs=[tile, tile, col, col, vec, vec,
                  pl.BlockSpec((n_groups, GMLP_CHUNK, GMLP_CHUNK), lambda r, g: (g, 0, 0)),
                  pl.BlockSpec((n_groups, GMLP_CHUNK, 1), lambda r, g: (g, 0, 0))],
        out_specs=tile,
        out_shape=jax.ShapeDtypeStruct(uz.shape, BF16),
        compiler_params=_cparams(2),
        name="gmlp_spatial_gate",
    )(uz, gv, mu, rstd, ln_g.reshape(1, D_BRANCH), ln_b.reshape(1, D_BRANCH),
      w_s, b_s.reshape(GMLP_GROUPS, GMLP_CHUNK, 1))


def _wout_kernel(mode, y_ref, w_ref, x_ref, gate_ref, *refs):
    if len(y_ref.shape) == 3:
        y = jnp.concatenate([y_ref[s] for s in range(y_ref.shape[0])], axis=1)
    else:
        y = y_ref[...]
    acc = jnp.dot(y, w_ref[...], preferred_element_type=F32)
    xn = x_ref[...] + gate_ref[...] * acc
    if mode == "final":
        g_ref, o_ref = refs
        o_ref[...] = xn * lax.rsqrt(jnp.mean(xn * xn, axis=-1, keepdims=True) + EPS) * g_ref[...]
    elif mode == "next":
        g_ref, sc_ref, sh_ref, xo_ref, h_ref = refs
        xo_ref[...] = xn
        h_ref[...] = _mod_rmsnorm(xn, g_ref[...], sc_ref[...], sh_ref[...]).astype(BF16)
    else:
        refs[0][...] = xn


def _out_proj(y, w_out, x2, mods, layer, who, mode, norm_vec, tm):
    m = x2.shape[0]
    row = pl.BlockSpec((tm, D_MODEL), lambda r: (r, 0))
    if y.ndim == 3:
        y_spec = pl.BlockSpec((SLABS, tm, LANES), lambda r: (0, r, 0))
    else:
        y_spec = pl.BlockSpec((tm, D_BRANCH), lambda r: (r, 0))
    in_specs = [y_spec,
                pl.BlockSpec((None, D_BRANCH, D_MODEL), lambda r: (layer, 0, 0),
                             pipeline_mode=pl.Buffered(1)),
                row, _mod_spec(layer, 2, who)]
    args = [y, w_out, x2, mods]
    xs = jax.ShapeDtypeStruct((m, D_MODEL), F32)
    if mode == "final":
        in_specs.append(pl.BlockSpec((1, D_MODEL), lambda r: (0, 0)))
        args.append(norm_vec.reshape(1, D_MODEL))
        out_shape, out_specs = xs, row
    elif mode == "next":
        in_specs += [_row_spec(layer + 1), _mod_spec(layer + 1, 1, who), _mod_spec(layer + 1, 0, who)]
        args += [norm_vec, mods, mods]
        out_shape = (xs, jax.ShapeDtypeStruct((m, D_MODEL), BF16))
        out_specs = (row, row)
    else:
        out_shape, out_specs = xs, row
    return pl.pallas_call(
        functools.partial(_wout_kernel, mode),
        grid=(m // tm,),
        in_specs=in_specs,
        out_specs=out_specs,
        out_shape=out_shape,
        compiler_params=_cparams(1),
        name="out_proj_" + mode,
    )(*args)


def kernel(x, c, ctx, c_ctx, norm_g, ada_w, ada_b, w_out, fnet_w_in, fnet_w_mix, attn_w_in, attn_sink,
           gmlp_w_in, gmlp_w_s, gmlp_b_s, gmlp_ln_g, gmlp_ln_b, final_g):
    batch, seq, d = x.shape
    ctx_len = ctx.shape[1]
    assert d == D_MODEL and seq % (DFT_NB * 8) == 0 and seq % GRID_W == 0 and batch < 8
    m_lat, m_ctx = batch * seq, batch * ctx_len
    tm_lat = min(PROJ_TM, seq)
    tm_ctx = m_ctx
    tm_out = OUT_TM

    def who_lat(tm):
        return lambda r: (r * tm) // seq

    who_ctx = lambda r: batch

    cvec = jnp.zeros((8, d), F32).at[:batch].set(c).at[batch].set(c_ctx)
    mods = _mods(cvec, ada_w, ada_b).reshape(DEPTH, 8, 3, 1, d)
    norm_g3 = norm_g.reshape(DEPTH, 1, d)
    w_out_b = w_out.astype(BF16)

    xl = x.reshape(m_lat, d)
    xc = ctx.reshape(m_ctx, d)
    hl = _prenorm(xl, norm_g3, mods, 0, who_lat(NORM_TM), NORM_TM)
    hc = _prenorm(xc, norm_g3, mods, 0, who_ctx, tm_ctx)


    def finish(y, x2, layer, who_fn, need_next, is_final):
        if is_final:
            return _out_proj(y, w_out_b, x2, mods, layer, who_fn, "final", final_g, tm_out), None
        if need_next:
            return _out_proj(y, w_out_b, x2, mods, layer, who_fn, "next", norm_g3, tm_out)
        return _out_proj(y, w_out_b, x2, mods, layer, who_fn, "plain", None, tm_out), None

    cs_all = _fold_mix(fnet_w_mix)
    out = None
    for i in range(DEPTH):
        kind, j = i % 3, i // 3
        need_ctx = i < DEPTH - 1
        last = i == DEPTH - 1
        if kind == 0:
            cs = cs_all[j]
            tm_dft = (seq // DFT_NB) * DFT_ROWS
            w_u = fnet_w_in[j, :, :D_BRANCH].astype(BF16)
            ar, ai = _proj_channel_dft(hl, w_u, cs, "fnet_in_u", tm_dft, seq=seq)
            zg = _proj_silu_slabs(hl, fnet_w_in, D_BRANCH, "fnet_in_z", tm_lat, w_layer=j)
            y = _position_dft(ar, ai, zg, batch, seq)
            yc = None
            if need_ctx:
                ar, ai = _proj_channel_dft(hc, w_u, cs, "fnet_in_u_ctx", tm_ctx)
                zg = _proj_silu_slabs(hc, fnet_w_in, D_BRANCH, "fnet_in_z_ctx", tm_ctx, w_layer=j)
                yc = _position_dft_dense(ar, ai, zg, batch, ctx_len)
        elif kind == 1:
            kvw = KV_HEADS * HEAD_DIM
            k0, v0, z0 = D_BRANCH, D_BRANCH + kvw, D_BRANCH + 2 * kvw
            sink = attn_sink[j].reshape(KV_HEADS, Q_GROUP)
            scale = HEAD_DIM ** -0.5 * LOG2_E
            q_tabs = [jnp.asarray(t) for t in _rope_tables(seq, scale)]
            k_tabs = [jnp.asarray(t) for t in _rope_tables(seq)]
            q = _proj_rope(hl, attn_w_in, 0, D_BRANCH, *q_tabs, seq, "attn_in_q", tm_lat, w_layer=j)
            kd, vd = _proj_keys_values(hl, attn_w_in, k0, v0, kvw, *k_tabs, seq, "attn_in_kv", tm_lat, j)
            zg = _proj_simple(hl, attn_w_in, z0, D_BRANCH, _epi_silu, "attn_in_z", tm_lat, w_layer=j)
            kcd = _proj_simple(hc, attn_w_in, k0, kvw, _epi_cast_dup, "attn_in_kc", tm_ctx, w_layer=j, out_mult=2)
            vcd = _proj_simple(hc, attn_w_in, v0, kvw, _epi_cast_dup, "attn_in_vc", tm_ctx, w_layer=j, out_mult=2)
            y = _attention(q, kd, vd, kcd, vcd, zg, sink, batch, seq, ctx_len, True)
            yc = None
            if need_ctx:
                qc = _proj_simple(hc, attn_w_in, 0, D_BRANCH, functools.partial(_epi_scale_cast, scale),
                                  "attn_in_qc", tm_ctx, w_layer=j)
                zgc = _proj_simple(hc, attn_w_in, z0, D_BRANCH, _epi_silu, "attn_in_zc", tm_ctx, w_layer=j)
                yc = _attention(qc, None, None, kcd, vcd, zgc, sink, batch, ctx_len, ctx_len, False)
        else:
            w_v = gmlp_w_in[j, :, D_BRANCH:2 * D_BRANCH].astype(BF16)
            ws = gmlp_w_s[j].astype(BF16)

            def gmlp_branch(h, tm):
                uz = _proj_simple(h, gmlp_w_in, (0, 2 * D_BRANCH), D_BRANCH, _epi_gelu_times_silu,
                                  "gmlp_in_uz", tm, tn=PROJ_TN // 2, w_layer=j)
                gv, mu, rstd = _proj_gelu_stats(h, w_v, 0, min(tm, STATS_TM))
                return _spatial_gate(uz, gv, mu, rstd, gmlp_ln_g[j], gmlp_ln_b[j], ws, gmlp_b_s[j],
                                     min(tm, GATE_TM), n_groups=GMLP_GROUPS)

            y = gmlp_branch(hl, tm_lat)
            yc = gmlp_branch(hc, tm_ctx) if need_ctx else None

        res, hl = finish(y, xl, i, who_lat(tm_out), not last, last)
        if last:
            out = res
        else:
            xl = res
        if need_ctx:
            xc, hc = finish(yc, xc, i, who_ctx, i + 1 < DEPTH - 1 or (i + 1) % 3 == 1, False)
    return out.reshape(batch, seq, d)
```

```python
import functools
import math

import numpy as np
import jax
import jax.numpy as jnp
from jax import lax
from jax.experimental import pallas as pl
from jax.experimental.pallas import tpu as pltpu

F32 = jnp.float32
BF16 = jnp.bfloat16

D_MODEL = 2048
D_BRANCH = 4096
DEPTH = 4
GRID_W = 64
FNET_GROUPS = 16
FNET_GROUP_DIM = 256
HEAD_DIM = 64
KV_HEADS = 8
Q_GROUP = 8
ATTN_BLOCK = 128
ATTN_HEADS_PER_STEP = 8
ROPE_BASE = 10000.0
GMLP_CHUNK = 128
GMLP_GROUPS = 16
EPS = 1e-6
NEG_INF = -1e30
LOG2_E = math.log2(math.e)

LANES = 128
MXU_COLS = 256
PROJ_TN = 1024
PROJ_TM = 2048
OUT_TM = 512
STATS_TM = 256
GATE_TM = 256
NORM_TM = 1024
DFT_NB = 128
DFT_ROWS = 16
SLABS = D_BRANCH // LANES
VMEM_LIMIT = 56 * 1024 * 1024


def _cparams(n_axes, vmem=VMEM_LIMIT):
    return pltpu.CompilerParams(dimension_semantics=("arbitrary",) * n_axes,
                                vmem_limit_bytes=vmem)


def _silu(z):
    return 0.5 * z * (1.0 + jnp.tanh(0.5 * z))


def _gelu_tanh(x):
    c = math.sqrt(2.0 / math.pi)
    return 0.5 * x * (1.0 + jnp.tanh(c * (x + 0.044715 * (x * x * x))))


def _mod_rmsnorm(x, g, scale, shift):
    y = x * lax.rsqrt(jnp.mean(x * x, axis=-1, keepdims=True) + EPS) * g
    return y * (1.0 + scale) + shift


def _channel_dft_matrix():
    n = FNET_GROUP_DIM
    k = np.arange(n, dtype=np.float64)
    ang = 2.0 * np.pi * np.outer(k, k) / n
    s = 1.0 / math.sqrt(n)
    return np.concatenate([np.cos(ang) * s, -np.sin(ang) * s], axis=1).astype(np.float32)


def _position_dft_matrices(seq):
    na, nb = seq // DFT_NB, DFT_NB
    a = np.arange(na, dtype=np.float64)
    b = np.arange(nb, dtype=np.float64)
    ang = 2.0 * np.pi * (a[None, None, :] * a[None, :, None] / na + b[:, None, None] * a[None, :, None] / seq)
    mr = np.cos(ang) / math.sqrt(na)
    mi = -np.sin(ang) / math.sqrt(na)
    fa = np.concatenate([np.concatenate([mr, -mi], axis=2), np.concatenate([mi, mr], axis=2)], axis=1)
    fa = np.concatenate([fa[0::2], fa[1::2]], axis=2)
    angb = 2.0 * np.pi * np.outer(b, b) / nb
    fb = np.concatenate([np.cos(angb), np.sin(angb)], axis=1) / math.sqrt(nb)
    return fa.astype(np.float32), fb.astype(np.float32)


def _dense_dft_matrix(n):
    k = np.arange(n, dtype=np.float64)
    ang = 2.0 * np.pi * np.outer(k, k) / n
    return (np.concatenate([np.cos(ang), np.sin(ang)], axis=1) / math.sqrt(n)).astype(np.float32)


def _rope_tables(seq, scale=1.0):
    nf = HEAD_DIM // 4
    inv = ROPE_BASE ** (-np.arange(nf, dtype=np.float64) / nf)
    t = np.arange(seq)
    rows = (t // GRID_W).astype(np.float64)
    cols = (t % GRID_W).astype(np.float64)
    parts_c, parts_s = [], []
    for pos in (rows, cols):
        ang = pos[:, None] * inv[None, :]
        parts_c += [np.cos(ang), np.cos(ang)]
        parts_s += [-np.sin(ang), np.sin(ang)]
    cos = np.concatenate(parts_c, axis=1) * scale
    sin = np.concatenate(parts_s, axis=1) * scale
    reps = LANES // HEAD_DIM
    return (np.tile(cos, (1, reps)).astype(np.float32), np.tile(sin, (1, reps)).astype(np.float32))


def _mods_kernel(cv_ref, w_ref, b_ref, o_ref):
    a = _silu(cv_ref[...])
    a_hi = a.astype(BF16)
    a_lo = (a - a_hi.astype(F32)).astype(BF16)
    w = w_ref[...].astype(BF16)
    part = jnp.dot(a_hi, w, preferred_element_type=F32) + jnp.dot(a_lo, w, preferred_element_type=F32)

    @pl.when(pl.program_id(1) == 0)
    def _():
        o_ref[...] = part + b_ref[...]

    @pl.when(pl.program_id(1) > 0)
    def _():
        o_ref[...] += part


def _mods(cvec, ada_w, ada_b):
    depth, d, n3 = ada_w.shape
    tk = 512
    return pl.pallas_call(
        _mods_kernel,
        grid=(depth, d // tk),
        in_specs=[pl.BlockSpec((8, tk), lambda i, k: (0, k)),
                  pl.BlockSpec((None, tk, n3), lambda i, k: (i, k, 0)),
                  pl.BlockSpec((None, 1, n3), lambda i, k: (i, 0, 0))],
        out_specs=pl.BlockSpec((None, 8, n3), lambda i, k: (i, 0, 0)),
        out_shape=jax.ShapeDtypeStruct((depth, 8, n3), F32),
        compiler_params=_cparams(2),
        name="ada_mods",
    )(cvec, ada_w, ada_b.reshape(depth, 1, n3))


def _mod_spec(layer, kind, who_of_row):
    return pl.BlockSpec((None, None, None, 1, D_MODEL),
                        lambda r, *_: (layer, who_of_row(r), kind, 0, 0))


def _row_spec(vec_layer):
    return pl.BlockSpec((None, 1, D_MODEL), lambda r, *_: (vec_layer, 0, 0))


def _prenorm_kernel(x_ref, g_ref, sc_ref, sh_ref, h_ref):
    h_ref[...] = _mod_rmsnorm(x_ref[...], g_ref[...], sc_ref[...], sh_ref[...]).astype(BF16)


def _prenorm(x2, norm_g3, mods, layer, who, tm):
    m = x2.shape[0]
    return pl.pallas_call(
        _prenorm_kernel,
        grid=(m // tm,),
        in_specs=[pl.BlockSpec((tm, D_MODEL), lambda r: (r, 0)),
                  _row_spec(layer), _mod_spec(layer, 1, who), _mod_spec(layer, 0, who)],
        out_specs=pl.BlockSpec((tm, D_MODEL), lambda r: (r, 0)),
        out_shape=jax.ShapeDtypeStruct((m, D_MODEL), BF16),
        compiler_params=_cparams(1),
        name="prenorm",
    )(x2, norm_g3, mods, mods)


def _proj_kernel(epilogue, n_extra, chunk, n_w, a_ref, *rest):
    w_refs, rest = rest[:n_w], rest[n_w:]
    if len(a_ref.shape) == 3:
        a_flat = rest[-1]
        rest = rest[:-1]

        @pl.when(pl.program_id(1) == 0)
        def _():
            a_flat[...] = a_ref[...].reshape(a_flat.shape)

        a = a_flat[...]
    else:
        a = a_ref[...]
    for c0 in range(0, w_refs[0].shape[1], chunk):
        accs = [jnp.dot(a, w_ref[:, c0:c0 + chunk].astype(BF16), preferred_element_type=F32)
                for w_ref in w_refs]
        epilogue(accs[0] if n_w == 1 else accs, c0, rest[:n_extra], rest[n_extra:])


def _proj(h, w, col0, ncols, tm, tn, epilogue, extras, extra_specs, out_shapes, out_specs, name,
          lhs_spec=None, chunk=MXU_COLS, w_layer=None):
    k = h.shape[-1]
    m = h.size // k
    col0s = col0 if isinstance(col0, tuple) else (col0,)
    scratch = [pltpu.VMEM((tm, k), h.dtype)] if lhs_spec is not None else []
    if lhs_spec is None:
        lhs_spec = pl.BlockSpec((tm, k), lambda i, j: (i, 0))

    def w_spec(off):
        if w.ndim == 3:
            return pl.BlockSpec((None, k, tn), lambda i, j: (w_layer, 0, j + off))
        return pl.BlockSpec((k, tn), lambda i, j: (0, j + off))

    return pl.pallas_call(
        functools.partial(_proj_kernel, epilogue, len(extras), min(chunk, tn), len(col0s)),
        grid=(m // tm, ncols // tn),
        in_specs=[lhs_spec] + [w_spec(c // tn) for c in col0s] + list(extra_specs),
        out_specs=out_specs,
        out_shape=out_shapes,
        scratch_shapes=scratch,
        compiler_params=_cparams(2),
        name=name,
    )(h, *([w] * len(col0s)), *extras)


def _epi_silu(acc, c0, extras, outs):
    outs[0][:, c0:c0 + acc.shape[1]] = _silu(acc).astype(BF16)


def _epi_silu_slabs(acc, c0, extras, outs):
    for t in range(acc.shape[1] // LANES):
        outs[0][c0 // LANES + t] = _silu(acc[:, t * LANES:(t + 1) * LANES]).astype(BF16)


def _epi_gelu_times_silu(accs, c0, extras, outs):
    acc_u, acc_z = accs
    outs[0][:, c0:c0 + acc_u.shape[1]] = (_gelu_tanh(acc_u) * _silu(acc_z)).astype(BF16)


def _epi_scale_cast(scale, acc, c0, extras, outs):
    outs[0][:, c0:c0 + acc.shape[1]] = (acc * scale).astype(BF16)


def _epi_channel_dft(split_rows, acc, c0, extras, outs):
    cs_ref = extras[0]
    ar_ref, ai_ref = outs
    tm = acc.shape[0]
    for gl in range(acc.shape[1] // FNET_GROUP_DIM):
        g = c0 // FNET_GROUP_DIM + gl
        ub = acc[:, gl * FNET_GROUP_DIM:(gl + 1) * FNET_GROUP_DIM].astype(BF16)
        ab = jnp.dot(ub, cs_ref[g], preferred_element_type=F32)
        if split_rows:
            ab4 = ab.reshape(tm // DFT_ROWS, 2, DFT_ROWS // 2, ab.shape[1])
            ab = jnp.concatenate([ab4[:, hf].reshape(tm // 2, ab.shape[1]) for hf in range(2)], axis=0)
        ab = ab.astype(BF16)
        ar_ref[2 * g] = ab[:, 0:128]
        ar_ref[2 * g + 1] = ab[:, 128:256]
        ai_ref[2 * g] = ab[:, 256:384]
        ai_ref[2 * g + 1] = ab[:, 384:512]


def _store_tile(out_ref, col, y, dup_heads):
    if not dup_heads:
        out_ref[:, col:col + LANES] = y.astype(BF16)
        return
    lane = lax.broadcasted_iota(jnp.int32, (1, LANES), 1)
    lo = lane < HEAD_DIM
    swapped = pltpu.roll(y, HEAD_DIM, 1)
    out_ref[:, 2 * col:2 * col + LANES] = jnp.where(lo, y, swapped).astype(BF16)
    out_ref[:, 2 * col + LANES:2 * col + 2 * LANES] = jnp.where(lo, swapped, y).astype(BF16)


def _epi_rope(dup_heads, acc, c0, extras, outs):
    cos = extras[0][...]
    sin = extras[1][...]
    lane = lax.broadcasted_iota(jnp.int32, (1, LANES), 1)
    first = (lane % 32) < 16
    for t in range(acc.shape[1] // LANES):
        x = acc[:, t * LANES:(t + 1) * LANES]
        partner = jnp.where(first, pltpu.roll(x, LANES - 16, 1), pltpu.roll(x, 16, 1))
        _store_tile(outs[0], c0 + t * LANES, x * cos + partner * sin, dup_heads)


def _epi_cast_dup(acc, c0, extras, outs):
    for t in range(acc.shape[1] // LANES):
        _store_tile(outs[0], c0 + t * LANES, acc[:, t * LANES:(t + 1) * LANES], True)


def _epi_rope_k_cast_v(accs, c0, extras, outs):
    _epi_rope(True, accs[0], c0, extras, outs[:1])
    _epi_cast_dup(accs[1], c0, extras, outs[1:])


def _epi_rope_q_silu_z(accs, c0, extras, outs):
    _epi_rope(False, accs[0], c0, extras, outs[:1])
    _epi_silu(accs[1], c0, extras, outs[1:])


def _proj_queries_gate(h, w, q0, z0, ncols, cos_t, sin_t, seq, name, tm, w_layer, tn=PROJ_TN // 2):
    m = h.shape[0]
    per_batch = seq // tm
    tab_spec = pl.BlockSpec((tm, LANES), lambda i, j: (i % per_batch, 0))
    shape = jax.ShapeDtypeStruct((m, ncols), BF16)
    spec = pl.BlockSpec((tm, tn), lambda i, j: (i, j))
    return _proj(h, w, (q0, z0), ncols, tm, tn, _epi_rope_q_silu_z, (cos_t, sin_t), (tab_spec, tab_spec),
                 (shape, shape), (spec, spec), name, w_layer=w_layer)


def _proj_keys_values(h, w, k0, v0, ncols, cos_t, sin_t, seq, name, tm, w_layer):
    m = h.shape[0]
    per_batch = seq // tm
    tab_spec = pl.BlockSpec((tm, LANES), lambda i, j: (i % per_batch, 0))
    shape = jax.ShapeDtypeStruct((m, 2 * ncols), BF16)
    spec = pl.BlockSpec((tm, 2 * ncols), lambda i, j: (i, j))
    return _proj(h, w, (k0, v0), ncols, tm, ncols, _epi_rope_k_cast_v, (cos_t, sin_t), (tab_spec, tab_spec),
                 (shape, shape), (spec, spec), name, w_layer=w_layer)


def _proj_simple(h, w, col0, ncols, epilogue, name, tm, tn=PROJ_TN, w_layer=None, out_mult=1):
    m = h.shape[0]
    tn = min(tn, ncols)
    return _proj(h, w, col0, ncols, tm, tn, epilogue, (), (),
                 jax.ShapeDtypeStruct((m, out_mult * ncols), BF16),
                 pl.BlockSpec((tm, out_mult * tn), lambda i, j: (i, j)), name, w_layer=w_layer)


def _proj_silu_slabs(h, w, col0, name, tm, tn=PROJ_TN, w_layer=None):
    m = h.shape[0]
    return _proj(h, w, col0, D_BRANCH, tm, tn, _epi_silu_slabs, (), (),
                 jax.ShapeDtypeStruct((SLABS, m, LANES), BF16),
                 pl.BlockSpec((tn // LANES, tm, LANES), lambda i, j: (j, i, 0)), name, w_layer=w_layer)


def _proj_channel_dft(h, w, cs, name, tm, seq=None, tn=PROJ_TN):
    m = h.shape[0]
    lhs_spec = None
    if seq is not None:
        na, tiles = seq // DFT_NB, DFT_NB // DFT_ROWS
        assert tm == na * DFT_ROWS
        h = h.reshape(m // seq, na, DFT_NB, D_MODEL)
        lhs_spec = pl.BlockSpec((None, na, DFT_ROWS, D_MODEL), lambda i, j: (i // tiles, 0, i % tiles, 0))
    slab_shape = jax.ShapeDtypeStruct((SLABS, m, LANES), BF16)
    slab_spec = pl.BlockSpec((tn // LANES, tm, LANES), lambda i, j: (j, i, 0))
    groups = tn // FNET_GROUP_DIM
    cs_spec = pl.BlockSpec((groups,) + cs.shape[1:], lambda i, j: (j, 0, 0))
    return _proj(h, w, 0, D_BRANCH, tm, tn, functools.partial(_epi_channel_dft, seq is not None), (cs,), (cs_spec,),
                 (slab_shape, slab_shape), (slab_spec, slab_spec), name, lhs_spec=lhs_spec,
                 chunk=2 * MXU_COLS)


def _fold_mix_kernel(c_ref, s_ref, wm_ref, o_ref):
    gd = FNET_GROUP_DIM
    hp = lax.Precision.HIGHEST
    for g in range(wm_ref.shape[0]):
        wm = wm_ref[g]
        o_ref[g, :, 0:gd] = jnp.dot(c_ref[...], wm, preferred_element_type=F32, precision=hp).astype(BF16)
        o_ref[g, :, gd:2 * gd] = jnp.dot(s_ref[...], wm, preferred_element_type=F32, precision=hp).astype(BF16)


def _fold_mix(w_mix, groups_per_step=4):
    layers = w_mix.shape[0]
    gd = FNET_GROUP_DIM
    cs = _channel_dft_matrix()
    mat = pl.BlockSpec((gd, gd), lambda l, g: (0, 0))
    return pl.pallas_call(
        _fold_mix_kernel,
        grid=(layers, FNET_GROUPS // groups_per_step),
        in_specs=[mat, mat, pl.BlockSpec((None, groups_per_step, gd, gd), lambda l, g: (l, g, 0, 0))],
        out_specs=pl.BlockSpec((None, groups_per_step, gd, 2 * gd), lambda l, g: (l, g, 0, 0)),
        out_shape=jax.ShapeDtypeStruct((layers, FNET_GROUPS, gd, 2 * gd), BF16),
        compiler_params=_cparams(2),
        name="fnet_fold_mix",
    )(jnp.asarray(cs[:, :gd]), jnp.asarray(cs[:, gd:]), w_mix)


def _dft_kernel(na, pitch, gpitch, ar_ref, ai_ref, zg_ref, fa_ref, fb_ref, y_ref,
                xr_ref, xi_ref, er_ref, ei_ref, g_ref):
    half_rows = na * (DFT_ROWS // 2)
    xr_ref[...] = ar_ref[...].astype(F32)
    xi_ref[...] = ai_ref[...].astype(F32)

    def gather_a(b):
        start = (b // (DFT_ROWS // 2)) * half_rows + b % (DFT_ROWS // 2)
        zr = xr_ref[pl.ds(start, na, stride=DFT_ROWS // 2), :]
        zi = xi_ref[pl.ds(start, na, stride=DFT_ROWS // 2), :]
        return jnp.concatenate([zr, zi], axis=0).astype(BF16)

    def stage_a(pair, carry):
        b0 = 2 * pair
        d0, d1 = gather_a(b0), gather_a(b0 + 1)
        zero = jnp.zeros_like(d0)
        rhs = jnp.concatenate([jnp.concatenate([d0, zero], axis=1),
                               jnp.concatenate([zero, d1], axis=1)], axis=0)
        e = jnp.dot(fa_ref[pair], rhs, preferred_element_type=F32)
        for j in range(2):
            off = pl.multiple_of((b0 + j) * pitch, 8)
            er_ref[pl.ds(off, na), :] = e[:na, j * LANES:(j + 1) * LANES]
            ei_ref[pl.ds(off, na), :] = e[na:, j * LANES:(j + 1) * LANES]
        return carry

    lax.fori_loop(0, DFT_NB // 2, stage_a, 0, unroll=64)

    def gather_b(ka):
        er = er_ref[pl.ds(ka, DFT_NB, stride=pitch), :]
        ei = ei_ref[pl.ds(ka, DFT_NB, stride=pitch), :]
        return jnp.concatenate([er, ei], axis=0).astype(BF16)

    def stage_b(pair, carry):
        ka0 = 2 * pair
        rhs = jnp.concatenate([gather_b(ka0), gather_b(ka0 + 1)], axis=1)
        g = jnp.dot(fb_ref[...], rhs, preferred_element_type=F32)
        for j in range(2):
            off = pl.multiple_of((ka0 + j) * gpitch, 8)
            g_ref[pl.ds(off, DFT_NB), :] = g[:, j * LANES:(j + 1) * LANES]
        return carry

    lax.fori_loop(0, na // 2, stage_b, 0, unroll=32)

    def gate(kb, carry):
        rows = pl.ds(pl.multiple_of(kb * na, na), na)
        g = g_ref[pl.ds(kb, na, stride=gpitch), :]
        y_ref[rows, :] = (g * zg_ref[rows, :].astype(F32)).astype(BF16)
        return carry

    lax.fori_loop(0, DFT_NB, gate, 0, unroll=32)


def _position_dft(ar, ai, zg, batch, seq):
    na = seq // DFT_NB
    pitch = na + 8
    gpitch = DFT_NB + 8
    fa_np, fb_np = _position_dft_matrices(seq)
    fa = jnp.asarray(fa_np).astype(BF16)
    fb = jnp.asarray(fb_np).astype(BF16)
    slab = pl.BlockSpec((None, seq, LANES), lambda s, b: (s, b, 0))
    return pl.pallas_call(
        functools.partial(_dft_kernel, na, pitch, gpitch),
        grid=(SLABS, batch),
        in_specs=[slab, slab, slab,
                  pl.BlockSpec(fa.shape, lambda s, b: (0, 0, 0)),
                  pl.BlockSpec(fb.shape, lambda s, b: (0, 0))],
        out_specs=slab,
        out_shape=jax.ShapeDtypeStruct(ar.shape, BF16),
        scratch_shapes=[pltpu.VMEM((seq, LANES), F32),
                        pltpu.VMEM((seq, LANES), F32),
                        pltpu.VMEM((DFT_NB * pitch, LANES), F32),
                        pltpu.VMEM((DFT_NB * pitch, LANES), F32),
                        pltpu.VMEM((na * gpitch, LANES), F32)],
        compiler_params=_cparams(2),
        name="position_dft",
    )(ar, ai, zg, fa, fb)


def _dft_dense_kernel(ar_ref, ai_ref, zg_ref, fd_ref, y_ref):
    n = ar_ref.shape[0]
    d = jnp.concatenate([jnp.concatenate([ar_ref[s] for s in range(n)], axis=1),
                         jnp.concatenate([ai_ref[s] for s in range(n)], axis=1)], axis=0).astype(BF16)
    g = jnp.dot(fd_ref[...], d, preferred_element_type=F32)
    for s in range(n):
        y_ref[s] = (g[:, s * LANES:(s + 1) * LANES] * zg_ref[s].astype(F32)).astype(BF16)


def _position_dft_dense(ar, ai, zg, batch, seq, slabs_per_step=8):
    fd = jnp.asarray(_dense_dft_matrix(seq)).astype(BF16)
    slab = pl.BlockSpec((slabs_per_step, seq, LANES), lambda s, b: (s, b, 0))
    return pl.pallas_call(
        _dft_dense_kernel,
        grid=(SLABS // slabs_per_step, batch),
        in_specs=[slab, slab, slab, pl.BlockSpec(fd.shape, lambda s, b: (0, 0))],
        out_specs=slab,
        out_shape=jax.ShapeDtypeStruct(ar.shape, BF16),
        compiler_params=_cparams(2),
        name="position_dft_dense",
    )(ar, ai, zg, fd)


def _attn_kernel(n_band, nblk, sink_ref, q_ref, *refs):
    k_refs = refs[:n_band + 1]
    v_refs = refs[n_band + 1:2 * n_band + 2]
    zg_ref, y_ref = refs[2 * n_band + 2:]
    i = pl.program_id(2)
    blk = ATTN_BLOCK
    n_pair = Q_GROUP // 2
    qw = Q_GROUP * HEAD_DIM

    lane = lax.broadcasted_iota(jnp.int32, (1, LANES), 1)
    lo = lane < HEAD_DIM
    dn = (((1,), (1,)), ((), ()))
    if n_band:
        r = lax.broadcasted_iota(jnp.int32, (blk, blk), 0)
        c = lax.broadcasted_iota(jnp.int32, (blk, blk), 1)
        prev_ok = c >= r + jnp.where(i > 0, 0, blk)
        next_ok = c <= r - jnp.where(i < nblk - 1, 0, blk)

    def scores(hh):
        kl = slice(hh * LANES, (hh + 1) * LANES)
        keys = jnp.concatenate([ref[:, kl] for ref in k_refs], axis=0)
        q4 = jnp.concatenate([q_ref[:, hh * qw + t * LANES:hh * qw + (t + 1) * LANES]
                              for t in range(n_pair)], axis=0)
        qzero = jnp.zeros_like(q4)
        return [lax.dot_general(qm, keys, dn, preferred_element_type=F32)
                for qm in (jnp.where(lo, q4, qzero), jnp.where(lo, qzero, q4))]

    def finish(hh, s_both):
        h = pl.program_id(1) * ATTN_HEADS_PER_STEP + hh
        kl = slice(hh * LANES, (hh + 1) * LANES)
        vals = jnp.concatenate([ref[:, kl] for ref in v_refs], axis=0)
        ones = jnp.ones_like(vals)
        v_ext = (jnp.where(lo, vals, ones), jnp.where(lo, ones, vals))
        o_ext, sink_term = [], []
        for hd, s in enumerate(s_both):
            p_rows, sink_rows = [], []
            for t in range(n_pair):
                st = s[t * blk:(t + 1) * blk]
                parts = [st[:, j * blk:(j + 1) * blk] for j in range(st.shape[1] // blk)]
                if n_band:
                    parts[0] = jnp.where(prev_ok, parts[0], NEG_INF)
                    parts[2] = jnp.where(next_ok, parts[2], NEG_INF)
                sk = sink_ref[h, 2 * t + hd] * LOG2_E
                mx = parts[0]
                for part in parts[1:]:
                    mx = jnp.maximum(mx, part)
                mx = jnp.maximum(jnp.max(mx, axis=-1, keepdims=True), sk)
                p_rows.append(jnp.concatenate([jnp.exp2(part - mx).astype(BF16) for part in parts], axis=1))
                sink_rows.append(jnp.exp2(sk - mx))
            p = jnp.concatenate(p_rows, axis=0)
            o_ext.append(jnp.dot(p, v_ext[hd], preferred_element_type=F32))
            sink_term.append(jnp.concatenate(sink_rows, axis=0))
        o = jnp.where(lo, o_ext[0], o_ext[1])
        denom = jnp.where(lo, pltpu.roll(o_ext[0], HEAD_DIM, 1) + sink_term[0],
                          pltpu.roll(o_ext[1], HEAD_DIM, 1) + sink_term[1])
        res = o / denom
        for t in range(n_pair):
            tile = slice(hh * qw + t * LANES, hh * qw + (t + 1) * LANES)
            y_ref[:, tile] = (res[t * blk:(t + 1) * blk] * zg_ref[:, tile].astype(F32)).astype(BF16)

    pending = scores(0)
    for hh in range(ATTN_HEADS_PER_STEP):
        upcoming = scores(hh + 1) if hh + 1 < ATTN_HEADS_PER_STEP else None
        finish(hh, pending)
        pending = upcoming


def _attention(q, kd, vd, kcd, vcd, zg, sink, batch, seq, ctx_len, use_band):
    blk = ATTN_BLOCK
    nblk = seq // blk
    hps = ATTN_HEADS_PER_STEP
    qw = hps * Q_GROUP * HEAD_DIM
    kw = hps * LANES
    q_spec = pl.BlockSpec((blk, qw), lambda b, h, i: (b * nblk + i, h))
    ctx_spec = pl.BlockSpec((ctx_len, kw), lambda b, h, i: (b, h))
    if use_band:
        def band(delta):
            return pl.BlockSpec(
                (blk, kw), lambda b, h, i: (b * nblk + jnp.clip(i + delta, 0, nblk - 1), h))
        k_specs = [band(-1), band(0), band(1), ctx_spec]
        k_args, v_args = [kd, kd, kd, kcd], [vd, vd, vd, vcd]
        n_band = 3
    else:
        k_specs, k_args, v_args, n_band = [ctx_spec], [kcd], [vcd], 0
    return pl.pallas_call(
        functools.partial(_attn_kernel, n_band, nblk),
        grid=(batch, KV_HEADS // hps, nblk),
        in_specs=[pl.BlockSpec(memory_space=pltpu.SMEM), q_spec] + k_specs + k_specs + [q_spec],
        out_specs=q_spec,
        out_shape=jax.ShapeDtypeStruct(q.shape, BF16),
        compiler_params=_cparams(3),
        name="attention_band" if use_band else "attention_ctx",
    )(sink, q, *k_args, *v_args, zg)


def _gelu_stats_kernel(chunk, a_ref, w_ref, gv_ref, mu_ref, rstd_ref):
    a = a_ref[...]
    n = w_ref.shape[1]
    mean = m2 = None
    for idx, c0 in enumerate(range(0, n, chunk)):
        ge = _gelu_tanh(jnp.dot(a, w_ref[:, c0:c0 + chunk], preferred_element_type=F32))
        gv_ref[:, c0:c0 + chunk] = ge.astype(BF16)
        cmean = jnp.mean(ge, axis=-1, keepdims=True)
        d = ge - cmean
        cm2 = jnp.sum(d * d, axis=-1, keepdims=True)
        if idx == 0:
            mean, m2 = cmean, cm2
        else:
            delta = cmean - mean
            mean = mean + delta * (1.0 / (idx + 1))
            m2 = m2 + cm2 + delta * delta * (chunk * idx / (idx + 1))
    mu_ref[...] = jnp.broadcast_to(mean, mu_ref.shape)
    rstd_ref[...] = jnp.broadcast_to(lax.rsqrt(m2 * (1.0 / n) + EPS), rstd_ref.shape)


def _proj_gelu_stats(h, w, col0, tm, chunk=2 * MXU_COLS):
    m, k = h.shape
    n = D_BRANCH
    col = pl.BlockSpec((tm, LANES), lambda i: (i, 0))
    stat = jax.ShapeDtypeStruct((m, LANES), F32)
    return pl.pallas_call(
        functools.partial(_gelu_stats_kernel, chunk),
        grid=(m // tm,),
        in_specs=[pl.BlockSpec((tm, k), lambda i: (i, 0)),
                  pl.BlockSpec((k, n), lambda i: (0, col0 // n), pipeline_mode=pl.Buffered(1))],
        out_specs=(pl.BlockSpec((tm, n), lambda i: (i, 0)), col, col),
        out_shape=(jax.ShapeDtypeStruct((m, n), BF16), stat, stat),
        compiler_params=_cparams(1),
        name="gmlp_in_v",
    )(h, w)


def _sgu_kernel(n_chunks, n_groups, uz_ref, gv_ref, mu_ref, rstd_ref, lg_ref, lb_ref,
                ws_ref, bs_ref, y_ref):
    gd = D_BRANCH // GMLP_GROUPS
    for c in range(n_chunks):
        rows = slice(c * GMLP_CHUNK, (c + 1) * GMLP_CHUNK)
        mu = jnp.concatenate([mu_ref[rows, :]] * (gd // LANES), axis=1)
        rstd = jnp.concatenate([rstd_ref[rows, :]] * (gd // LANES), axis=1)
        for gl in range(n_groups):
            cols = slice(gl * gd, (gl + 1) * gd)
            ws = ws_ref[gl]
            bs = bs_ref[gl]
            lg = lg_ref[:, cols]
            lb = lb_ref[:, cols]
            vn = ((gv_ref[rows, cols].astype(F32) - mu) * rstd * lg + lb).astype(BF16)
            s = jnp.dot(ws, vn, preferred_element_type=F32) + bs
            y_ref[rows, cols] = (uz_ref[rows, cols].astype(F32) * s).astype(BF16)


def _spatial_gate(uz, gv, mu, rstd, ln_g, ln_b, w_s, b_s, tm, n_groups=4):
    m = uz.shape[0]
    gd = D_BRANCH // GMLP_GROUPS
    tn = n_groups * gd
    tile = pl.BlockSpec((tm, tn), lambda r, g: (r, g))
    col = pl.BlockSpec((tm, LANES), lambda r, g: (r, 0))
    vec = pl.BlockSpec((1, tn), lambda r, g: (0, g))
    return pl.pallas_call(
        functools.partial(_sgu_kernel, tm // GMLP_CHUNK, n_groups),
        grid=(m // tm, GMLP_GROUPS // n_groups),
        in_specs=[tile, tile, col, col, vec, vec,
                  pl.BlockSpec((n_groups, GMLP_CHUNK, GMLP_CHUNK), lambda r, g: (g, 0, 0)),
                  pl.BlockSpec((n_groups, GMLP_CHUNK, 1), lambda r, g: (g, 0, 0))],
        out_specs=tile,
        out_shape=jax.ShapeDtypeStruct(uz.shape, BF16),
        compiler_params=_cparams(2),
        name="gmlp_spatial_gate",
    )(uz, gv, mu, rstd, ln_g.reshape(1, D_BRANCH), ln_b.reshape(1, D_BRANCH),
      w_s, b_s.reshape(GMLP_GROUPS, GMLP_CHUNK, 1))


def _wout_kernel(mode, y_ref, w_ref, x_ref, gate_ref, *refs):
    if len(y_ref.shape) == 3:
        y = jnp.concatenate([y_ref[s] for s in range(y_ref.shape[0])], axis=1)
    else:
        y = y_ref[...]
    acc = jnp.dot(y, w_ref[...], preferred_element_type=F32)
    xn = x_ref[...] + gate_ref[...] * acc
    if mode == "final":
        g_ref, o_ref = refs
        o_ref[...] = xn * lax.rsqrt(jnp.mean(xn * xn, axis=-1, keepdims=True) + EPS) * g_ref[...]
    elif mode == "next":
        g_ref, sc_ref, sh_ref, xo_ref, h_ref = refs
        xo_ref[...] = xn
        h_ref[...] = _mod_rmsnorm(xn, g_ref[...], sc_ref[...], sh_ref[...]).astype(BF16)
    else:
        refs[0][...] = xn


def _out_proj(y, w_out, x2, mods, layer, who, mode, norm_vec, tm):
    m = x2.shape[0]
    row = pl.BlockSpec((tm, D_MODEL), lambda r: (r, 0))
    if y.ndim == 3:
        y_spec = pl.BlockSpec((SLABS, tm, LANES), lambda r: (0, r, 0))
    else:
        y_spec = pl.BlockSpec((tm, D_BRANCH), lambda r: (r, 0))
    in_specs = [y_spec,
                pl.BlockSpec((None, D_BRANCH, D_MODEL), lambda r: (layer, 0, 0),
                             pipeline_mode=pl.Buffered(1)),
                row, _mod_spec(layer, 2, who)]
    args = [y, w_out, x2, mods]
    xs = jax.ShapeDtypeStruct((m, D_MODEL), F32)
    if mode == "final":
        in_specs.append(pl.BlockSpec((1, D_MODEL), lambda r: (0, 0)))
        args.append(norm_vec.reshape(1, D_MODEL))
        out_shape, out_specs = xs, row
    elif mode == "next":
        in_specs += [_row_spec(layer + 1), _mod_spec(layer + 1, 1, who), _mod_spec(layer + 1, 0, who)]
        args += [norm_vec, mods, mods]
        out_shape = (xs, jax.ShapeDtypeStruct((m, D_MODEL), BF16))
        out_specs = (row, row)
    else:
        out_shape, out_specs = xs, row
    return pl.pallas_call(
        functools.partial(_wout_kernel, mode),
        grid=(m // tm,),
        in_specs=in_specs,
        out_specs=out_specs,
        out_shape=out_shape,
        compiler_params=_cparams(1),
        name="out_proj_" + mode,
    )(*args)


def kernel(x, c, ctx, c_ctx, norm_g, ada_w, ada_b, w_out, fnet_w_in, fnet_w_mix, attn_w_in, attn_sink,
           gmlp_w_in, gmlp_w_s, gmlp_b_s, gmlp_ln_g, gmlp_ln_b, final_g):
    batch, seq, d = x.shape
    ctx_len = ctx.shape[1]
    assert d == D_MODEL and seq % (DFT_NB * 8) == 0 and seq % GRID_W == 0 and batch < 8
    m_lat, m_ctx = batch * seq, batch * ctx_len
    tm_lat = min(PROJ_TM, seq)
    tm_ctx = m_ctx
    tm_out = OUT_TM

    def who_lat(tm):
        return lambda r: (r * tm) // seq

    who_ctx = lambda r: batch

    cvec = jnp.zeros((8, d), F32).at[:batch].set(c).at[batch].set(c_ctx)
    mods = _mods(cvec, ada_w, ada_b).reshape(DEPTH, 8, 3, 1, d)
    norm_g3 = norm_g.reshape(DEPTH, 1, d)
    w_out_b = w_out.astype(BF16)

    xl = x.reshape(m_lat, d)
    xc = ctx.reshape(m_ctx, d)
    hl = _prenorm(xl, norm_g3, mods, 0, who_lat(NORM_TM), NORM_TM)
    hc = _prenorm(xc, norm_g3, mods, 0, who_ctx, tm_ctx)


    def finish(y, x2, layer, who_fn, need_next, is_final):
        if is_final:
            return _out_proj(y, w_out_b, x2, mods, layer, who_fn, "final", final_g, tm_out), None
        if need_next:
            return _out_proj(y, w_out_b, x2, mods, layer, who_fn, "next", norm_g3, tm_out)
        return _out_proj(y, w_out_b, x2, mods, layer, who_fn, "plain", None, tm_out), None

    cs_all = _fold_mix(fnet_w_mix)
    out = None
    for i in range(DEPTH):
        kind, j = i % 3, i // 3
        need_ctx = i < DEPTH - 1
        last = i == DEPTH - 1
        if kind == 0:
            cs = cs_all[j]
            tm_dft = (seq // DFT_NB) * DFT_ROWS
            w_u = fnet_w_in[j, :, :D_BRANCH].astype(BF16)
            ar, ai = _proj_channel_dft(hl, w_u, cs, "fnet_in_u", tm_dft, seq=seq)
            zg = _proj_silu_slabs(hl, fnet_w_in, D_BRANCH, "fnet_in_z", tm_lat, w_layer=j)
            y = _position_dft(ar, ai, zg, batch, seq)
            yc = None
            if need_ctx:
                ar, ai = _proj_channel_dft(hc, w_u, cs, "fnet_in_u_ctx", tm_ctx)
                zg = _proj_silu_slabs(hc, fnet_w_in, D_BRANCH, "fnet_in_z_ctx", tm_ctx, w_layer=j)
                yc = _position_dft_dense(ar, ai, zg, batch, ctx_len)
        elif kind == 1:
            kvw = KV_HEADS * HEAD_DIM
            k0, v0, z0 = D_BRANCH, D_BRANCH + kvw, D_BRANCH + 2 * kvw
            sink = attn_sink[j].reshape(KV_HEADS, Q_GROUP)
            scale = HEAD_DIM ** -0.5 * LOG2_E
            q_tabs = [jnp.asarray(t) for t in _rope_tables(seq, scale)]
            k_tabs = [jnp.asarray(t) for t in _rope_tables(seq)]
            q, zg = _proj_queries_gate(hl, attn_w_in, 0, z0, D_BRANCH, *q_tabs, seq, "attn_in_qz", tm_lat, j)
            kd, vd = _proj_keys_values(hl, attn_w_in, k0, v0, kvw, *k_tabs, seq, "attn_in_kv", tm_lat, j)
            kcd = _proj_simple(hc, attn_w_in, k0, kvw, _epi_cast_dup, "attn_in_kc", tm_ctx, w_layer=j, out_mult=2)
            vcd = _proj_simple(hc, attn_w_in, v0, kvw, _epi_cast_dup, "attn_in_vc", tm_ctx, w_layer=j, out_mult=2)
            y = _attention(q, kd, vd, kcd, vcd, zg, sink, batch, seq, ctx_len, True)
            yc = None
            if need_ctx:
                qc = _proj_simple(hc, attn_w_in, 0, D_BRANCH, functools.partial(_epi_scale_cast, scale),
                                  "attn_in_qc", tm_ctx, w_layer=j)
                zgc = _proj_simple(hc, attn_w_in, z0, D_BRANCH, _epi_silu, "attn_in_zc", tm_ctx, w_layer=j)
                yc = _attention(qc, None, None, kcd, vcd, zgc, sink, batch, ctx_len, ctx_len, False)
        else:
            w_v = gmlp_w_in[j, :, D_BRANCH:2 * D_BRANCH].astype(BF16)
            ws = gmlp_w_s[j].astype(BF16)

            def gmlp_branch(h, tm):
                uz = _proj_simple(h, gmlp_w_in, (0, 2 * D_BRANCH), D_BRANCH, _epi_gelu_times_silu,
                                  "gmlp_in_uz", tm, tn=PROJ_TN // 2, w_layer=j)
                gv, mu, rstd = _proj_gelu_stats(h, w_v, 0, min(tm, STATS_TM))
                return _spatial_gate(uz, gv, mu, rstd, gmlp_ln_g[j], gmlp_ln_b[j], ws, gmlp_b_s[j],
                                     min(tm, GATE_TM), n_groups=GMLP_GROUPS)

            y = gmlp_branch(hl, tm_lat)
            yc = gmlp_branch(hc, tm_ctx) if need_ctx else None

        res, hl = finish(y, xl, i, who_lat(tm_out), not last, last)
        if last:
            out = res
        else:
            xl = res
        if need_ctx:
            xc, hc = finish(yc, xc, i, who_ctx, i + 1 < DEPTH - 1 or (i + 1) % 3 == 1, False)
    return out.reshape(batch, seq, d)
```

```python
import functools
import math

import numpy as np
import jax
import jax.numpy as jnp
from jax import lax
from jax.experimental import pallas as pl
from jax.experimental.pallas import tpu as pltpu

F32 = jnp.float32
BF16 = jnp.bfloat16

D_MODEL = 2048
D_BRANCH = 4096
DEPTH = 4
GRID_W = 64
FNET_GROUPS = 16
FNET_GROUP_DIM = 256
HEAD_DIM = 64
KV_HEADS = 8
Q_GROUP = 8
ATTN_BLOCK = 128
ATTN_HEADS_PER_STEP = 8
ROPE_BASE = 10000.0
GMLP_CHUNK = 128
GMLP_GROUPS = 16
EPS = 1e-6
NEG_INF = -1e30
LOG2_E = math.log2(math.e)

LANES = 128
MXU_COLS = 256
PROJ_TN = 1024
PROJ_TM = 2048
OUT_TM = 512
STATS_TM = 256
GATE_TM = 256
GATE_SLOTS = 3
NORM_TM = 1024
DFT_NB = 128
DFT_ROWS = 16
SLABS = D_BRANCH // LANES
VMEM_LIMIT = 56 * 1024 * 1024


def _cparams(n_axes, vmem=VMEM_LIMIT):
    return pltpu.CompilerParams(dimension_semantics=("arbitrary",) * n_axes,
                                vmem_limit_bytes=vmem)


def _silu(z):
    return 0.5 * z * (1.0 + jnp.tanh(0.5 * z))


def _gelu_tanh(x):
    c = math.sqrt(2.0 / math.pi)
    return 0.5 * x * (1.0 + jnp.tanh(c * (x + 0.044715 * (x * x * x))))


def _mod_rmsnorm(x, g, scale, shift):
    y = x * lax.rsqrt(jnp.mean(x * x, axis=-1, keepdims=True) + EPS) * g
    return y * (1.0 + scale) + shift


def _channel_dft_matrix():
    n = FNET_GROUP_DIM
    k = np.arange(n, dtype=np.float64)
    ang = 2.0 * np.pi * np.outer(k, k) / n
    s = 1.0 / math.sqrt(n)
    return np.concatenate([np.cos(ang) * s, -np.sin(ang) * s], axis=1).astype(np.float32)


def _position_dft_matrices(seq):
    na, nb = seq // DFT_NB, DFT_NB
    a = np.arange(na, dtype=np.float64)
    b = np.arange(nb, dtype=np.float64)
    ang = 2.0 * np.pi * (a[None, None, :] * a[None, :, None] / na + b[:, None, None] * a[None, :, None] / seq)
    mr = np.cos(ang) / math.sqrt(na)
    mi = -np.sin(ang) / math.sqrt(na)
    fa = np.concatenate([np.concatenate([mr, -mi], axis=2), np.concatenate([mi, mr], axis=2)], axis=1)
    fa = np.concatenate([fa[0::2], fa[1::2]], axis=2)
    angb = 2.0 * np.pi * np.outer(b, b) / nb
    fb = np.concatenate([np.cos(angb), np.sin(angb)], axis=1) / math.sqrt(nb)
    return fa.astype(np.float32), fb.astype(np.float32)


def _dense_dft_matrix(n):
    k = np.arange(n, dtype=np.float64)
    ang = 2.0 * np.pi * np.outer(k, k) / n
    return (np.concatenate([np.cos(ang), np.sin(ang)], axis=1) / math.sqrt(n)).astype(np.float32)


def _rope_tables(seq, scale=1.0):
    nf = HEAD_DIM // 4
    inv = ROPE_BASE ** (-np.arange(nf, dtype=np.float64) / nf)
    t = np.arange(seq)
    rows = (t // GRID_W).astype(np.float64)
    cols = (t % GRID_W).astype(np.float64)
    parts_c, parts_s = [], []
    for pos in (rows, cols):
        ang = pos[:, None] * inv[None, :]
        parts_c += [np.cos(ang), np.cos(ang)]
        parts_s += [-np.sin(ang), np.sin(ang)]
    cos = np.concatenate(parts_c, axis=1) * scale
    sin = np.concatenate(parts_s, axis=1) * scale
    reps = LANES // HEAD_DIM
    return (np.tile(cos, (1, reps)).astype(np.float32), np.tile(sin, (1, reps)).astype(np.float32))


def _mods_kernel(cv_ref, w_ref, b_ref, o_ref):
    a = _silu(cv_ref[...])
    a_hi = a.astype(BF16)
    a_lo = (a - a_hi.astype(F32)).astype(BF16)
    w = w_ref[...].astype(BF16)
    part = jnp.dot(a_hi, w, preferred_element_type=F32) + jnp.dot(a_lo, w, preferred_element_type=F32)

    @pl.when(pl.program_id(1) == 0)
    def _():
        o_ref[...] = part + b_ref[...]

    @pl.when(pl.program_id(1) > 0)
    def _():
        o_ref[...] += part


def _mods(cvec, ada_w, ada_b):
    depth, d, n3 = ada_w.shape
    tk = 512
    return pl.pallas_call(
        _mods_kernel,
        grid=(depth, d // tk),
        in_specs=[pl.BlockSpec((8, tk), lambda i, k: (0, k)),
                  pl.BlockSpec((None, tk, n3), lambda i, k: (i, k, 0)),
                  pl.BlockSpec((None, 1, n3), lambda i, k: (i, 0, 0))],
        out_specs=pl.BlockSpec((None, 8, n3), lambda i, k: (i, 0, 0)),
        out_shape=jax.ShapeDtypeStruct((depth, 8, n3), F32),
        compiler_params=_cparams(2),
        name="ada_mods",
    )(cvec, ada_w, ada_b.reshape(depth, 1, n3))


def _mod_spec(layer, kind, who_of_row):
    return pl.BlockSpec((None, None, None, 1, D_MODEL),
                        lambda r, *_: (layer, who_of_row(r), kind, 0, 0))


def _row_spec(vec_layer):
    return pl.BlockSpec((None, 1, D_MODEL), lambda r, *_: (vec_layer, 0, 0))


def _prenorm_kernel(x_ref, g_ref, sc_ref, sh_ref, h_ref):
    h_ref[...] = _mod_rmsnorm(x_ref[...], g_ref[...], sc_ref[...], sh_ref[...]).astype(BF16)


def _prenorm(x2, norm_g3, mods, layer, who, tm):
    m = x2.shape[0]
    return pl.pallas_call(
        _prenorm_kernel,
        grid=(m // tm,),
        in_specs=[pl.BlockSpec((tm, D_MODEL), lambda r: (r, 0)),
                  _row_spec(layer), _mod_spec(layer, 1, who), _mod_spec(layer, 0, who)],
        out_specs=pl.BlockSpec((tm, D_MODEL), lambda r: (r, 0)),
        out_shape=jax.ShapeDtypeStruct((m, D_MODEL), BF16),
        compiler_params=_cparams(1),
        name="prenorm",
    )(x2, norm_g3, mods, mods)


def _proj_kernel(epilogue, n_extra, chunk, n_w, a_ref, *rest):
    w_refs, rest = rest[:n_w], rest[n_w:]
    if len(a_ref.shape) == 3:
        a_flat = rest[-1]
        rest = rest[:-1]

        @pl.when(pl.program_id(1) == 0)
        def _():
            a_flat[...] = a_ref[...].reshape(a_flat.shape)

        a = a_flat[...]
    else:
        a = a_ref[...]
    for c0 in range(0, w_refs[0].shape[1], chunk):
        accs = [jnp.dot(a, w_ref[:, c0:c0 + chunk].astype(BF16), preferred_element_type=F32)
                for w_ref in w_refs]
        epilogue(accs[0] if n_w == 1 else accs, c0, rest[:n_extra], rest[n_extra:])


def _proj(h, w, col0, ncols, tm, tn, epilogue, extras, extra_specs, out_shapes, out_specs, name,
          lhs_spec=None, chunk=MXU_COLS, w_layer=None):
    k = h.shape[-1]
    m = h.size // k
    col0s = col0 if isinstance(col0, tuple) else (col0,)
    scratch = [pltpu.VMEM((tm, k), h.dtype)] if lhs_spec is not None else []
    if lhs_spec is None:
        lhs_spec = pl.BlockSpec((tm, k), lambda i, j: (i, 0))

    def w_spec(off):
        if w.ndim == 3:
            return pl.BlockSpec((None, k, tn), lambda i, j: (w_layer, 0, j + off))
        return pl.BlockSpec((k, tn), lambda i, j: (0, j + off))

    return pl.pallas_call(
        functools.partial(_proj_kernel, epilogue, len(extras), min(chunk, tn), len(col0s)),
        grid=(m // tm, ncols // tn),
        in_specs=[lhs_spec] + [w_spec(c // tn) for c in col0s] + list(extra_specs),
        out_specs=out_specs,
        out_shape=out_shapes,
        scratch_shapes=scratch,
        compiler_params=_cparams(2),
        name=name,
    )(h, *([w] * len(col0s)), *extras)


def _epi_silu(acc, c0, extras, outs):
    outs[0][:, c0:c0 + acc.shape[1]] = _silu(acc).astype(BF16)


def _epi_silu_slabs(acc, c0, extras, outs):
    for t in range(acc.shape[1] // LANES):
        outs[0][c0 // LANES + t] = _silu(acc[:, t * LANES:(t + 1) * LANES]).astype(BF16)


def _epi_gelu_times_silu(accs, c0, extras, outs):
    acc_u, acc_z = accs
    outs[0][:, c0:c0 + acc_u.shape[1]] = (_gelu_tanh(acc_u) * _silu(acc_z)).astype(BF16)


def _epi_scale_cast(scale, acc, c0, extras, outs):
    outs[0][:, c0:c0 + acc.shape[1]] = (acc * scale).astype(BF16)


def _epi_channel_dft(split_rows, acc, c0, extras, outs):
    cs_ref = extras[0]
    ar_ref, ai_ref = outs
    tm = acc.shape[0]
    for gl in range(acc.shape[1] // FNET_GROUP_DIM):
        g = c0 // FNET_GROUP_DIM + gl
        ub = acc[:, gl * FNET_GROUP_DIM:(gl + 1) * FNET_GROUP_DIM].astype(BF16)
        ab = jnp.dot(ub, cs_ref[g], preferred_element_type=F32)
        if split_rows:
            ab4 = ab.reshape(tm // DFT_ROWS, 2, DFT_ROWS // 2, ab.shape[1])
            ab = jnp.concatenate([ab4[:, hf].reshape(tm // 2, ab.shape[1]) for hf in range(2)], axis=0)
        ab = ab.astype(BF16)
        ar_ref[2 * g] = ab[:, 0:128]
        ar_ref[2 * g + 1] = ab[:, 128:256]
        ai_ref[2 * g] = ab[:, 256:384]
        ai_ref[2 * g + 1] = ab[:, 384:512]


def _store_tile(out_ref, col, y, dup_heads):
    if not dup_heads:
        out_ref[:, col:col + LANES] = y.astype(BF16)
        return
    lane = lax.broadcasted_iota(jnp.int32, (1, LANES), 1)
    lo = lane < HEAD_DIM
    swapped = pltpu.roll(y, HEAD_DIM, 1)
    out_ref[:, 2 * col:2 * col + LANES] = jnp.where(lo, y, swapped).astype(BF16)
    out_ref[:, 2 * col + LANES:2 * col + 2 * LANES] = jnp.where(lo, swapped, y).astype(BF16)


def _epi_rope(dup_heads, acc, c0, extras, outs):
    cos = extras[0][...]
    sin = extras[1][...]
    lane = lax.broadcasted_iota(jnp.int32, (1, LANES), 1)
    first = (lane % 32) < 16
    for t in range(acc.shape[1] // LANES):
        x = acc[:, t * LANES:(t + 1) * LANES]
        partner = jnp.where(first, pltpu.roll(x, LANES - 16, 1), pltpu.roll(x, 16, 1))
        _store_tile(outs[0], c0 + t * LANES, x * cos + partner * sin, dup_heads)


def _epi_cast_dup(acc, c0, extras, outs):
    for t in range(acc.shape[1] // LANES):
        _store_tile(outs[0], c0 + t * LANES, acc[:, t * LANES:(t + 1) * LANES], True)


def _epi_rope_k_cast_v(accs, c0, extras, outs):
    _epi_rope(True, accs[0], c0, extras, outs[:1])
    _epi_cast_dup(accs[1], c0, extras, outs[1:])


def _proj_keys_values(h, w, k0, v0, ncols, cos_t, sin_t, seq, name, tm, w_layer):
    m = h.shape[0]
    per_batch = seq // tm
    tab_spec = pl.BlockSpec((tm, LANES), lambda i, j: (i % per_batch, 0))
    shape = jax.ShapeDtypeStruct((m, 2 * ncols), BF16)
    spec = pl.BlockSpec((tm, 2 * ncols), lambda i, j: (i, j))
    return _proj(h, w, (k0, v0), ncols, tm, ncols, _epi_rope_k_cast_v, (cos_t, sin_t), (tab_spec, tab_spec),
                 (shape, shape), (spec, spec), name, w_layer=w_layer)


def _proj_simple(h, w, col0, ncols, epilogue, name, tm, tn=PROJ_TN, w_layer=None, out_mult=1):
    m = h.shape[0]
    tn = min(tn, ncols)
    return _proj(h, w, col0, ncols, tm, tn, epilogue, (), (),
                 jax.ShapeDtypeStruct((m, out_mult * ncols), BF16),
                 pl.BlockSpec((tm, out_mult * tn), lambda i, j: (i, j)), name, w_layer=w_layer)


def _proj_rope(h, w, col0, ncols, cos_t, sin_t, seq, name, tm, tn=PROJ_TN, w_layer=None):
    m = h.shape[0]
    per_batch = seq // tm
    tab_spec = pl.BlockSpec((tm, LANES), lambda i, j: (i % per_batch, 0))
    return _proj(h, w, col0, ncols, tm, tn, functools.partial(_epi_rope, False),
                 (cos_t, sin_t), (tab_spec, tab_spec),
                 jax.ShapeDtypeStruct((m, ncols), BF16),
                 pl.BlockSpec((tm, tn), lambda i, j: (i, j)), name, w_layer=w_layer)


def _proj_silu_slabs(h, w, col0, name, tm, tn=PROJ_TN, w_layer=None):
    m = h.shape[0]
    return _proj(h, w, col0, D_BRANCH, tm, tn, _epi_silu_slabs, (), (),
                 jax.ShapeDtypeStruct((SLABS, m, LANES), BF16),
                 pl.BlockSpec((tn // LANES, tm, LANES), lambda i, j: (j, i, 0)), name, w_layer=w_layer)


def _proj_channel_dft(h, w, cs, name, tm, seq=None, tn=PROJ_TN):
    m = h.shape[0]
    lhs_spec = None
    if seq is not None:
        na, tiles = seq // DFT_NB, DFT_NB // DFT_ROWS
        assert tm == na * DFT_ROWS
        h = h.reshape(m // seq, na, DFT_NB, D_MODEL)
        lhs_spec = pl.BlockSpec((None, na, DFT_ROWS, D_MODEL), lambda i, j: (i // tiles, 0, i % tiles, 0))
    slab_shape = jax.ShapeDtypeStruct((SLABS, m, LANES), BF16)
    slab_spec = pl.BlockSpec((tn // LANES, tm, LANES), lambda i, j: (j, i, 0))
    groups = tn // FNET_GROUP_DIM
    cs_spec = pl.BlockSpec((groups,) + cs.shape[1:], lambda i, j: (j, 0, 0))
    return _proj(h, w, 0, D_BRANCH, tm, tn, functools.partial(_epi_channel_dft, seq is not None), (cs,), (cs_spec,),
                 (slab_shape, slab_shape), (slab_spec, slab_spec), name, lhs_spec=lhs_spec,
                 chunk=2 * MXU_COLS)


def _fold_mix_kernel(c_ref, s_ref, wm_ref, o_ref):
    gd = FNET_GROUP_DIM
    hp = lax.Precision.HIGHEST
    for g in range(wm_ref.shape[0]):
        wm = wm_ref[g]
        o_ref[g, :, 0:gd] = jnp.dot(c_ref[...], wm, preferred_element_type=F32, precision=hp).astype(BF16)
        o_ref[g, :, gd:2 * gd] = jnp.dot(s_ref[...], wm, preferred_element_type=F32, precision=hp).astype(BF16)


def _fold_mix(w_mix, groups_per_step=4):
    layers = w_mix.shape[0]
    gd = FNET_GROUP_DIM
    cs = _channel_dft_matrix()
    mat = pl.BlockSpec((gd, gd), lambda l, g: (0, 0))
    return pl.pallas_call(
        _fold_mix_kernel,
        grid=(layers, FNET_GROUPS // groups_per_step),
        in_specs=[mat, mat, pl.BlockSpec((None, groups_per_step, gd, gd), lambda l, g: (l, g, 0, 0))],
        out_specs=pl.BlockSpec((None, groups_per_step, gd, 2 * gd), lambda l, g: (l, g, 0, 0)),
        out_shape=jax.ShapeDtypeStruct((layers, FNET_GROUPS, gd, 2 * gd), BF16),
        compiler_params=_cparams(2),
        name="fnet_fold_mix",
    )(jnp.asarray(cs[:, :gd]), jnp.asarray(cs[:, gd:]), w_mix)


def _dft_kernel(na, pitch, gpitch, ar_ref, ai_ref, zg_ref, fa_ref, fb_ref, y_ref,
                xr_ref, xi_ref, er_ref, ei_ref, g_ref):
    half_rows = na * (DFT_ROWS // 2)
    xr_ref[...] = ar_ref[...].astype(F32)
    xi_ref[...] = ai_ref[...].astype(F32)

    def gather_a(b):
        start = (b // (DFT_ROWS // 2)) * half_rows + b % (DFT_ROWS // 2)
        zr = xr_ref[pl.ds(start, na, stride=DFT_ROWS // 2), :]
        zi = xi_ref[pl.ds(start, na, stride=DFT_ROWS // 2), :]
        return jnp.concatenate([zr, zi], axis=0).astype(BF16)

    def stage_a(pair, carry):
        b0 = 2 * pair
        d0, d1 = gather_a(b0), gather_a(b0 + 1)
        zero = jnp.zeros_like(d0)
        rhs = jnp.concatenate([jnp.concatenate([d0, zero], axis=1),
                               jnp.concatenate([zero, d1], axis=1)], axis=0)
        e = jnp.dot(fa_ref[pair], rhs, preferred_element_type=F32)
        for j in range(2):
            off = pl.multiple_of((b0 + j) * pitch, 8)
            er_ref[pl.ds(off, na), :] = e[:na, j * LANES:(j + 1) * LANES]
            ei_ref[pl.ds(off, na), :] = e[na:, j * LANES:(j + 1) * LANES]
        return carry

    lax.fori_loop(0, DFT_NB // 2, stage_a, 0, unroll=64)

    def gather_b(ka):
        er = er_ref[pl.ds(ka, DFT_NB, stride=pitch), :]
        ei = ei_ref[pl.ds(ka, DFT_NB, stride=pitch), :]
        return jnp.concatenate([er, ei], axis=0).astype(BF16)

    def stage_b(pair, carry):
        ka0 = 2 * pair
        rhs = jnp.concatenate([gather_b(ka0), gather_b(ka0 + 1)], axis=1)
        g = jnp.dot(fb_ref[...], rhs, preferred_element_type=F32)
        for j in range(2):
            off = pl.multiple_of((ka0 + j) * gpitch, 8)
            g_ref[pl.ds(off, DFT_NB), :] = g[:, j * LANES:(j + 1) * LANES]
        return carry

    lax.fori_loop(0, na // 2, stage_b, 0, unroll=32)

    def gate(kb, carry):
        rows = pl.ds(pl.multiple_of(kb * na, na), na)
        g = g_ref[pl.ds(kb, na, stride=gpitch), :]
        y_ref[rows, :] = (g * zg_ref[rows, :].astype(F32)).astype(BF16)
        return carry

    lax.fori_loop(0, DFT_NB, gate, 0, unroll=32)


def _position_dft(ar, ai, zg, batch, seq):
    na = seq // DFT_NB
    pitch = na + 8
    gpitch = DFT_NB + 8
    fa_np, fb_np = _position_dft_matrices(seq)
    fa = jnp.asarray(fa_np).astype(BF16)
    fb = jnp.asarray(fb_np).astype(BF16)
    slab = pl.BlockSpec((None, seq, LANES), lambda s, b: (s, b, 0))
    return pl.pallas_call(
        functools.partial(_dft_kernel, na, pitch, gpitch),
        grid=(SLABS, batch),
        in_specs=[slab, slab, slab,
                  pl.BlockSpec(fa.shape, lambda s, b: (0, 0, 0)),
                  pl.BlockSpec(fb.shape, lambda s, b: (0, 0))],
        out_specs=slab,
        out_shape=jax.ShapeDtypeStruct(ar.shape, BF16),
        scratch_shapes=[pltpu.VMEM((seq, LANES), F32),
                        pltpu.VMEM((seq, LANES), F32),
                        pltpu.VMEM((DFT_NB * pitch, LANES), F32),
                        pltpu.VMEM((DFT_NB * pitch, LANES), F32),
                        pltpu.VMEM((na * gpitch, LANES), F32)],
        compiler_params=_cparams(2),
        name="position_dft",
    )(ar, ai, zg, fa, fb)


def _dft_dense_kernel(ar_ref, ai_ref, zg_ref, fd_ref, y_ref):
    n = ar_ref.shape[0]
    d = jnp.concatenate([jnp.concatenate([ar_ref[s] for s in range(n)], axis=1),
                         jnp.concatenate([ai_ref[s] for s in range(n)], axis=1)], axis=0).astype(BF16)
    g = jnp.dot(fd_ref[...], d, preferred_element_type=F32)
    for s in range(n):
        y_ref[s] = (g[:, s * LANES:(s + 1) * LANES] * zg_ref[s].astype(F32)).astype(BF16)


def _position_dft_dense(ar, ai, zg, batch, seq, slabs_per_step=8):
    fd = jnp.asarray(_dense_dft_matrix(seq)).astype(BF16)
    slab = pl.BlockSpec((slabs_per_step, seq, LANES), lambda s, b: (s, b, 0))
    return pl.pallas_call(
        _dft_dense_kernel,
        grid=(SLABS // slabs_per_step, batch),
        in_specs=[slab, slab, slab, pl.BlockSpec(fd.shape, lambda s, b: (0, 0))],
        out_specs=slab,
        out_shape=jax.ShapeDtypeStruct(ar.shape, BF16),
        compiler_params=_cparams(2),
        name="position_dft_dense",
    )(ar, ai, zg, fd)


def _attn_kernel(n_band, nblk, sink_ref, q_ref, *refs):
    k_refs = refs[:n_band + 1]
    v_refs = refs[n_band + 1:2 * n_band + 2]
    zg_ref, y_ref = refs[2 * n_band + 2:]
    i = pl.program_id(2)
    blk = ATTN_BLOCK
    n_pair = Q_GROUP // 2
    qw = Q_GROUP * HEAD_DIM

    lane = lax.broadcasted_iota(jnp.int32, (1, LANES), 1)
    lo = lane < HEAD_DIM
    dn = (((1,), (1,)), ((), ()))
    if n_band:
        r = lax.broadcasted_iota(jnp.int32, (blk, blk), 0)
        c = lax.broadcasted_iota(jnp.int32, (blk, blk), 1)
        prev_ok = c >= r + jnp.where(i > 0, 0, blk)
        next_ok = c <= r - jnp.where(i < nblk - 1, 0, blk)

    def scores(hh):
        kl = slice(hh * LANES, (hh + 1) * LANES)
        keys = jnp.concatenate([ref[:, kl] for ref in k_refs], axis=0)
        q4 = jnp.concatenate([q_ref[:, hh * qw + t * LANES:hh * qw + (t + 1) * LANES]
                              for t in range(n_pair)], axis=0)
        qzero = jnp.zeros_like(q4)
        return [lax.dot_general(qm, keys, dn, preferred_element_type=F32)
                for qm in (jnp.where(lo, q4, qzero), jnp.where(lo, qzero, q4))]

    def finish(hh, s_both):
        h = pl.program_id(1) * ATTN_HEADS_PER_STEP + hh
        kl = slice(hh * LANES, (hh + 1) * LANES)
        vals = jnp.concatenate([ref[:, kl] for ref in v_refs], axis=0)
        ones = jnp.ones_like(vals)
        v_ext = (jnp.where(lo, vals, ones), jnp.where(lo, ones, vals))
        o_ext, sink_term = [], []
        for hd, s in enumerate(s_both):
            p_rows, sink_rows = [], []
            for t in range(n_pair):
                st = s[t * blk:(t + 1) * blk]
                parts = [st[:, j * blk:(j + 1) * blk] for j in range(st.shape[1] // blk)]
                if n_band:
                    parts[0] = jnp.where(prev_ok, parts[0], NEG_INF)
                    parts[2] = jnp.where(next_ok, parts[2], NEG_INF)
                sk = sink_ref[h, 2 * t + hd] * LOG2_E
                mx = parts[0]
                for part in parts[1:]:
                    mx = jnp.maximum(mx, part)
                mx = jnp.maximum(jnp.max(mx, axis=-1, keepdims=True), sk)
                p_rows.append(jnp.concatenate([jnp.exp2(part - mx).astype(BF16) for part in parts], axis=1))
                sink_rows.append(jnp.exp2(sk - mx))
            p = jnp.concatenate(p_rows, axis=0)
            o_ext.append(jnp.dot(p, v_ext[hd], preferred_element_type=F32))
            sink_term.append(jnp.concatenate(sink_rows, axis=0))
        o = jnp.where(lo, o_ext[0], o_ext[1])
        denom = jnp.where(lo, pltpu.roll(o_ext[0], HEAD_DIM, 1) + sink_term[0],
                          pltpu.roll(o_ext[1], HEAD_DIM, 1) + sink_term[1])
        res = o / denom
        for t in range(n_pair):
            tile = slice(hh * qw + t * LANES, hh * qw + (t + 1) * LANES)
            y_ref[:, tile] = (res[t * blk:(t + 1) * blk] * zg_ref[:, tile].astype(F32)).astype(BF16)

    pending = scores(0)
    for hh in range(ATTN_HEADS_PER_STEP):
        upcoming = scores(hh + 1) if hh + 1 < ATTN_HEADS_PER_STEP else None
        finish(hh, pending)
        pending = upcoming


def _attention(q, kd, vd, kcd, vcd, zg, sink, batch, seq, ctx_len, use_band):
    blk = ATTN_BLOCK
    nblk = seq // blk
    hps = ATTN_HEADS_PER_STEP
    qw = hps * Q_GROUP * HEAD_DIM
    kw = hps * LANES
    q_spec = pl.BlockSpec((blk, qw), lambda b, h, i: (b * nblk + i, h))
    ctx_spec = pl.BlockSpec((ctx_len, kw), lambda b, h, i: (b, h))
    if use_band:
        def band(delta):
            return pl.BlockSpec(
                (blk, kw), lambda b, h, i: (b * nblk + jnp.clip(i + delta, 0, nblk - 1), h))
        k_specs = [band(-1), band(0), band(1), ctx_spec]
        k_args, v_args = [kd, kd, kd, kcd], [vd, vd, vd, vcd]
        n_band = 3
    else:
        k_specs, k_args, v_args, n_band = [ctx_spec], [kcd], [vcd], 0
    return pl.pallas_call(
        functools.partial(_attn_kernel, n_band, nblk),
        grid=(batch, KV_HEADS // hps, nblk),
        in_specs=[pl.BlockSpec(memory_space=pltpu.SMEM), q_spec] + k_specs + k_specs + [q_spec],
        out_specs=q_spec,
        out_shape=jax.ShapeDtypeStruct(q.shape, BF16),
        compiler_params=_cparams(3),
        name="attention_band" if use_band else "attention_ctx",
    )(sink, q, *k_args, *v_args, zg)


def _gelu_stats_kernel(chunk, a_ref, w_ref, gv_ref, mu_ref, rstd_ref):
    a = a_ref[...]
    n = w_ref.shape[1]
    mean = m2 = None
    for idx, c0 in enumerate(range(0, n, chunk)):
        ge = _gelu_tanh(jnp.dot(a, w_ref[:, c0:c0 + chunk], preferred_element_type=F32))
        gv_ref[:, c0:c0 + chunk] = ge.astype(BF16)
        cmean = jnp.mean(ge, axis=-1, keepdims=True)
        d = ge - cmean
        cm2 = jnp.sum(d * d, axis=-1, keepdims=True)
        if idx == 0:
            mean, m2 = cmean, cm2
        else:
            delta = cmean - mean
            mean = mean + delta * (1.0 / (idx + 1))
            m2 = m2 + cm2 + delta * delta * (chunk * idx / (idx + 1))
    mu_ref[...] = jnp.broadcast_to(mean, mu_ref.shape)
    rstd_ref[...] = jnp.broadcast_to(lax.rsqrt(m2 * (1.0 / n) + EPS), rstd_ref.shape)


def _proj_gelu_stats(h, w, col0, tm, chunk=2 * MXU_COLS):
    m, k = h.shape
    n = D_BRANCH
    col = pl.BlockSpec((tm, LANES), lambda i: (i, 0))
    stat = jax.ShapeDtypeStruct((m, LANES), F32)
    return pl.pallas_call(
        functools.partial(_gelu_stats_kernel, chunk),
        grid=(m // tm,),
        in_specs=[pl.BlockSpec((tm, k), lambda i: (i, 0)),
                  pl.BlockSpec((k, n), lambda i: (0, col0 // n), pipeline_mode=pl.Buffered(1))],
        out_specs=(pl.BlockSpec((tm, n), lambda i: (i, 0)), col, col),
        out_shape=(jax.ShapeDtypeStruct((m, n), BF16), stat, stat),
        compiler_params=_cparams(1),
        name="gmlp_in_v",
    )(h, w)


def _sgu_kernel(n_chunks, n_groups, uz_ref, gv_ref, mu_ref, rstd_ref, lg_ref, lb_ref,
                ws_ref, bs_ref, y_ref):
    gd = D_BRANCH // GMLP_GROUPS
    for c in range(n_chunks):
        rows = slice(c * GMLP_CHUNK, (c + 1) * GMLP_CHUNK)
        mu = jnp.concatenate([mu_ref[rows, :]] * (gd // LANES), axis=1)
        rstd = jnp.concatenate([rstd_ref[rows, :]] * (gd // LANES), axis=1)
        for gl in range(n_groups):
            cols = slice(gl * gd, (gl + 1) * gd)
            ws = ws_ref[gl]
            bs = bs_ref[gl]
            lg = lg_ref[:, cols]
            lb = lb_ref[:, cols]
            vn = ((gv_ref[rows, cols].astype(F32) - mu) * rstd * lg + lb).astype(BF16)
            s = jnp.dot(ws, vn, preferred_element_type=F32) + bs
            y_ref[rows, cols] = (uz_ref[rows, cols].astype(F32) * s).astype(BF16)


def _sgu_ring_kernel(n_steps, tm, n_chunks, n_groups, uz_hbm, gv_hbm, mu_ref, rstd_ref, lg_ref, lb_ref,
                     ws_ref, bs_ref, y_ref, uz_buf, gv_buf, sems):
    r = pl.program_id(0)

    def copies(step, slot):
        rows = pl.ds(step * tm, tm)
        return (pltpu.make_async_copy(uz_hbm.at[rows, :], uz_buf.at[slot], sems.at[0, slot]),
                pltpu.make_async_copy(gv_hbm.at[rows, :], gv_buf.at[slot], sems.at[1, slot]))

    @pl.when(r == 0)
    def _():
        for k in range(min(GATE_SLOTS - 1, n_steps)):
            for cp in copies(k, k):
                cp.start()

    ahead = r + (GATE_SLOTS - 1)

    @pl.when(ahead < n_steps)
    def _():
        for cp in copies(ahead, ahead % GATE_SLOTS):
            cp.start()

    slot = r % GATE_SLOTS
    for cp in copies(r, slot):
        cp.wait()
    _sgu_kernel(n_chunks, n_groups, uz_buf.at[slot], gv_buf.at[slot], mu_ref, rstd_ref, lg_ref, lb_ref,
                ws_ref, bs_ref, y_ref)


def _spatial_gate(uz, gv, mu, rstd, ln_g, ln_b, w_s, b_s, tm):
    m = uz.shape[0]
    n_groups = GMLP_GROUPS
    n_steps = m // tm
    tile = pl.BlockSpec((tm, D_BRANCH), lambda r: (r, 0))
    col = pl.BlockSpec((tm, LANES), lambda r: (r, 0))
    vec = pl.BlockSpec((1, D_BRANCH), lambda r: (0, 0))
    hbm = pl.BlockSpec(memory_space=pl.ANY)
    return pl.pallas_call(
        functools.partial(_sgu_ring_kernel, n_steps, tm, tm // GMLP_CHUNK, n_groups),
        grid=(n_steps,),
        in_specs=[hbm, hbm, col, col, vec, vec,
                  pl.BlockSpec((n_groups, GMLP_CHUNK, GMLP_CHUNK), lambda r: (0, 0, 0)),
                  pl.BlockSpec((n_groups, GMLP_CHUNK, 1), lambda r: (0, 0, 0))],
        out_specs=tile,
        out_shape=jax.ShapeDtypeStruct(uz.shape, BF16),
        scratch_shapes=[pltpu.VMEM((GATE_SLOTS, tm, D_BRANCH), BF16),
                        pltpu.VMEM((GATE_SLOTS, tm, D_BRANCH), BF16),
                        pltpu.SemaphoreType.DMA((2, GATE_SLOTS))],
        compiler_params=_cparams(1),
        name="gmlp_spatial_gate",
    )(uz, gv, mu, rstd, ln_g.reshape(1, D_BRANCH), ln_b.reshape(1, D_BRANCH),
      w_s, b_s.reshape(GMLP_GROUPS, GMLP_CHUNK, 1))


def _wout_kernel(mode, y_ref, w_ref, x_ref, gate_ref, *refs):
    if len(y_ref.shape) == 3:
        y = jnp.concatenate([y_ref[s] for s in range(y_ref.shape[0])], axis=1)
    else:
        y = y_ref[...]
    acc = jnp.dot(y, w_ref[...], preferred_element_type=F32)
    xn = x_ref[...] + gate_ref[...] * acc
    if mode == "final":
        g_ref, o_ref = refs
        o_ref[...] = xn * lax.rsqrt(jnp.mean(xn * xn, axis=-1, keepdims=True) + EPS) * g_ref[...]
    elif mode == "next":
        g_ref, sc_ref, sh_ref, xo_ref, h_ref = refs
        xo_ref[...] = xn
        h_ref[...] = _mod_rmsnorm(xn, g_ref[...], sc_ref[...], sh_ref[...]).astype(BF16)
    else:
        refs[0][...] = xn


def _out_proj(y, w_out, x2, mods, layer, who, mode, norm_vec, tm):
    m = x2.shape[0]
    row = pl.BlockSpec((tm, D_MODEL), lambda r: (r, 0))
    if y.ndim == 3:
        y_spec = pl.BlockSpec((SLABS, tm, LANES), lambda r: (0, r, 0))
    else:
        y_spec = pl.BlockSpec((tm, D_BRANCH), lambda r: (r, 0))
    in_specs = [y_spec,
                pl.BlockSpec((None, D_BRANCH, D_MODEL), lambda r: (layer, 0, 0),
                             pipeline_mode=pl.Buffered(1)),
                row, _mod_spec(layer, 2, who)]
    args = [y, w_out, x2, mods]
    xs = jax.ShapeDtypeStruct((m, D_MODEL), F32)
    if mode == "final":
        in_specs.append(pl.BlockSpec((1, D_MODEL), lambda r: (0, 0)))
        args.append(norm_vec.reshape(1, D_MODEL))
        out_shape, out_specs = xs, row
    elif mode == "next":
        in_specs += [_row_spec(layer + 1), _mod_spec(layer + 1, 1, who), _mod_spec(layer + 1, 0, who)]
        args += [norm_vec, mods, mods]
        out_shape = (xs, jax.ShapeDtypeStruct((m, D_MODEL), BF16))
        out_specs = (row, row)
    else:
        out_shape, out_specs = xs, row
    return pl.pallas_call(
        functools.partial(_wout_kernel, mode),
        grid=(m // tm,),
        in_specs=in_specs,
        out_specs=out_specs,
        out_shape=out_shape,
        compiler_params=_cparams(1),
        name="out_proj_" + mode,
    )(*args)


def kernel(x, c, ctx, c_ctx, norm_g, ada_w, ada_b, w_out, fnet_w_in, fnet_w_mix, attn_w_in, attn_sink,
           gmlp_w_in, gmlp_w_s, gmlp_b_s, gmlp_ln_g, gmlp_ln_b, final_g):
    batch, seq, d = x.shape
    ctx_len = ctx.shape[1]
    assert d == D_MODEL and seq % (DFT_NB * 8) == 0 and seq % GRID_W == 0 and batch < 8
    m_lat, m_ctx = batch * seq, batch * ctx_len
    tm_lat = min(PROJ_TM, seq)
    tm_ctx = m_ctx
    tm_out = OUT_TM

    def who_lat(tm):
        return lambda r: (r * tm) // seq

    who_ctx = lambda r: batch

    cvec = jnp.zeros((8, d), F32).at[:batch].set(c).at[batch].set(c_ctx)
    mods = _mods(cvec, ada_w, ada_b).reshape(DEPTH, 8, 3, 1, d)
    norm_g3 = norm_g.reshape(DEPTH, 1, d)
    w_out_b = w_out.astype(BF16)

    xl = x.reshape(m_lat, d)
    xc = ctx.reshape(m_ctx, d)
    hl = _prenorm(xl, norm_g3, mods, 0, who_lat(NORM_TM), NORM_TM)
    hc = _prenorm(xc, norm_g3, mods, 0, who_ctx, tm_ctx)


    def finish(y, x2, layer, who_fn, need_next, is_final):
        if is_final:
            return _out_proj(y, w_out_b, x2, mods, layer, who_fn, "final", final_g, tm_out), None
        if need_next:
            return _out_proj(y, w_out_b, x2, mods, layer, who_fn, "next", norm_g3, tm_out)
        return _out_proj(y, w_out_b, x2, mods, layer, who_fn, "plain", None, tm_out), None

    cs_all = _fold_mix(fnet_w_mix)
    out = None
    for i in range(DEPTH):
        kind, j = i % 3, i // 3
        need_ctx = i < DEPTH - 1
        last = i == DEPTH - 1
        if kind == 0:
            cs = cs_all[j]
            tm_dft = (seq // DFT_NB) * DFT_ROWS
            w_u = fnet_w_in[j, :, :D_BRANCH].astype(BF16)
            ar, ai = _proj_channel_dft(hl, w_u, cs, "fnet_in_u", tm_dft, seq=seq)
            zg = _proj_silu_slabs(hl, fnet_w_in, D_BRANCH, "fnet_in_z", tm_lat, w_layer=j)
            y = _position_dft(ar, ai, zg, batch, seq)
            yc = None
            if need_ctx:
                ar, ai = _proj_channel_dft(hc, w_u, cs, "fnet_in_u_ctx", tm_ctx)
                zg = _proj_silu_slabs(hc, fnet_w_in, D_BRANCH, "fnet_in_z_ctx", tm_ctx, w_layer=j)
                yc = _position_dft_dense(ar, ai, zg, batch, ctx_len)
        elif kind == 1:
            kvw = KV_HEADS * HEAD_DIM
            k0, v0, z0 = D_BRANCH, D_BRANCH + kvw, D_BRANCH + 2 * kvw
            sink = attn_sink[j].reshape(KV_HEADS, Q_GROUP)
            scale = HEAD_DIM ** -0.5 * LOG2_E
            q_tabs = [jnp.asarray(t) for t in _rope_tables(seq, scale)]
            k_tabs = [jnp.asarray(t) for t in _rope_tables(seq)]
            q = _proj_rope(hl, attn_w_in, 0, D_BRANCH, *q_tabs, seq, "attn_in_q", tm_lat, w_layer=j)
            kd, vd = _proj_keys_values(hl, attn_w_in, k0, v0, kvw, *k_tabs, seq, "attn_in_kv", tm_lat, j)
            zg = _proj_simple(hl, attn_w_in, z0, D_BRANCH, _epi_silu, "attn_in_z", tm_lat, w_layer=j)
            kcd = _proj_simple(hc, attn_w_in, k0, kvw, _epi_cast_dup, "attn_in_kc", tm_ctx, w_layer=j, out_mult=2)
            vcd = _proj_simple(hc, attn_w_in, v0, kvw, _epi_cast_dup, "attn_in_vc", tm_ctx, w_layer=j, out_mult=2)
            y = _attention(q, kd, vd, kcd, vcd, zg, sink, batch, seq, ctx_len, True)
            yc = None
            if need_ctx:
                qc = _proj_simple(hc, attn_w_in, 0, D_BRANCH, functools.partial(_epi_scale_cast, scale),
                                  "attn_in_qc", tm_ctx, w_layer=j)
                zgc = _proj_simple(hc, attn_w_in, z0, D_BRANCH, _epi_silu, "attn_in_zc", tm_ctx, w_layer=j)
                yc = _attention(qc, None, None, kcd, vcd, zgc, sink, batch, ctx_len, ctx_len, False)
        else:
            w_v = gmlp_w_in[j, :, D_BRANCH:2 * D_BRANCH].astype(BF16)
            ws = gmlp_w_s[j].astype(BF16)

            def gmlp_branch(h, tm):
                uz = _proj_simple(h, gmlp_w_in, (0, 2 * D_BRANCH), D_BRANCH, _epi_gelu_times_silu,
                                  "gmlp_in_uz", tm, tn=PROJ_TN // 2, w_layer=j)
                gv, mu, rstd = _proj_gelu_stats(h, w_v, 0, min(tm, STATS_TM))
                return _spatial_gate(uz, gv, mu, rstd, gmlp_ln_g[j], gmlp_ln_b[j], ws, gmlp_b_s[j],
                                     min(tm, GATE_TM))

            y = gmlp_branch(hl, tm_lat)
            yc = gmlp_branch(hc, tm_ctx) if need_ctx else None

        res, hl = finish(y, xl, i, who_lat(tm_out), not last, last)
        if last:
            out = res
        else:
            xl = res
        if need_ctx:
            xc, hc = finish(yc, xc, i, who_ctx, i + 1 < DEPTH - 1 or (i + 1) % 3 == 1, False)
    return out.reshape(batch, seq, d)
```
